```python
import math
import jax, jax.numpy as jnp
from jax import lax
import numpy as np

D_MODEL = 2048
BATCH = 1
SEQ = 8192
DEPTH = 2

S5_WIDTH = D_MODEL // 2
S5_GROUP = 16
S5_GROUPS = S5_WIDTH // S5_GROUP
S5_STATE = 64
RET_WIDTH = D_MODEL // 2
RET_HEADS = 4
RET_HEAD_DIM = RET_WIDTH // RET_HEADS
RET_CHUNK = 128
RET_ROPE_THETA = 10000.0
EVEN_IN_WIDTH = S5_WIDTH + 4 * RET_WIDTH
ATT_HEADS = 16
ATT_KV_HEADS = 4
ATT_HEAD_DIM = D_MODEL // ATT_HEADS
ATT_GROUP = ATT_HEADS // ATT_KV_HEADS
ATT_WINDOW = 128
ATT_BLOCK = 128
ROPE_THETA = 500000.0
ROPE_DIM = ATT_HEAD_DIM // 4
ODD_IN_WIDTH = (ATT_HEADS + 2 * ATT_KV_HEADS) * ATT_HEAD_DIM
D_FF = 4 * D_MODEL
N_EVEN = (DEPTH + 1) // 2
N_ODD = DEPTH // 2
DEEPNORM_ALPHA = (2 * DEPTH) ** 0.25
DEEPNORM_BETA = (8 * DEPTH) ** -0.25
LN_EPS = 1e-5
HEAD_NORM_EPS = 1e-6
NEG_INF = -1e30

kernel_name = 'hybrid_s5_retention_swa_encoder'

F32 = jnp.float32


def _layer_norm(x, g, b):
    xf = x.astype(F32)
    mu = xf.mean(-1, keepdims=True)
    var = jnp.square(xf - mu).mean(-1, keepdims=True)
    return ((xf - mu) * lax.rsqrt(var + LN_EPS) * g.astype(F32) + b.astype(F32)).astype(x.dtype)


def _sq_relu_mlp(x, w1, w2):
    return jnp.square(jax.nn.relu(x @ w1)) @ w2


def _rotary(x, pos, rot_dim, theta):
    half = rot_dim // 2
    inv_freq = 1.0 / (theta ** (jnp.arange(half, dtype=F32) / half))
    ang = pos.astype(F32)[:, None] * inv_freq[None, :]
    cos = jnp.cos(ang)[None, :, None, :]
    sin = jnp.sin(ang)[None, :, None, :]
    xr = x[..., :rot_dim].astype(F32)
    x1, x2 = xr[..., :half], xr[..., half:]
    rot = jnp.concatenate([x1 * cos - x2 * sin, x2 * cos + x1 * sin], axis=-1).astype(x.dtype)
    return jnp.concatenate([rot, x[..., rot_dim:]], axis=-1)


def _cplx_combine(e1, e2):
    a1r, a1i, b1r, b1i = e1
    a2r, a2i, b2r, b2i = e2
    return (a2r * a1r - a2i * a1i,
            a2r * a1i + a2i * a1r,
            a2r * b1r - a2i * b1i + b2r,
            a2r * b1i + a2i * b1r + b2i)


def _s5_scan(u, lam_re, lam_im, log_step, b_re, b_im, c_re, c_im):
    lam_re = jnp.minimum(lam_re.astype(F32), -1e-4)
    lam_im = lam_im.astype(F32)
    step = jnp.exp(log_step.astype(F32))[:, None]
    mag = jnp.exp(lam_re * step)
    ar = mag * jnp.cos(lam_im * step)
    ai = mag * jnp.sin(lam_im * step)
    nr, ni = ar - 1.0, ai
    den = lam_re * lam_re + lam_im * lam_im
    zr = (nr * lam_re + ni * lam_im) / den
    zi = (ni * lam_re - nr * lam_im) / den
    b_re = b_re.astype(F32)
    b_im = b_im.astype(F32)
    bbr = zr[..., None] * b_re - zi[..., None] * b_im
    bbi = zr[..., None] * b_im + zi[..., None] * b_re
    bu_r = jnp.einsum('blgc,gpc->blgp', u, bbr)
    bu_i = jnp.einsum('blgc,gpc->blgp', u, bbi)
    a_r = jnp.broadcast_to(ar, bu_r.shape)
    a_i = jnp.broadcast_to(ai, bu_r.shape)
    _, _, h_r, h_i = lax.associative_scan(_cplx_combine, (a_r, a_i, bu_r, bu_i), axis=1)
    return (jnp.einsum('gcp,blgp->blgc', c_re.astype(F32), h_r)
            - jnp.einsum('gcp,blgp->blgc', c_im.astype(F32), h_i))


def _retention_dir(q, k, v, log_g, include_diag):
    bsz, L, H, dk = q.shape
    dv = v.shape[-1]
    C = RET_CHUNK
    nc = L // C
    qc = q.reshape(bsz, nc, C, H, dk)
    kc = k.reshape(bsz, nc, C, H, dk)
    vc = v.reshape(bsz, nc, C, H, dv)
    idx = jnp.arange(C)
    diff = idx[:, None] - idx[None, :]
    mask = diff >= 0 if include_diag else diff > 0
    decay_intra = jnp.where(mask[None],
                            jnp.exp(log_g[:, None, None] * jnp.maximum(diff, 0)[None].astype(F32)),
                            0.0)
    scores = jnp.einsum('bnihd,bnjhd->bnhij', qc, kc) * decay_intra
    intra = jnp.einsum('bnhij,bnjhe->bnihe', scores, vc)
    idx_f = idx.astype(F32)
    k_decay = jnp.exp((C - 1.0 - idx_f)[:, None] * log_g[None, :])
    q_decay = jnp.exp((idx_f + 1.0)[:, None] * log_g[None, :])
    chunk_decay = jnp.exp(C * log_g)
    chunk_kv = jnp.einsum('bnjhd,jh,bnjhe->bnhde', kc, k_decay, vc)

    def step(state, kv):
        return state * chunk_decay[None, :, None, None] + kv, state

    init = jnp.zeros((bsz, H, dk, dv), F32)
    _, s_prev = lax.scan(step, init, jnp.moveaxis(chunk_kv, 1, 0))
    s_prev = jnp.moveaxis(s_prev, 0, 1)
    cross = jnp.einsum('bnihd,ih,bnhde->bnihe', qc, q_decay, s_prev)
    return (intra + cross).reshape(bsz, L, H, dv)


def _even_mixer(x, w_in, w_out, lam_re, lam_im, log_step, b_re, b_im, c_re, c_im,
                d_skip, w_glu, b_glu, ret_log_decay):
    bsz, L, _ = x.shape
    pos = jnp.arange(L)
    h = x @ w_in
    u = h[..., :S5_WIDTH]
    q, k, v, gate = jnp.split(h[..., S5_WIDTH:], 4, axis=-1)

    uf = u.astype(F32).reshape(bsz, L, S5_GROUPS, S5_GROUP)
    y_fwd = _s5_scan(uf, lam_re[0], lam_im[0], log_step[0], b_re[0], b_im[0], c_re[0], c_im[0])
    y_bwd = jnp.flip(_s5_scan(jnp.flip(uf, 1), lam_re[1], lam_im[1], log_step[1],
                              b_re[1], b_im[1], c_re[1], c_im[1]), 1)
    y = jax.nn.gelu((y_fwd + y_bwd + d_skip.astype(F32) * uf).reshape(bsz, L, S5_WIDTH))
    s5_out = (y * jax.nn.sigmoid(y @ w_glu.astype(F32) + b_glu.astype(F32))).astype(x.dtype)

    q = _rotary(q.reshape(bsz, L, RET_HEADS, RET_HEAD_DIM), pos, RET_HEAD_DIM, RET_ROPE_THETA).astype(F32)
    k = _rotary(k.reshape(bsz, L, RET_HEADS, RET_HEAD_DIM), pos, RET_HEAD_DIM, RET_ROPE_THETA).astype(F32)
    k = k * (RET_HEAD_DIM ** -0.5)
    v = v.reshape(bsz, L, RET_HEADS, RET_HEAD_DIM).astype(F32)
    log_g = -jnp.abs(ret_log_decay.astype(F32))
    o = (_retention_dir(q, k, v, log_g[0], True)
         + jnp.flip(_retention_dir(jnp.flip(q, 1), jnp.flip(k, 1), jnp.flip(v, 1), log_g[1], False), 1))
    mu = o.mean(-1, keepdims=True)
    var = jnp.square(o - mu).mean(-1, keepdims=True)
    o = (o - mu) * lax.rsqrt(var + HEAD_NORM_EPS)
    ret_out = (o.reshape(bsz, L, RET_WIDTH) * jax.nn.silu(gate.astype(F32))).astype(x.dtype)

    return jnp.concatenate([s5_out, ret_out], axis=-1) @ w_out


def _band(t, bsz, nb):
    tb = t.reshape(bsz, nb, ATT_BLOCK, ATT_KV_HEADS, ATT_HEAD_DIM)
    tp = jnp.pad(tb, ((0, 0), (1, 1), (0, 0), (0, 0), (0, 0)))
    return jnp.concatenate([tp[:, :-2], tp[:, 1:-1], tp[:, 2:]], axis=2)


def _odd_mixer(x, w_in, w_out, sink):
    bsz, L, _ = x.shape
    nb = L // ATT_BLOCK
    pos = jnp.arange(L)
    hd = ATT_HEAD_DIM
    h = x @ w_in
    q = h[..., :ATT_HEADS * hd].reshape(bsz, L, ATT_HEADS, hd)
    k = h[..., ATT_HEADS * hd:(ATT_HEADS + ATT_KV_HEADS) * hd].reshape(bsz, L, ATT_KV_HEADS, hd)
    v = h[..., (ATT_HEADS + ATT_KV_HEADS) * hd:].reshape(bsz, L, ATT_KV_HEADS, hd)
    q = _rotary(q, pos, ROPE_DIM, ROPE_THETA)
    k = _rotary(k, pos, ROPE_DIM, ROPE_THETA)
    qb = q.reshape(bsz, nb, ATT_BLOCK, ATT_KV_HEADS, ATT_GROUP, hd)
    kb = _band(k, bsz, nb)
    vb = _band(v, bsz, nb)
    s = jnp.einsum('bnqhgd,bnkhd->bnhgqk', qb, kb).astype(F32) * (hd ** -0.5)
    t_idx = jnp.arange(ATT_BLOCK)[:, None]
    s_idx = jnp.arange(3 * ATT_BLOCK)[None, :]
    in_win = jnp.abs(t_idx - s_idx + ATT_BLOCK) <= ATT_WINDOW
    key_pos = (jnp.arange(nb)[:, None] - 1) * ATT_BLOCK + jnp.arange(3 * ATT_BLOCK)[None, :]
    valid = (key_pos >= 0) & (key_pos < L)
    mask = in_win[None] & valid[:, None, :]
    s = jnp.where(mask[None, :, None, None], s, NEG_INF)
    sink_b = sink.astype(F32).reshape(1, 1, ATT_KV_HEADS, ATT_GROUP, 1, 1)
    m = jnp.maximum(s.max(-1, keepdims=True), sink_b)
    p = jnp.exp(s - m)
    p = p / (p.sum(-1, keepdims=True) + jnp.exp(sink_b - m))
    o = jnp.einsum('bnhgqk,bnkhd->bnqhgd', p.astype(vb.dtype), vb)
    return o.reshape(bsz, L, ATT_HEADS * hd) @ w_out


def setup_inputs(seed: int = 0) -> dict:
    key = jax.random.key(seed)
    ks = jax.random.split(key, 24)
    nrm = jax.random.normal
    G, P, Cg = S5_GROUPS, S5_STATE, S5_GROUP
    x = nrm(ks[0], (BATCH, SEQ, D_MODEL), F32)
    ln_g = 1.0 + 0.02 * nrm(ks[1], (DEPTH, 2, D_MODEL), F32)
    ln_b = 0.02 * nrm(ks[2], (DEPTH, 2, D_MODEL), F32)
    mlp_w1 = nrm(ks[3], (DEPTH, D_MODEL, D_FF), F32) * D_MODEL ** -0.5
    mlp_w2 = nrm(ks[4], (DEPTH, D_FF, D_MODEL), F32) * (D_FF ** -0.5 * DEEPNORM_BETA)
    even_w_in = nrm(ks[5], (N_EVEN, D_MODEL, EVEN_IN_WIDTH), F32) * D_MODEL ** -0.5
    even_w_out = nrm(ks[6], (N_EVEN, S5_WIDTH + RET_WIDTH, D_MODEL), F32) * ((S5_WIDTH + RET_WIDTH) ** -0.5 * DEEPNORM_BETA)
    n_idx = jnp.arange(P, dtype=F32)
    s5_lambda_re = -0.5 * (1.0 + 0.02 * nrm(ks[7], (N_EVEN, 2, G, P), F32))
    s5_lambda_im = math.pi * n_idx + 0.02 * nrm(ks[8], (N_EVEN, 2, G, P), F32)
    s5_log_step = jax.random.uniform(ks[9], (N_EVEN, 2, G), F32, math.log(1e-3), math.log(1e-1))
    s5_b_re = nrm(ks[10], (N_EVEN, 2, G, P, Cg), F32) * (2 * Cg) ** -0.5
    s5_b_im = nrm(ks[11], (N_EVEN, 2, G, P, Cg), F32) * (2 * Cg) ** -0.5
    s5_c_re = nrm(ks[12], (N_EVEN, 2, G, Cg, P), F32) * P ** -0.5
    s5_c_im = nrm(ks[13], (N_EVEN, 2, G, Cg, P), F32) * P ** -0.5
    s5_d = nrm(ks[14], (N_EVEN, G, Cg), F32)
    s5_w_glu = nrm(ks[15], (N_EVEN, S5_WIDTH, S5_WIDTH), F32) * S5_WIDTH ** -0.5
    s5_b_glu = 0.01 * nrm(ks[16], (N_EVEN, S5_WIDTH), F32)
    base_decay = jnp.log(1.0 - 2.0 ** (-5.0 - jnp.arange(RET_HEADS, dtype=F32)))
    ret_log_decay = base_decay * (1.0 + 0.05 * nrm(ks[17], (N_EVEN, 2, RET_HEADS), F32))
    odd_w_in = nrm(ks[18], (N_ODD, D_MODEL, ODD_IN_WIDTH), F32) * D_MODEL ** -0.5
    odd_w_out = nrm(ks[19], (N_ODD, ATT_HEADS * ATT_HEAD_DIM, D_MODEL), F32) * ((ATT_HEADS * ATT_HEAD_DIM) ** -0.5 * DEEPNORM_BETA)
    attn_sink = 0.5 * nrm(ks[20], (N_ODD, ATT_HEADS), F32)
    return {'x': x, 'ln_g': ln_g, 'ln_b': ln_b, 'mlp_w1': mlp_w1, 'mlp_w2': mlp_w2,
            'even_w_in': even_w_in, 'even_w_out': even_w_out,
            's5_lambda_re': s5_lambda_re, 's5_lambda_im': s5_lambda_im, 's5_log_step': s5_log_step,
            's5_b_re': s5_b_re, 's5_b_im': s5_b_im, 's5_c_re': s5_c_re, 's5_c_im': s5_c_im,
            's5_d': s5_d, 's5_w_glu': s5_w_glu, 's5_b_glu': s5_b_glu, 'ret_log_decay': ret_log_decay,
            'odd_w_in': odd_w_in, 'odd_w_out': odd_w_out, 'attn_sink': attn_sink}


def reference(x, ln_g, ln_b, mlp_w1, mlp_w2, even_w_in, even_w_out,
              s5_lambda_re, s5_lambda_im, s5_log_step, s5_b_re, s5_b_im, s5_c_re, s5_c_im,
              s5_d, s5_w_glu, s5_b_glu, ret_log_decay, odd_w_in, odd_w_out, attn_sink):
    for layer in range(DEPTH):
        if layer % 2 == 0:
            e = layer // 2
            mix = _even_mixer(x, even_w_in[e], even_w_out[e], s5_lambda_re[e], s5_lambda_im[e],
                              s5_log_step[e], s5_b_re[e], s5_b_im[e], s5_c_re[e], s5_c_im[e],
                              s5_d[e], s5_w_glu[e], s5_b_glu[e], ret_log_decay[e])
        else:
            o = layer // 2
            mix = _odd_mixer(x, odd_w_in[o], odd_w_out[o], attn_sink[o])
        x = _layer_norm(DEEPNORM_ALPHA * x + mix, ln_g[layer, 0], ln_b[layer, 0])
        x = _layer_norm(DEEPNORM_ALPHA * x + _sq_relu_mlp(x, mlp_w1[layer], mlp_w2[layer]),
                        ln_g[layer, 1], ln_b[layer, 1])
    return x
```

```python
import functools
import math

import jax
import jax.numpy as jnp
from jax import lax
from jax.experimental import pallas as pl
from jax.experimental.pallas import tpu as pltpu

F32 = jnp.float32
BF16 = jnp.bfloat16

D_MODEL = 2048
DEPTH = 2
S5_WIDTH = 1024
S5_GROUP = 16
S5_GROUPS = 64
S5_STATE = 64
RET_WIDTH = 1024
RET_HEADS = 4
RET_HEAD_DIM = 256
RET_ROPE_THETA = 10000.0
ATT_HEADS = 16
ATT_KV_HEADS = 4
ATT_HEAD_DIM = 128
ATT_GROUP = 4
ATT_WINDOW = 128
ATT_BLOCK = 128
ROPE_THETA = 500000.0
ROPE_DIM = 32
D_FF = 4 * D_MODEL
DEEPNORM_ALPHA = (2 * DEPTH) ** 0.25
LN_EPS = 1e-5
HEAD_NORM_EPS = 1e-6
NEG_INF = -1e30

V7X_VMEM_BYTES = 64 * 1024 * 1024
VMEM_LIMIT = V7X_VMEM_BYTES - 8 * 1024 * 1024

S5_T = 32
S5_TL = S5_T * S5_GROUP
S5_GB = 8
RET_C = 256
LN_ROWS = 256


def _params(sem):
    return pltpu.CompilerParams(dimension_semantics=sem, vmem_limit_bytes=VMEM_LIMIT)


def _layer_norm_rows(y, g, b):
    mu = jnp.mean(y, axis=-1, keepdims=True)
    yc = y - mu
    var = jnp.mean(yc * yc, axis=-1, keepdims=True)
    return yc * lax.rsqrt(var + LN_EPS) * g + b


def _inproj_even_body(x_ref, w_ref, cos_ref, sin_ref, o_ref, xb_ref, acc_ref):
    j = pl.program_id(1)

    @pl.when(j == 0)
    def _():
        xb_ref[...] = x_ref[...].astype(BF16)

    acc_ref[...] = jnp.dot(xb_ref[...], w_ref[...], preferred_element_type=F32)

    @pl.when((j == 0) | (j == 3))
    def _():
        o_ref[...] = acc_ref[...].astype(BF16)

    def rotary(scale):
        cos = cos_ref[...]
        sin = sin_ref[...]
        half = RET_HEAD_DIM // 2
        for hh in range(RET_HEADS):
            lo = hh * RET_HEAD_DIM
            a = acc_ref[:, lo:lo + half]
            b = acc_ref[:, lo + half:lo + RET_HEAD_DIM]
            o_ref[:, lo:lo + half] = ((a * cos - b * sin) * scale).astype(BF16)
            o_ref[:, lo + half:lo + RET_HEAD_DIM] = ((b * cos + a * sin) * scale).astype(BF16)

    @pl.when(j == 1)
    def _():
        rotary(1.0)

    @pl.when(j == 2)
    def _():
        rotary(RET_HEAD_DIM ** -0.5)

    @pl.when(j == 4)
    def _():
        g = acc_ref[...]
        o_ref[...] = (g * jax.nn.sigmoid(g)).astype(BF16)


def _inproj_even(x, w_in_bf, cos, sin, tm=1024):
    L = x.shape[0]
    tn = S5_WIDTH
    nseg = w_in_bf.shape[1] // tn
    return pl.pallas_call(
        _inproj_even_body,
        grid=(L // tm, nseg),
        in_specs=[
            pl.BlockSpec((tm, D_MODEL), lambda i, j: (i, 0)),
            pl.BlockSpec((D_MODEL, tn), lambda i, j: (0, j)),
            pl.BlockSpec((tm, RET_HEAD_DIM // 2), lambda i, j: (i, 0)),
            pl.BlockSpec((tm, RET_HEAD_DIM // 2), lambda i, j: (i, 0)),
        ],
        out_specs=pl.BlockSpec((None, tm, tn), lambda i, j: (j, i, 0)),
        out_shape=jax.ShapeDtypeStruct((nseg, L, tn), BF16),
        scratch_shapes=[pltpu.VMEM((tm, D_MODEL), BF16), pltpu.VMEM((tm, tn), F32)],
        compiler_params=_params(("parallel", "arbitrary")),
        name="inproj_even",
    )(x, w_in_bf, cos, sin)


def _s5_weights(lam_re, lam_im, log_step, b_re, b_im, c_re, c_im):
    T = S5_T
    hi = lax.Precision.HIGHEST
    lr = jnp.minimum(lam_re.astype(F32), -1e-4)
    li = lam_im.astype(F32)
    step = jnp.exp(log_step.astype(F32))[..., None]
    mag = jnp.exp(lr * step)
    ar = mag * jnp.cos(li * step)
    ai = mag * jnp.sin(li * step)
    nr, ni = ar - 1.0, ai
    den = lr * lr + li * li
    zr = (nr * lr + ni * li) / den
    zi = (ni * lr - nr * li) / den
    b_re = b_re.astype(F32)
    b_im = b_im.astype(F32)
    bbr = zr[..., None] * b_re - zi[..., None] * b_im
    bbi = zr[..., None] * b_im + zi[..., None] * b_re
    e = jnp.arange(T + 1, dtype=F32)
    pw_mag = jnp.exp(e * (lr * step)[..., None])
    pw_r = pw_mag * jnp.cos(e * (li * step)[..., None])
    pw_i = pw_mag * jnp.sin(e * (li * step)[..., None])
    cr = c_re.astype(F32)
    ci = c_im.astype(F32)
    car = cr[..., None] * pw_r[:, :, None] - ci[..., None] * pw_i[:, :, None]
    cai = cr[..., None] * pw_i[:, :, None] + ci[..., None] * pw_r[:, :, None]
    kk = (jnp.einsum('dgxpe,dgpc->dgexc', car, bbr, precision=hi)
          - jnp.einsum('dgxpe,dgpc->dgexc', cai, bbi, precision=hi))
    kf, kb = kk[0], kk[1]
    s_idx = jnp.arange(T)[:, None]
    t_idx = jnp.arange(T)[None, :]
    d_f = t_idx - s_idx
    mf = jnp.where((d_f >= 0)[None, :, :, None, None], kf[:, jnp.maximum(d_f, 0)], 0.0)
    mb = jnp.where((d_f <= 0)[None, :, :, None, None], kb[:, jnp.maximum(-d_f, 0)], 0.0)
    m = (mf + mb).transpose(0, 1, 4, 2, 3).reshape(S5_GROUPS, S5_TL, S5_TL)

    def ba(d, ex):
        pr = pw_r[d][:, :, ex]
        pi = pw_i[d][:, :, ex]
        re = pr[..., None] * bbr[d][:, :, None, :] - pi[..., None] * bbi[d][:, :, None, :]
        im = pr[..., None] * bbi[d][:, :, None, :] + pi[..., None] * bbr[d][:, :, None, :]
        return re.reshape(S5_GROUPS, S5_STATE, S5_TL), im.reshape(S5_GROUPS, S5_STATE, S5_TL)

    fre, fim = ba(0, T - 1 - jnp.arange(T))
    bre, bim = ba(1, jnp.arange(T))
    ws_t = jnp.concatenate([fre, fim, fim, fre, bre, bim, bim, bre], axis=1)

    def ca(d, ex):
        re = car[d][..., ex]
        im = cai[d][..., ex]
        re = re.transpose(0, 2, 3, 1).reshape(S5_GROUPS, S5_STATE, S5_TL)
        im = im.transpose(0, 2, 3, 1).reshape(S5_GROUPS, S5_STATE, S5_TL)
        return re, -im

    wfr, wfi = ca(0, jnp.arange(T) + 1)
    wbr, wbi = ca(1, T - jnp.arange(T))
    wc = jnp.concatenate([wfr, wfi, wbr, wbi], axis=1)

    ar_c = pw_r[..., T]
    ai_c = pw_i[..., T]
    a1 = jnp.concatenate([ar_c, ar_c], axis=-1)
    a2 = jnp.concatenate([-ai_c, ai_c], axis=-1)
    a3 = jnp.concatenate([ai_c, -ai_c], axis=-1)
    return m.astype(BF16), ws_t.astype(BF16), wc.astype(BF16), a1, a2, a3


def _s5_body(u_ref, m_ref, ws_ref, wc_ref, dt_ref, a1_ref, a2_ref, a3_ref, y_ref, s_ref, h_ref):
    nc = u_ref.shape[1]
    gb = u_ref.shape[0]
    w = gb * 128
    for gi in range(gb):
        s = lax.dot_general(u_ref[gi], ws_ref[gi], (((1,), (1,)), ((), ())),
                            preferred_element_type=F32)
        for r in range(4):
            s_ref[:, r * w + gi * 128:r * w + (gi + 1) * 128] = s[:, r * 128:(r + 1) * 128]

    a1f, a2f, a3f = a1_ref[0], a2_ref[0], a3_ref[0]
    a1b, a2b, a3b = a1_ref[1], a2_ref[1], a3_ref[1]

    def step(n, carry):
        hf, gf, hb, gb_ = carry
        m = nc - 1 - n
        h_ref[pl.ds(n, 1), 0:w] = hf
        h_ref[pl.ds(m, 1), w:2 * w] = hb
        sfh = s_ref[pl.ds(n, 1), 0:w]
        sfg = s_ref[pl.ds(n, 1), w:2 * w]
        sbh = s_ref[pl.ds(m, 1), 2 * w:3 * w]
        sbg = s_ref[pl.ds(m, 1), 3 * w:4 * w]
        hf2 = a1f * hf + a2f * gf + sfh
        gf2 = a1f * gf + a3f * hf + sfg
        hb2 = a1b * hb + a2b * gb_ + sbh
        gb2 = a1b * gb_ + a3b * hb + sbg
        return hf2, gf2, hb2, gb2

    z = jnp.zeros((1, w), F32)
    lax.fori_loop(0, nc, step, (z, z, z, z))

    for gi in range(gb):
        u = u_ref[gi]
        hcat = jnp.concatenate([h_ref[:, gi * 128:(gi + 1) * 128],
                                h_ref[:, w + gi * 128:w + (gi + 1) * 128]], axis=1).astype(BF16)
        y = (jnp.dot(u, m_ref[gi], preferred_element_type=F32)
             + jnp.dot(hcat, wc_ref[gi], preferred_element_type=F32)
             + dt_ref[gi] * u.astype(F32))
        y_ref[gi] = jax.nn.gelu(y).astype(BF16)


def _s5_mix(u_g, m, ws_t, wc, d_tile, a1, a2, a3):
    G, nc, TL = u_g.shape
    gb = S5_GB
    w = gb * 128
    a_spec = pl.BlockSpec((2, 1, w), lambda i: (0, 0, i))
    return pl.pallas_call(
        _s5_body,
        grid=(G // gb,),
        in_specs=[
            pl.BlockSpec((gb, nc, TL), lambda i: (i, 0, 0)),
            pl.BlockSpec((gb, TL, TL), lambda i: (i, 0, 0)),
            pl.BlockSpec((gb, 512, TL), lambda i: (i, 0, 0)),
            pl.BlockSpec((gb, 256, TL), lambda i: (i, 0, 0)),
            pl.BlockSpec((gb, 1, TL), lambda i: (i, 0, 0)),
            a_spec, a_spec, a_spec,
        ],
        out_specs=pl.BlockSpec((gb, nc, TL), lambda i: (i, 0, 0)),
        out_shape=jax.ShapeDtypeStruct((G, nc, TL), BF16),
        scratch_shapes=[pltpu.VMEM((nc, 4 * w), F32), pltpu.VMEM((nc, 2 * w), F32)],
        compiler_params=_params(("parallel",)),
        name="s5_mix",
    )(u_g, m, ws_t, wc, d_tile, a1, a2, a3)


def _row_index(n):
    return lax.broadcasted_iota(jnp.int32, (n, 1), 0).astype(F32)


def _log_decay(lg_ref, d, h):
    return -jnp.abs(jnp.full((1, 1), lg_ref[d, h], F32))


SEG_U, SEG_Q, SEG_K, SEG_V, SEG_GATE = range(5)


def _ret_bstate_body(lg_ref, k_ref, v_ref, sb_ref, st_ref):
    C = k_ref.shape[0]

    @pl.when(pl.program_id(0) == 0)
    def _():
        st_ref[...] = jnp.zeros_like(st_ref)

    jj = _row_index(C)
    for h in range(RET_HEADS):
        lo = h * RET_HEAD_DIM
        lgb = _log_decay(lg_ref, 1, h)
        sb_ref[h] = st_ref[h].astype(BF16)
        kd = (k_ref[:, lo:lo + RET_HEAD_DIM].astype(F32) * jnp.exp(jj * lgb)).astype(BF16)
        kv = lax.dot_general(kd, v_ref[:, lo:lo + RET_HEAD_DIM], (((0,), (0,)), ((), ())),
                             preferred_element_type=F32)
        st_ref[h] = st_ref[h] * jnp.exp(C * lgb) + kv


def _ret_bstate(lg, h5):
    L = h5.shape[1]
    C = RET_C
    nc = L // C
    return pl.pallas_call(
        _ret_bstate_body,
        grid=(nc,),
        in_specs=[
            pl.BlockSpec(memory_space=pltpu.SMEM),
            pl.BlockSpec((None, C, RET_WIDTH), lambda i: (SEG_K, nc - 1 - i, 0)),
            pl.BlockSpec((None, C, RET_WIDTH), lambda i: (SEG_V, nc - 1 - i, 0)),
        ],
        out_specs=pl.BlockSpec((None, RET_HEADS, RET_HEAD_DIM, RET_HEAD_DIM),
                               lambda i: (nc - 1 - i, 0, 0, 0)),
        out_shape=jax.ShapeDtypeStruct((nc, RET_HEADS, RET_HEAD_DIM, RET_HEAD_DIM), BF16),
        scratch_shapes=[pltpu.VMEM((RET_HEADS, RET_HEAD_DIM, RET_HEAD_DIM), F32)],
        compiler_params=_params(("arbitrary",)),
        name="ret_bstate",
    )(lg, h5, h5)


def _ret_main_body(lg_ref, q_ref, k_ref, v_ref, g_ref, sb_ref, o_ref, st_ref):
    C = q_ref.shape[0]

    @pl.when(pl.program_id(0) == 0)
    def _():
        st_ref[...] = jnp.zeros_like(st_ref)

    ii = _row_index(C)
    diff = (lax.broadcasted_iota(jnp.int32, (C, C), 0)
            - lax.broadcasted_iota(jnp.int32, (C, C), 1)).astype(F32)
    for h in range(RET_HEADS):
        lo = h * RET_HEAD_DIM
        lgf = _log_decay(lg_ref, 0, h)
        lgb = _log_decay(lg_ref, 1, h)
        q = q_ref[:, lo:lo + RET_HEAD_DIM]
        k = k_ref[:, lo:lo + RET_HEAD_DIM]
        v = v_ref[:, lo:lo + RET_HEAD_DIM]
        qf = q.astype(F32)
        s = lax.dot_general(q, k, (((1,), (1,)), ((), ())), preferred_element_type=F32)
        decay = jnp.where(diff >= 0, jnp.exp(lgf * jnp.maximum(diff, 0.0)),
                          jnp.exp(lgb * jnp.maximum(-diff, 0.0)))
        o = jnp.dot((s * decay).astype(BF16), v, preferred_element_type=F32)
        qdf = (qf * jnp.exp((ii + 1.0) * lgf)).astype(BF16)
        o = o + jnp.dot(qdf, st_ref[h].astype(BF16), preferred_element_type=F32)
        qdb = (qf * jnp.exp((C - ii) * lgb)).astype(BF16)
        o = o + jnp.dot(qdb, sb_ref[h], preferred_element_type=F32)
        mu = jnp.mean(o, axis=-1, keepdims=True)
        oc = o - mu
        var = jnp.mean(oc * oc, axis=-1, keepdims=True)
        on = oc * lax.rsqrt(var + HEAD_NORM_EPS)
        o_ref[:, lo:lo + RET_HEAD_DIM] = (on * g_ref[:, lo:lo + RET_HEAD_DIM].astype(F32)).astype(BF16)
        kd = (k.astype(F32) * jnp.exp((C - 1.0 - ii) * lgf)).astype(BF16)
        kv = lax.dot_general(kd, v, (((0,), (0,)), ((), ())), preferred_element_type=F32)
        st_ref[h] = st_ref[h] * jnp.exp(C * lgf) + kv


def _ret_main(lg, h5, sb):
    L = h5.shape[1]
    C = RET_C
    nc = L // C
    seg = lambda s: pl.BlockSpec((None, C, RET_WIDTH), lambda i: (s, i, 0))
    return pl.pallas_call(
        _ret_main_body,
        grid=(nc,),
        in_specs=[
            pl.BlockSpec(memory_space=pltpu.SMEM),
            seg(SEG_Q), seg(SEG_K), seg(SEG_V), seg(SEG_GATE),
            pl.BlockSpec((None, RET_HEADS, RET_HEAD_DIM, RET_HEAD_DIM), lambda i: (i, 0, 0, 0)),
        ],
        out_specs=pl.BlockSpec((C, RET_WIDTH), lambda i: (i, 0)),
        out_shape=jax.ShapeDtypeStruct((L, RET_WIDTH), BF16),
        scratch_shapes=[pltpu.VMEM((RET_HEADS, RET_HEAD_DIM, RET_HEAD_DIM), F32)],
        compiler_params=_params(("arbitrary",)),
        name="ret_main",
    )(lg, h5, h5, h5, h5, sb)


def _outproj_even_body(y_ref, r_ref, x_ref, wg_ref, bg_ref, wo_ref, g_ref, b_ref, o_ref):
    y = y_ref[...]
    z = jnp.dot(y, wg_ref[...], preferred_element_type=F32) + bg_ref[...]
    s5 = (y.astype(F32) * jax.nn.sigmoid(z)).astype(BF16)
    mix = (jnp.dot(s5, wo_ref[0:S5_WIDTH, :], preferred_element_type=F32)
           + jnp.dot(r_ref[...], wo_ref[S5_WIDTH:S5_WIDTH + RET_WIDTH, :], preferred_element_type=F32))
    o_ref[...] = _layer_norm_rows(DEEPNORM_ALPHA * x_ref[...] + mix, g_ref[...], b_ref[...])


def _outproj_even(y, ret, x, w_glu_bf, b_glu, w_out_bf, ln_g, ln_b, tm=256):
    L = x.shape[0]
    row = lambda n: pl.BlockSpec((tm, n), lambda i: (i, 0))
    full = lambda a: pl.BlockSpec(a.shape, lambda i: (0,) * a.ndim)
    return pl.pallas_call(
        _outproj_even_body,
        grid=(L // tm,),
        in_specs=[row(S5_WIDTH), row(RET_WIDTH), row(D_MODEL), full(w_glu_bf), full(b_glu),
                  full(w_out_bf), full(ln_g), full(ln_b)],
        out_specs=row(D_MODEL),
        out_shape=jax.ShapeDtypeStruct((L, D_MODEL), F32),
        compiler_params=_params(("parallel",)),
        name="outproj_even",
    )(y, ret, x, w_glu_bf, b_glu, w_out_bf, ln_g, ln_b)


def _inproj_odd_body(x_ref, w_ref, c_ref, s1_ref, s2_ref, o_ref, xb_ref):
    j = pl.program_id(1)

    @pl.when(j == 0)
    def _():
        xb_ref[...] = x_ref[...].astype(BF16)

    acc = jnp.dot(xb_ref[...], w_ref[...], preferred_element_type=F32)
    tn = acc.shape[1]
    nq = (ATT_HEADS * ATT_HEAD_DIM) // tn
    nk = (ATT_KV_HEADS * ATT_HEAD_DIM) // tn if tn <= ATT_KV_HEADS * ATT_HEAD_DIM else 1

    def rotary(scale):
        cc = c_ref[...]
        s1 = s1_ref[...]
        s2 = s2_ref[...]
        for hh in range(tn // ATT_HEAD_DIM):
            lo = hh * ATT_HEAD_DIM
            a = acc[:, lo:lo + ATT_HEAD_DIM]
            up = pltpu.roll(a, ATT_HEAD_DIM - ROPE_DIM // 2, axis=1)
            dn = pltpu.roll(a, ROPE_DIM // 2, axis=1)
            o_ref[:, lo:lo + ATT_HEAD_DIM] = ((a * cc + up * s1 + dn * s2) * scale).astype(BF16)

    @pl.when(j < nq)
    def _():
        rotary(ATT_HEAD_DIM ** -0.5)

    @pl.when((j >= nq) & (j < nq + nk))
    def _():
        rotary(1.0)

    @pl.when(j >= nq + nk)
    def _():
        o_ref[...] = acc.astype(BF16)


def _inproj_odd(x, w_in_bf, cc, s1, s2, tm=1024, tn=512):
    L = x.shape[0]
    n_out = w_in_bf.shape[1]
    tab = pl.BlockSpec((tm, ATT_HEAD_DIM), lambda i, j: (i, 0))
    return pl.pallas_call(
        _inproj_odd_body,
        grid=(L // tm, n_out // tn),
        in_specs=[
            pl.BlockSpec((tm, D_MODEL), lambda i, j: (i, 0)),
            pl.BlockSpec((D_MODEL, tn), lambda i, j: (0, j)),
            tab, tab, tab,
        ],
        out_specs=pl.BlockSpec((tm, tn), lambda i, j: (i, j)),
        out_shape=jax.ShapeDtypeStruct((L, n_out), BF16),
        scratch_shapes=[pltpu.VMEM((tm, D_MODEL), BF16)],
        compiler_params=_params(("parallel", "arbitrary")),
        name="inproj_odd",
    )(x, w_in_bf, cc, s1, s2)


def _attn_body(sink_ref, q_ref, kp_ref, kc_ref, kn_ref, vp_ref, vc_ref, vn_ref, o_ref):
    c = pl.program_id(0)
    nb = pl.num_programs(0)
    B = ATT_BLOCK
    hd = ATT_HEAD_DIM
    rows = ATT_GROUP * B
    r_i = lax.broadcasted_iota(jnp.int32, (rows, 3 * B), 0)
    s_i = lax.broadcasted_iota(jnp.int32, (rows, 3 * B), 1)
    rel = (r_i & (B - 1)) - s_i + B
    key_lo = jnp.where(c > 0, 0, B)
    key_hi = jnp.where(c < nb - 1, 3 * B, 2 * B)
    ok = (jnp.abs(rel) <= ATT_WINDOW) & (s_i >= key_lo) & (s_i < key_hi)
    head_of_row = lax.shift_right_logical(lax.broadcasted_iota(jnp.int32, (rows, 1), 0),
                                          int(math.log2(B)))
    for g in range(ATT_KV_HEADS):
        q = jnp.concatenate([q_ref[:, (g * ATT_GROUP + hh) * hd:(g * ATT_GROUP + hh + 1) * hd]
                             for hh in range(ATT_GROUP)], axis=0)
        ksl = slice(g * hd, (g + 1) * hd)
        k = jnp.concatenate([kp_ref[:, ksl], kc_ref[:, ksl], kn_ref[:, ksl]], axis=0)
        v = jnp.concatenate([vp_ref[:, ksl], vc_ref[:, ksl], vn_ref[:, ksl]], axis=0)
        s = lax.dot_general(q, k, (((1,), (1,)), ((), ())), preferred_element_type=F32)
        s = jnp.where(ok, s, NEG_INF)
        sink = jnp.zeros((rows, 1), F32)
        for hh in range(ATT_GROUP):
            sink = jnp.where(head_of_row == hh, sink_ref[g * ATT_GROUP + hh], sink)
        m = jnp.maximum(jnp.max(s, axis=-1, keepdims=True), sink)
        p = jnp.exp(s - m)
        den = jnp.sum(p, axis=-1, keepdims=True) + jnp.exp(sink - m)
        o = jnp.dot(p.astype(BF16), v, preferred_element_type=F32) / den
        for hh in range(ATT_GROUP):
            hcol = (g * ATT_GROUP + hh) * hd
            o_ref[:, hcol:hcol + hd] = o[hh * B:(hh + 1) * B, :].astype(BF16)


def _attention(sink, qkv):
    L = qkv.shape[0]
    B = ATT_BLOCK
    nb = L // B
    kvw = ATT_KV_HEADS * ATT_HEAD_DIM
    qw = ATT_HEADS * ATT_HEAD_DIM
    kcol = qw // kvw
    vcol = kcol + 1

    def kv_spec(col, off):
        return pl.BlockSpec((B, kvw), lambda i: (jnp.clip(i + off, 0, nb - 1), col))

    return pl.pallas_call(
        _attn_body,
        grid=(nb,),
        in_specs=[
            pl.BlockSpec(memory_space=pltpu.SMEM),
            pl.BlockSpec((B, qw), lambda i: (i, 0)),
            kv_spec(kcol, -1), kv_spec(kcol, 0), kv_spec(kcol, 1),
            kv_spec(vcol, -1), kv_spec(vcol, 0), kv_spec(vcol, 1),
        ],
        out_specs=pl.BlockSpec((B, qw), lambda i: (i, 0)),
        out_shape=jax.ShapeDtypeStruct((L, qw), BF16),
        compiler_params=_params(("parallel",)),
        name="attention",
    )(sink, qkv, qkv, qkv, qkv, qkv, qkv, qkv)


def _outproj_odd_body(a_ref, x_ref, wo_ref, g_ref, b_ref, o_ref):
    mix = jnp.dot(a_ref[...], wo_ref[...], preferred_element_type=F32)
    o_ref[...] = _layer_norm_rows(DEEPNORM_ALPHA * x_ref[...] + mix, g_ref[...], b_ref[...])


def _outproj_odd(a, x, w_out_bf, ln_g, ln_b, tm=256):
    L = x.shape[0]
    row = lambda n: pl.BlockSpec((tm, n), lambda i: (i, 0))
    full = lambda t: pl.BlockSpec(t.shape, lambda i: (0,) * t.ndim)
    return pl.pallas_call(
        _outproj_odd_body,
        grid=(L // tm,),
        in_specs=[row(a.shape[1]), row(D_MODEL), full(w_out_bf), full(ln_g), full(ln_b)],
        out_specs=row(D_MODEL),
        out_shape=jax.ShapeDtypeStruct((L, D_MODEL), F32),
        compiler_params=_params(("parallel",)),
        name="outproj_odd",
    )(a, x, w_out_bf, ln_g, ln_b)


def _mlp_body(x_ref, w1_ref, w2_ref, g_ref, b_ref, o_ref, xb_ref, acc_ref):
    f = pl.program_id(1)

    @pl.when(f == 0)
    def _():
        xb_ref[...] = x_ref[...].astype(BF16)
        acc_ref[...] = jnp.zeros_like(acc_ref)

    h = jnp.dot(xb_ref[...], w1_ref[...], preferred_element_type=F32)
    h = jnp.square(jnp.maximum(h, 0.0)).astype(BF16)
    acc_ref[...] += jnp.dot(h, w2_ref[...], preferred_element_type=F32)

    @pl.when(f == pl.num_programs(1) - 1)
    def _():
        tm = x_ref.shape[0]
        for r in range(0, tm, LN_ROWS):
            y = DEEPNORM_ALPHA * x_ref[r:r + LN_ROWS, :] + acc_ref[r:r + LN_ROWS, :]
            o_ref[r:r + LN_ROWS, :] = _layer_norm_rows(y, g_ref[...], b_ref[...])


def _mlp(x, w1_bf, w2_bf, ln_g, ln_b, tm=512, tf=1024):
    L = x.shape[0]
    return pl.pallas_call(
        _mlp_body,
        grid=(L // tm, D_FF // tf),
        in_specs=[
            pl.BlockSpec((tm, D_MODEL), lambda i, f: (i, 0)),
            pl.BlockSpec((D_MODEL, tf), lambda i, f: (0, f)),
            pl.BlockSpec((tf, D_MODEL), lambda i, f: (f, 0)),
            pl.BlockSpec((1, D_MODEL), lambda i, f: (0, 0)),
            pl.BlockSpec((1, D_MODEL), lambda i, f: (0, 0)),
        ],
        out_specs=pl.BlockSpec((tm, D_MODEL), lambda i, f: (i, 0)),
        out_shape=jax.ShapeDtypeStruct((L, D_MODEL), F32),
        scratch_shapes=[pltpu.VMEM((tm, D_MODEL), BF16), pltpu.VMEM((tm, D_MODEL), F32)],
        compiler_params=_params(("parallel", "arbitrary")),
        name="mlp",
    )(x, w1_bf, w2_bf, ln_g, ln_b)


def _rotary_tables(L, rot_dim, theta):
    half = rot_dim // 2
    inv_freq = 1.0 / (theta ** (jnp.arange(half, dtype=F32) / half))
    ang = jnp.arange(L).astype(F32)[:, None] * inv_freq[None, :]
    return jnp.cos(ang), jnp.sin(ang)


def _even_layer(x, w_in, w_out, lam_re, lam_im, log_step, b_re, b_im, c_re, c_im,
                d_skip, w_glu, b_glu, ret_log_decay, ln_g, ln_b):
    L = x.shape[0]
    cos, sin = _rotary_tables(L, RET_HEAD_DIM, RET_ROPE_THETA)
    h5 = _inproj_even(x, w_in.astype(BF16), cos, sin)
    nc = L // S5_T
    u_g = h5[0].reshape(nc, S5_T, S5_GROUPS, S5_GROUP).transpose(2, 0, 1, 3).reshape(S5_GROUPS, nc, S5_TL)
    m, ws_t, wc, a1, a2, a3 = _s5_weights(lam_re, lam_im, log_step, b_re, b_im, c_re, c_im)
    d_tile = jnp.tile(d_skip.astype(F32), (1, S5_T)).reshape(S5_GROUPS, 1, S5_TL)
    a1, a2, a3 = (a.reshape(2, 1, S5_GROUPS * 128) for a in (a1, a2, a3))
    y_g = _s5_mix(u_g, m, ws_t, wc, d_tile, a1, a2, a3)
    y = y_g.reshape(S5_GROUPS, nc, S5_T, S5_GROUP).transpose(1, 2, 0, 3).reshape(L, S5_WIDTH)
    lg = ret_log_decay.astype(F32)
    sb = _ret_bstate(lg, h5)
    ret = _ret_main(lg, h5, sb)
    return _outproj_even(y, ret, x, w_glu.astype(BF16), b_glu.astype(F32).reshape(1, -1),
                         w_out.astype(BF16), ln_g.reshape(1, -1), ln_b.reshape(1, -1))


def _odd_layer(x, w_in, w_out, sink, ln_g, ln_b):
    L = x.shape[0]
    cos, sin = _rotary_tables(L, ROPE_DIM, ROPE_THETA)
    half = ROPE_DIM // 2
    pad = ATT_HEAD_DIM - ROPE_DIM
    cc = jnp.concatenate([cos, cos, jnp.ones((L, pad), F32)], axis=1)
    s1 = jnp.concatenate([-sin, jnp.zeros((L, ATT_HEAD_DIM - half), F32)], axis=1)
    s2 = jnp.concatenate([jnp.zeros((L, half), F32), sin, jnp.zeros((L, pad), F32)], axis=1)
    qkv = _inproj_odd(x, w_in.astype(BF16), cc, s1, s2)
    att = _attention(sink.astype(F32), qkv)
    return _outproj_odd(att, x, w_out.astype(BF16), ln_g.reshape(1, -1), ln_b.reshape(1, -1))


def kernel(x, ln_g, ln_b, mlp_w1, mlp_w2, even_w_in, even_w_out, s5_lambda_re, s5_lambda_im, s5_log_step, s5_b_re, s5_b_im, s5_c_re, s5_c_im, s5_d, s5_w_glu, s5_b_glu, ret_log_decay, odd_w_in, odd_w_out, attn_sink):
    bsz = x.shape[0]
    outs = []
    for b in range(bsz):
        xb = x[b]
        for layer in range(DEPTH):
            if layer % 2 == 0:
                e = layer // 2
                xb = _even_layer(xb, even_w_in[e], even_w_out[e], s5_lambda_re[e], s5_lambda_im[e],
                                 s5_log_step[e], s5_b_re[e], s5_b_im[e], s5_c_re[e], s5_c_im[e],
                                 s5_d[e], s5_w_glu[e], s5_b_glu[e], ret_log_decay[e],
                                 ln_g[layer, 0], ln_b[layer, 0])
            else:
                o = layer // 2
                xb = _odd_layer(xb, odd_w_in[o], odd_w_out[o], attn_sink[o], ln_g[layer, 0], ln_b[layer, 0])
            xb = _mlp(xb, mlp_w1[layer].astype(BF16), mlp_w2[layer].astype(BF16),
                      ln_g[layer, 1].reshape(1, -1), ln_b[layer, 1].reshape(1, -1))
        outs.append(xb)
    return jnp.stack(outs, axis=0)
```

```python
import functools
import math

import jax
import jax.numpy as jnp
from jax import lax
from jax.experimental import pallas as pl
from jax.experimental.pallas import tpu as pltpu

F32 = jnp.float32
BF16 = jnp.bfloat16

D_MODEL = 2048
DEPTH = 2
S5_WIDTH = 1024
S5_GROUP = 16
S5_GROUPS = 64
S5_STATE = 64
RET_WIDTH = 1024
RET_HEADS = 4
RET_HEAD_DIM = 256
RET_ROPE_THETA = 10000.0
ATT_HEADS = 16
ATT_KV_HEADS = 4
ATT_HEAD_DIM = 128
ATT_GROUP = 4
ATT_WINDOW = 128
ATT_BLOCK = 128
ROPE_THETA = 500000.0
ROPE_DIM = 32
D_FF = 4 * D_MODEL
DEEPNORM_ALPHA = (2 * DEPTH) ** 0.25
LN_EPS = 1e-5
HEAD_NORM_EPS = 1e-6
NEG_INF = -1e30

V7X_VMEM_BYTES = 64 * 1024 * 1024
VMEM_LIMIT = V7X_VMEM_BYTES - 8 * 1024 * 1024

S5_T = 32
S5_TL = S5_T * S5_GROUP
S5_GB = 8
S5_GEN_GB = 8
RET_C = 256
LN_ROWS = 256


def _params(sem):
    return pltpu.CompilerParams(dimension_semantics=sem, vmem_limit_bytes=VMEM_LIMIT)


def _layer_norm_rows(y, g, b):
    mu = jnp.mean(y, axis=-1, keepdims=True)
    yc = y - mu
    var = jnp.mean(yc * yc, axis=-1, keepdims=True)
    return yc * lax.rsqrt(var + LN_EPS) * g + b


def _inproj_even_body(x_ref, w_ref, cos_ref, sin_ref, o_ref, xb_ref, acc_ref):
    j = pl.program_id(1)

    @pl.when(j == 0)
    def _():
        xb_ref[...] = x_ref[...].astype(BF16)

    acc_ref[...] = jnp.dot(xb_ref[...], w_ref[...], preferred_element_type=F32)

    @pl.when((j == 0) | (j == 3))
    def _():
        o_ref[...] = acc_ref[...].astype(BF16)

    def rotary(scale):
        cos = cos_ref[...]
        sin = sin_ref[...]
        half = RET_HEAD_DIM // 2
        for hh in range(RET_HEADS):
            lo = hh * RET_HEAD_DIM
            a = acc_ref[:, lo:lo + half]
            b = acc_ref[:, lo + half:lo + RET_HEAD_DIM]
            o_ref[:, lo:lo + half] = ((a * cos - b * sin) * scale).astype(BF16)
            o_ref[:, lo + half:lo + RET_HEAD_DIM] = ((b * cos + a * sin) * scale).astype(BF16)

    @pl.when(j == 1)
    def _():
        rotary(1.0)

    @pl.when(j == 2)
    def _():
        rotary(RET_HEAD_DIM ** -0.5)

    @pl.when(j == 4)
    def _():
        g = acc_ref[...]
        o_ref[...] = (g * jax.nn.sigmoid(g)).astype(BF16)


def _inproj_even(x, w_in_bf, cos, sin, tm=1024):
    L = x.shape[0]
    tn = S5_WIDTH
    nseg = w_in_bf.shape[1] // tn
    return pl.pallas_call(
        _inproj_even_body,
        grid=(L // tm, nseg),
        in_specs=[
            pl.BlockSpec((tm, D_MODEL), lambda i, j: (i, 0)),
            pl.BlockSpec((D_MODEL, tn), lambda i, j: (0, j)),
            pl.BlockSpec((tm, RET_HEAD_DIM // 2), lambda i, j: (i, 0)),
            pl.BlockSpec((tm, RET_HEAD_DIM // 2), lambda i, j: (i, 0)),
        ],
        out_specs=pl.BlockSpec((None, tm, tn), lambda i, j: (j, i, 0)),
        out_shape=jax.ShapeDtypeStruct((nseg, L, tn), BF16),
        scratch_shapes=[pltpu.VMEM((tm, D_MODEL), BF16), pltpu.VMEM((tm, tn), F32)],
        compiler_params=_params(("parallel", "arbitrary")),
        name="inproj_even",
    )(x, w_in_bf, cos, sin)


def _cmul(ar, ai, br, bi):
    return ar * br - ai * bi, ar * bi + ai * br


(DISC_A1_RE, DISC_A1_IM, DISC_A2_RE, DISC_A2_IM, DISC_A4_RE, DISC_A4_IM, DISC_A8_RE, DISC_A8_IM,
 DISC_A16_RE, DISC_A16_IM, DISC_AT_RE, DISC_AT_IM, DISC_Z_RE, DISC_Z_IM, DISC_SCAN_A2, DISC_SCAN_A3) = range(16)


def _s5_disc_body(lr_ref, li_ref, ls_ref, sg_ref, o_ref):
    lr = jnp.minimum(lr_ref[...], -1e-4)
    li = li_ref[...]
    step = jnp.exp(ls_ref[...])
    mag = jnp.exp(lr * step)
    ar = mag * jnp.cos(li * step)
    ai = mag * jnp.sin(li * step)
    nr, ni = ar - 1.0, ai
    den = lr * lr + li * li
    o_ref[DISC_Z_RE] = (nr * lr + ni * li) / den
    o_ref[DISC_Z_IM] = (ni * lr - nr * li) / den
    pr, pi = ar, ai
    for k in range(6):
        o_ref[2 * k] = pr
        o_ref[2 * k + 1] = pi
        if k < 5:
            pr, pi = _cmul(pr, pi, pr, pi)
    o_ref[DISC_SCAN_A2] = pi * sg_ref[...]
    o_ref[DISC_SCAN_A3] = -pi * sg_ref[...]


def _s5_disc(lam_re, lam_im, log_step):
    assert S5_T == 32
    rows = 2 * S5_GROUPS
    two = lambda a: jnp.tile(a.astype(F32).reshape(rows, -1), (1, 2))
    lr = two(lam_re)
    li = two(lam_im)
    ls = jnp.broadcast_to(log_step.astype(F32).reshape(rows, 1), (rows, 2 * S5_STATE))
    sg = jnp.broadcast_to(jnp.concatenate([-jnp.ones((S5_STATE,), F32), jnp.ones((S5_STATE,), F32)])[None],
                          (rows, 2 * S5_STATE))
    return pl.pallas_call(
        _s5_disc_body,
        out_shape=jax.ShapeDtypeStruct((16, rows, 2 * S5_STATE), F32),
        name="s5_disc",
    )(lr, li, ls, sg)


def _s5_gen_body(col_ref, zrow_ref, bt_ref, btile_ref, ctile_ref, m_ref, ws_ref, wc_ref):
    P = S5_STATE
    hi = lax.Precision.HIGHEST
    tlo = lax.shift_right_logical(lax.broadcasted_iota(jnp.int32, (P, 128), 1), 4)
    lane = lax.broadcasted_iota(jnp.int32, (S5_GROUP, S5_TL), 1)
    ones = jnp.ones((P, 128), F32)
    zeros = jnp.zeros((P, 128), F32)

    def one_group(gi, carry):
        kt = []
        for d in range(2):
            col = col_ref[gi, d]
            c = lambda k: jnp.broadcast_to(col[:, k:k + 1], (P, 128))
            a1 = (c(DISC_A1_RE), c(DISC_A1_IM))
            a2 = (c(DISC_A2_RE), c(DISC_A2_IM))
            a4 = (c(DISC_A4_RE), c(DISC_A4_IM))
            a8 = (c(DISC_A8_RE), c(DISC_A8_IM))
            a16 = (c(DISC_A16_RE), c(DISC_A16_IM))
            blk = [None, a8, a16, _cmul(*a8, *a16)]

            def low_powers(reverse):
                xr, xi = ones, zeros
                for k, ak in enumerate((a1, a2, a4)):
                    bit = (lax.shift_right_logical(tlo, k) & 1) == (0 if reverse else 1)
                    yr, yi = _cmul(xr, xi, *ak)
                    xr = jnp.where(bit, yr, xr)
                    xi = jnp.where(bit, yi, xi)
                return xr, xi

            def expand(base, reverse):
                out = []
                for j in range(4):
                    f = blk[3 - j] if reverse else blk[j]
                    out.append(base if f is None else _cmul(*base, *f))
                return out

            ct = (ctile_ref[gi, d, 0], ctile_ref[gi, d, 1])
            bbar = _cmul(c(DISC_Z_RE), c(DISC_Z_IM), btile_ref[gi, d, 0], btile_ref[gi, d, 1])
            zrow = zrow_ref[gi, d]
            zr_row, zi_row = zrow[0:1, :], zrow[1:2, :]
            bbt_r, bbt_i = _cmul(zr_row, zi_row, bt_ref[gi, d, 0], bt_ref[gi, d, 1])
            ca = expand(_cmul(*ct, *low_powers(d == 1)), d == 1)
            ba = expand(_cmul(*bbar, *low_powers(d == 0)), d == 0)
            wcj = [_cmul(*x, *a1) for x in ca]
            cat = lambda parts, k: jnp.concatenate([x[k] for x in parts], axis=1)
            kt.append(jnp.dot(bbt_r, cat(ca, 0), precision=hi, preferred_element_type=F32)
                      - jnp.dot(bbt_i, cat(ca, 1), precision=hi, preferred_element_type=F32))
            ba_r, ba_i = cat(ba, 0).astype(BF16), cat(ba, 1).astype(BF16)
            for r, part in enumerate((ba_r, ba_i, ba_i, ba_r)):
                ws_ref[gi, (4 * d + r) * P:(4 * d + r + 1) * P, :] = part
            wc_ref[gi, 2 * d * P:(2 * d + 1) * P, :] = cat(wcj, 0).astype(BF16)
            wc_ref[gi, (2 * d + 1) * P:(2 * d + 2) * P, :] = (-cat(wcj, 1)).astype(BF16)
        ktf, ktb = kt
        for s in range(S5_T):
            lo, hi_lane = S5_GROUP * s, S5_GROUP * (s + 1)
            f = ktf if s == 0 else jnp.where(lane >= lo, pltpu.roll(ktf, lo, axis=1), 0.0)
            b = ktb if s == S5_T - 1 else jnp.where(lane < hi_lane, pltpu.roll(ktb, hi_lane, axis=1), 0.0)
            m_ref[gi, lo:hi_lane, :] = (f + b).astype(BF16)
        return carry

    lax.fori_loop(0, col_ref.shape[0], one_group, 0)


def _s5_gen(disc, b_re, b_im, c_re, c_im):
    G, P, Cg = S5_GROUPS, S5_STATE, S5_GROUP
    gb = S5_GEN_GB
    d4 = disc[:, :, :P].reshape(16, 2, G, P)
    col = d4.transpose(2, 1, 3, 0)
    zrow = d4[DISC_Z_RE:DISC_Z_IM + 1].transpose(2, 1, 0, 3)
    b = jnp.stack([b_re, b_im], axis=1).astype(F32)
    c = jnp.stack([c_re, c_im], axis=1).astype(F32)
    bt = b.transpose(2, 0, 1, 4, 3)
    btile = jnp.tile(b.transpose(2, 0, 1, 3, 4), (1, 1, 1, 1, 128 // Cg))
    ctile = jnp.tile(c.transpose(2, 0, 1, 4, 3), (1, 1, 1, 1, 128 // Cg))
    spec = lambda a: pl.BlockSpec((gb,) + a.shape[1:], lambda i: (i,) + (0,) * (a.ndim - 1))
    out = lambda rows: pl.BlockSpec((gb, rows, S5_TL), lambda i: (i, 0, 0))
    return pl.pallas_call(
        _s5_gen_body,
        grid=(G // gb,),
        in_specs=[spec(col), spec(zrow), spec(bt), spec(btile), spec(ctile)],
        out_specs=[out(S5_TL), out(8 * P), out(4 * P)],
        out_shape=[jax.ShapeDtypeStruct((G, S5_TL, S5_TL), BF16),
                   jax.ShapeDtypeStruct((G, 8 * P, S5_TL), BF16),
                   jax.ShapeDtypeStruct((G, 4 * P, S5_TL), BF16)],
        compiler_params=_params(("parallel",)),
        name="s5_gen",
    )(col, zrow, bt, btile, ctile)


def _s5_body(u_ref, m_ref, ws_ref, wc_ref, dt_ref, a1_ref, a2_ref, a3_ref, y_ref, s_ref, h_ref):
    nc = u_ref.shape[1]
    gb = u_ref.shape[0]
    w = gb * 128
    for gi in range(gb):
        s = lax.dot_general(u_ref[gi], ws_ref[gi], (((1,), (1,)), ((), ())),
                            preferred_element_type=F32)
        for r in range(4):
            s_ref[:, r * w + gi * 128:r * w + (gi + 1) * 128] = s[:, r * 128:(r + 1) * 128]

    a1f, a2f, a3f = a1_ref[0], a2_ref[0], a3_ref[0]
    a1b, a2b, a3b = a1_ref[1], a2_ref[1], a3_ref[1]

    def step(n, carry):
        hf, gf, hb, gb_ = carry
        m = nc - 1 - n
        h_ref[pl.ds(n, 1), 0:w] = hf
        h_ref[pl.ds(m, 1), w:2 * w] = hb
        sfh = s_ref[pl.ds(n, 1), 0:w]
        sfg = s_ref[pl.ds(n, 1), w:2 * w]
        sbh = s_ref[pl.ds(m, 1), 2 * w:3 * w]
        sbg = s_ref[pl.ds(m, 1), 3 * w:4 * w]
        hf2 = a1f * hf + a2f * gf + sfh
        gf2 = a1f * gf + a3f * hf + sfg
        hb2 = a1b * hb + a2b * gb_ + sbh
        gb2 = a1b * gb_ + a3b * hb + sbg
        return hf2, gf2, hb2, gb2

    z = jnp.zeros((1, w), F32)
    lax.fori_loop(0, nc, step, (z, z, z, z))

    for gi in range(gb):
        u = u_ref[gi]
        hcat = jnp.concatenate([h_ref[:, gi * 128:(gi + 1) * 128],
                                h_ref[:, w + gi * 128:w + (gi + 1) * 128]], axis=1).astype(BF16)
        y = (jnp.dot(u, m_ref[gi], preferred_element_type=F32)
             + jnp.dot(hcat, wc_ref[gi], preferred_element_type=F32)
             + dt_ref[gi] * u.astype(F32))
        y_ref[gi] = jax.nn.gelu(y).astype(BF16)


def _s5_mix(u_g, m, ws_t, wc, d_tile, a1, a2, a3):
    G, nc, TL = u_g.shape
    gb = S5_GB
    w = gb * 128
    a_spec = pl.BlockSpec((2, 1, w), lambda i: (0, 0, i))
    return pl.pallas_call(
        _s5_body,
        grid=(G // gb,),
        in_specs=[
            pl.BlockSpec((gb, nc, TL), lambda i: (i, 0, 0)),
            pl.BlockSpec((gb, TL, TL), lambda i: (i, 0, 0)),
            pl.BlockSpec((gb, 512, TL), lambda i: (i, 0, 0)),
            pl.BlockSpec((gb, 256, TL), lambda i: (i, 0, 0)),
            pl.BlockSpec((gb, 1, TL), lambda i: (i, 0, 0)),
            a_spec, a_spec, a_spec,
        ],
        out_specs=pl.BlockSpec((gb, nc, TL), lambda i: (i, 0, 0)),
        out_shape=jax.ShapeDtypeStruct((G, nc, TL), BF16),
        scratch_shapes=[pltpu.VMEM((nc, 4 * w), F32), pltpu.VMEM((nc, 2 * w), F32)],
        compiler_params=_params(("parallel",)),
        name="s5_mix",
    )(u_g, m, ws_t, wc, d_tile, a1, a2, a3)


def _row_index(n):
    return lax.broadcasted_iota(jnp.int32, (n, 1), 0).astype(F32)


def _log_decay(lg_ref, d, h):
    return -jnp.abs(jnp.full((1, 1), lg_ref[d, h], F32))


SEG_U, SEG_Q, SEG_K, SEG_V, SEG_GATE = range(5)


def _ret_bstate_body(lg_ref, k_ref, v_ref, sb_ref, st_ref):
    C = k_ref.shape[0]

    @pl.when(pl.program_id(0) == 0)
    def _():
        st_ref[...] = jnp.zeros_like(st_ref)

    jj = _row_index(C)
    for h in range(RET_HEADS):
        lo = h * RET_HEAD_DIM
        lgb = _log_decay(lg_ref, 1, h)
        sb_ref[h] = st_ref[h].astype(BF16)
        kd = (k_ref[:, lo:lo + RET_HEAD_DIM].astype(F32) * jnp.exp(jj * lgb)).astype(BF16)
        kv = lax.dot_general(kd, v_ref[:, lo:lo + RET_HEAD_DIM], (((0,), (0,)), ((), ())),
                             preferred_element_type=F32)
        st_ref[h] = st_ref[h] * jnp.exp(C * lgb) + kv


def _ret_bstate(lg, h5):
    L = h5.shape[1]
    C = RET_C
    nc = L // C
    return pl.pallas_call(
        _ret_bstate_body,
        grid=(nc,),
        in_specs=[
            pl.BlockSpec(memory_space=pltpu.SMEM),
            pl.BlockSpec((None, C, RET_WIDTH), lambda i: (SEG_K, nc - 1 - i, 0)),
            pl.BlockSpec((None, C, RET_WIDTH), lambda i: (SEG_V, nc - 1 - i, 0)),
        ],
        out_specs=pl.BlockSpec((None, RET_HEADS, RET_HEAD_DIM, RET_HEAD_DIM),
                               lambda i: (nc - 1 - i, 0, 0, 0)),
        out_shape=jax.ShapeDtypeStruct((nc, RET_HEADS, RET_HEAD_DIM, RET_HEAD_DIM), BF16),
        scratch_shapes=[pltpu.VMEM((RET_HEADS, RET_HEAD_DIM, RET_HEAD_DIM), F32)],
        compiler_params=_params(("arbitrary",)),
        name="ret_bstate",
    )(lg, h5, h5)


def _ret_main_body(lg_ref, q_ref, k_ref, v_ref, g_ref, sb_ref, o_ref, st_ref):
    C = q_ref.shape[0]

    @pl.when(pl.program_id(0) == 0)
    def _():
        st_ref[...] = jnp.zeros_like(st_ref)

    ii = _row_index(C)
    diff = (lax.broadcasted_iota(jnp.int32, (C, C), 0)
            - lax.broadcasted_iota(jnp.int32, (C, C), 1)).astype(F32)
    for h in range(RET_HEADS):
        lo = h * RET_HEAD_DIM
        lgf = _log_decay(lg_ref, 0, h)
        lgb = _log_decay(lg_ref, 1, h)
        q = q_ref[:, lo:lo + RET_HEAD_DIM]
        k = k_ref[:, lo:lo + RET_HEAD_DIM]
        v = v_ref[:, lo:lo + RET_HEAD_DIM]
        qf = q.astype(F32)
        s = lax.dot_general(q, k, (((1,), (1,)), ((), ())), preferred_element_type=F32)
        decay = jnp.where(diff >= 0, jnp.exp(lgf * jnp.maximum(diff, 0.0)),
                          jnp.exp(lgb * jnp.maximum(-diff, 0.0)))
        o = jnp.dot((s * decay).astype(BF16), v, preferred_element_type=F32)
        qdf = (qf * jnp.exp((ii + 1.0) * lgf)).astype(BF16)
        o = o + jnp.dot(qdf, st_ref[h].astype(BF16), preferred_element_type=F32)
        qdb = (qf * jnp.exp((C - ii) * lgb)).astype(BF16)
        o = o + jnp.dot(qdb, sb_ref[h], preferred_element_type=F32)
        mu = jnp.mean(o, axis=-1, keepdims=True)
        oc = o - mu
        var = jnp.mean(oc * oc, axis=-1, keepdims=True)
        on = oc * lax.rsqrt(var + HEAD_NORM_EPS)
        o_ref[:, lo:lo + RET_HEAD_DIM] = (on * g_ref[:, lo:lo + RET_HEAD_DIM].astype(F32)).astype(BF16)
        kd = (k.astype(F32) * jnp.exp((C - 1.0 - ii) * lgf)).astype(BF16)
        kv = lax.dot_general(kd, v, (((0,), (0,)), ((), ())), preferred_element_type=F32)
        st_ref[h] = st_ref[h] * jnp.exp(C * lgf) + kv


def _ret_main(lg, h5, sb):
    L = h5.shape[1]
    C = RET_C
    nc = L // C
    seg = lambda s: pl.BlockSpec((None, C, RET_WIDTH), lambda i: (s, i, 0))
    return pl.pallas_call(
        _ret_main_body,
        grid=(nc,),
        in_specs=[
            pl.BlockSpec(memory_space=pltpu.SMEM),
            seg(SEG_Q), seg(SEG_K), seg(SEG_V), seg(SEG_GATE),
            pl.BlockSpec((None, RET_HEADS, RET_HEAD_DIM, RET_HEAD_DIM), lambda i: (i, 0, 0, 0)),
        ],
        out_specs=pl.BlockSpec((C, RET_WIDTH), lambda i: (i, 0)),
        out_shape=jax.ShapeDtypeStruct((L, RET_WIDTH), BF16),
        scratch_shapes=[pltpu.VMEM((RET_HEADS, RET_HEAD_DIM, RET_HEAD_DIM), F32)],
        compiler_params=_params(("arbitrary",)),
        name="ret_main",
    )(lg, h5, h5, h5, h5, sb)


def _outproj_even_body(y_ref, r_ref, x_ref, wg_ref, bg_ref, wo_ref, g_ref, b_ref, o_ref):
    y = y_ref[...]
    z = jnp.dot(y, wg_ref[...], preferred_element_type=F32) + bg_ref[...]
    s5 = (y.astype(F32) * jax.nn.sigmoid(z)).astype(BF16)
    mix = (jnp.dot(s5, wo_ref[0:S5_WIDTH, :], preferred_element_type=F32)
           + jnp.dot(r_ref[...], wo_ref[S5_WIDTH:S5_WIDTH + RET_WIDTH, :], preferred_element_type=F32))
    o_ref[...] = _layer_norm_rows(DEEPNORM_ALPHA * x_ref[...] + mix, g_ref[...], b_ref[...])


def _outproj_even(y, ret, x, w_glu_bf, b_glu, w_out_bf, ln_g, ln_b, tm=256):
    L = x.shape[0]
    row = lambda n: pl.BlockSpec((tm, n), lambda i: (i, 0))
    full = lambda a: pl.BlockSpec(a.shape, lambda i: (0,) * a.ndim)
    return pl.pallas_call(
        _outproj_even_body,
        grid=(L // tm,),
        in_specs=[row(S5_WIDTH), row(RET_WIDTH), row(D_MODEL), full(w_glu_bf), full(b_glu),
                  full(w_out_bf), full(ln_g), full(ln_b)],
        out_specs=row(D_MODEL),
        out_shape=jax.ShapeDtypeStruct((L, D_MODEL), F32),
        compiler_params=_params(("parallel",)),
        name="outproj_even",
    )(y, ret, x, w_glu_bf, b_glu, w_out_bf, ln_g, ln_b)


def _inproj_odd_body(x_ref, w_ref, c_ref, s1_ref, s2_ref, o_ref, xb_ref):
    j = pl.program_id(1)

    @pl.when(j == 0)
    def _():
        xb_ref[...] = x_ref[...].astype(BF16)

    acc = jnp.dot(xb_ref[...], w_ref[...], preferred_element_type=F32)
    tn = acc.shape[1]
    nq = (ATT_HEADS * ATT_HEAD_DIM) // tn
    nk = (ATT_KV_HEADS * ATT_HEAD_DIM) // tn if tn <= ATT_KV_HEADS * ATT_HEAD_DIM else 1

    def rotary(scale):
        cc = c_ref[...]
        s1 = s1_ref[...]
        s2 = s2_ref[...]
        for hh in range(tn // ATT_HEAD_DIM):
            lo = hh * ATT_HEAD_DIM
            a = acc[:, lo:lo + ATT_HEAD_DIM]
            up = pltpu.roll(a, ATT_HEAD_DIM - ROPE_DIM // 2, axis=1)
            dn = pltpu.roll(a, ROPE_DIM // 2, axis=1)
            o_ref[:, lo:lo + ATT_HEAD_DIM] = ((a * cc + up * s1 + dn * s2) * scale).astype(BF16)

    @pl.when(j < nq)
    def _():
        rotary(ATT_HEAD_DIM ** -0.5)

    @pl.when((j >= nq) & (j < nq + nk))
    def _():
        rotary(1.0)

    @pl.when(j >= nq + nk)
    def _():
        o_ref[...] = acc.astype(BF16)


def _inproj_odd(x, w_in_bf, cc, s1, s2, tm=1024, tn=512):
    L = x.shape[0]
    n_out = w_in_bf.shape[1]
    tab = pl.BlockSpec((tm, ATT_HEAD_DIM), lambda i, j: (i, 0))
    return pl.pallas_call(
        _inproj_odd_body,
        grid=(L // tm, n_out // tn),
        in_specs=[
            pl.BlockSpec((tm, D_MODEL), lambda i, j: (i, 0)),
            pl.BlockSpec((D_MODEL, tn), lambda i, j: (0, j)),
            tab, tab, tab,
        ],
        out_specs=pl.BlockSpec((tm, tn), lambda i, j: (i, j)),
        out_shape=jax.ShapeDtypeStruct((L, n_out), BF16),
        scratch_shapes=[pltpu.VMEM((tm, D_MODEL), BF16)],
        compiler_params=_params(("parallel", "arbitrary")),
        name="inproj_odd",
    )(x, w_in_bf, cc, s1, s2)


def _attn_body(sink_ref, q_ref, kp_ref, kc_ref, kn_ref, vp_ref, vc_ref, vn_ref, o_ref):
    c = pl.program_id(0)
    nb = pl.num_programs(0)
    B = ATT_BLOCK
    hd = ATT_HEAD_DIM
    rows = ATT_GROUP * B
    r_i = lax.broadcasted_iota(jnp.int32, (rows, 3 * B), 0)
    s_i = lax.broadcasted_iota(jnp.int32, (rows, 3 * B), 1)
    rel = (r_i & (B - 1)) - s_i + B
    key_lo = jnp.where(c > 0, 0, B)
    key_hi = jnp.where(c < nb - 1, 3 * B, 2 * B)
    ok = (jnp.abs(rel) <= ATT_WINDOW) & (s_i >= key_lo) & (s_i < key_hi)
    head_of_row = lax.shift_right_logical(lax.broadcasted_iota(jnp.int32, (rows, 1), 0),
                                          int(math.log2(B)))
    for g in range(ATT_KV_HEADS):
        q = jnp.concatenate([q_ref[:, (g * ATT_GROUP + hh) * hd:(g * ATT_GROUP + hh + 1) * hd]
                             for hh in range(ATT_GROUP)], axis=0)
        ksl = slice(g * hd, (g + 1) * hd)
        k = jnp.concatenate([kp_ref[:, ksl], kc_ref[:, ksl], kn_ref[:, ksl]], axis=0)
        v = jnp.concatenate([vp_ref[:, ksl], vc_ref[:, ksl], vn_ref[:, ksl]], axis=0)
        s = lax.dot_general(q, k, (((1,), (1,)), ((), ())), preferred_element_type=F32)
        s = jnp.where(ok, s, NEG_INF)
        sink = jnp.zeros((rows, 1), F32)
        for hh in range(ATT_GROUP):
            sink = jnp.where(head_of_row == hh, sink_ref[g * ATT_GROUP + hh], sink)
        m = jnp.maximum(jnp.max(s, axis=-1, keepdims=True), sink)
        p = jnp.exp(s - m)
        den = jnp.sum(p, axis=-1, keepdims=True) + jnp.exp(sink - m)
        o = jnp.dot(p.astype(BF16), v, preferred_element_type=F32) / den
        for hh in range(ATT_GROUP):
            hcol = (g * ATT_GROUP + hh) * hd
            o_ref[:, hcol:hcol + hd] = o[hh * B:(hh + 1) * B, :].astype(BF16)


def _attention(sink, qkv):
    L = qkv.shape[0]
    B = ATT_BLOCK
    nb = L // B
    kvw = ATT_KV_HEADS * ATT_HEAD_DIM
    qw = ATT_HEADS * ATT_HEAD_DIM
    kcol = qw // kvw
    vcol = kcol + 1

    def kv_spec(col, off):
        return pl.BlockSpec((B, kvw), lambda i: (jnp.clip(i + off, 0, nb - 1), col))

    return pl.pallas_call(
        _attn_body,
        grid=(nb,),
        in_specs=[
            pl.BlockSpec(memory_space=pltpu.SMEM),
            pl.BlockSpec((B, qw), lambda i: (i, 0)),
            kv_spec(kcol, -1), kv_spec(kcol, 0), kv_spec(kcol, 1),
            kv_spec(vcol, -1), kv_spec(vcol, 0), kv_spec(vcol, 1),
        ],
        out_specs=pl.BlockSpec((B, qw), lambda i: (i, 0)),
        out_shape=jax.ShapeDtypeStruct((L, qw), BF16),
        compiler_params=_params(("parallel",)),
        name="attention",
    )(sink, qkv, qkv, qkv, qkv, qkv, qkv, qkv)


def _outproj_odd_body(a_ref, x_ref, wo_ref, g_ref, b_ref, o_ref):
    mix = jnp.dot(a_ref[...], wo_ref[...], preferred_element_type=F32)
    o_ref[...] = _layer_norm_rows(DEEPNORM_ALPHA * x_ref[...] + mix, g_ref[...], b_ref[...])


def _outproj_odd(a, x, w_out_bf, ln_g, ln_b, tm=256):
    L = x.shape[0]
    row = lambda n: pl.BlockSpec((tm, n), lambda i: (i, 0))
    full = lambda t: pl.BlockSpec(t.shape, lambda i: (0,) * t.ndim)
    return pl.pallas_call(
        _outproj_odd_body,
        grid=(L // tm,),
        in_specs=[row(a.shape[1]), row(D_MODEL), full(w_out_bf), full(ln_g), full(ln_b)],
        out_specs=row(D_MODEL),
        out_shape=jax.ShapeDtypeStruct((L, D_MODEL), F32),
        compiler_params=_params(("parallel",)),
        name="outproj_odd",
    )(a, x, w_out_bf, ln_g, ln_b)


def _mlp_body(x_ref, w1_ref, w2_ref, g_ref, b_ref, o_ref, xb_ref, acc_ref):
    f = pl.program_id(1)

    @pl.when(f == 0)
    def _():
        xb_ref[...] = x_ref[...].astype(BF16)
        acc_ref[...] = jnp.zeros_like(acc_ref)

    h = jnp.dot(xb_ref[...], w1_ref[...], preferred_element_type=F32)
    h = jnp.square(jnp.maximum(h, 0.0)).astype(BF16)
    acc_ref[...] += jnp.dot(h, w2_ref[...], preferred_element_type=F32)

    @pl.when(f == pl.num_programs(1) - 1)
    def _():
        tm = x_ref.shape[0]
        for r in range(0, tm, LN_ROWS):
            y = DEEPNORM_ALPHA * x_ref[r:r + LN_ROWS, :] + acc_ref[r:r + LN_ROWS, :]
            o_ref[r:r + LN_ROWS, :] = _layer_norm_rows(y, g_ref[...], b_ref[...])


def _mlp(x, w1_bf, w2_bf, ln_g, ln_b, tm=512, tf=1024):
    L = x.shape[0]
    return pl.pallas_call(
        _mlp_body,
        grid=(L // tm, D_FF // tf),
        in_specs=[
            pl.BlockSpec((tm, D_MODEL), lambda i, f: (i, 0)),
            pl.BlockSpec((D_MODEL, tf), lambda i, f: (0, f)),
            pl.BlockSpec((tf, D_MODEL), lambda i, f: (f, 0)),
            pl.BlockSpec((1, D_MODEL), lambda i, f: (0, 0)),
            pl.BlockSpec((1, D_MODEL), lambda i, f: (0, 0)),
        ],
        out_specs=pl.BlockSpec((tm, D_MODEL), lambda i, f: (i, 0)),
        out_shape=jax.ShapeDtypeStruct((L, D_MODEL), F32),
        scratch_shapes=[pltpu.VMEM((tm, D_MODEL), BF16), pltpu.VMEM((tm, D_MODEL), F32)],
        compiler_params=_params(("parallel", "arbitrary")),
        name="mlp",
    )(x, w1_bf, w2_bf, ln_g, ln_b)


def _rotary_tables(L, rot_dim, theta):
    half = rot_dim // 2
    inv_freq = 1.0 / (theta ** (jnp.arange(half, dtype=F32) / half))
    ang = jnp.arange(L).astype(F32)[:, None] * inv_freq[None, :]
    return jnp.cos(ang), jnp.sin(ang)


def _even_layer(x, w_in, w_out, lam_re, lam_im, log_step, b_re, b_im, c_re, c_im,
                d_skip, w_glu, b_glu, ret_log_decay, ln_g, ln_b):
    L = x.shape[0]
    cos, sin = _rotary_tables(L, RET_HEAD_DIM, RET_ROPE_THETA)
    h5 = _inproj_even(x, w_in.astype(BF16), cos, sin)
    nc = L // S5_T
    u_g = h5[0].reshape(nc, S5_T, S5_GROUPS, S5_GROUP).transpose(2, 0, 1, 3).reshape(S5_GROUPS, nc, S5_TL)
    disc = _s5_disc(lam_re, lam_im, log_step)
    m, ws_t, wc = _s5_gen(disc, b_re, b_im, c_re, c_im)
    d_tile = jnp.tile(d_skip.astype(F32), (1, S5_T)).reshape(S5_GROUPS, 1, S5_TL)
    a1, a2, a3 = (disc[k].reshape(2, 1, S5_GROUPS * 128) for k in (DISC_AT_RE, DISC_SCAN_A2, DISC_SCAN_A3))
    y_g = _s5_mix(u_g, m, ws_t, wc, d_tile, a1, a2, a3)
    y = y_g.reshape(S5_GROUPS, nc, S5_T, S5_GROUP).transpose(1, 2, 0, 3).reshape(L, S5_WIDTH)
    lg = ret_log_decay.astype(F32)
    sb = _ret_bstate(lg, h5)
    ret = _ret_main(lg, h5, sb)
    return _outproj_even(y, ret, x, w_glu.astype(BF16), b_glu.astype(F32).reshape(1, -1),
                         w_out.astype(BF16), ln_g.reshape(1, -1), ln_b.reshape(1, -1))


def _odd_layer(x, w_in, w_out, sink, ln_g, ln_b):
    L = x.shape[0]
    cos, sin = _rotary_tables(L, ROPE_DIM, ROPE_THETA)
    half = ROPE_DIM // 2
    pad = ATT_HEAD_DIM - ROPE_DIM
    cc = jnp.concatenate([cos, cos, jnp.ones((L, pad), F32)], axis=1)
    s1 = jnp.concatenate([-sin, jnp.zeros((L, ATT_HEAD_DIM - half), F32)], axis=1)
    s2 = jnp.concatenate([jnp.zeros((L, half), F32), sin, jnp.zeros((L, pad), F32)], axis=1)
    qkv = _inproj_odd(x, w_in.astype(BF16), cc, s1, s2)
    att = _attention(sink.astype(F32), qkv)
    return _outproj_odd(att, x, w_out.astype(BF16), ln_g.reshape(1, -1), ln_b.reshape(1, -1))


def kernel(x, ln_g, ln_b, mlp_w1, mlp_w2, even_w_in, even_w_out, s5_lambda_re, s5_lambda_im, s5_log_step, s5_b_re, s5_b_im, s5_c_re, s5_c_im, s5_d, s5_w_glu, s5_b_glu, ret_log_decay, odd_w_in, odd_w_out, attn_sink):
    bsz = x.shape[0]
    outs = []
    for b in range(bsz):
        xb = x[b]
        for layer in range(DEPTH):
            if layer % 2 == 0:
                e = layer // 2
                xb = _even_layer(xb, even_w_in[e], even_w_out[e], s5_lambda_re[e], s5_lambda_im[e],
                                 s5_log_step[e], s5_b_re[e], s5_b_im[e], s5_c_re[e], s5_c_im[e],
                                 s5_d[e], s5_w_glu[e], s5_b_glu[e], ret_log_decay[e],
                                 ln_g[layer, 0], ln_b[layer, 0])
            else:
                o = layer // 2
                xb = _odd_layer(xb, odd_w_in[o], odd_w_out[o], attn_sink[o], ln_g[layer, 0], ln_b[layer, 0])
            xb = _mlp(xb, mlp_w1[layer].astype(BF16), mlp_w2[layer].astype(BF16),
                      ln_g[layer, 1].reshape(1, -1), ln_b[layer, 1].reshape(1, -1))
        outs.append(xb)
    return jnp.stack(outs, axis=0)
```

```python
import functools
import math

import jax
import jax.numpy as jnp
from jax import lax
from jax.experimental import pallas as pl
from jax.experimental.pallas import tpu as pltpu

F32 = jnp.float32
BF16 = jnp.bfloat16

D_MODEL = 2048
DEPTH = 2
S5_WIDTH = 1024
S5_GROUP = 16
S5_GROUPS = 64
S5_STATE = 64
RET_WIDTH = 1024
RET_HEADS = 4
RET_HEAD_DIM = 256
RET_ROPE_THETA = 10000.0
ATT_HEADS = 16
ATT_KV_HEADS = 4
ATT_HEAD_DIM = 128
ATT_GROUP = 4
ATT_WINDOW = 128
ATT_BLOCK = 128
ROPE_THETA = 500000.0
ROPE_DIM = 32
D_FF = 4 * D_MODEL
DEEPNORM_ALPHA = (2 * DEPTH) ** 0.25
LN_EPS = 1e-5
HEAD_NORM_EPS = 1e-6
NEG_INF = -1e30

V7X_VMEM_BYTES = 64 * 1024 * 1024
VMEM_LIMIT = V7X_VMEM_BYTES - 8 * 1024 * 1024

S5_T = 32
S5_TL = S5_T * S5_GROUP
S5_GB = 8
S5_GEN_GB = 8
RET_C = 256
LN_ROWS = 256


def _params(sem):
    return pltpu.CompilerParams(dimension_semantics=sem, vmem_limit_bytes=VMEM_LIMIT)


def _layer_norm_rows(y, g, b):
    mu = jnp.mean(y, axis=-1, keepdims=True)
    yc = y - mu
    var = jnp.mean(yc * yc, axis=-1, keepdims=True)
    return yc * lax.rsqrt(var + LN_EPS) * g + b


def _block_transpose8(xs):
    blk = lax.shift_right_logical(lax.broadcasted_iota(jnp.int32, xs[0].shape, 1), 4)
    xs = list(xs)
    for k in range(3):
        d = 1 << k
        upper = (blk & d) != 0
        for i in range(8):
            if i & d:
                continue
            a, b = xs[i], xs[i + d]
            xs[i] = jnp.where(upper, pltpu.roll(b, S5_GROUP * d, axis=1), a)
            xs[i + d] = jnp.where(upper, b, pltpu.roll(a, 128 - S5_GROUP * d, axis=1))
    return xs


SEG_Q, SEG_K, SEG_V, SEG_GATE = range(4)


def _inproj_even_body(x_ref, w_ref, cos_ref, sin_ref, u_ref, o_ref, xb_ref, acc_ref, ut_ref):
    j = pl.program_id(1)

    @pl.when(j == 0)
    def _():
        xb_ref[...] = x_ref[...].astype(BF16)

    acc_ref[...] = jnp.dot(xb_ref[...], w_ref[...], preferred_element_type=F32)

    @pl.when(j == 0)
    def _():
        nchunk = acc_ref.shape[0] // S5_T
        for q in range(S5_GROUPS // 8):
            ut_ref[...] = acc_ref[:, q * 128:(q + 1) * 128]
            for jb in range(S5_T // 8):
                ys = _block_transpose8([ut_ref[pl.ds(8 * jb + t8, nchunk, stride=S5_T), :]
                                        for t8 in range(8)])
                for g8 in range(8):
                    u_ref[8 * q + g8, :, jb * 128:(jb + 1) * 128] = ys[g8].astype(BF16)

    @pl.when(j == 1 + SEG_V)
    def _():
        o_ref[...] = acc_ref[...].astype(BF16)

    def rotary(scale):
        cos = cos_ref[...]
        sin = sin_ref[...]
        half = RET_HEAD_DIM // 2
        for hh in range(RET_HEADS):
            lo = hh * RET_HEAD_DIM
            a = acc_ref[:, lo:lo + half]
            b = acc_ref[:, lo + half:lo + RET_HEAD_DIM]
            o_ref[:, lo:lo + half] = ((a * cos - b * sin) * scale).astype(BF16)
            o_ref[:, lo + half:lo + RET_HEAD_DIM] = ((b * cos + a * sin) * scale).astype(BF16)

    @pl.when(j == 1 + SEG_Q)
    def _():
        rotary(1.0)

    @pl.when(j == 1 + SEG_K)
    def _():
        rotary(RET_HEAD_DIM ** -0.5)

    @pl.when(j == 1 + SEG_GATE)
    def _():
        g = acc_ref[...]
        o_ref[...] = (g * jax.nn.sigmoid(g)).astype(BF16)


def _inproj_even(x, w_in_bf, cos, sin, tm=1024):
    L = x.shape[0]
    tn = S5_WIDTH
    nseg = w_in_bf.shape[1] // tn
    return pl.pallas_call(
        _inproj_even_body,
        grid=(L // tm, nseg),
        in_specs=[
            pl.BlockSpec((tm, D_MODEL), lambda i, j: (i, 0)),
            pl.BlockSpec((D_MODEL, tn), lambda i, j: (0, j)),
            pl.BlockSpec((tm, RET_HEAD_DIM // 2), lambda i, j: (i, 0)),
            pl.BlockSpec((tm, RET_HEAD_DIM // 2), lambda i, j: (i, 0)),
        ],
        out_specs=[pl.BlockSpec((S5_GROUPS, tm // S5_T, S5_TL), lambda i, j: (0, i, 0)),
                   pl.BlockSpec((None, tm, tn), lambda i, j: (jnp.maximum(j - 1, 0), i, 0))],
        out_shape=[jax.ShapeDtypeStruct((S5_GROUPS, L // S5_T, S5_TL), BF16),
                   jax.ShapeDtypeStruct((nseg - 1, L, tn), BF16)],
        scratch_shapes=[pltpu.VMEM((tm, D_MODEL), BF16), pltpu.VMEM((tm, tn), F32),
                        pltpu.VMEM((tm, 128), F32)],
        compiler_params=_params(("parallel", "arbitrary")),
        name="inproj_even",
    )(x, w_in_bf, cos, sin)


def _cmul(ar, ai, br, bi):
    return ar * br - ai * bi, ar * bi + ai * br


(DISC_A1_RE, DISC_A1_IM, DISC_A2_RE, DISC_A2_IM, DISC_A4_RE, DISC_A4_IM, DISC_A8_RE, DISC_A8_IM,
 DISC_A16_RE, DISC_A16_IM, DISC_AT_RE, DISC_AT_IM, DISC_Z_RE, DISC_Z_IM, DISC_SCAN_A2, DISC_SCAN_A3) = range(16)


def _s5_disc_body(lr_ref, li_ref, ls_ref, sg_ref, o_ref):
    lr = jnp.minimum(lr_ref[...], -1e-4)
    li = li_ref[...]
    step = jnp.exp(ls_ref[...])
    mag = jnp.exp(lr * step)
    ar = mag * jnp.cos(li * step)
    ai = mag * jnp.sin(li * step)
    nr, ni = ar - 1.0, ai
    den = lr * lr + li * li
    o_ref[DISC_Z_RE] = (nr * lr + ni * li) / den
    o_ref[DISC_Z_IM] = (ni * lr - nr * li) / den
    pr, pi = ar, ai
    for k in range(6):
        o_ref[2 * k] = pr
        o_ref[2 * k + 1] = pi
        if k < 5:
            pr, pi = _cmul(pr, pi, pr, pi)
    o_ref[DISC_SCAN_A2] = pi * sg_ref[...]
    o_ref[DISC_SCAN_A3] = -pi * sg_ref[...]


def _s5_disc(lam_re, lam_im, log_step):
    assert S5_T == 32
    rows = 2 * S5_GROUPS
    two = lambda a: jnp.tile(a.astype(F32).reshape(rows, -1), (1, 2))
    lr = two(lam_re)
    li = two(lam_im)
    ls = jnp.broadcast_to(log_step.astype(F32).reshape(rows, 1), (rows, 2 * S5_STATE))
    sg = jnp.broadcast_to(jnp.concatenate([-jnp.ones((S5_STATE,), F32), jnp.ones((S5_STATE,), F32)])[None],
                          (rows, 2 * S5_STATE))
    return pl.pallas_call(
        _s5_disc_body,
        out_shape=jax.ShapeDtypeStruct((16, rows, 2 * S5_STATE), F32),
        name="s5_disc",
    )(lr, li, ls, sg)


def _s5_gen_body(col_ref, zrow_ref, bt_ref, btile_ref, ctile_ref, m_ref, ws_ref, wc_ref):
    P = S5_STATE
    hi = lax.Precision.HIGHEST
    tlo = lax.shift_right_logical(lax.broadcasted_iota(jnp.int32, (P, 128), 1), 4)
    lane = lax.broadcasted_iota(jnp.int32, (S5_GROUP, S5_TL), 1)
    ones = jnp.ones((P, 128), F32)
    zeros = jnp.zeros((P, 128), F32)

    def one_group(gi, carry):
        kt = []
        for d in range(2):
            col = col_ref[gi, d]
            c = lambda k: jnp.broadcast_to(col[:, k:k + 1], (P, 128))
            a1 = (c(DISC_A1_RE), c(DISC_A1_IM))
            a2 = (c(DISC_A2_RE), c(DISC_A2_IM))
            a4 = (c(DISC_A4_RE), c(DISC_A4_IM))
            a8 = (c(DISC_A8_RE), c(DISC_A8_IM))
            a16 = (c(DISC_A16_RE), c(DISC_A16_IM))
            blk = [None, a8, a16, _cmul(*a8, *a16)]

            def low_powers(reverse):
                xr, xi = ones, zeros
                for k, ak in enumerate((a1, a2, a4)):
                    bit = (lax.shift_right_logical(tlo, k) & 1) == (0 if reverse else 1)
                    yr, yi = _cmul(xr, xi, *ak)
                    xr = jnp.where(bit, yr, xr)
                    xi = jnp.where(bit, yi, xi)
                return xr, xi

            def expand(base, reverse):
                out = []
                for j in range(4):
                    f = blk[3 - j] if reverse else blk[j]
                    out.append(base if f is None else _cmul(*base, *f))
                return out

            ct = (ctile_ref[gi, d, 0], ctile_ref[gi, d, 1])
            bbar = _cmul(c(DISC_Z_RE), c(DISC_Z_IM), btile_ref[gi, d, 0], btile_ref[gi, d, 1])
            zrow = zrow_ref[gi, d]
            zr_row, zi_row = zrow[0:1, :], zrow[1:2, :]
            bbt_r, bbt_i = _cmul(zr_row, zi_row, bt_ref[gi, d, 0], bt_ref[gi, d, 1])
            ca = expand(_cmul(*ct, *low_powers(d == 1)), d == 1)
            ba = expand(_cmul(*bbar, *low_powers(d == 0)), d == 0)
            wcj = [_cmul(*x, *a1) for x in ca]
            cat = lambda parts, k: jnp.concatenate([x[k] for x in parts], axis=1)
            kt.append(jnp.dot(bbt_r, cat(ca, 0), precision=hi, preferred_element_type=F32)
                      - jnp.dot(bbt_i, cat(ca, 1), precision=hi, preferred_element_type=F32))
            ba_r, ba_i = cat(ba, 0).astype(BF16), cat(ba, 1).astype(BF16)
            for r, part in enumerate((ba_r, ba_i, ba_i, ba_r)):
                ws_ref[gi, (4 * d + r) * P:(4 * d + r + 1) * P, :] = part
            wc_ref[gi, 2 * d * P:(2 * d + 1) * P, :] = cat(wcj, 0).astype(BF16)
            wc_ref[gi, (2 * d + 1) * P:(2 * d + 2) * P, :] = (-cat(wcj, 1)).astype(BF16)
        ktf, ktb = kt
        for s in range(S5_T):
            lo, hi_lane = S5_GROUP * s, S5_GROUP * (s + 1)
            f = ktf if s == 0 else jnp.where(lane >= lo, pltpu.roll(ktf, lo, axis=1), 0.0)
            b = ktb if s == S5_T - 1 else jnp.where(lane < hi_lane, pltpu.roll(ktb, hi_lane, axis=1), 0.0)
            m_ref[gi, lo:hi_lane, :] = (f + b).astype(BF16)
        return carry

    lax.fori_loop(0, col_ref.shape[0], one_group, 0)


def _s5_gen(disc, b_re, b_im, c_re, c_im):
    G, P, Cg = S5_GROUPS, S5_STATE, S5_GROUP
    gb = S5_GEN_GB
    d4 = disc[:, :, :P].reshape(16, 2, G, P)
    col = d4.transpose(2, 1, 3, 0)
    zrow = d4[DISC_Z_RE:DISC_Z_IM + 1].transpose(2, 1, 0, 3)
    b = jnp.stack([b_re, b_im], axis=1).astype(F32)
    c = jnp.stack([c_re, c_im], axis=1).astype(F32)
    bt = b.transpose(2, 0, 1, 4, 3)
    btile = jnp.tile(b.transpose(2, 0, 1, 3, 4), (1, 1, 1, 1, 128 // Cg))
    ctile = jnp.tile(c.transpose(2, 0, 1, 4, 3), (1, 1, 1, 1, 128 // Cg))
    spec = lambda a: pl.BlockSpec((gb,) + a.shape[1:], lambda i: (i,) + (0,) * (a.ndim - 1))
    out = lambda rows: pl.BlockSpec((gb, rows, S5_TL), lambda i: (i, 0, 0))
    return pl.pallas_call(
        _s5_gen_body,
        grid=(G // gb,),
        in_specs=[spec(col), spec(zrow), spec(bt), spec(btile), spec(ctile)],
        out_specs=[out(S5_TL), out(8 * P), out(4 * P)],
        out_shape=[jax.ShapeDtypeStruct((G, S5_TL, S5_TL), BF16),
                   jax.ShapeDtypeStruct((G, 8 * P, S5_TL), BF16),
                   jax.ShapeDtypeStruct((G, 4 * P, S5_TL), BF16)],
        compiler_params=_params(("parallel",)),
        name="s5_gen",
    )(col, zrow, bt, btile, ctile)


def _s5_body(u_ref, m_ref, ws_ref, wc_ref, dt_ref, a1_ref, a2_ref, a3_ref, y_ref, s_ref, h_ref,
             yg_ref, yt_ref):
    nc = u_ref.shape[1]
    gb = u_ref.shape[0]
    w = gb * 128
    for gi in range(gb):
        s = lax.dot_general(u_ref[gi], ws_ref[gi], (((1,), (1,)), ((), ())),
                            preferred_element_type=F32)
        for r in range(4):
            s_ref[:, r * w + gi * 128:r * w + (gi + 1) * 128] = s[:, r * 128:(r + 1) * 128]

    a1f, a2f, a3f = a1_ref[0], a2_ref[0], a3_ref[0]
    a1b, a2b, a3b = a1_ref[1], a2_ref[1], a3_ref[1]

    def step(n, carry):
        hf, gf, hb, gb_ = carry
        m = nc - 1 - n
        h_ref[pl.ds(n, 1), 0:w] = hf
        h_ref[pl.ds(m, 1), w:2 * w] = hb
        sfh = s_ref[pl.ds(n, 1), 0:w]
        sfg = s_ref[pl.ds(n, 1), w:2 * w]
        sbh = s_ref[pl.ds(m, 1), 2 * w:3 * w]
        sbg = s_ref[pl.ds(m, 1), 3 * w:4 * w]
        hf2 = a1f * hf + a2f * gf + sfh
        gf2 = a1f * gf + a3f * hf + sfg
        hb2 = a1b * hb + a2b * gb_ + sbh
        gb2 = a1b * gb_ + a3b * hb + sbg
        return hf2, gf2, hb2, gb2

    z = jnp.zeros((1, w), F32)
    lax.fori_loop(0, nc, step, (z, z, z, z))

    for gi in range(gb):
        u = u_ref[gi]
        hcat = jnp.concatenate([h_ref[:, gi * 128:(gi + 1) * 128],
                                h_ref[:, w + gi * 128:w + (gi + 1) * 128]], axis=1).astype(BF16)
        y = (jnp.dot(u, m_ref[gi], preferred_element_type=F32)
             + jnp.dot(hcat, wc_ref[gi], preferred_element_type=F32)
             + dt_ref[gi] * u.astype(F32))
        yg_ref[gi] = jax.nn.gelu(y)

    for jb in range(S5_T // 8):
        zs = _block_transpose8([yg_ref[g8, :, jb * 128:(jb + 1) * 128] for g8 in range(8)])
        for t8 in range(8):
            yt_ref[pl.ds(8 * jb + t8, nc, stride=S5_T), :] = zs[t8]
    y_ref[...] = yt_ref[...].astype(BF16)


def _s5_mix(u_g, m, ws_t, wc, d_tile, a1, a2, a3):
    G, nc, TL = u_g.shape
    gb = S5_GB
    assert gb * S5_GROUP == 128
    L = nc * S5_T
    w = gb * 128
    a_spec = pl.BlockSpec((2, 1, w), lambda i: (0, 0, i))
    return pl.pallas_call(
        _s5_body,
        grid=(G // gb,),
        in_specs=[
            pl.BlockSpec((gb, nc, TL), lambda i: (i, 0, 0)),
            pl.BlockSpec((gb, TL, TL), lambda i: (i, 0, 0)),
            pl.BlockSpec((gb, 512, TL), lambda i: (i, 0, 0)),
            pl.BlockSpec((gb, 256, TL), lambda i: (i, 0, 0)),
            pl.BlockSpec((gb, 1, TL), lambda i: (i, 0, 0)),
            a_spec, a_spec, a_spec,
        ],
        out_specs=pl.BlockSpec((L, 128), lambda i: (0, i)),
        out_shape=jax.ShapeDtypeStruct((L, G * S5_GROUP), BF16),
        scratch_shapes=[pltpu.VMEM((nc, 4 * w), F32), pltpu.VMEM((nc, 2 * w), F32),
                        pltpu.VMEM((gb, nc, TL), F32), pltpu.VMEM((L, 128), F32)],
        compiler_params=_params(("parallel",)),
        name="s5_mix",
    )(u_g, m, ws_t, wc, d_tile, a1, a2, a3)


def _row_index(n):
    return lax.broadcasted_iota(jnp.int32, (n, 1), 0).astype(F32)


def _log_decay(lg_ref, d, h):
    return -jnp.abs(jnp.full((1, 1), lg_ref[d, h], F32))


def _ret_bstate_body(lg_ref, k_ref, v_ref, sb_ref, st_ref):
    C = k_ref.shape[0]

    @pl.when(pl.program_id(0) == 0)
    def _():
        st_ref[...] = jnp.zeros_like(st_ref)

    jj = _row_index(C)
    for h in range(RET_HEADS):
        lo = h * RET_HEAD_DIM
        lgb = _log_decay(lg_ref, 1, h)
        sb_ref[h] = st_ref[h].astype(BF16)
        kd = (k_ref[:, lo:lo + RET_HEAD_DIM].astype(F32) * jnp.exp(jj * lgb)).astype(BF16)
        kv = lax.dot_general(kd, v_ref[:, lo:lo + RET_HEAD_DIM], (((0,), (0,)), ((), ())),
                             preferred_element_type=F32)
        st_ref[h] = st_ref[h] * jnp.exp(C * lgb) + kv


def _ret_bstate(lg, h5):
    L = h5.shape[1]
    C = RET_C
    nc = L // C
    return pl.pallas_call(
        _ret_bstate_body,
        grid=(nc,),
        in_specs=[
            pl.BlockSpec(memory_space=pltpu.SMEM),
            pl.BlockSpec((None, C, RET_WIDTH), lambda i: (SEG_K, nc - 1 - i, 0)),
            pl.BlockSpec((None, C, RET_WIDTH), lambda i: (SEG_V, nc - 1 - i, 0)),
        ],
        out_specs=pl.BlockSpec((None, RET_HEADS, RET_HEAD_DIM, RET_HEAD_DIM),
                               lambda i: (nc - 1 - i, 0, 0, 0)),
        out_shape=jax.ShapeDtypeStruct((nc, RET_HEADS, RET_HEAD_DIM, RET_HEAD_DIM), BF16),
        scratch_shapes=[pltpu.VMEM((RET_HEADS, RET_HEAD_DIM, RET_HEAD_DIM), F32)],
        compiler_params=_params(("arbitrary",)),
        name="ret_bstate",
    )(lg, h5, h5)


def _ret_main_body(lg_ref, q_ref, k_ref, v_ref, g_ref, sb_ref, o_ref, st_ref):
    C = q_ref.shape[0]

    @pl.when(pl.program_id(0) == 0)
    def _():
        st_ref[...] = jnp.zeros_like(st_ref)

    ii = _row_index(C)
    diff = (lax.broadcasted_iota(jnp.int32, (C, C), 0)
            - lax.broadcasted_iota(jnp.int32, (C, C), 1)).astype(F32)
    for h in range(RET_HEADS):
        lo = h * RET_HEAD_DIM
        lgf = _log_decay(lg_ref, 0, h)
        lgb = _log_decay(lg_ref, 1, h)
        q = q_ref[:, lo:lo + RET_HEAD_DIM]
        k = k_ref[:, lo:lo + RET_HEAD_DIM]
        v = v_ref[:, lo:lo + RET_HEAD_DIM]
        qf = q.astype(F32)
        s = lax.dot_general(q, k, (((1,), (1,)), ((), ())), preferred_element_type=F32)
        decay = jnp.where(diff >= 0, jnp.exp(lgf * jnp.maximum(diff, 0.0)),
                          jnp.exp(lgb * jnp.maximum(-diff, 0.0)))
        o = jnp.dot((s * decay).astype(BF16), v, preferred_element_type=F32)
        qdf = (qf * jnp.exp((ii + 1.0) * lgf)).astype(BF16)
        o = o + jnp.dot(qdf, st_ref[h].astype(BF16), preferred_element_type=F32)
        qdb = (qf * jnp.exp((C - ii) * lgb)).astype(BF16)
        o = o + jnp.dot(qdb, sb_ref[h], preferred_element_type=F32)
        mu = jnp.mean(o, axis=-1, keepdims=True)
        oc = o - mu
        var = jnp.mean(oc * oc, axis=-1, keepdims=True)
        on = oc * lax.rsqrt(var + HEAD_NORM_EPS)
        o_ref[:, lo:lo + RET_HEAD_DIM] = (on * g_ref[:, lo:lo + RET_HEAD_DIM].astype(F32)).astype(BF16)
        kd = (k.astype(F32) * jnp.exp((C - 1.0 - ii) * lgf)).astype(BF16)
        kv = lax.dot_general(kd, v, (((0,), (0,)), ((), ())), preferred_element_type=F32)
        st_ref[h] = st_ref[h] * jnp.exp(C * lgf) + kv


def _ret_main(lg, h5, sb):
    L = h5.shape[1]
    C = RET_C
    nc = L // C
    seg = lambda s: pl.BlockSpec((None, C, RET_WIDTH), lambda i: (s, i, 0))
    return pl.pallas_call(
        _ret_main_body,
        grid=(nc,),
        in_specs=[
            pl.BlockSpec(memory_space=pltpu.SMEM),
            seg(SEG_Q), seg(SEG_K), seg(SEG_V), seg(SEG_GATE),
            pl.BlockSpec((None, RET_HEADS, RET_HEAD_DIM, RET_HEAD_DIM), lambda i: (i, 0, 0, 0)),
        ],
        out_specs=pl.BlockSpec((C, RET_WIDTH), lambda i: (i, 0)),
        out_shape=jax.ShapeDtypeStruct((L, RET_WIDTH), BF16),
        scratch_shapes=[pltpu.VMEM((RET_HEADS, RET_HEAD_DIM, RET_HEAD_DIM), F32)],
        compiler_params=_params(("arbitrary",)),
        name="ret_main",
    )(lg, h5, h5, h5, h5, sb)


def _outproj_even_body(y_ref, r_ref, x_ref, wg_ref, bg_ref, wo_ref, g_ref, b_ref, o_ref):
    y = y_ref[...]
    z = jnp.dot(y, wg_ref[...], preferred_element_type=F32) + bg_ref[...]
    s5 = (y.astype(F32) * jax.nn.sigmoid(z)).astype(BF16)
    mix = (jnp.dot(s5, wo_ref[0:S5_WIDTH, :], preferred_element_type=F32)
           + jnp.dot(r_ref[...], wo_ref[S5_WIDTH:S5_WIDTH + RET_WIDTH, :], preferred_element_type=F32))
    o_ref[...] = _layer_norm_rows(DEEPNORM_ALPHA * x_ref[...] + mix, g_ref[...], b_ref[...])


def _outproj_even(y, ret, x, w_glu_bf, b_glu, w_out_bf, ln_g, ln_b, tm=256):
    L = x.shape[0]
    row = lambda n: pl.BlockSpec((tm, n), lambda i: (i, 0))
    full = lambda a: pl.BlockSpec(a.shape, lambda i: (0,) * a.ndim)
    return pl.pallas_call(
        _outproj_even_body,
        grid=(L // tm,),
        in_specs=[row(S5_WIDTH), row(RET_WIDTH), row(D_MODEL), full(w_glu_bf), full(b_glu),
                  full(w_out_bf), full(ln_g), full(ln_b)],
        out_specs=row(D_MODEL),
        out_shape=jax.ShapeDtypeStruct((L, D_MODEL), F32),
        compiler_params=_params(("parallel",)),
        name="outproj_even",
    )(y, ret, x, w_glu_bf, b_glu, w_out_bf, ln_g, ln_b)


def _inproj_odd_body(x_ref, w_ref, c_ref, s1_ref, s2_ref, o_ref, xb_ref):
    j = pl.program_id(1)

    @pl.when(j == 0)
    def _():
        xb_ref[...] = x_ref[...].astype(BF16)

    acc = jnp.dot(xb_ref[...], w_ref[...], preferred_element_type=F32)
    tn = acc.shape[1]
    nq = (ATT_HEADS * ATT_HEAD_DIM) // tn
    nk = (ATT_KV_HEADS * ATT_HEAD_DIM) // tn if tn <= ATT_KV_HEADS * ATT_HEAD_DIM else 1

    def rotary(scale):
        cc = c_ref[...]
        s1 = s1_ref[...]
        s2 = s2_ref[...]
        for hh in range(tn // ATT_HEAD_DIM):
            lo = hh * ATT_HEAD_DIM
            a = acc[:, lo:lo + ATT_HEAD_DIM]
            up = pltpu.roll(a, ATT_HEAD_DIM - ROPE_DIM // 2, axis=1)
            dn = pltpu.roll(a, ROPE_DIM // 2, axis=1)
            o_ref[:, lo:lo + ATT_HEAD_DIM] = ((a * cc + up * s1 + dn * s2) * scale).astype(BF16)

    @pl.when(j < nq)
    def _():
        rotary(ATT_HEAD_DIM ** -0.5)

    @pl.when((j >= nq) & (j < nq + nk))
    def _():
        rotary(1.0)

    @pl.when(j >= nq + nk)
    def _():
        o_ref[...] = acc.astype(BF16)


def _inproj_odd(x, w_in_bf, cc, s1, s2, tm=1024, tn=512):
    L = x.shape[0]
    n_out = w_in_bf.shape[1]
    tab = pl.BlockSpec((tm, ATT_HEAD_DIM), lambda i, j: (i, 0))
    return pl.pallas_call(
        _inproj_odd_body,
        grid=(L // tm, n_out // tn),
        in_specs=[
            pl.BlockSpec((tm, D_MODEL), lambda i, j: (i, 0)),
            pl.BlockSpec((D_MODEL, tn), lambda i, j: (0, j)),
            tab, tab, tab,
        ],
        out_specs=pl.BlockSpec((tm, tn), lambda i, j: (i, j)),
        out_shape=jax.ShapeDtypeStruct((L, n_out), BF16),
        scratch_shapes=[pltpu.VMEM((tm, D_MODEL), BF16)],
        compiler_params=_params(("parallel", "arbitrary")),
        name="inproj_odd",
    )(x, w_in_bf, cc, s1, s2)


def _attn_body(sink_ref, q_ref, kp_ref, kc_ref, kn_ref, vp_ref, vc_ref, vn_ref, o_ref):
    c = pl.program_id(0)
    nb = pl.num_programs(0)
    B = ATT_BLOCK
    hd = ATT_HEAD_DIM
    rows = ATT_GROUP * B
    r_i = lax.broadcasted_iota(jnp.int32, (rows, 3 * B), 0)
    s_i = lax.broadcasted_iota(jnp.int32, (rows, 3 * B), 1)
    rel = (r_i & (B - 1)) - s_i + B
    key_lo = jnp.where(c > 0, 0, B)
    key_hi = jnp.where(c < nb - 1, 3 * B, 2 * B)
    ok = (jnp.abs(rel) <= ATT_WINDOW) & (s_i >= key_lo) & (s_i < key_hi)
    head_of_row = lax.shift_right_logical(lax.broadcasted_iota(jnp.int32, (rows, 1), 0),
                                          int(math.log2(B)))
    for g in range(ATT_KV_HEADS):
        q = jnp.concatenate([q_ref[:, (g * ATT_GROUP + hh) * hd:(g * ATT_GROUP + hh + 1) * hd]
                             for hh in range(ATT_GROUP)], axis=0)
        ksl = slice(g * hd, (g + 1) * hd)
        k = jnp.concatenate([kp_ref[:, ksl], kc_ref[:, ksl], kn_ref[:, ksl]], axis=0)
        v = jnp.concatenate([vp_ref[:, ksl], vc_ref[:, ksl], vn_ref[:, ksl]], axis=0)
        s = lax.dot_general(q, k, (((1,), (1,)), ((), ())), preferred_element_type=F32)
        s = jnp.where(ok, s, NEG_INF)
        sink = jnp.zeros((rows, 1), F32)
        for hh in range(ATT_GROUP):
            sink = jnp.where(head_of_row == hh, sink_ref[g * ATT_GROUP + hh], sink)
        m = jnp.maximum(jnp.max(s, axis=-1, keepdims=True), sink)
        p = jnp.exp(s - m)
        den = jnp.sum(p, axis=-1, keepdims=True) + jnp.exp(sink - m)
        o = jnp.dot(p.astype(BF16), v, preferred_element_type=F32) / den
        for hh in range(ATT_GROUP):
            hcol = (g * ATT_GROUP + hh) * hd
            o_ref[:, hcol:hcol + hd] = o[hh * B:(hh + 1) * B, :].astype(BF16)


def _attention(sink, qkv):
    L = qkv.shape[0]
    B = ATT_BLOCK
    nb = L // B
    kvw = ATT_KV_HEADS * ATT_HEAD_DIM
    qw = ATT_HEADS * ATT_HEAD_DIM
    kcol = qw // kvw
    vcol = kcol + 1

    def kv_spec(col, off):
        return pl.BlockSpec((B, kvw), lambda i: (jnp.clip(i + off, 0, nb - 1), col))

    return pl.pallas_call(
        _attn_body,
        grid=(nb,),
        in_specs=[
            pl.BlockSpec(memory_space=pltpu.SMEM),
            pl.BlockSpec((B, qw), lambda i: (i, 0)),
            kv_spec(kcol, -1), kv_spec(kcol, 0), kv_spec(kcol, 1),
            kv_spec(vcol, -1), kv_spec(vcol, 0), kv_spec(vcol, 1),
        ],
        out_specs=pl.BlockSpec((B, qw), lambda i: (i, 0)),
        out_shape=jax.ShapeDtypeStruct((L, qw), BF16),
        compiler_params=_params(("parallel",)),
        name="attention",
    )(sink, qkv, qkv, qkv, qkv, qkv, qkv, qkv)


def _outproj_odd_body(a_ref, x_ref, wo_ref, g_ref, b_ref, o_ref):
    mix = jnp.dot(a_ref[...], wo_ref[...], preferred_element_type=F32)
    o_ref[...] = _layer_norm_rows(DEEPNORM_ALPHA * x_ref[...] + mix, g_ref[...], b_ref[...])


def _outproj_odd(a, x, w_out_bf, ln_g, ln_b, tm=256):
    L = x.shape[0]
    row = lambda n: pl.BlockSpec((tm, n), lambda i: (i, 0))
    full = lambda t: pl.BlockSpec(t.shape, lambda i: (0,) * t.ndim)
    return pl.pallas_call(
        _outproj_odd_body,
        grid=(L // tm,),
        in_specs=[row(a.shape[1]), row(D_MODEL), full(w_out_bf), full(ln_g), full(ln_b)],
        out_specs=row(D_MODEL),
        out_shape=jax.ShapeDtypeStruct((L, D_MODEL), F32),
        compiler_params=_params(("parallel",)),
        name="outproj_odd",
    )(a, x, w_out_bf, ln_g, ln_b)


def _mlp_body(x_ref, w1_ref, w2_ref, g_ref, b_ref, o_ref, xb_ref, acc_ref):
    f = pl.program_id(1)

    @pl.when(f == 0)
    def _():
        xb_ref[...] = x_ref[...].astype(BF16)
        acc_ref[...] = jnp.zeros_like(acc_ref)

    h = jnp.dot(xb_ref[...], w1_ref[...], preferred_element_type=F32)
    h = jnp.square(jnp.maximum(h, 0.0)).astype(BF16)
    acc_ref[...] += jnp.dot(h, w2_ref[...], preferred_element_type=F32)

    @pl.when(f == pl.num_programs(1) - 1)
    def _():
        tm = x_ref.shape[0]
        for r in range(0, tm, LN_ROWS):
            y = DEEPNORM_ALPHA * x_ref[r:r + LN_ROWS, :] + acc_ref[r:r + LN_ROWS, :]
            o_ref[r:r + LN_ROWS, :] = _layer_norm_rows(y, g_ref[...], b_ref[...])


def _mlp(x, w1_bf, w2_bf, layer, ln_g, ln_b, tm=512, tf=1024):
    L = x.shape[0]
    return pl.pallas_call(
        _mlp_body,
        grid=(L // tm, D_FF // tf),
        in_specs=[
            pl.BlockSpec((tm, D_MODEL), lambda i, f: (i, 0)),
            pl.BlockSpec((None, D_MODEL, tf), lambda i, f: (layer, 0, f)),
            pl.BlockSpec((None, tf, D_MODEL), lambda i, f: (layer, f, 0)),
            pl.BlockSpec((1, D_MODEL), lambda i, f: (0, 0)),
            pl.BlockSpec((1, D_MODEL), lambda i, f: (0, 0)),
        ],
        out_specs=pl.BlockSpec((tm, D_MODEL), lambda i, f: (i, 0)),
        out_shape=jax.ShapeDtypeStruct((L, D_MODEL), F32),
        scratch_shapes=[pltpu.VMEM((tm, D_MODEL), BF16), pltpu.VMEM((tm, D_MODEL), F32)],
        compiler_params=_params(("parallel", "arbitrary")),
        name="mlp",
    )(x, w1_bf, w2_bf, ln_g, ln_b)


def _rotary_tables(L, rot_dim, theta):
    half = rot_dim // 2
    inv_freq = 1.0 / (theta ** (jnp.arange(half, dtype=F32) / half))
    ang = jnp.arange(L).astype(F32)[:, None] * inv_freq[None, :]
    return jnp.cos(ang), jnp.sin(ang)


def _even_layer(x, w_in, w_out, lam_re, lam_im, log_step, b_re, b_im, c_re, c_im,
                d_skip, w_glu, b_glu, ret_log_decay, ln_g, ln_b):
    L = x.shape[0]
    cos, sin = _rotary_tables(L, RET_HEAD_DIM, RET_ROPE_THETA)
    u_g, h5 = _inproj_even(x, w_in.astype(BF16), cos, sin)
    disc = _s5_disc(lam_re, lam_im, log_step)
    m, ws_t, wc = _s5_gen(disc, b_re, b_im, c_re, c_im)
    d_tile = jnp.tile(d_skip.astype(F32), (1, S5_T)).reshape(S5_GROUPS, 1, S5_TL)
    a1, a2, a3 = (disc[k].reshape(2, 1, S5_GROUPS * 128) for k in (DISC_AT_RE, DISC_SCAN_A2, DISC_SCAN_A3))
    y = _s5_mix(u_g, m, ws_t, wc, d_tile, a1, a2, a3)
    lg = ret_log_decay.astype(F32)
    sb = _ret_bstate(lg, h5)
    ret = _ret_main(lg, h5, sb)
    return _outproj_even(y, ret, x, w_glu.astype(BF16), b_glu.astype(F32).reshape(1, -1),
                         w_out.astype(BF16), ln_g.reshape(1, -1), ln_b.reshape(1, -1))


def _odd_layer(x, w_in, w_out, sink, ln_g, ln_b):
    L = x.shape[0]
    cos, sin = _rotary_tables(L, ROPE_DIM, ROPE_THETA)
    half = ROPE_DIM // 2
    pad = ATT_HEAD_DIM - ROPE_DIM
    cc = jnp.concatenate([cos, cos, jnp.ones((L, pad), F32)], axis=1)
    s1 = jnp.concatenate([-sin, jnp.zeros((L, ATT_HEAD_DIM - half), F32)], axis=1)
    s2 = jnp.concatenate([jnp.zeros((L, half), F32), sin, jnp.zeros((L, pad), F32)], axis=1)
    qkv = _inproj_odd(x, w_in.astype(BF16), cc, s1, s2)
    att = _attention(sink.astype(F32), qkv)
    return _outproj_odd(att, x, w_out.astype(BF16), ln_g.reshape(1, -1), ln_b.reshape(1, -1))


def kernel(x, ln_g, ln_b, mlp_w1, mlp_w2, even_w_in, even_w_out, s5_lambda_re, s5_lambda_im, s5_log_step, s5_b_re, s5_b_im, s5_c_re, s5_c_im, s5_d, s5_w_glu, s5_b_glu, ret_log_decay, odd_w_in, odd_w_out, attn_sink):
    bsz = x.shape[0]
    w1_bf = mlp_w1.astype(BF16)
    w2_bf = mlp_w2.astype(BF16)
    outs = []
    for b in range(bsz):
        xb = x[b]
        for layer in range(DEPTH):
            if layer % 2 == 0:
                e = layer // 2
                xb = _even_layer(xb, even_w_in[e], even_w_out[e], s5_lambda_re[e], s5_lambda_im[e],
                                 s5_log_step[e], s5_b_re[e], s5_b_im[e], s5_c_re[e], s5_c_im[e],
                                 s5_d[e], s5_w_glu[e], s5_b_glu[e], ret_log_decay[e],
                                 ln_g[layer, 0], ln_b[layer, 0])
            else:
                o = layer // 2
                xb = _odd_layer(xb, odd_w_in[o], odd_w_out[o], attn_sink[o], ln_g[layer, 0], ln_b[layer, 0])
            xb = _mlp(xb, w1_bf, w2_bf, layer,
                      ln_g[layer, 1].reshape(1, -1), ln_b[layer, 1].reshape(1, -1))
        outs.append(xb)
    return jnp.stack(outs, axis=0)
```

```python
import functools
import math

import jax
import jax.numpy as jnp
from jax import lax
from jax.experimental import pallas as pl
from jax.experimental.pallas import tpu as pltpu

F32 = jnp.float32
BF16 = jnp.bfloat16

D_MODEL = 2048
DEPTH = 2
S5_WIDTH = 1024
S5_GROUP = 16
S5_GROUPS = 64
S5_STATE = 64
RET_WIDTH = 1024
RET_HEADS = 4
RET_HEAD_DIM = 256
RET_ROPE_THETA = 10000.0
ATT_HEADS = 16
ATT_KV_HEADS = 4
ATT_HEAD_DIM = 128
ATT_GROUP = 4
ATT_WINDOW = 128
ATT_BLOCK = 128
ROPE_THETA = 500000.0
ROPE_DIM = 32
D_FF = 4 * D_MODEL
DEEPNORM_ALPHA = (2 * DEPTH) ** 0.25
LN_EPS = 1e-5
HEAD_NORM_EPS = 1e-6
NEG_INF = -1e30

V7X_VMEM_BYTES = 64 * 1024 * 1024
VMEM_LIMIT = V7X_VMEM_BYTES - 8 * 1024 * 1024

S5_T = 32
S5_TL = S5_T * S5_GROUP
S5_GB = 8
S5_GEN_GB = 8
RET_C = 256
LN_ROWS = 256


def _params(sem):
    return pltpu.CompilerParams(dimension_semantics=sem, vmem_limit_bytes=VMEM_LIMIT)


def _layer_norm_rows(y, g, b):
    mu = jnp.mean(y, axis=-1, keepdims=True)
    yc = y - mu
    var = jnp.mean(yc * yc, axis=-1, keepdims=True)
    return yc * lax.rsqrt(var + LN_EPS) * g + b


def _block_transpose8(xs):
    blk = lax.shift_right_logical(lax.broadcasted_iota(jnp.int32, xs[0].shape, 1), 4)
    xs = list(xs)
    for k in range(3):
        d = 1 << k
        upper = (blk & d) != 0
        for i in range(8):
            if i & d:
                continue
            a, b = xs[i], xs[i + d]
            xs[i] = jnp.where(upper, pltpu.roll(b, S5_GROUP * d, axis=1), a)
            xs[i + d] = jnp.where(upper, b, pltpu.roll(a, 128 - S5_GROUP * d, axis=1))
    return xs


SEG_U, SEG_Q, SEG_K, SEG_V, SEG_GATE = range(5)
ROT_Q, ROT_K, ROT_NONE = range(3)


def _inproj_even_body(x_ref, w_ref, t1_ref, t2_ref, o_ref, xb_ref):
    j = pl.program_id(1)

    @pl.when(j == 0)
    def _():
        xb_ref[...] = x_ref[...].astype(BF16)

    acc = jnp.dot(xb_ref[...], w_ref[...], preferred_element_type=F32)
    t1 = t1_ref[...]
    t2 = t2_ref[...]
    is_gate = j == SEG_GATE
    half = RET_HEAD_DIM // 2
    for hh in range(RET_HEADS):
        lo = hh * RET_HEAD_DIM
        a = acc[:, lo:lo + half]
        b = acc[:, lo + half:lo + RET_HEAD_DIM]
        for off, r in ((0, a * t1 - b * t2), (half, b * t1 + a * t2)):
            r = jnp.where(is_gate, r * jax.nn.sigmoid(r), r)
            o_ref[:, lo + off:lo + off + half] = r.astype(BF16)


def _inproj_even(x, w_in_bf, t1, t2, tm=1024):
    L = x.shape[0]
    tn = S5_WIDTH
    nseg = w_in_bf.shape[1] // tn
    kind = lambda j: jnp.where(j == SEG_Q, ROT_Q, jnp.where(j == SEG_K, ROT_K, ROT_NONE))
    tab = pl.BlockSpec((None, tm, RET_HEAD_DIM // 2), lambda i, j: (kind(j), i, 0))
    return pl.pallas_call(
        _inproj_even_body,
        grid=(L // tm, nseg),
        in_specs=[
            pl.BlockSpec((tm, D_MODEL), lambda i, j: (i, 0)),
            pl.BlockSpec((D_MODEL, tn), lambda i, j: (0, j)),
            tab, tab,
        ],
        out_specs=pl.BlockSpec((None, tm, tn), lambda i, j: (j, i, 0)),
        out_shape=jax.ShapeDtypeStruct((nseg, L, tn), BF16),
        scratch_shapes=[pltpu.VMEM((tm, D_MODEL), BF16)],
        compiler_params=_params(("parallel", "arbitrary")),
        name="inproj_even",
    )(x, w_in_bf, t1, t2)


def _cmul(ar, ai, br, bi):
    return ar * br - ai * bi, ar * bi + ai * br


(DISC_A1_RE, DISC_A1_IM, DISC_A2_RE, DISC_A2_IM, DISC_A4_RE, DISC_A4_IM, DISC_A8_RE, DISC_A8_IM,
 DISC_A16_RE, DISC_A16_IM, DISC_AT_RE, DISC_AT_IM, DISC_Z_RE, DISC_Z_IM, DISC_SCAN_A2, DISC_SCAN_A3) = range(16)


def _s5_disc_body(lr_ref, li_ref, ls_ref, sg_ref, o_ref):
    lr = jnp.minimum(lr_ref[...], -1e-4)
    li = li_ref[...]
    step = jnp.exp(ls_ref[...])
    mag = jnp.exp(lr * step)
    ar = mag * jnp.cos(li * step)
    ai = mag * jnp.sin(li * step)
    nr, ni = ar - 1.0, ai
    den = lr * lr + li * li
    o_ref[DISC_Z_RE] = (nr * lr + ni * li) / den
    o_ref[DISC_Z_IM] = (ni * lr - nr * li) / den
    pr, pi = ar, ai
    for k in range(6):
        o_ref[2 * k] = pr
        o_ref[2 * k + 1] = pi
        if k < 5:
            pr, pi = _cmul(pr, pi, pr, pi)
    o_ref[DISC_SCAN_A2] = pi * sg_ref[...]
    o_ref[DISC_SCAN_A3] = -pi * sg_ref[...]


def _s5_disc(lam_re, lam_im, log_step):
    assert S5_T == 32
    rows = 2 * S5_GROUPS
    two = lambda a: jnp.tile(a.astype(F32).reshape(rows, -1), (1, 2))
    lr = two(lam_re)
    li = two(lam_im)
    ls = jnp.broadcast_to(log_step.astype(F32).reshape(rows, 1), (rows, 2 * S5_STATE))
    sg = jnp.broadcast_to(jnp.concatenate([-jnp.ones((S5_STATE,), F32), jnp.ones((S5_STATE,), F32)])[None],
                          (rows, 2 * S5_STATE))
    return pl.pallas_call(
        _s5_disc_body,
        out_shape=jax.ShapeDtypeStruct((16, rows, 2 * S5_STATE), F32),
        name="s5_disc",
    )(lr, li, ls, sg)


def _s5_gen_body(col_ref, zrow_ref, bt_ref, btile_ref, ctile_ref, m_ref, ws_ref, wc_ref):
    P = S5_STATE
    hi = lax.Precision.HIGHEST
    tlo = lax.shift_right_logical(lax.broadcasted_iota(jnp.int32, (P, 128), 1), 4)
    lane = lax.broadcasted_iota(jnp.int32, (S5_GROUP, S5_TL), 1)
    ones = jnp.ones((P, 128), F32)
    zeros = jnp.zeros((P, 128), F32)

    def one_group(gi, carry):
        kt = []
        for d in range(2):
            col = col_ref[gi, d]
            c = lambda k: jnp.broadcast_to(col[:, k:k + 1], (P, 128))
            a1 = (c(DISC_A1_RE), c(DISC_A1_IM))
            a2 = (c(DISC_A2_RE), c(DISC_A2_IM))
            a4 = (c(DISC_A4_RE), c(DISC_A4_IM))
            a8 = (c(DISC_A8_RE), c(DISC_A8_IM))
            a16 = (c(DISC_A16_RE), c(DISC_A16_IM))
            blk = [None, a8, a16, _cmul(*a8, *a16)]

            def low_powers(reverse):
                xr, xi = ones, zeros
                for k, ak in enumerate((a1, a2, a4)):
                    bit = (lax.shift_right_logical(tlo, k) & 1) == (0 if reverse else 1)
                    yr, yi = _cmul(xr, xi, *ak)
                    xr = jnp.where(bit, yr, xr)
                    xi = jnp.where(bit, yi, xi)
                return xr, xi

            def expand(base, reverse):
                out = []
                for j in range(4):
                    f = blk[3 - j] if reverse else blk[j]
                    out.append(base if f is None else _cmul(*base, *f))
                return out

            ct = (ctile_ref[gi, d, 0], ctile_ref[gi, d, 1])
            bbar = _cmul(c(DISC_Z_RE), c(DISC_Z_IM), btile_ref[gi, d, 0], btile_ref[gi, d, 1])
            zrow = zrow_ref[gi, d]
            zr_row, zi_row = zrow[0:1, :], zrow[1:2, :]
            bbt_r, bbt_i = _cmul(zr_row, zi_row, bt_ref[gi, d, 0], bt_ref[gi, d, 1])
            ca = expand(_cmul(*ct, *low_powers(d == 1)), d == 1)
            ba = expand(_cmul(*bbar, *low_powers(d == 0)), d == 0)
            wcj = [_cmul(*x, *a1) for x in ca]
            cat = lambda parts, k: jnp.concatenate([x[k] for x in parts], axis=1)
            kt.append(jnp.dot(bbt_r, cat(ca, 0), precision=hi, preferred_element_type=F32)
                      - jnp.dot(bbt_i, cat(ca, 1), precision=hi, preferred_element_type=F32))
            ba_r, ba_i = cat(ba, 0).astype(BF16), cat(ba, 1).astype(BF16)
            for r, part in enumerate((ba_r, ba_i, ba_i, ba_r)):
                ws_ref[gi, (4 * d + r) * P:(4 * d + r + 1) * P, :] = part
            wc_ref[gi, 2 * d * P:(2 * d + 1) * P, :] = cat(wcj, 0).astype(BF16)
            wc_ref[gi, (2 * d + 1) * P:(2 * d + 2) * P, :] = (-cat(wcj, 1)).astype(BF16)
        ktf, ktb = kt
        for s in range(S5_T):
            lo, hi_lane = S5_GROUP * s, S5_GROUP * (s + 1)
            f = ktf if s == 0 else jnp.where(lane >= lo, pltpu.roll(ktf, lo, axis=1), 0.0)
            b = ktb if s == S5_T - 1 else jnp.where(lane < hi_lane, pltpu.roll(ktb, hi_lane, axis=1), 0.0)
            m_ref[gi, lo:hi_lane, :] = (f + b).astype(BF16)
        return carry

    lax.fori_loop(0, col_ref.shape[0], one_group, 0)


def _s5_gen(disc, b_re, b_im, c_re, c_im):
    G, P, Cg = S5_GROUPS, S5_STATE, S5_GROUP
    gb = S5_GEN_GB
    d4 = disc[:, :, :P].reshape(16, 2, G, P)
    col = d4.transpose(2, 1, 3, 0)
    zrow = d4[DISC_Z_RE:DISC_Z_IM + 1].transpose(2, 1, 0, 3)
    b = jnp.stack([b_re, b_im], axis=1).astype(F32)
    c = jnp.stack([c_re, c_im], axis=1).astype(F32)
    bt = b.transpose(2, 0, 1, 4, 3)
    btile = jnp.tile(b.transpose(2, 0, 1, 3, 4), (1, 1, 1, 1, 128 // Cg))
    ctile = jnp.tile(c.transpose(2, 0, 1, 4, 3), (1, 1, 1, 1, 128 // Cg))
    spec = lambda a: pl.BlockSpec((gb,) + a.shape[1:], lambda i: (i,) + (0,) * (a.ndim - 1))
    out = lambda rows: pl.BlockSpec((gb, rows, S5_TL), lambda i: (i, 0, 0))
    return pl.pallas_call(
        _s5_gen_body,
        grid=(G // gb,),
        in_specs=[spec(col), spec(zrow), spec(bt), spec(btile), spec(ctile)],
        out_specs=[out(S5_TL), out(8 * P), out(4 * P)],
        out_shape=[jax.ShapeDtypeStruct((G, S5_TL, S5_TL), BF16),
                   jax.ShapeDtypeStruct((G, 8 * P, S5_TL), BF16),
                   jax.ShapeDtypeStruct((G, 4 * P, S5_TL), BF16)],
        compiler_params=_params(("parallel",)),
        name="s5_gen",
    )(col, zrow, bt, btile, ctile)


def _s5_body(*refs):
    ut_refs = refs[:S5_T]
    (m_ref, ws_ref, wc_ref, dt_ref, a1_ref, a2_ref, a3_ref, y_ref,
     s_ref, h_ref, yg_ref, yt_ref, u_ref) = refs[S5_T:]
    nc = u_ref.shape[1]
    gb = u_ref.shape[0]
    w = gb * 128
    for jb in range(S5_T // 8):
        ys = _block_transpose8([ut_refs[8 * jb + t8][...].astype(F32) for t8 in range(8)])
        for g8 in range(8):
            u_ref[g8, :, jb * 128:(jb + 1) * 128] = ys[g8].astype(BF16)
    for gi in range(gb):
        s = lax.dot_general(u_ref[gi], ws_ref[gi], (((1,), (1,)), ((), ())),
                            preferred_element_type=F32)
        for r in range(4):
            s_ref[:, r * w + gi * 128:r * w + (gi + 1) * 128] = s[:, r * 128:(r + 1) * 128]

    a1f, a2f, a3f = a1_ref[0], a2_ref[0], a3_ref[0]
    a1b, a2b, a3b = a1_ref[1], a2_ref[1], a3_ref[1]

    def step(n, carry):
        hf, gf, hb, gb_ = carry
        m = nc - 1 - n
        h_ref[pl.ds(n, 1), 0:w] = hf
        h_ref[pl.ds(m, 1), w:2 * w] = hb
        sfh = s_ref[pl.ds(n, 1), 0:w]
        sfg = s_ref[pl.ds(n, 1), w:2 * w]
        sbh = s_ref[pl.ds(m, 1), 2 * w:3 * w]
        sbg = s_ref[pl.ds(m, 1), 3 * w:4 * w]
        hf2 = a1f * hf + a2f * gf + sfh
        gf2 = a1f * gf + a3f * hf + sfg
        hb2 = a1b * hb + a2b * gb_ + sbh
        gb2 = a1b * gb_ + a3b * hb + sbg
        return hf2, gf2, hb2, gb2

    z = jnp.zeros((1, w), F32)
    lax.fori_loop(0, nc, step, (z, z, z, z))

    for gi in range(gb):
        u = u_ref[gi]
        hcat = jnp.concatenate([h_ref[:, gi * 128:(gi + 1) * 128],
                                h_ref[:, w + gi * 128:w + (gi + 1) * 128]], axis=1).astype(BF16)
        y = (jnp.dot(u, m_ref[gi], preferred_element_type=F32)
             + jnp.dot(hcat, wc_ref[gi], preferred_element_type=F32)
             + dt_ref[gi] * u.astype(F32))
        yg_ref[gi] = jax.nn.gelu(y)

    for jb in range(S5_T // 8):
        zs = _block_transpose8([yg_ref[g8, :, jb * 128:(jb + 1) * 128] for g8 in range(8)])
        for t8 in range(8):
            yt_ref[pl.ds(8 * jb + t8, nc, stride=S5_T), :] = zs[t8]
    y_ref[...] = yt_ref[...].astype(BF16)


def _s5_mix(h5, m, ws_t, wc, d_tile, a1, a2, a3):
    L = h5.shape[1]
    G, TL = S5_GROUPS, S5_TL
    nc = L // S5_T
    gb = S5_GB
    assert gb * S5_GROUP == 128
    w = gb * 128
    nblk = (G * S5_GROUP) // 128
    h5_steps = h5.reshape(h5.shape[0], nc, S5_T * G * S5_GROUP)
    step_spec = lambda t: pl.BlockSpec((None, nc, 128), lambda i: (SEG_U, 0, t * nblk + i))
    a_spec = pl.BlockSpec((2, 1, w), lambda i: (0, 0, i))
    return pl.pallas_call(
        _s5_body,
        grid=(G // gb,),
        in_specs=[step_spec(t) for t in range(S5_T)] + [
            pl.BlockSpec((gb, TL, TL), lambda i: (i, 0, 0)),
            pl.BlockSpec((gb, 512, TL), lambda i: (i, 0, 0)),
            pl.BlockSpec((gb, 256, TL), lambda i: (i, 0, 0)),
            pl.BlockSpec((gb, 1, TL), lambda i: (i, 0, 0)),
            a_spec, a_spec, a_spec,
        ],
        out_specs=pl.BlockSpec((L, 128), lambda i: (0, i)),
        out_shape=jax.ShapeDtypeStruct((L, G * S5_GROUP), BF16),
        scratch_shapes=[pltpu.VMEM((nc, 4 * w), F32), pltpu.VMEM((nc, 2 * w), F32),
                        pltpu.VMEM((gb, nc, TL), F32), pltpu.VMEM((L, 128), F32),
                        pltpu.VMEM((gb, nc, TL), BF16)],
        compiler_params=_params(("parallel",)),
        name="s5_mix",
    )(*([h5_steps] * S5_T), m, ws_t, wc, d_tile, a1, a2, a3)


def _row_index(n):
    return lax.broadcasted_iota(jnp.int32, (n, 1), 0).astype(F32)


def _log_decay(lg_ref, d, h):
    return -jnp.abs(jnp.full((1, 1), lg_ref[d, h], F32))


def _ret_bstate_body(lg_ref, k_ref, v_ref, sb_ref, st_ref):
    C = k_ref.shape[0]

    @pl.when(pl.program_id(0) == 0)
    def _():
        st_ref[...] = jnp.zeros_like(st_ref)

    jj = _row_index(C)
    for h in range(RET_HEADS):
        lo = h * RET_HEAD_DIM
        lgb = _log_decay(lg_ref, 1, h)
        sb_ref[h] = st_ref[h].astype(BF16)
        kd = (k_ref[:, lo:lo + RET_HEAD_DIM].astype(F32) * jnp.exp(jj * lgb)).astype(BF16)
        kv = lax.dot_general(kd, v_ref[:, lo:lo + RET_HEAD_DIM], (((0,), (0,)), ((), ())),
                             preferred_element_type=F32)
        st_ref[h] = st_ref[h] * jnp.exp(C * lgb) + kv


def _ret_bstate(lg, h5):
    L = h5.shape[1]
    C = RET_C
    nc = L // C
    return pl.pallas_call(
        _ret_bstate_body,
        grid=(nc,),
        in_specs=[
            pl.BlockSpec(memory_space=pltpu.SMEM),
            pl.BlockSpec((None, C, RET_WIDTH), lambda i: (SEG_K, nc - 1 - i, 0)),
            pl.BlockSpec((None, C, RET_WIDTH), lambda i: (SEG_V, nc - 1 - i, 0)),
        ],
        out_specs=pl.BlockSpec((None, RET_HEADS, RET_HEAD_DIM, RET_HEAD_DIM),
                               lambda i: (nc - 1 - i, 0, 0, 0)),
        out_shape=jax.ShapeDtypeStruct((nc, RET_HEADS, RET_HEAD_DIM, RET_HEAD_DIM), BF16),
        scratch_shapes=[pltpu.VMEM((RET_HEADS, RET_HEAD_DIM, RET_HEAD_DIM), F32)],
        compiler_params=_params(("arbitrary",)),
        name="ret_bstate",
    )(lg, h5, h5)


def _ret_main_body(lg_ref, q_ref, k_ref, v_ref, g_ref, sb_ref, o_ref, st_ref):
    C = q_ref.shape[0]

    @pl.when(pl.program_id(0) == 0)
    def _():
        st_ref[...] = jnp.zeros_like(st_ref)

    ii = _row_index(C)
    diff = (lax.broadcasted_iota(jnp.int32, (C, C), 0)
            - lax.broadcasted_iota(jnp.int32, (C, C), 1)).astype(F32)
    for h in range(RET_HEADS):
        lo = h * RET_HEAD_DIM
        lgf = _log_decay(lg_ref, 0, h)
        lgb = _log_decay(lg_ref, 1, h)
        q = q_ref[:, lo:lo + RET_HEAD_DIM]
        k = k_ref[:, lo:lo + RET_HEAD_DIM]
        v = v_ref[:, lo:lo + RET_HEAD_DIM]
        qf = q.astype(F32)
        s = lax.dot_general(q, k, (((1,), (1,)), ((), ())), preferred_element_type=F32)
        decay = jnp.where(diff >= 0, jnp.exp(lgf * jnp.maximum(diff, 0.0)),
                          jnp.exp(lgb * jnp.maximum(-diff, 0.0)))
        o = jnp.dot((s * decay).astype(BF16), v, preferred_element_type=F32)
        qdf = (qf * jnp.exp((ii + 1.0) * lgf)).astype(BF16)
        o = o + jnp.dot(qdf, st_ref[h].astype(BF16), preferred_element_type=F32)
        qdb = (qf * jnp.exp((C - ii) * lgb)).astype(BF16)
        o = o + jnp.dot(qdb, sb_ref[h], preferred_element_type=F32)
        mu = jnp.mean(o, axis=-1, keepdims=True)
        oc = o - mu
        var = jnp.mean(oc * oc, axis=-1, keepdims=True)
        on = oc * lax.rsqrt(var + HEAD_NORM_EPS)
        o_ref[:, lo:lo + RET_HEAD_DIM] = (on * g_ref[:, lo:lo + RET_HEAD_DIM].astype(F32)).astype(BF16)
        kd = (k.astype(F32) * jnp.exp((C - 1.0 - ii) * lgf)).astype(BF16)
        kv = lax.dot_general(kd, v, (((0,), (0,)), ((), ())), preferred_element_type=F32)
        st_ref[h] = st_ref[h] * jnp.exp(C * lgf) + kv


def _ret_main(lg, h5, sb):
    L = h5.shape[1]
    C = RET_C
    nc = L // C
    seg = lambda s: pl.BlockSpec((None, C, RET_WIDTH), lambda i: (s, i, 0))
    return pl.pallas_call(
        _ret_main_body,
        grid=(nc,),
        in_specs=[
            pl.BlockSpec(memory_space=pltpu.SMEM),
            seg(SEG_Q), seg(SEG_K), seg(SEG_V), seg(SEG_GATE),
            pl.BlockSpec((None, RET_HEADS, RET_HEAD_DIM, RET_HEAD_DIM), lambda i: (i, 0, 0, 0)),
        ],
        out_specs=pl.BlockSpec((C, RET_WIDTH), lambda i: (i, 0)),
        out_shape=jax.ShapeDtypeStruct((L, RET_WIDTH), BF16),
        scratch_shapes=[pltpu.VMEM((RET_HEADS, RET_HEAD_DIM, RET_HEAD_DIM), F32)],
        compiler_params=_params(("arbitrary",)),
        name="ret_main",
    )(lg, h5, h5, h5, h5, sb)


def _outproj_even_body(y_ref, r_ref, x_ref, wg_ref, bg_ref, wo_ref, g_ref, b_ref, o_ref):
    for r in range(0, y_ref.shape[0], LN_ROWS):
        rows = slice(r, r + LN_ROWS)
        y = y_ref[rows, :]
        z = jnp.dot(y, wg_ref[...], preferred_element_type=F32) + bg_ref[...]
        s5 = (y.astype(F32) * jax.nn.sigmoid(z)).astype(BF16)
        mix = (jnp.dot(s5, wo_ref[0:S5_WIDTH, :], preferred_element_type=F32)
               + jnp.dot(r_ref[rows, :], wo_ref[S5_WIDTH:S5_WIDTH + RET_WIDTH, :],
                         preferred_element_type=F32))
        o_ref[rows, :] = _layer_norm_rows(DEEPNORM_ALPHA * x_ref[rows, :] + mix, g_ref[...], b_ref[...])


def _outproj_even(y, ret, x, w_glu_bf, b_glu, w_out_bf, ln_g, ln_b, tm=512):
    L = x.shape[0]
    row = lambda n: pl.BlockSpec((tm, n), lambda i: (i, 0))
    full = lambda a: pl.BlockSpec(a.shape, lambda i: (0,) * a.ndim)
    return pl.pallas_call(
        _outproj_even_body,
        grid=(L // tm,),
        in_specs=[row(S5_WIDTH), row(RET_WIDTH), row(D_MODEL), full(w_glu_bf), full(b_glu),
                  full(w_out_bf), full(ln_g), full(ln_b)],
        out_specs=row(D_MODEL),
        out_shape=jax.ShapeDtypeStruct((L, D_MODEL), F32),
        compiler_params=_params(("parallel",)),
        name="outproj_even",
    )(y, ret, x, w_glu_bf, b_glu, w_out_bf, ln_g, ln_b)


def _inproj_odd_body(x_ref, w_ref, c_ref, s1_ref, s2_ref, o_ref, xb_ref):
    j = pl.program_id(1)

    @pl.when(j == 0)
    def _():
        xb_ref[...] = x_ref[...].astype(BF16)

    acc = jnp.dot(xb_ref[...], w_ref[...], preferred_element_type=F32)
    cc = c_ref[...]
    s1 = s1_ref[...]
    s2 = s2_ref[...]
    for hh in range(acc.shape[1] // ATT_HEAD_DIM):
        lo = hh * ATT_HEAD_DIM
        a = acc[:, lo:lo + ATT_HEAD_DIM]
        up = pltpu.roll(a, ATT_HEAD_DIM - ROPE_DIM // 2, axis=1)
        dn = pltpu.roll(a, ROPE_DIM // 2, axis=1)
        o_ref[:, lo:lo + ATT_HEAD_DIM] = (a * cc + up * s1 + dn * s2).astype(BF16)


def _inproj_odd(x, w_in_bf, cc, s1, s2, tm=1024, tn=512):
    L = x.shape[0]
    n_out = w_in_bf.shape[1]
    assert tn == ATT_KV_HEADS * ATT_HEAD_DIM
    nq = (ATT_HEADS * ATT_HEAD_DIM) // tn
    tab = pl.BlockSpec((None, tm, ATT_HEAD_DIM), lambda i, j: (jnp.clip(j - (nq - 1), 0, 2), i, 0))
    return pl.pallas_call(
        _inproj_odd_body,
        grid=(L // tm, n_out // tn),
        in_specs=[
            pl.BlockSpec((tm, D_MODEL), lambda i, j: (i, 0)),
            pl.BlockSpec((D_MODEL, tn), lambda i, j: (0, j)),
            tab, tab, tab,
        ],
        out_specs=pl.BlockSpec((tm, tn), lambda i, j: (i, j)),
        out_shape=jax.ShapeDtypeStruct((L, n_out), BF16),
        scratch_shapes=[pltpu.VMEM((tm, D_MODEL), BF16)],
        compiler_params=_params(("parallel", "arbitrary")),
        name="inproj_odd",
    )(x, w_in_bf, cc, s1, s2)


def _attn_body(sink_ref, q_ref, kp_ref, kc_ref, kn_ref, vp_ref, vc_ref, vn_ref, o_ref):
    c = pl.program_id(0)
    nb = pl.num_programs(0)
    B = ATT_BLOCK
    hd = ATT_HEAD_DIM
    rows = ATT_GROUP * B
    r_i = lax.broadcasted_iota(jnp.int32, (rows, 3 * B), 0)
    s_i = lax.broadcasted_iota(jnp.int32, (rows, 3 * B), 1)
    rel = (r_i & (B - 1)) - s_i + B
    key_lo = jnp.where(c > 0, 0, B)
    key_hi = jnp.where(c < nb - 1, 3 * B, 2 * B)
    ok = (jnp.abs(rel) <= ATT_WINDOW) & (s_i >= key_lo) & (s_i < key_hi)
    head_of_row = lax.shift_right_logical(lax.broadcasted_iota(jnp.int32, (rows, 1), 0),
                                          int(math.log2(B)))
    for g in range(ATT_KV_HEADS):
        q = jnp.concatenate([q_ref[:, (g * ATT_GROUP + hh) * hd:(g * ATT_GROUP + hh + 1) * hd]
                             for hh in range(ATT_GROUP)], axis=0)
        ksl = slice(g * hd, (g + 1) * hd)
        k = jnp.concatenate([kp_ref[:, ksl], kc_ref[:, ksl], kn_ref[:, ksl]], axis=0)
        v = jnp.concatenate([vp_ref[:, ksl], vc_ref[:, ksl], vn_ref[:, ksl]], axis=0)
        s = lax.dot_general(q, k, (((1,), (1,)), ((), ())), preferred_element_type=F32)
        s = jnp.where(ok, s, NEG_INF)
        sink = jnp.zeros((rows, 1), F32)
        for hh in range(ATT_GROUP):
            sink = jnp.where(head_of_row == hh, sink_ref[g * ATT_GROUP + hh], sink)
        m = jnp.maximum(jnp.max(s, axis=-1, keepdims=True), sink)
        p = jnp.exp(s - m)
        den = jnp.sum(p, axis=-1, keepdims=True) + jnp.exp(sink - m)
        o = jnp.dot(p.astype(BF16), v, preferred_element_type=F32) / den
        for hh in range(ATT_GROUP):
            hcol = (g * ATT_GROUP + hh) * hd
            o_ref[:, hcol:hcol + hd] = o[hh * B:(hh + 1) * B, :].astype(BF16)


def _attention(sink, qkv):
    L = qkv.shape[0]
    B = ATT_BLOCK
    nb = L // B
    kvw = ATT_KV_HEADS * ATT_HEAD_DIM
    qw = ATT_HEADS * ATT_HEAD_DIM
    kcol = qw // kvw
    vcol = kcol + 1

    def kv_spec(col, off):
        return pl.BlockSpec((B, kvw), lambda i: (jnp.clip(i + off, 0, nb - 1), col))

    return pl.pallas_call(
        _attn_body,
        grid=(nb,),
        in_specs=[
            pl.BlockSpec(memory_space=pltpu.SMEM),
            pl.BlockSpec((B, qw), lambda i: (i, 0)),
            kv_spec(kcol, -1), kv_spec(kcol, 0), kv_spec(kcol, 1),
            kv_spec(vcol, -1), kv_spec(vcol, 0), kv_spec(vcol, 1),
        ],
        out_specs=pl.BlockSpec((B, qw), lambda i: (i, 0)),
        out_shape=jax.ShapeDtypeStruct((L, qw), BF16),
        compiler_params=_params(("parallel",)),
        name="attention",
    )(sink, qkv, qkv, qkv, qkv, qkv, qkv, qkv)


def _outproj_odd_body(a_ref, x_ref, wo_ref, g_ref, b_ref, o_ref):
    for r in range(0, a_ref.shape[0], LN_ROWS):
        rows = slice(r, r + LN_ROWS)
        mix = jnp.dot(a_ref[rows, :], wo_ref[...], preferred_element_type=F32)
        o_ref[rows, :] = _layer_norm_rows(DEEPNORM_ALPHA * x_ref[rows, :] + mix, g_ref[...], b_ref[...])


def _outproj_odd(a, x, w_out_bf, ln_g, ln_b, tm=512):
    L = x.shape[0]
    row = lambda n: pl.BlockSpec((tm, n), lambda i: (i, 0))
    full = lambda t: pl.BlockSpec(t.shape, lambda i: (0,) * t.ndim)
    return pl.pallas_call(
        _outproj_odd_body,
        grid=(L // tm,),
        in_specs=[row(a.shape[1]), row(D_MODEL), full(w_out_bf), full(ln_g), full(ln_b)],
        out_specs=row(D_MODEL),
        out_shape=jax.ShapeDtypeStruct((L, D_MODEL), F32),
        compiler_params=_params(("parallel",)),
        name="outproj_odd",
    )(a, x, w_out_bf, ln_g, ln_b)


def _mlp_body(x_ref, w1_ref, w2_ref, g_ref, b_ref, o_ref, xb_ref, acc_ref):
    f = pl.program_id(1)

    @pl.when(f == 0)
    def _():
        xb_ref[...] = x_ref[...].astype(BF16)
        acc_ref[...] = jnp.zeros_like(acc_ref)

    h = jnp.dot(xb_ref[...], w1_ref[...], preferred_element_type=F32)
    h = jnp.square(jnp.maximum(h, 0.0)).astype(BF16)
    acc_ref[...] += jnp.dot(h, w2_ref[...], preferred_element_type=F32)

    @pl.when(f == pl.num_programs(1) - 1)
    def _():
        tm = x_ref.shape[0]
        for r in range(0, tm, LN_ROWS):
            y = DEEPNORM_ALPHA * x_ref[r:r + LN_ROWS, :] + acc_ref[r:r + LN_ROWS, :]
            o_ref[r:r + LN_ROWS, :] = _layer_norm_rows(y, g_ref[...], b_ref[...])


def _mlp(x, w1_bf, w2_bf, layer, ln_g, ln_b, tm=512, tf=1024):
    L = x.shape[0]
    return pl.pallas_call(
        _mlp_body,
        grid=(L // tm, D_FF // tf),
        in_specs=[
            pl.BlockSpec((tm, D_MODEL), lambda i, f: (i, 0)),
            pl.BlockSpec((None, D_MODEL, tf), lambda i, f: (layer, 0, f)),
            pl.BlockSpec((None, tf, D_MODEL), lambda i, f: (layer, f, 0)),
            pl.BlockSpec((1, D_MODEL), lambda i, f: (0, 0)),
            pl.BlockSpec((1, D_MODEL), lambda i, f: (0, 0)),
        ],
        out_specs=pl.BlockSpec((tm, D_MODEL), lambda i, f: (i, 0)),
        out_shape=jax.ShapeDtypeStruct((L, D_MODEL), F32),
        scratch_shapes=[pltpu.VMEM((tm, D_MODEL), BF16), pltpu.VMEM((tm, D_MODEL), F32)],
        compiler_params=_params(("parallel", "arbitrary")),
        name="mlp",
    )(x, w1_bf, w2_bf, ln_g, ln_b)


def _rotary_tables(L, rot_dim, theta):
    half = rot_dim // 2
    inv_freq = 1.0 / (theta ** (jnp.arange(half, dtype=F32) / half))
    ang = jnp.arange(L).astype(F32)[:, None] * inv_freq[None, :]
    return jnp.cos(ang), jnp.sin(ang)


def _even_layer(x, w_in, w_out, lam_re, lam_im, log_step, b_re, b_im, c_re, c_im,
                d_skip, w_glu, b_glu, ret_log_decay, ln_g, ln_b):
    L = x.shape[0]
    cos, sin = _rotary_tables(L, RET_HEAD_DIM, RET_ROPE_THETA)
    k_scale = RET_HEAD_DIM ** -0.5
    t1 = jnp.stack([cos, cos * k_scale, jnp.ones_like(cos)])
    t2 = jnp.stack([sin, sin * k_scale, jnp.zeros_like(sin)])
    h5 = _inproj_even(x, w_in.astype(BF16), t1, t2)
    disc = _s5_disc(lam_re, lam_im, log_step)
    m, ws_t, wc = _s5_gen(disc, b_re, b_im, c_re, c_im)
    d_tile = jnp.tile(d_skip.astype(F32), (1, S5_T)).reshape(S5_GROUPS, 1, S5_TL)
    a1, a2, a3 = (disc[k].reshape(2, 1, S5_GROUPS * 128) for k in (DISC_AT_RE, DISC_SCAN_A2, DISC_SCAN_A3))
    y = _s5_mix(h5, m, ws_t, wc, d_tile, a1, a2, a3)
    lg = ret_log_decay.astype(F32)
    sb = _ret_bstate(lg, h5)
    ret = _ret_main(lg, h5, sb)
    return _outproj_even(y, ret, x, w_glu.astype(BF16), b_glu.astype(F32).reshape(1, -1),
                         w_out.astype(BF16), ln_g.reshape(1, -1), ln_b.reshape(1, -1))


def _odd_layer(x, w_in, w_out, sink, ln_g, ln_b):
    L = x.shape[0]
    cos, sin = _rotary_tables(L, ROPE_DIM, ROPE_THETA)
    half = ROPE_DIM // 2
    pad = ATT_HEAD_DIM - ROPE_DIM
    cc = jnp.concatenate([cos, cos, jnp.ones((L, pad), F32)], axis=1)
    s1 = jnp.concatenate([-sin, jnp.zeros((L, ATT_HEAD_DIM - half), F32)], axis=1)
    s2 = jnp.concatenate([jnp.zeros((L, half), F32), sin, jnp.zeros((L, pad), F32)], axis=1)
    q_scale = ATT_HEAD_DIM ** -0.5
    zero = jnp.zeros_like(cc)
    cc, s1, s2 = (jnp.stack([t * q_scale, t, ident])
                  for t, ident in ((cc, jnp.ones_like(cc)), (s1, zero), (s2, zero)))
    qkv = _inproj_odd(x, w_in.astype(BF16), cc, s1, s2)
    att = _attention(sink.astype(F32), qkv)
    return _outproj_odd(att, x, w_out.astype(BF16), ln_g.reshape(1, -1), ln_b.reshape(1, -1))


def kernel(x, ln_g, ln_b, mlp_w1, mlp_w2, even_w_in, even_w_out, s5_lambda_re, s5_lambda_im, s5_log_step, s5_b_re, s5_b_im, s5_c_re, s5_c_im, s5_d, s5_w_glu, s5_b_glu, ret_log_decay, odd_w_in, odd_w_out, attn_sink):
    bsz = x.shape[0]
    w1_bf = mlp_w1.astype(BF16)
    w2_bf = mlp_w2.astype(BF16)
    outs = []
    for b in range(bsz):
        xb = x[b]
        for layer in range(DEPTH):
            if layer % 2 == 0:
                e = layer // 2
                xb = _even_layer(xb, even_w_in[e], even_w_out[e], s5_lambda_re[e], s5_lambda_im[e],
                                 s5_log_step[e], s5_b_re[e], s5_b_im[e], s5_c_re[e], s5_c_im[e],
                                 s5_d[e], s5_w_glu[e], s5_b_glu[e], ret_log_decay[e],
                                 ln_g[layer, 0], ln_b[layer, 0])
            else:
                o = layer // 2
                xb = _odd_layer(xb, odd_w_in[o], odd_w_out[o], attn_sink[o], ln_g[layer, 0], ln_b[layer, 0])
            xb = _mlp(xb, w1_bf, w2_bf, layer,
                      ln_g[layer, 1].reshape(1, -1), ln_b[layer, 1].reshape(1, -1))
        outs.append(xb)
    return jnp.stack(outs, axis=0)
```

```python
import functools
import math

import jax
import jax.numpy as jnp
from jax import lax
from jax.experimental import pallas as pl
from jax.experimental.pallas import tpu as pltpu

F32 = jnp.float32
BF16 = jnp.bfloat16

D_MODEL = 2048
DEPTH = 2
S5_WIDTH = 1024
S5_GROUP = 16
S5_GROUPS = 64
S5_STATE = 64
RET_WIDTH = 1024
RET_HEADS = 4
RET_HEAD_DIM = 256
RET_ROPE_THETA = 10000.0
ATT_HEADS = 16
ATT_KV_HEADS = 4
ATT_HEAD_DIM = 128
ATT_GROUP = 4
ATT_WINDOW = 128
ATT_BLOCK = 128
ROPE_THETA = 500000.0
ROPE_DIM = 32
D_FF = 4 * D_MODEL
DEEPNORM_ALPHA = (2 * DEPTH) ** 0.25
LN_EPS = 1e-5
HEAD_NORM_EPS = 1e-6
NEG_INF = -1e30

V7X_VMEM_BYTES = 64 * 1024 * 1024
VMEM_LIMIT = V7X_VMEM_BYTES - 8 * 1024 * 1024

S5_T = 32
S5_TL = S5_T * S5_GROUP
S5_GB = 8
S5_GEN_GB = 8
RET_C = 256
LN_ROWS = 256


def _params(sem):
    return pltpu.CompilerParams(dimension_semantics=sem, vmem_limit_bytes=VMEM_LIMIT)


def _layer_norm_rows(y, g, b):
    mu = jnp.mean(y, axis=-1, keepdims=True)
    yc = y - mu
    var = jnp.mean(yc * yc, axis=-1, keepdims=True)
    return yc * lax.rsqrt(var + LN_EPS) * g + b


def _block_transpose8(xs):
    blk = lax.shift_right_logical(lax.broadcasted_iota(jnp.int32, xs[0].shape, 1), 4)
    xs = list(xs)
    for k in range(3):
        d = 1 << k
        upper = (blk & d) != 0
        for i in range(8):
            if i & d:
                continue
            a, b = xs[i], xs[i + d]
            xs[i] = jnp.where(upper, pltpu.roll(b, S5_GROUP * d, axis=1), a)
            xs[i + d] = jnp.where(upper, b, pltpu.roll(a, 128 - S5_GROUP * d, axis=1))
    return xs


SEG_U, SEG_Q, SEG_K, SEG_V, SEG_GATE = range(5)
ROT_Q, ROT_K, ROT_NONE = range(3)


def _row_permutation(n_rows, src_of_row):
    r = lax.broadcasted_iota(jnp.int32, (n_rows, n_rows), 0)
    c = lax.broadcasted_iota(jnp.int32, (n_rows, n_rows), 1)
    return jnp.where(c == src_of_row(r), 1.0, 0.0).astype(BF16)


def _inproj_even_body(x_ref, w_ref, t1_ref, t2_ref, u_ref, o_ref, xb_ref):
    j = pl.program_id(1)

    @pl.when(j == 0)
    def _():
        xb_ref[...] = x_ref[...].astype(BF16)
        u = jnp.dot(xb_ref[...], w_ref[...], preferred_element_type=F32).astype(BF16)
        tm = u.shape[0]
        nl = tm // S5_T
        shift = nl.bit_length() - 1
        perm = _row_permutation(tm, lambda r: (r & (nl - 1)) * S5_T + lax.shift_right_logical(r, shift))
        up = jnp.dot(perm, u, preferred_element_type=F32).astype(BF16)
        for t in range(S5_T):
            u_ref[t] = up[t * nl:(t + 1) * nl, :]

    @pl.when(j > 0)
    def _():
        acc = jnp.dot(xb_ref[...], w_ref[...], preferred_element_type=F32)
        t1 = t1_ref[...]
        t2 = t2_ref[...]
        is_gate = j == SEG_GATE
        half = RET_HEAD_DIM // 2
        for hh in range(RET_HEADS):
            lo = hh * RET_HEAD_DIM
            a = acc[:, lo:lo + half]
            b = acc[:, lo + half:lo + RET_HEAD_DIM]
            for off, r in ((0, a * t1 - b * t2), (half, b * t1 + a * t2)):
                r = jnp.where(is_gate, r * jax.nn.sigmoid(r), r)
                o_ref[:, lo + off:lo + off + half] = r.astype(BF16)


def _inproj_even(x, w_in_bf, t1, t2, tm=1024):
    L = x.shape[0]
    tn = S5_WIDTH
    nseg = w_in_bf.shape[1] // tn
    kind = lambda j: jnp.where(j == SEG_Q, ROT_Q, jnp.where(j == SEG_K, ROT_K, ROT_NONE))
    tab = pl.BlockSpec((None, tm, RET_HEAD_DIM // 2), lambda i, j: (kind(j), i, 0))
    return pl.pallas_call(
        _inproj_even_body,
        grid=(L // tm, nseg),
        in_specs=[
            pl.BlockSpec((tm, D_MODEL), lambda i, j: (i, 0)),
            pl.BlockSpec((D_MODEL, tn), lambda i, j: (0, j)),
            tab, tab,
        ],
        out_specs=[pl.BlockSpec((S5_T, tm // S5_T, tn), lambda i, j: (0, i, 0)),
                   pl.BlockSpec((None, tm, tn), lambda i, j: (jnp.maximum(j - 1, 0), i, 0))],
        out_shape=[jax.ShapeDtypeStruct((S5_T, L // S5_T, tn), BF16),
                   jax.ShapeDtypeStruct((nseg - 1, L, tn), BF16)],
        scratch_shapes=[pltpu.VMEM((tm, D_MODEL), BF16)],
        compiler_params=_params(("parallel", "arbitrary")),
        name="inproj_even",
    )(x, w_in_bf, t1, t2)


def _cmul(ar, ai, br, bi):
    return ar * br - ai * bi, ar * bi + ai * br


(DISC_A1_RE, DISC_A1_IM, DISC_A2_RE, DISC_A2_IM, DISC_A4_RE, DISC_A4_IM, DISC_A8_RE, DISC_A8_IM,
 DISC_A16_RE, DISC_A16_IM, DISC_AT_RE, DISC_AT_IM, DISC_Z_RE, DISC_Z_IM, DISC_SCAN_A2, DISC_SCAN_A3) = range(16)


def _s5_disc_body(lr_ref, li_ref, ls_ref, sg_ref, o_ref):
    lr = jnp.minimum(lr_ref[...], -1e-4)
    li = li_ref[...]
    step = jnp.exp(ls_ref[...])
    mag = jnp.exp(lr * step)
    ar = mag * jnp.cos(li * step)
    ai = mag * jnp.sin(li * step)
    nr, ni = ar - 1.0, ai
    den = lr * lr + li * li
    o_ref[DISC_Z_RE] = (nr * lr + ni * li) / den
    o_ref[DISC_Z_IM] = (ni * lr - nr * li) / den
    pr, pi = ar, ai
    for k in range(6):
        o_ref[2 * k] = pr
        o_ref[2 * k + 1] = pi
        if k < 5:
            pr, pi = _cmul(pr, pi, pr, pi)
    o_ref[DISC_SCAN_A2] = pi * sg_ref[...]
    o_ref[DISC_SCAN_A3] = -pi * sg_ref[...]


def _s5_disc(lam_re, lam_im, log_step):
    assert S5_T == 32
    rows = 2 * S5_GROUPS
    two = lambda a: jnp.tile(a.astype(F32).reshape(rows, -1), (1, 2))
    lr = two(lam_re)
    li = two(lam_im)
    ls = jnp.broadcast_to(log_step.astype(F32).reshape(rows, 1), (rows, 2 * S5_STATE))
    sg = jnp.broadcast_to(jnp.concatenate([-jnp.ones((S5_STATE,), F32), jnp.ones((S5_STATE,), F32)])[None],
                          (rows, 2 * S5_STATE))
    return pl.pallas_call(
        _s5_disc_body,
        out_shape=jax.ShapeDtypeStruct((16, rows, 2 * S5_STATE), F32),
        name="s5_disc",
    )(lr, li, ls, sg)


def _s5_gen_body(col_ref, zrow_ref, bt_ref, btile_ref, ctile_ref, m_ref, ws_ref, wc_ref):
    P = S5_STATE
    hi = lax.Precision.HIGHEST
    tlo = lax.shift_right_logical(lax.broadcasted_iota(jnp.int32, (P, 128), 1), 4)
    lane = lax.broadcasted_iota(jnp.int32, (S5_GROUP, S5_TL), 1)
    ones = jnp.ones((P, 128), F32)
    zeros = jnp.zeros((P, 128), F32)

    def one_group(gi, carry):
        kt = []
        for d in range(2):
            col = col_ref[gi, d]
            c = lambda k: jnp.broadcast_to(col[:, k:k + 1], (P, 128))
            a1 = (c(DISC_A1_RE), c(DISC_A1_IM))
            a2 = (c(DISC_A2_RE), c(DISC_A2_IM))
            a4 = (c(DISC_A4_RE), c(DISC_A4_IM))
            a8 = (c(DISC_A8_RE), c(DISC_A8_IM))
            a16 = (c(DISC_A16_RE), c(DISC_A16_IM))
            blk = [None, a8, a16, _cmul(*a8, *a16)]

            def low_powers(reverse):
                xr, xi = ones, zeros
                for k, ak in enumerate((a1, a2, a4)):
                    bit = (lax.shift_right_logical(tlo, k) & 1) == (0 if reverse else 1)
                    yr, yi = _cmul(xr, xi, *ak)
                    xr = jnp.where(bit, yr, xr)
                    xi = jnp.where(bit, yi, xi)
                return xr, xi

            def expand(base, reverse):
                out = []
                for j in range(4):
                    f = blk[3 - j] if reverse else blk[j]
                    out.append(base if f is None else _cmul(*base, *f))
                return out

            ct = (ctile_ref[gi, d, 0], ctile_ref[gi, d, 1])
            bbar = _cmul(c(DISC_Z_RE), c(DISC_Z_IM), btile_ref[gi, d, 0], btile_ref[gi, d, 1])
            zrow = zrow_ref[gi, d]
            zr_row, zi_row = zrow[0:1, :], zrow[1:2, :]
            bbt_r, bbt_i = _cmul(zr_row, zi_row, bt_ref[gi, d, 0], bt_ref[gi, d, 1])
            ca = expand(_cmul(*ct, *low_powers(d == 1)), d == 1)
            ba = expand(_cmul(*bbar, *low_powers(d == 0)), d == 0)
            wcj = [_cmul(*x, *a1) for x in ca]
            cat = lambda parts, k: jnp.concatenate([x[k] for x in parts], axis=1)
            kt.append(jnp.dot(bbt_r, cat(ca, 0), precision=hi, preferred_element_type=F32)
                      - jnp.dot(bbt_i, cat(ca, 1), precision=hi, preferred_element_type=F32))
            ba_r, ba_i = cat(ba, 0).astype(BF16), cat(ba, 1).astype(BF16)
            for r, part in enumerate((ba_r, ba_i, ba_i, ba_r)):
                ws_ref[gi, (4 * d + r) * P:(4 * d + r + 1) * P, :] = part
            wc_ref[gi, 2 * d * P:(2 * d + 1) * P, :] = cat(wcj, 0).astype(BF16)
            wc_ref[gi, (2 * d + 1) * P:(2 * d + 2) * P, :] = (-cat(wcj, 1)).astype(BF16)
        ktf, ktb = kt
        for s in range(S5_T):
            lo, hi_lane = S5_GROUP * s, S5_GROUP * (s + 1)
            f = ktf if s == 0 else jnp.where(lane >= lo, pltpu.roll(ktf, lo, axis=1), 0.0)
            b = ktb if s == S5_T - 1 else jnp.where(lane < hi_lane, pltpu.roll(ktb, hi_lane, axis=1), 0.0)
            m_ref[gi, lo:hi_lane, :] = (f + b).astype(BF16)
        return carry

    lax.fori_loop(0, col_ref.shape[0], one_group, 0)


def _s5_gen(disc, b_re, b_im, c_re, c_im):
    G, P, Cg = S5_GROUPS, S5_STATE, S5_GROUP
    gb = S5_GEN_GB
    d4 = disc[:, :, :P].reshape(16, 2, G, P)
    col = d4.transpose(2, 1, 3, 0)
    zrow = d4[DISC_Z_RE:DISC_Z_IM + 1].transpose(2, 1, 0, 3)
    b = jnp.stack([b_re, b_im], axis=1).astype(F32)
    c = jnp.stack([c_re, c_im], axis=1).astype(F32)
    bt = b.transpose(2, 0, 1, 4, 3)
    btile = jnp.tile(b.transpose(2, 0, 1, 3, 4), (1, 1, 1, 1, 128 // Cg))
    ctile = jnp.tile(c.transpose(2, 0, 1, 4, 3), (1, 1, 1, 1, 128 // Cg))
    spec = lambda a: pl.BlockSpec((gb,) + a.shape[1:], lambda i: (i,) + (0,) * (a.ndim - 1))
    out = lambda rows: pl.BlockSpec((gb, rows, S5_TL), lambda i: (i, 0, 0))
    return pl.pallas_call(
        _s5_gen_body,
        grid=(G // gb,),
        in_specs=[spec(col), spec(zrow), spec(bt), spec(btile), spec(ctile)],
        out_specs=[out(S5_TL), out(8 * P), out(4 * P)],
        out_shape=[jax.ShapeDtypeStruct((G, S5_TL, S5_TL), BF16),
                   jax.ShapeDtypeStruct((G, 8 * P, S5_TL), BF16),
                   jax.ShapeDtypeStruct((G, 4 * P, S5_TL), BF16)],
        compiler_params=_params(("parallel",)),
        name="s5_gen",
    )(col, zrow, bt, btile, ctile)


def _s5_body(ut_ref, m_ref, ws_ref, wc_ref, dt_ref, a1_ref, a2_ref, a3_ref, y_ref,
             s_ref, h_ref, yg_ref, u_ref):
    nc = u_ref.shape[1]
    gb = u_ref.shape[0]
    w = gb * 128
    for jb in range(S5_T // 8):
        ys = _block_transpose8([ut_ref[8 * jb + t8].astype(F32) for t8 in range(8)])
        for g8 in range(8):
            u_ref[g8, :, jb * 128:(jb + 1) * 128] = ys[g8].astype(BF16)
    for gi in range(gb):
        s = lax.dot_general(u_ref[gi], ws_ref[gi], (((1,), (1,)), ((), ())),
                            preferred_element_type=F32)
        for r in range(4):
            s_ref[:, r * w + gi * 128:r * w + (gi + 1) * 128] = s[:, r * 128:(r + 1) * 128]

    a1f, a2f, a3f = a1_ref[0], a2_ref[0], a3_ref[0]
    a1b, a2b, a3b = a1_ref[1], a2_ref[1], a3_ref[1]

    def step(n, carry):
        hf, gf, hb, gb_ = carry
        m = nc - 1 - n
        h_ref[pl.ds(n, 1), 0:w] = hf
        h_ref[pl.ds(m, 1), w:2 * w] = hb
        sfh = s_ref[pl.ds(n, 1), 0:w]
        sfg = s_ref[pl.ds(n, 1), w:2 * w]
        sbh = s_ref[pl.ds(m, 1), 2 * w:3 * w]
        sbg = s_ref[pl.ds(m, 1), 3 * w:4 * w]
        hf2 = a1f * hf + a2f * gf + sfh
        gf2 = a1f * gf + a3f * hf + sfg
        hb2 = a1b * hb + a2b * gb_ + sbh
        gb2 = a1b * gb_ + a3b * hb + sbg
        return hf2, gf2, hb2, gb2

    z = jnp.zeros((1, w), F32)
    lax.fori_loop(0, nc, step, (z, z, z, z))

    for gi in range(gb):
        u = u_ref[gi]
        hcat = jnp.concatenate([h_ref[:, gi * 128:(gi + 1) * 128],
                                h_ref[:, w + gi * 128:w + (gi + 1) * 128]], axis=1).astype(BF16)
        y = (jnp.dot(u, m_ref[gi], preferred_element_type=F32)
             + jnp.dot(hcat, wc_ref[gi], preferred_element_type=F32)
             + dt_ref[gi] * u.astype(F32))
        yg_ref[gi] = jax.nn.gelu(y)

    for jb in range(S5_T // 8):
        zs = _block_transpose8([yg_ref[g8, :, jb * 128:(jb + 1) * 128] for g8 in range(8)])
        for t8 in range(8):
            y_ref[8 * jb + t8] = zs[t8].astype(BF16)


def _s5_mix(u_steps, m, ws_t, wc, d_tile, a1, a2, a3):
    _, nc, width = u_steps.shape
    G, TL = S5_GROUPS, S5_TL
    gb = S5_GB
    assert gb * S5_GROUP == 128
    w = gb * 128
    steps_spec = pl.BlockSpec((S5_T, nc, 128), lambda i: (0, 0, i))
    a_spec = pl.BlockSpec((2, 1, w), lambda i: (0, 0, i))
    return pl.pallas_call(
        _s5_body,
        grid=(G // gb,),
        in_specs=[
            steps_spec,
            pl.BlockSpec((gb, TL, TL), lambda i: (i, 0, 0)),
            pl.BlockSpec((gb, 512, TL), lambda i: (i, 0, 0)),
            pl.BlockSpec((gb, 256, TL), lambda i: (i, 0, 0)),
            pl.BlockSpec((gb, 1, TL), lambda i: (i, 0, 0)),
            a_spec, a_spec, a_spec,
        ],
        out_specs=steps_spec,
        out_shape=jax.ShapeDtypeStruct((S5_T, nc, width), BF16),
        scratch_shapes=[pltpu.VMEM((nc, 4 * w), F32), pltpu.VMEM((nc, 2 * w), F32),
                        pltpu.VMEM((gb, nc, TL), F32), pltpu.VMEM((gb, nc, TL), BF16)],
        compiler_params=_params(("parallel",)),
        name="s5_mix",
    )(u_steps, m, ws_t, wc, d_tile, a1, a2, a3)


def _row_index(n):
    return lax.broadcasted_iota(jnp.int32, (n, 1), 0).astype(F32)


def _log_decay(lg_ref, d, h):
    return -jnp.abs(jnp.full((1, 1), lg_ref[d, h], F32))


def _ret_bstate_body(lg_ref, k_ref, v_ref, sb_ref, st_ref):
    C = k_ref.shape[0]

    @pl.when(pl.program_id(0) == 0)
    def _():
        st_ref[...] = jnp.zeros_like(st_ref)

    jj = _row_index(C)
    for h in range(RET_HEADS):
        lo = h * RET_HEAD_DIM
        lgb = _log_decay(lg_ref, 1, h)
        sb_ref[h] = st_ref[h].astype(BF16)
        kd = (k_ref[:, lo:lo + RET_HEAD_DIM].astype(F32) * jnp.exp(jj * lgb)).astype(BF16)
        kv = lax.dot_general(kd, v_ref[:, lo:lo + RET_HEAD_DIM], (((0,), (0,)), ((), ())),
                             preferred_element_type=F32)
        st_ref[h] = st_ref[h] * jnp.exp(C * lgb) + kv


def _ret_bstate(lg, h5):
    L = h5.shape[1]
    C = RET_C
    nc = L // C
    return pl.pallas_call(
        _ret_bstate_body,
        grid=(nc,),
        in_specs=[
            pl.BlockSpec(memory_space=pltpu.SMEM),
            pl.BlockSpec((None, C, RET_WIDTH), lambda i: (SEG_K - 1, nc - 1 - i, 0)),
            pl.BlockSpec((None, C, RET_WIDTH), lambda i: (SEG_V - 1, nc - 1 - i, 0)),
        ],
        out_specs=pl.BlockSpec((None, RET_HEADS, RET_HEAD_DIM, RET_HEAD_DIM),
                               lambda i: (nc - 1 - i, 0, 0, 0)),
        out_shape=jax.ShapeDtypeStruct((nc, RET_HEADS, RET_HEAD_DIM, RET_HEAD_DIM), BF16),
        scratch_shapes=[pltpu.VMEM((RET_HEADS, RET_HEAD_DIM, RET_HEAD_DIM), F32)],
        compiler_params=_params(("arbitrary",)),
        name="ret_bstate",
    )(lg, h5, h5)


def _ret_main_body(lg_ref, q_ref, k_ref, v_ref, g_ref, sb_ref, o_ref, st_ref):
    C = q_ref.shape[0]

    @pl.when(pl.program_id(0) == 0)
    def _():
        st_ref[...] = jnp.zeros_like(st_ref)

    ii = _row_index(C)
    diff = (lax.broadcasted_iota(jnp.int32, (C, C), 0)
            - lax.broadcasted_iota(jnp.int32, (C, C), 1)).astype(F32)
    for h in range(RET_HEADS):
        lo = h * RET_HEAD_DIM
        lgf = _log_decay(lg_ref, 0, h)
        lgb = _log_decay(lg_ref, 1, h)
        q = q_ref[:, lo:lo + RET_HEAD_DIM]
        k = k_ref[:, lo:lo + RET_HEAD_DIM]
        v = v_ref[:, lo:lo + RET_HEAD_DIM]
        qf = q.astype(F32)
        s = lax.dot_general(q, k, (((1,), (1,)), ((), ())), preferred_element_type=F32)
        decay = jnp.where(diff >= 0, jnp.exp(lgf * jnp.maximum(diff, 0.0)),
                          jnp.exp(lgb * jnp.maximum(-diff, 0.0)))
        o = jnp.dot((s * decay).astype(BF16), v, preferred_element_type=F32)
        qdf = (qf * jnp.exp((ii + 1.0) * lgf)).astype(BF16)
        o = o + jnp.dot(qdf, st_ref[h].astype(BF16), preferred_element_type=F32)
        qdb = (qf * jnp.exp((C - ii) * lgb)).astype(BF16)
        o = o + jnp.dot(qdb, sb_ref[h], preferred_element_type=F32)
        mu = jnp.mean(o, axis=-1, keepdims=True)
        oc = o - mu
        var = jnp.mean(oc * oc, axis=-1, keepdims=True)
        on = oc * lax.rsqrt(var + HEAD_NORM_EPS)
        o_ref[:, lo:lo + RET_HEAD_DIM] = (on * g_ref[:, lo:lo + RET_HEAD_DIM].astype(F32)).astype(BF16)
        kd = (k.astype(F32) * jnp.exp((C - 1.0 - ii) * lgf)).astype(BF16)
        kv = lax.dot_general(kd, v, (((0,), (0,)), ((), ())), preferred_element_type=F32)
        st_ref[h] = st_ref[h] * jnp.exp(C * lgf) + kv


def _ret_main(lg, h5, sb):
    L = h5.shape[1]
    C = RET_C
    nc = L // C
    seg = lambda s: pl.BlockSpec((None, C, RET_WIDTH), lambda i: (s - 1, i, 0))
    return pl.pallas_call(
        _ret_main_body,
        grid=(nc,),
        in_specs=[
            pl.BlockSpec(memory_space=pltpu.SMEM),
            seg(SEG_Q), seg(SEG_K), seg(SEG_V), seg(SEG_GATE),
            pl.BlockSpec((None, RET_HEADS, RET_HEAD_DIM, RET_HEAD_DIM), lambda i: (i, 0, 0, 0)),
        ],
        out_specs=pl.BlockSpec((C, RET_WIDTH), lambda i: (i, 0)),
        out_shape=jax.ShapeDtypeStruct((L, RET_WIDTH), BF16),
        scratch_shapes=[pltpu.VMEM((RET_HEADS, RET_HEAD_DIM, RET_HEAD_DIM), F32)],
        compiler_params=_params(("arbitrary",)),
        name="ret_main",
    )(lg, h5, h5, h5, h5, sb)


def _outproj_even_body(y_ref, r_ref, x_ref, wg_ref, bg_ref, wo_ref, g_ref, b_ref, o_ref):
    nl = y_ref.shape[1]
    tm = S5_T * nl
    shift = S5_T.bit_length() - 1
    y_steps = y_ref[...].reshape(tm, y_ref.shape[2])
    perm = _row_permutation(tm, lambda r: (r & (S5_T - 1)) * nl + lax.shift_right_logical(r, shift))
    for r in range(0, tm, LN_ROWS):
        rows = slice(r, r + LN_ROWS)
        y = jnp.dot(perm[rows, :], y_steps, preferred_element_type=F32).astype(BF16)
        z = jnp.dot(y, wg_ref[...], preferred_element_type=F32) + bg_ref[...]
        s5 = (y.astype(F32) * jax.nn.sigmoid(z)).astype(BF16)
        mix = (jnp.dot(s5, wo_ref[0:S5_WIDTH, :], preferred_element_type=F32)
               + jnp.dot(r_ref[rows, :], wo_ref[S5_WIDTH:S5_WIDTH + RET_WIDTH, :],
                         preferred_element_type=F32))
        o_ref[rows, :] = _layer_norm_rows(DEEPNORM_ALPHA * x_ref[rows, :] + mix, g_ref[...], b_ref[...])


def _outproj_even(y_steps, ret, x, w_glu_bf, b_glu, w_out_bf, ln_g, ln_b, tm=512):
    L = x.shape[0]
    row = lambda n: pl.BlockSpec((tm, n), lambda i: (i, 0))
    full = lambda a: pl.BlockSpec(a.shape, lambda i: (0,) * a.ndim)
    return pl.pallas_call(
        _outproj_even_body,
        grid=(L // tm,),
        in_specs=[pl.BlockSpec((S5_T, tm // S5_T, S5_WIDTH), lambda i: (0, i, 0)),
                  row(RET_WIDTH), row(D_MODEL), full(w_glu_bf), full(b_glu),
                  full(w_out_bf), full(ln_g), full(ln_b)],
        out_specs=row(D_MODEL),
        out_shape=jax.ShapeDtypeStruct((L, D_MODEL), F32),
        compiler_params=_params(("parallel",)),
        name="outproj_even",
    )(y_steps, ret, x, w_glu_bf, b_glu, w_out_bf, ln_g, ln_b)


def _inproj_odd_body(x_ref, w_ref, c_ref, s1_ref, s2_ref, o_ref, xb_ref):
    j = pl.program_id(1)

    @pl.when(j == 0)
    def _():
        xb_ref[...] = x_ref[...].astype(BF16)

    acc = jnp.dot(xb_ref[...], w_ref[...], preferred_element_type=F32)
    cc = c_ref[...]
    s1 = s1_ref[...]
    s2 = s2_ref[...]
    for hh in range(acc.shape[1] // ATT_HEAD_DIM):
        lo = hh * ATT_HEAD_DIM
        a = acc[:, lo:lo + ATT_HEAD_DIM]
        up = pltpu.roll(a, ATT_HEAD_DIM - ROPE_DIM // 2, axis=1)
        dn = pltpu.roll(a, ROPE_DIM // 2, axis=1)
        o_ref[:, lo:lo + ATT_HEAD_DIM] = (a * cc + up * s1 + dn * s2).astype(BF16)


def _inproj_odd(x, w_in_bf, cc, s1, s2, tm=1024, tn=512):
    L = x.shape[0]
    n_out = w_in_bf.shape[1]
    assert tn == ATT_KV_HEADS * ATT_HEAD_DIM
    nq = (ATT_HEADS * ATT_HEAD_DIM) // tn
    tab = pl.BlockSpec((None, tm, ATT_HEAD_DIM), lambda i, j: (jnp.clip(j - (nq - 1), 0, 2), i, 0))
    return pl.pallas_call(
        _inproj_odd_body,
        grid=(L // tm, n_out // tn),
        in_specs=[
            pl.BlockSpec((tm, D_MODEL), lambda i, j: (i, 0)),
            pl.BlockSpec((D_MODEL, tn), lambda i, j: (0, j)),
            tab, tab, tab,
        ],
        out_specs=pl.BlockSpec((tm, tn), lambda i, j: (i, j)),
        out_shape=jax.ShapeDtypeStruct((L, n_out), BF16),
        scratch_shapes=[pltpu.VMEM((tm, D_MODEL), BF16)],
        compiler_params=_params(("parallel", "arbitrary")),
        name="inproj_odd",
    )(x, w_in_bf, cc, s1, s2)


def _attn_body(sink_ref, q_ref, kp_ref, kc_ref, kn_ref, vp_ref, vc_ref, vn_ref, o_ref):
    c = pl.program_id(0)
    nb = pl.num_programs(0)
    B = ATT_BLOCK
    hd = ATT_HEAD_DIM
    rows = ATT_GROUP * B
    r_i = lax.broadcasted_iota(jnp.int32, (rows, 3 * B), 0)
    s_i = lax.broadcasted_iota(jnp.int32, (rows, 3 * B), 1)
    rel = (r_i & (B - 1)) - s_i + B
    key_lo = jnp.where(c > 0, 0, B)
    key_hi = jnp.where(c < nb - 1, 3 * B, 2 * B)
    ok = (jnp.abs(rel) <= ATT_WINDOW) & (s_i >= key_lo) & (s_i < key_hi)
    head_of_row = lax.shift_right_logical(lax.broadcasted_iota(jnp.int32, (rows, 1), 0),
                                          int(math.log2(B)))
    for g in range(ATT_KV_HEADS):
        q = jnp.concatenate([q_ref[:, (g * ATT_GROUP + hh) * hd:(g * ATT_GROUP + hh + 1) * hd]
                             for hh in range(ATT_GROUP)], axis=0)
        ksl = slice(g * hd, (g + 1) * hd)
        k = jnp.concatenate([kp_ref[:, ksl], kc_ref[:, ksl], kn_ref[:, ksl]], axis=0)
        v = jnp.concatenate([vp_ref[:, ksl], vc_ref[:, ksl], vn_ref[:, ksl]], axis=0)
        s = lax.dot_general(q, k, (((1,), (1,)), ((), ())), preferred_element_type=F32)
        s = jnp.where(ok, s, NEG_INF)
        sink = jnp.zeros((rows, 1), F32)
        for hh in range(ATT_GROUP):
            sink = jnp.where(head_of_row == hh, sink_ref[g * ATT_GROUP + hh], sink)
        m = jnp.maximum(jnp.max(s, axis=-1, keepdims=True), sink)
        p = jnp.exp(s - m)
        den = jnp.sum(p, axis=-1, keepdims=True) + jnp.exp(sink - m)
        o = jnp.dot(p.astype(BF16), v, preferred_element_type=F32) / den
        for hh in range(ATT_GROUP):
            hcol = (g * ATT_GROUP + hh) * hd
            o_ref[:, hcol:hcol + hd] = o[hh * B:(hh + 1) * B, :].astype(BF16)


def _attention(sink, qkv):
    L = qkv.shape[0]
    B = ATT_BLOCK
    nb = L // B
    kvw = ATT_KV_HEADS * ATT_HEAD_DIM
    qw = ATT_HEADS * ATT_HEAD_DIM
    kcol = qw // kvw
    vcol = kcol + 1

    def kv_spec(col, off):
        return pl.BlockSpec((B, kvw), lambda i: (jnp.clip(i + off, 0, nb - 1), col))

    return pl.pallas_call(
        _attn_body,
        grid=(nb,),
        in_specs=[
            pl.BlockSpec(memory_space=pltpu.SMEM),
            pl.BlockSpec((B, qw), lambda i: (i, 0)),
            kv_spec(kcol, -1), kv_spec(kcol, 0), kv_spec(kcol, 1),
            kv_spec(vcol, -1), kv_spec(vcol, 0), kv_spec(vcol, 1),
        ],
        out_specs=pl.BlockSpec((B, qw), lambda i: (i, 0)),
        out_shape=jax.ShapeDtypeStruct((L, qw), BF16),
        compiler_params=_params(("parallel",)),
        name="attention",
    )(sink, qkv, qkv, qkv, qkv, qkv, qkv, qkv)


def _outproj_odd_body(a_ref, x_ref, wo_ref, g_ref, b_ref, o_ref):
    for r in range(0, a_ref.shape[0], LN_ROWS):
        rows = slice(r, r + LN_ROWS)
        mix = jnp.dot(a_ref[rows, :], wo_ref[...], preferred_element_type=F32)
        o_ref[rows, :] = _layer_norm_rows(DEEPNORM_ALPHA * x_ref[rows, :] + mix, g_ref[...], b_ref[...])


def _outproj_odd(a, x, w_out_bf, ln_g, ln_b, tm=512):
    L = x.shape[0]
    row = lambda n: pl.BlockSpec((tm, n), lambda i: (i, 0))
    full = lambda t: pl.BlockSpec(t.shape, lambda i: (0,) * t.ndim)
    return pl.pallas_call(
        _outproj_odd_body,
        grid=(L // tm,),
        in_specs=[row(a.shape[1]), row(D_MODEL), full(w_out_bf), full(ln_g), full(ln_b)],
        out_specs=row(D_MODEL),
        out_shape=jax.ShapeDtypeStruct((L, D_MODEL), F32),
        compiler_params=_params(("parallel",)),
        name="outproj_odd",
    )(a, x, w_out_bf, ln_g, ln_b)


def _mlp_body(x_ref, w1_ref, w2_ref, g_ref, b_ref, o_ref, xb_ref, acc_ref):
    f = pl.program_id(1)

    @pl.when(f == 0)
    def _():
        xb_ref[...] = x_ref[...].astype(BF16)
        acc_ref[...] = jnp.zeros_like(acc_ref)

    h = jnp.dot(xb_ref[...], w1_ref[...], preferred_element_type=F32)
    h = jnp.square(jnp.maximum(h, 0.0)).astype(BF16)
    acc_ref[...] += jnp.dot(h, w2_ref[...], preferred_element_type=F32)

    @pl.when(f == pl.num_programs(1) - 1)
    def _():
        tm = x_ref.shape[0]
        for r in range(0, tm, LN_ROWS):
            y = DEEPNORM_ALPHA * x_ref[r:r + LN_ROWS, :] + acc_ref[r:r + LN_ROWS, :]
            o_ref[r:r + LN_ROWS, :] = _layer_norm_rows(y, g_ref[...], b_ref[...])


def _mlp(x, w1_bf, w2_bf, layer, ln_g, ln_b, tm=512, tf=1024):
    L = x.shape[0]
    return pl.pallas_call(
        _mlp_body,
        grid=(L // tm, D_FF // tf),
        in_specs=[
            pl.BlockSpec((tm, D_MODEL), lambda i, f: (i, 0)),
            pl.BlockSpec((None, D_MODEL, tf), lambda i, f: (layer, 0, f)),
            pl.BlockSpec((None, tf, D_MODEL), lambda i, f: (layer, f, 0)),
            pl.BlockSpec((1, D_MODEL), lambda i, f: (0, 0)),
            pl.BlockSpec((1, D_MODEL), lambda i, f: (0, 0)),
        ],
        out_specs=pl.BlockSpec((tm, D_MODEL), lambda i, f: (i, 0)),
        out_shape=jax.ShapeDtypeStruct((L, D_MODEL), F32),
        scratch_shapes=[pltpu.VMEM((tm, D_MODEL), BF16), pltpu.VMEM((tm, D_MODEL), F32)],
        compiler_params=_params(("parallel", "arbitrary")),
        name="mlp",
    )(x, w1_bf, w2_bf, ln_g, ln_b)


def _rotary_tables(L, rot_dim, theta):
    half = rot_dim // 2
    inv_freq = 1.0 / (theta ** (jnp.arange(half, dtype=F32) / half))
    ang = jnp.arange(L).astype(F32)[:, None] * inv_freq[None, :]
    return jnp.cos(ang), jnp.sin(ang)


def _even_layer(x, w_in, w_out, lam_re, lam_im, log_step, b_re, b_im, c_re, c_im,
                d_skip, w_glu, b_glu, ret_log_decay, ln_g, ln_b):
    L = x.shape[0]
    cos, sin = _rotary_tables(L, RET_HEAD_DIM, RET_ROPE_THETA)
    k_scale = RET_HEAD_DIM ** -0.5
    t1 = jnp.stack([cos, cos * k_scale, jnp.ones_like(cos)])
    t2 = jnp.stack([sin, sin * k_scale, jnp.zeros_like(sin)])
    u_steps, h5 = _inproj_even(x, w_in.astype(BF16), t1, t2)
    disc = _s5_disc(lam_re, lam_im, log_step)
    m, ws_t, wc = _s5_gen(disc, b_re, b_im, c_re, c_im)
    d_tile = jnp.tile(d_skip.astype(F32), (1, S5_T)).reshape(S5_GROUPS, 1, S5_TL)
    a1, a2, a3 = (disc[k].reshape(2, 1, S5_GROUPS * 128) for k in (DISC_AT_RE, DISC_SCAN_A2, DISC_SCAN_A3))
    y = _s5_mix(u_steps, m, ws_t, wc, d_tile, a1, a2, a3)
    lg = ret_log_decay.astype(F32)
    sb = _ret_bstate(lg, h5)
    ret = _ret_main(lg, h5, sb)
    return _outproj_even(y, ret, x, w_glu.astype(BF16), b_glu.astype(F32).reshape(1, -1),
                         w_out.astype(BF16), ln_g.reshape(1, -1), ln_b.reshape(1, -1))


def _odd_layer(x, w_in, w_out, sink, ln_g, ln_b):
    L = x.shape[0]
    cos, sin = _rotary_tables(L, ROPE_DIM, ROPE_THETA)
    half = ROPE_DIM // 2
    pad = ATT_HEAD_DIM - ROPE_DIM
    cc = jnp.concatenate([cos, cos, jnp.ones((L, pad), F32)], axis=1)
    s1 = jnp.concatenate([-sin, jnp.zeros((L, ATT_HEAD_DIM - half), F32)], axis=1)
    s2 = jnp.concatenate([jnp.zeros((L, half), F32), sin, jnp.zeros((L, pad), F32)], axis=1)
    q_scale = ATT_HEAD_DIM ** -0.5
    zero = jnp.zeros_like(cc)
    cc, s1, s2 = (jnp.stack([t * q_scale, t, ident])
                  for t, ident in ((cc, jnp.ones_like(cc)), (s1, zero), (s2, zero)))
    qkv = _inproj_odd(x, w_in.astype(BF16), cc, s1, s2)
    att = _attention(sink.astype(F32), qkv)
    return _outproj_odd(att, x, w_out.astype(BF16), ln_g.reshape(1, -1), ln_b.reshape(1, -1))


def kernel(x, ln_g, ln_b, mlp_w1, mlp_w2, even_w_in, even_w_out, s5_lambda_re, s5_lambda_im, s5_log_step, s5_b_re, s5_b_im, s5_c_re, s5_c_im, s5_d, s5_w_glu, s5_b_glu, ret_log_decay, odd_w_in, odd_w_out, attn_sink):
    bsz = x.shape[0]
    w1_bf = mlp_w1.astype(BF16)
    w2_bf = mlp_w2.astype(BF16)
    outs = []
    for b in range(bsz):
        xb = x[b]
        for layer in range(DEPTH):
            if layer % 2 == 0:
                e = layer // 2
                xb = _even_layer(xb, even_w_in[e], even_w_out[e], s5_lambda_re[e], s5_lambda_im[e],
                                 s5_log_step[e], s5_b_re[e], s5_b_im[e], s5_c_re[e], s5_c_im[e],
                                 s5_d[e], s5_w_glu[e], s5_b_glu[e], ret_log_decay[e],
                                 ln_g[layer, 0], ln_b[layer, 0])
            else:
                o = layer // 2
                xb = _odd_layer(xb, odd_w_in[o], odd_w_out[o], attn_sink[o], ln_g[layer, 0], ln_b[layer, 0])
            xb = _mlp(xb, w1_bf, w2_bf, layer,
                      ln_g[layer, 1].reshape(1, -1), ln_b[layer, 1].reshape(1, -1))
        outs.append(xb)
    return jnp.stack(outs, axis=0)
```

```python
import functools
import math

import jax
import jax.numpy as jnp
from jax import lax
from jax.experimental import pallas as pl
from jax.experimental.pallas import tpu as pltpu

F32 = jnp.float32
BF16 = jnp.bfloat16

D_MODEL = 2048
DEPTH = 2
S5_WIDTH = 1024
S5_GROUP = 16
S5_GROUPS = 64
S5_STATE = 64
RET_WIDTH = 1024
RET_HEADS = 4
RET_HEAD_DIM = 256
RET_ROPE_THETA = 10000.0
ATT_HEADS = 16
ATT_KV_HEADS = 4
ATT_HEAD_DIM = 128
ATT_GROUP = 4
ATT_WINDOW = 128
ATT_BLOCK = 128
ROPE_THETA = 500000.0
ROPE_DIM = 32
D_FF = 4 * D_MODEL
DEEPNORM_ALPHA = (2 * DEPTH) ** 0.25
LN_EPS = 1e-5
HEAD_NORM_EPS = 1e-6
NEG_INF = -1e30

V7X_VMEM_BYTES = 64 * 1024 * 1024
VMEM_LIMIT = V7X_VMEM_BYTES - 8 * 1024 * 1024

S5_T = 32
S5_TL = S5_T * S5_GROUP
S5_GB = 8
S5_GEN_GB = 8
RET_C = 256
LN_ROWS = 256


def _params(sem):
    return pltpu.CompilerParams(dimension_semantics=sem, vmem_limit_bytes=VMEM_LIMIT)


def _layer_norm_rows(y, g, b):
    mu = jnp.mean(y, axis=-1, keepdims=True)
    yc = y - mu
    var = jnp.mean(yc * yc, axis=-1, keepdims=True)
    return yc * lax.rsqrt(var + LN_EPS) * g + b


def _block_transpose8(xs):
    blk = lax.shift_right_logical(lax.broadcasted_iota(jnp.int32, xs[0].shape, 1), 4)
    xs = list(xs)
    for k in range(3):
        d = 1 << k
        upper = (blk & d) != 0
        for i in range(8):
            if i & d:
                continue
            a, b = xs[i], xs[i + d]
            xs[i] = jnp.where(upper, pltpu.roll(b, S5_GROUP * d, axis=1), a)
            xs[i + d] = jnp.where(upper, b, pltpu.roll(a, 128 - S5_GROUP * d, axis=1))
    return xs


SEG_U, SEG_Q, SEG_K, SEG_V, SEG_GATE = range(5)


def _row_permutation(n_rows, src_of_row):
    r = lax.broadcasted_iota(jnp.int32, (n_rows, n_rows), 0)
    c = lax.broadcasted_iota(jnp.int32, (n_rows, n_rows), 1)
    return jnp.where(c == src_of_row(r), 1.0, 0.0).astype(BF16)


def _inproj_even_body(x_ref, w_ref, t1_ref, t2_ref, u_ref, o_ref, xb_ref):
    j = pl.program_id(1)

    @pl.when(j == 0)
    def _():
        xb_ref[...] = x_ref[...].astype(BF16)
        u = jnp.dot(xb_ref[...], w_ref[...], preferred_element_type=F32).astype(BF16)
        tm = u.shape[0]
        nl = tm // S5_T
        shift = nl.bit_length() - 1
        perm = _row_permutation(tm, lambda r: (r & (nl - 1)) * S5_T + lax.shift_right_logical(r, shift))
        up = jnp.dot(perm, u, preferred_element_type=F32).astype(BF16)
        for t in range(S5_T):
            u_ref[t] = up[t * nl:(t + 1) * nl, :]

    @pl.when(j > 0)
    def _():
        acc = jnp.dot(xb_ref[...], w_ref[...], preferred_element_type=F32)
        rotate = (j == SEG_Q) | (j == SEG_K)
        scale = jnp.where(j == SEG_K, RET_HEAD_DIM ** -0.5, 1.0)
        t1 = jnp.where(rotate, t1_ref[...] * scale, 1.0)
        t2 = jnp.where(rotate, t2_ref[...] * scale, 0.0)
        is_gate = j == SEG_GATE
        half = RET_HEAD_DIM // 2
        for hh in range(RET_HEADS):
            lo = hh * RET_HEAD_DIM
            a = acc[:, lo:lo + half]
            b = acc[:, lo + half:lo + RET_HEAD_DIM]
            for off, r in ((0, a * t1 - b * t2), (half, b * t1 + a * t2)):
                r = jnp.where(is_gate, r * jax.nn.sigmoid(r), r)
                o_ref[:, lo + off:lo + off + half] = r.astype(BF16)


def _inproj_even(x, w_in_bf, t1, t2, tm=1024):
    L = x.shape[0]
    tn = S5_WIDTH
    nseg = w_in_bf.shape[1] // tn
    tab = pl.BlockSpec((tm, RET_HEAD_DIM // 2), lambda i, j: (i, 0))
    return pl.pallas_call(
        _inproj_even_body,
        grid=(L // tm, nseg),
        in_specs=[
            pl.BlockSpec((tm, D_MODEL), lambda i, j: (i, 0)),
            pl.BlockSpec((D_MODEL, tn), lambda i, j: (0, j)),
            tab, tab,
        ],
        out_specs=[pl.BlockSpec((S5_T, tm // S5_T, tn), lambda i, j: (0, i, 0)),
                   pl.BlockSpec((None, tm, tn), lambda i, j: (jnp.maximum(j - 1, 0), i, 0))],
        out_shape=[jax.ShapeDtypeStruct((S5_T, L // S5_T, tn), BF16),
                   jax.ShapeDtypeStruct((nseg - 1, L, tn), BF16)],
        scratch_shapes=[pltpu.VMEM((tm, D_MODEL), BF16)],
        compiler_params=_params(("parallel", "arbitrary")),
        name="inproj_even",
    )(x, w_in_bf, t1, t2)


def _cmul(ar, ai, br, bi):
    return ar * br - ai * bi, ar * bi + ai * br


(DISC_A1_RE, DISC_A1_IM, DISC_A2_RE, DISC_A2_IM, DISC_A4_RE, DISC_A4_IM, DISC_A8_RE, DISC_A8_IM,
 DISC_A16_RE, DISC_A16_IM, DISC_AT_RE, DISC_AT_IM, DISC_Z_RE, DISC_Z_IM, DISC_SCAN_A2, DISC_SCAN_A3) = range(16)


def _s5_disc_body(lr_ref, li_ref, ls_ref, sg_ref, o_ref):
    lr = jnp.minimum(lr_ref[...], -1e-4)
    li = li_ref[...]
    step = jnp.exp(ls_ref[...])
    mag = jnp.exp(lr * step)
    ar = mag * jnp.cos(li * step)
    ai = mag * jnp.sin(li * step)
    nr, ni = ar - 1.0, ai
    den = lr * lr + li * li
    o_ref[DISC_Z_RE] = (nr * lr + ni * li) / den
    o_ref[DISC_Z_IM] = (ni * lr - nr * li) / den
    pr, pi = ar, ai
    for k in range(6):
        o_ref[2 * k] = pr
        o_ref[2 * k + 1] = pi
        if k < 5:
            pr, pi = _cmul(pr, pi, pr, pi)
    o_ref[DISC_SCAN_A2] = pi * sg_ref[...]
    o_ref[DISC_SCAN_A3] = -pi * sg_ref[...]


def _s5_disc(lam_re, lam_im, log_step):
    assert S5_T == 32
    rows = 2 * S5_GROUPS
    two = lambda a: jnp.tile(a.astype(F32).reshape(rows, -1), (1, 2))
    lr = two(lam_re)
    li = two(lam_im)
    ls = jnp.broadcast_to(log_step.astype(F32).reshape(rows, 1), (rows, 2 * S5_STATE))
    sg = jnp.broadcast_to(jnp.concatenate([-jnp.ones((S5_STATE,), F32), jnp.ones((S5_STATE,), F32)])[None],
                          (rows, 2 * S5_STATE))
    return pl.pallas_call(
        _s5_disc_body,
        out_shape=jax.ShapeDtypeStruct((16, rows, 2 * S5_STATE), F32),
        name="s5_disc",
    )(lr, li, ls, sg)


def _s5_gen_body(col_ref, zrow_ref, bt_ref, btile_ref, ctile_ref, m_ref, ws_ref, wc_ref):
    P = S5_STATE
    hi = lax.Precision.HIGHEST
    tlo = lax.shift_right_logical(lax.broadcasted_iota(jnp.int32, (P, 128), 1), 4)
    lane = lax.broadcasted_iota(jnp.int32, (S5_GROUP, S5_TL), 1)
    ones = jnp.ones((P, 128), F32)
    zeros = jnp.zeros((P, 128), F32)

    def one_group(gi, carry):
        kt = []
        for d in range(2):
            col = col_ref[gi, d]
            c = lambda k: jnp.broadcast_to(col[:, k:k + 1], (P, 128))
            a1 = (c(DISC_A1_RE), c(DISC_A1_IM))
            a2 = (c(DISC_A2_RE), c(DISC_A2_IM))
            a4 = (c(DISC_A4_RE), c(DISC_A4_IM))
            a8 = (c(DISC_A8_RE), c(DISC_A8_IM))
            a16 = (c(DISC_A16_RE), c(DISC_A16_IM))
            blk = [None, a8, a16, _cmul(*a8, *a16)]

            def low_powers(reverse):
                xr, xi = ones, zeros
                for k, ak in enumerate((a1, a2, a4)):
                    bit = (lax.shift_right_logical(tlo, k) & 1) == (0 if reverse else 1)
                    yr, yi = _cmul(xr, xi, *ak)
                    xr = jnp.where(bit, yr, xr)
                    xi = jnp.where(bit, yi, xi)
                return xr, xi

            def expand(base, reverse):
                out = []
                for j in range(4):
                    f = blk[3 - j] if reverse else blk[j]
                    out.append(base if f is None else _cmul(*base, *f))
                return out

            ct = (ctile_ref[gi, d, 0], ctile_ref[gi, d, 1])
            bbar = _cmul(c(DISC_Z_RE), c(DISC_Z_IM), btile_ref[gi, d, 0], btile_ref[gi, d, 1])
            zrow = zrow_ref[gi, d]
            zr_row, zi_row = zrow[0:1, :], zrow[1:2, :]
            bbt_r, bbt_i = _cmul(zr_row, zi_row, bt_ref[gi, d, 0], bt_ref[gi, d, 1])
            ca = expand(_cmul(*ct, *low_powers(d == 1)), d == 1)
            ba = expand(_cmul(*bbar, *low_powers(d == 0)), d == 0)
            wcj = [_cmul(*x, *a1) for x in ca]
            cat = lambda parts, k: jnp.concatenate([x[k] for x in parts], axis=1)
            kt.append(jnp.dot(bbt_r, cat(ca, 0), precision=hi, preferred_element_type=F32)
                      - jnp.dot(bbt_i, cat(ca, 1), precision=hi, preferred_element_type=F32))
            ba_r, ba_i = cat(ba, 0).astype(BF16), cat(ba, 1).astype(BF16)
            for r, part in enumerate((ba_r, ba_i, ba_i, ba_r)):
                ws_ref[gi, (4 * d + r) * P:(4 * d + r + 1) * P, :] = part
            wc_ref[gi, 2 * d * P:(2 * d + 1) * P, :] = cat(wcj, 0).astype(BF16)
            wc_ref[gi, (2 * d + 1) * P:(2 * d + 2) * P, :] = (-cat(wcj, 1)).astype(BF16)
        ktf, ktb = kt
        for s in range(S5_T):
            lo, hi_lane = S5_GROUP * s, S5_GROUP * (s + 1)
            f = ktf if s == 0 else jnp.where(lane >= lo, pltpu.roll(ktf, lo, axis=1), 0.0)
            b = ktb if s == S5_T - 1 else jnp.where(lane < hi_lane, pltpu.roll(ktb, hi_lane, axis=1), 0.0)
            m_ref[gi, lo:hi_lane, :] = (f + b).astype(BF16)
        return carry

    lax.fori_loop(0, col_ref.shape[0], one_group, 0)


def _s5_gen(disc, b_re, b_im, c_re, c_im):
    G, P, Cg = S5_GROUPS, S5_STATE, S5_GROUP
    gb = S5_GEN_GB
    d4 = disc[:, :, :P].reshape(16, 2, G, P)
    col = d4.transpose(2, 1, 3, 0)
    zrow = d4[DISC_Z_RE:DISC_Z_IM + 1].transpose(2, 1, 0, 3)
    b = jnp.stack([b_re, b_im], axis=1).astype(F32)
    c = jnp.stack([c_re, c_im], axis=1).astype(F32)
    bt = b.transpose(2, 0, 1, 4, 3)
    btile = jnp.tile(b.transpose(2, 0, 1, 3, 4), (1, 1, 1, 1, 128 // Cg))
    ctile = jnp.tile(c.transpose(2, 0, 1, 4, 3), (1, 1, 1, 1, 128 // Cg))
    spec = lambda a: pl.BlockSpec((gb,) + a.shape[1:], lambda i: (i,) + (0,) * (a.ndim - 1))
    out = lambda rows: pl.BlockSpec((gb, rows, S5_TL), lambda i: (i, 0, 0))
    return pl.pallas_call(
        _s5_gen_body,
        grid=(G // gb,),
        in_specs=[spec(col), spec(zrow), spec(bt), spec(btile), spec(ctile)],
        out_specs=[out(S5_TL), out(8 * P), out(4 * P)],
        out_shape=[jax.ShapeDtypeStruct((G, S5_TL, S5_TL), BF16),
                   jax.ShapeDtypeStruct((G, 8 * P, S5_TL), BF16),
                   jax.ShapeDtypeStruct((G, 4 * P, S5_TL), BF16)],
        compiler_params=_params(("parallel",)),
        name="s5_gen",
    )(col, zrow, bt, btile, ctile)


def _s5_body(ut_ref, m_ref, ws_ref, wc_ref, dt_ref, a1_ref, a2_ref, a3_ref, y_ref,
             s_ref, h_ref, yg_ref, u_ref):
    nc = u_ref.shape[1]
    gb = u_ref.shape[0]
    w = gb * 128
    for jb in range(S5_T // 8):
        ys = _block_transpose8([ut_ref[8 * jb + t8].astype(F32) for t8 in range(8)])
        for g8 in range(8):
            u_ref[g8, :, jb * 128:(jb + 1) * 128] = ys[g8].astype(BF16)
    for gi in range(gb):
        s = lax.dot_general(u_ref[gi], ws_ref[gi], (((1,), (1,)), ((), ())),
                            preferred_element_type=F32)
        for r in range(4):
            s_ref[:, r * w + gi * 128:r * w + (gi + 1) * 128] = s[:, r * 128:(r + 1) * 128]

    a1f, a2f, a3f = a1_ref[0], a2_ref[0], a3_ref[0]
    a1b, a2b, a3b = a1_ref[1], a2_ref[1], a3_ref[1]

    def step(n, carry):
        hf, gf, hb, gb_ = carry
        m = nc - 1 - n
        h_ref[pl.ds(n, 1), 0:w] = hf
        h_ref[pl.ds(m, 1), w:2 * w] = hb
        sfh = s_ref[pl.ds(n, 1), 0:w]
        sfg = s_ref[pl.ds(n, 1), w:2 * w]
        sbh = s_ref[pl.ds(m, 1), 2 * w:3 * w]
        sbg = s_ref[pl.ds(m, 1), 3 * w:4 * w]
        hf2 = a1f * hf + a2f * gf + sfh
        gf2 = a1f * gf + a3f * hf + sfg
        hb2 = a1b * hb + a2b * gb_ + sbh
        gb2 = a1b * gb_ + a3b * hb + sbg
        return hf2, gf2, hb2, gb2

    z = jnp.zeros((1, w), F32)
    lax.fori_loop(0, nc, step, (z, z, z, z))

    for gi in range(gb):
        u = u_ref[gi]
        hcat = jnp.concatenate([h_ref[:, gi * 128:(gi + 1) * 128],
                                h_ref[:, w + gi * 128:w + (gi + 1) * 128]], axis=1).astype(BF16)
        y = (jnp.dot(u, m_ref[gi], preferred_element_type=F32)
             + jnp.dot(hcat, wc_ref[gi], preferred_element_type=F32)
             + dt_ref[gi] * u.astype(F32))
        yg_ref[gi] = jax.nn.gelu(y)

    for jb in range(S5_T // 8):
        zs = _block_transpose8([yg_ref[g8, :, jb * 128:(jb + 1) * 128] for g8 in range(8)])
        for t8 in range(8):
            y_ref[8 * jb + t8] = zs[t8].astype(BF16)


def _s5_mix(u_steps, m, ws_t, wc, d_tile, a1, a2, a3):
    _, nc, width = u_steps.shape
    G, TL = S5_GROUPS, S5_TL
    gb = S5_GB
    assert gb * S5_GROUP == 128
    w = gb * 128
    steps_spec = pl.BlockSpec((S5_T, nc, 128), lambda i: (0, 0, i))
    a_spec = pl.BlockSpec((2, 1, w), lambda i: (0, 0, i))
    return pl.pallas_call(
        _s5_body,
        grid=(G // gb,),
        in_specs=[
            steps_spec,
            pl.BlockSpec((gb, TL, TL), lambda i: (i, 0, 0)),
            pl.BlockSpec((gb, 512, TL), lambda i: (i, 0, 0)),
            pl.BlockSpec((gb, 256, TL), lambda i: (i, 0, 0)),
            pl.BlockSpec((gb, 1, TL), lambda i: (i, 0, 0)),
            a_spec, a_spec, a_spec,
        ],
        out_specs=steps_spec,
        out_shape=jax.ShapeDtypeStruct((S5_T, nc, width), BF16),
        scratch_shapes=[pltpu.VMEM((nc, 4 * w), F32), pltpu.VMEM((nc, 2 * w), F32),
                        pltpu.VMEM((gb, nc, TL), F32), pltpu.VMEM((gb, nc, TL), BF16)],
        compiler_params=_params(("parallel",)),
        name="s5_mix",
    )(u_steps, m, ws_t, wc, d_tile, a1, a2, a3)


def _row_index(n):
    return lax.broadcasted_iota(jnp.int32, (n, 1), 0).astype(F32)


def _log_decay(lg_ref, d, h):
    return -jnp.abs(jnp.full((1, 1), lg_ref[d, h], F32))


def _ret_bstate_body(lg_ref, k_ref, v_ref, sb_ref, st_ref):
    C = k_ref.shape[0]

    @pl.when(pl.program_id(0) == 0)
    def _():
        st_ref[...] = jnp.zeros_like(st_ref)

    jj = _row_index(C)
    for h in range(RET_HEADS):
        lo = h * RET_HEAD_DIM
        lgb = _log_decay(lg_ref, 1, h)
        sb_ref[h] = st_ref[h].astype(BF16)
        kd = (k_ref[:, lo:lo + RET_HEAD_DIM].astype(F32) * jnp.exp(jj * lgb)).astype(BF16)
        kv = lax.dot_general(kd, v_ref[:, lo:lo + RET_HEAD_DIM], (((0,), (0,)), ((), ())),
                             preferred_element_type=F32)
        st_ref[h] = st_ref[h] * jnp.exp(C * lgb) + kv


def _ret_bstate(lg, h5):
    L = h5.shape[1]
    C = RET_C
    nc = L // C
    return pl.pallas_call(
        _ret_bstate_body,
        grid=(nc,),
        in_specs=[
            pl.BlockSpec(memory_space=pltpu.SMEM),
            pl.BlockSpec((None, C, RET_WIDTH), lambda i: (SEG_K - 1, nc - 1 - i, 0)),
            pl.BlockSpec((None, C, RET_WIDTH), lambda i: (SEG_V - 1, nc - 1 - i, 0)),
        ],
        out_specs=pl.BlockSpec((None, RET_HEADS, RET_HEAD_DIM, RET_HEAD_DIM),
                               lambda i: (nc - 1 - i, 0, 0, 0)),
        out_shape=jax.ShapeDtypeStruct((nc, RET_HEADS, RET_HEAD_DIM, RET_HEAD_DIM), BF16),
        scratch_shapes=[pltpu.VMEM((RET_HEADS, RET_HEAD_DIM, RET_HEAD_DIM), F32)],
        compiler_params=_params(("arbitrary",)),
        name="ret_bstate",
    )(lg, h5, h5)


def _ret_main_body(lg_ref, q_ref, k_ref, v_ref, g_ref, sb_ref, o_ref, st_ref):
    C = q_ref.shape[0]

    @pl.when(pl.program_id(0) == 0)
    def _():
        st_ref[...] = jnp.zeros_like(st_ref)

    ii = _row_index(C)
    diff = (lax.broadcasted_iota(jnp.int32, (C, C), 0)
            - lax.broadcasted_iota(jnp.int32, (C, C), 1)).astype(F32)
    for h in range(RET_HEADS):
        lo = h * RET_HEAD_DIM
        lgf = _log_decay(lg_ref, 0, h)
        lgb = _log_decay(lg_ref, 1, h)
        q = q_ref[:, lo:lo + RET_HEAD_DIM]
        k = k_ref[:, lo:lo + RET_HEAD_DIM]
        v = v_ref[:, lo:lo + RET_HEAD_DIM]
        qf = q.astype(F32)
        s = lax.dot_general(q, k, (((1,), (1,)), ((), ())), preferred_element_type=F32)
        decay = jnp.where(diff >= 0, jnp.exp(lgf * jnp.maximum(diff, 0.0)),
                          jnp.exp(lgb * jnp.maximum(-diff, 0.0)))
        o = jnp.dot((s * decay).astype(BF16), v, preferred_element_type=F32)
        qdf = (qf * jnp.exp((ii + 1.0) * lgf)).astype(BF16)
        o = o + jnp.dot(qdf, st_ref[h].astype(BF16), preferred_element_type=F32)
        qdb = (qf * jnp.exp((C - ii) * lgb)).astype(BF16)
        o = o + jnp.dot(qdb, sb_ref[h], preferred_element_type=F32)
        mu = jnp.mean(o, axis=-1, keepdims=True)
        oc = o - mu
        var = jnp.mean(oc * oc, axis=-1, keepdims=True)
        on = oc * lax.rsqrt(var + HEAD_NORM_EPS)
        o_ref[:, lo:lo + RET_HEAD_DIM] = (on * g_ref[:, lo:lo + RET_HEAD_DIM].astype(F32)).astype(BF16)
        kd = (k.astype(F32) * jnp.exp((C - 1.0 - ii) * lgf)).astype(BF16)
        kv = lax.dot_general(kd, v, (((0,), (0,)), ((), ())), preferred_element_type=F32)
        st_ref[h] = st_ref[h] * jnp.exp(C * lgf) + kv


def _ret_main(lg, h5, sb):
    L = h5.shape[1]
    C = RET_C
    nc = L // C
    seg = lambda s: pl.BlockSpec((None, C, RET_WIDTH), lambda i: (s - 1, i, 0))
    return pl.pallas_call(
        _ret_main_body,
        grid=(nc,),
        in_specs=[
            pl.BlockSpec(memory_space=pltpu.SMEM),
            seg(SEG_Q), seg(SEG_K), seg(SEG_V), seg(SEG_GATE),
            pl.BlockSpec((None, RET_HEADS, RET_HEAD_DIM, RET_HEAD_DIM), lambda i: (i, 0, 0, 0)),
        ],
        out_specs=pl.BlockSpec((C, RET_WIDTH), lambda i: (i, 0)),
        out_shape=jax.ShapeDtypeStruct((L, RET_WIDTH), BF16),
        scratch_shapes=[pltpu.VMEM((RET_HEADS, RET_HEAD_DIM, RET_HEAD_DIM), F32)],
        compiler_params=_params(("arbitrary",)),
        name="ret_main",
    )(lg, h5, h5, h5, h5, sb)


def _outproj_even_body(y_ref, r_ref, x_ref, wg_ref, bg_ref, wo_ref, g_ref, b_ref, o_ref):
    nl = y_ref.shape[1]
    tm = S5_T * nl
    shift = S5_T.bit_length() - 1
    y_steps = y_ref[...].reshape(tm, y_ref.shape[2])
    perm = _row_permutation(tm, lambda r: (r & (S5_T - 1)) * nl + lax.shift_right_logical(r, shift))
    for r in range(0, tm, LN_ROWS):
        rows = slice(r, r + LN_ROWS)
        y = jnp.dot(perm[rows, :], y_steps, preferred_element_type=F32).astype(BF16)
        z = jnp.dot(y, wg_ref[...], preferred_element_type=F32) + bg_ref[...]
        s5 = (y.astype(F32) * jax.nn.sigmoid(z)).astype(BF16)
        mix = (jnp.dot(s5, wo_ref[0:S5_WIDTH, :], preferred_element_type=F32)
               + jnp.dot(r_ref[rows, :], wo_ref[S5_WIDTH:S5_WIDTH + RET_WIDTH, :],
                         preferred_element_type=F32))
        o_ref[rows, :] = _layer_norm_rows(DEEPNORM_ALPHA * x_ref[rows, :] + mix, g_ref[...], b_ref[...])


def _outproj_even(y_steps, ret, x, w_glu_bf, b_glu, w_out_bf, ln_g, ln_b, tm=512):
    L = x.shape[0]
    row = lambda n: pl.BlockSpec((tm, n), lambda i: (i, 0))
    full = lambda a: pl.BlockSpec(a.shape, lambda i: (0,) * a.ndim)
    return pl.pallas_call(
        _outproj_even_body,
        grid=(L // tm,),
        in_specs=[pl.BlockSpec((S5_T, tm // S5_T, S5_WIDTH), lambda i: (0, i, 0)),
                  row(RET_WIDTH), row(D_MODEL), full(w_glu_bf), full(b_glu),
                  full(w_out_bf), full(ln_g), full(ln_b)],
        out_specs=row(D_MODEL),
        out_shape=jax.ShapeDtypeStruct((L, D_MODEL), F32),
        compiler_params=_params(("parallel",)),
        name="outproj_even",
    )(y_steps, ret, x, w_glu_bf, b_glu, w_out_bf, ln_g, ln_b)


def _inproj_odd_body(x_ref, w_ref, c_ref, s1_ref, s2_ref, o_ref, xb_ref):
    j = pl.program_id(1)

    @pl.when(j == 0)
    def _():
        xb_ref[...] = x_ref[...].astype(BF16)

    acc = jnp.dot(xb_ref[...], w_ref[...], preferred_element_type=F32)
    tn = acc.shape[1]
    nq = (ATT_HEADS * ATT_HEAD_DIM) // tn
    is_v = j > nq
    scale = jnp.where(j < nq, ATT_HEAD_DIM ** -0.5, 1.0)
    cc = jnp.where(is_v, 1.0, c_ref[...] * scale)
    s1 = jnp.where(is_v, 0.0, s1_ref[...] * scale)
    s2 = jnp.where(is_v, 0.0, s2_ref[...] * scale)
    for hh in range(tn // ATT_HEAD_DIM):
        lo = hh * ATT_HEAD_DIM
        a = acc[:, lo:lo + ATT_HEAD_DIM]
        up = pltpu.roll(a, ATT_HEAD_DIM - ROPE_DIM // 2, axis=1)
        dn = pltpu.roll(a, ROPE_DIM // 2, axis=1)
        o_ref[:, lo:lo + ATT_HEAD_DIM] = (a * cc + up * s1 + dn * s2).astype(BF16)


def _inproj_odd(x, w_in_bf, cc, s1, s2, tm=1024, tn=512):
    L = x.shape[0]
    n_out = w_in_bf.shape[1]
    assert tn == ATT_KV_HEADS * ATT_HEAD_DIM
    tab = pl.BlockSpec((tm, ATT_HEAD_DIM), lambda i, j: (i, 0))
    return pl.pallas_call(
        _inproj_odd_body,
        grid=(L // tm, n_out // tn),
        in_specs=[
            pl.BlockSpec((tm, D_MODEL), lambda i, j: (i, 0)),
            pl.BlockSpec((D_MODEL, tn), lambda i, j: (0, j)),
            tab, tab, tab,
        ],
        out_specs=pl.BlockSpec((tm, tn), lambda i, j: (i, j)),
        out_shape=jax.ShapeDtypeStruct((L, n_out), BF16),
        scratch_shapes=[pltpu.VMEM((tm, D_MODEL), BF16)],
        compiler_params=_params(("parallel", "arbitrary")),
        name="inproj_odd",
    )(x, w_in_bf, cc, s1, s2)


def _attn_body(sink_ref, q_ref, kp_ref, kc_ref, kn_ref, vp_ref, vc_ref, vn_ref, o_ref):
    c = pl.program_id(0)
    nb = pl.num_programs(0)
    B = ATT_BLOCK
    hd = ATT_HEAD_DIM
    rows = ATT_GROUP * B
    r_i = lax.broadcasted_iota(jnp.int32, (rows, 3 * B), 0)
    s_i = lax.broadcasted_iota(jnp.int32, (rows, 3 * B), 1)
    rel = (r_i & (B - 1)) - s_i + B
    key_lo = jnp.where(c > 0, 0, B)
    key_hi = jnp.where(c < nb - 1, 3 * B, 2 * B)
    ok = (jnp.abs(rel) <= ATT_WINDOW) & (s_i >= key_lo) & (s_i < key_hi)
    head_of_row = lax.shift_right_logical(lax.broadcasted_iota(jnp.int32, (rows, 1), 0),
                                          int(math.log2(B)))
    for g in range(ATT_KV_HEADS):
        q = jnp.concatenate([q_ref[:, (g * ATT_GROUP + hh) * hd:(g * ATT_GROUP + hh + 1) * hd]
                             for hh in range(ATT_GROUP)], axis=0)
        ksl = slice(g * hd, (g + 1) * hd)
        k = jnp.concatenate([kp_ref[:, ksl], kc_ref[:, ksl], kn_ref[:, ksl]], axis=0)
        v = jnp.concatenate([vp_ref[:, ksl], vc_ref[:, ksl], vn_ref[:, ksl]], axis=0)
        s = lax.dot_general(q, k, (((1,), (1,)), ((), ())), preferred_element_type=F32)
        s = jnp.where(ok, s, NEG_INF)
        sink = jnp.zeros((rows, 1), F32)
        for hh in range(ATT_GROUP):
            sink = jnp.where(head_of_row == hh, sink_ref[g * ATT_GROUP + hh], sink)
        m = jnp.maximum(jnp.max(s, axis=-1, keepdims=True), sink)
        p = jnp.exp(s - m)
        den = jnp.sum(p, axis=-1, keepdims=True) + jnp.exp(sink - m)
        o = jnp.dot(p.astype(BF16), v, preferred_element_type=F32) / den
        for hh in range(ATT_GROUP):
            hcol = (g * ATT_GROUP + hh) * hd
            o_ref[:, hcol:hcol + hd] = o[hh * B:(hh + 1) * B, :].astype(BF16)


def _attention(sink, qkv):
    L = qkv.shape[0]
    B = ATT_BLOCK
    nb = L // B
    kvw = ATT_KV_HEADS * ATT_HEAD_DIM
    qw = ATT_HEADS * ATT_HEAD_DIM
    kcol = qw // kvw
    vcol = kcol + 1

    def kv_spec(col, off):
        return pl.BlockSpec((B, kvw), lambda i: (jnp.clip(i + off, 0, nb - 1), col))

    return pl.pallas_call(
        _attn_body,
        grid=(nb,),
        in_specs=[
            pl.BlockSpec(memory_space=pltpu.SMEM),
            pl.BlockSpec((B, qw), lambda i: (i, 0)),
            kv_spec(kcol, -1), kv_spec(kcol, 0), kv_spec(kcol, 1),
            kv_spec(vcol, -1), kv_spec(vcol, 0), kv_spec(vcol, 1),
        ],
        out_specs=pl.BlockSpec((B, qw), lambda i: (i, 0)),
        out_shape=jax.ShapeDtypeStruct((L, qw), BF16),
        compiler_params=_params(("parallel",)),
        name="attention",
    )(sink, qkv, qkv, qkv, qkv, qkv, qkv, qkv)


def _outproj_odd_body(a_ref, x_ref, wo_ref, g_ref, b_ref, o_ref):
    for r in range(0, a_ref.shape[0], LN_ROWS):
        rows = slice(r, r + LN_ROWS)
        mix = jnp.dot(a_ref[rows, :], wo_ref[...], preferred_element_type=F32)
        o_ref[rows, :] = _layer_norm_rows(DEEPNORM_ALPHA * x_ref[rows, :] + mix, g_ref[...], b_ref[...])


def _outproj_odd(a, x, w_out_bf, ln_g, ln_b, tm=512):
    L = x.shape[0]
    row = lambda n: pl.BlockSpec((tm, n), lambda i: (i, 0))
    full = lambda t: pl.BlockSpec(t.shape, lambda i: (0,) * t.ndim)
    return pl.pallas_call(
        _outproj_odd_body,
        grid=(L // tm,),
        in_specs=[row(a.shape[1]), row(D_MODEL), full(w_out_bf), full(ln_g), full(ln_b)],
        out_specs=row(D_MODEL),
        out_shape=jax.ShapeDtypeStruct((L, D_MODEL), F32),
        compiler_params=_params(("parallel",)),
        name="outproj_odd",
    )(a, x, w_out_bf, ln_g, ln_b)


def _mlp_body(x_ref, w1_ref, w2_ref, g_ref, b_ref, o_ref, xb_ref):
    f = pl.program_id(1)

    @pl.when(f == 0)
    def _():
        xb_ref[...] = x_ref[...].astype(BF16)
        o_ref[...] = jnp.zeros_like(o_ref)

    h = jnp.dot(xb_ref[...], w1_ref[...].astype(BF16), preferred_element_type=F32)
    h = jnp.square(jnp.maximum(h, 0.0)).astype(BF16)
    o_ref[...] += jnp.dot(h, w2_ref[...].astype(BF16), preferred_element_type=F32)

    @pl.when(f == pl.num_programs(1) - 1)
    def _():
        tm = x_ref.shape[0]
        for r in range(0, tm, LN_ROWS):
            y = DEEPNORM_ALPHA * x_ref[r:r + LN_ROWS, :] + o_ref[r:r + LN_ROWS, :]
            o_ref[r:r + LN_ROWS, :] = _layer_norm_rows(y, g_ref[...], b_ref[...])


def _mlp(x, w1, w2, layer, ln_g, ln_b, tm=1024, tf=512):
    L = x.shape[0]
    return pl.pallas_call(
        _mlp_body,
        grid=(L // tm, D_FF // tf),
        in_specs=[
            pl.BlockSpec((tm, D_MODEL), lambda i, f: (i, 0), pipeline_mode=pl.Buffered(1)),
            pl.BlockSpec((None, D_MODEL, tf), lambda i, f: (layer, 0, f)),
            pl.BlockSpec((None, tf, D_MODEL), lambda i, f: (layer, f, 0)),
            pl.BlockSpec((1, D_MODEL), lambda i, f: (0, 0)),
            pl.BlockSpec((1, D_MODEL), lambda i, f: (0, 0)),
        ],
        out_specs=pl.BlockSpec((tm, D_MODEL), lambda i, f: (i, 0)),
        out_shape=jax.ShapeDtypeStruct((L, D_MODEL), F32),
        scratch_shapes=[pltpu.VMEM((tm, D_MODEL), BF16)],
        compiler_params=_params(("parallel", "arbitrary")),
        name="mlp",
    )(x, w1, w2, ln_g, ln_b)


def _rotary_tables(L, rot_dim, theta):
    half = rot_dim // 2
    inv_freq = 1.0 / (theta ** (jnp.arange(half, dtype=F32) / half))
    ang = jnp.arange(L).astype(F32)[:, None] * inv_freq[None, :]
    return jnp.cos(ang), jnp.sin(ang)


def _even_layer(x, w_in, w_out, lam_re, lam_im, log_step, b_re, b_im, c_re, c_im,
                d_skip, w_glu, b_glu, ret_log_decay, ln_g, ln_b):
    L = x.shape[0]
    cos, sin = _rotary_tables(L, RET_HEAD_DIM, RET_ROPE_THETA)
    u_steps, h5 = _inproj_even(x, w_in.astype(BF16), cos, sin)
    disc = _s5_disc(lam_re, lam_im, log_step)
    m, ws_t, wc = _s5_gen(disc, b_re, b_im, c_re, c_im)
    d_tile = jnp.tile(d_skip.astype(F32), (1, S5_T)).reshape(S5_GROUPS, 1, S5_TL)
    a1, a2, a3 = (disc[k].reshape(2, 1, S5_GROUPS * 128) for k in (DISC_AT_RE, DISC_SCAN_A2, DISC_SCAN_A3))
    y = _s5_mix(u_steps, m, ws_t, wc, d_tile, a1, a2, a3)
    lg = ret_log_decay.astype(F32)
    sb = _ret_bstate(lg, h5)
    ret = _ret_main(lg, h5, sb)
    return _outproj_even(y, ret, x, w_glu.astype(BF16), b_glu.astype(F32).reshape(1, -1),
                         w_out.astype(BF16), ln_g.reshape(1, -1), ln_b.reshape(1, -1))


def _odd_layer(x, w_in, w_out, sink, ln_g, ln_b):
    L = x.shape[0]
    cos, sin = _rotary_tables(L, ROPE_DIM, ROPE_THETA)
    half = ROPE_DIM // 2
    pad = ATT_HEAD_DIM - ROPE_DIM
    cc = jnp.concatenate([cos, cos, jnp.ones((L, pad), F32)], axis=1)
    s1 = jnp.concatenate([-sin, jnp.zeros((L, ATT_HEAD_DIM - half), F32)], axis=1)
    s2 = jnp.concatenate([jnp.zeros((L, half), F32), sin, jnp.zeros((L, pad), F32)], axis=1)
    qkv = _inproj_odd(x, w_in.astype(BF16), cc, s1, s2)
    att = _attention(sink.astype(F32), qkv)
    return _outproj_odd(att, x, w_out.astype(BF16), ln_g.reshape(1, -1), ln_b.reshape(1, -1))


def kernel(x, ln_g, ln_b, mlp_w1, mlp_w2, even_w_in, even_w_out, s5_lambda_re, s5_lambda_im, s5_log_step, s5_b_re, s5_b_im, s5_c_re, s5_c_im, s5_d, s5_w_glu, s5_b_glu, ret_log_decay, odd_w_in, odd_w_out, attn_sink):
    bsz = x.shape[0]
    outs = []
    for b in range(bsz):
        xb = x[b]
        for layer in range(DEPTH):
            if layer % 2 == 0:
                e = layer // 2
                xb = _even_layer(xb, even_w_in[e], even_w_out[e], s5_lambda_re[e], s5_lambda_im[e],
                                 s5_log_step[e], s5_b_re[e], s5_b_im[e], s5_c_re[e], s5_c_im[e],
                                 s5_d[e], s5_w_glu[e], s5_b_glu[e], ret_log_decay[e],
                                 ln_g[layer, 0], ln_b[layer, 0])
            else:
                o = layer // 2
                xb = _odd_layer(xb, odd_w_in[o], odd_w_out[o], attn_sink[o], ln_g[layer, 0], ln_b[layer, 0])
            xb = _mlp(xb, mlp_w1.astype(F32), mlp_w2.astype(F32), layer,
                      ln_g[layer, 1].reshape(1, -1), ln_b[layer, 1].reshape(1, -1))
        outs.append(xb)
    return jnp.stack(outs, axis=0)
```

```python
import functools
import math

import jax
import jax.numpy as jnp
from jax import lax
from jax.experimental import pallas as pl
from jax.experimental.pallas import tpu as pltpu

F32 = jnp.float32
BF16 = jnp.bfloat16

D_MODEL = 2048
DEPTH = 2
S5_WIDTH = 1024
S5_GROUP = 16
S5_GROUPS = 64
S5_STATE = 64
RET_WIDTH = 1024
RET_HEADS = 4
RET_HEAD_DIM = 256
RET_ROPE_THETA = 10000.0
ATT_HEADS = 16
ATT_KV_HEADS = 4
ATT_HEAD_DIM = 128
ATT_GROUP = 4
ATT_WINDOW = 128
ATT_BLOCK = 128
ROPE_THETA = 500000.0
ROPE_DIM = 32
D_FF = 4 * D_MODEL
DEEPNORM_ALPHA = (2 * DEPTH) ** 0.25
LN_EPS = 1e-5
HEAD_NORM_EPS = 1e-6
NEG_INF = -1e30

V7X_VMEM_BYTES = 64 * 1024 * 1024
VMEM_LIMIT = V7X_VMEM_BYTES - 8 * 1024 * 1024

S5_T = 32
S5_TL = S5_T * S5_GROUP
S5_GB = 8
S5_GEN_GB = 8
RET_C = 256
LN_ROWS = 256


def _params(sem):
    return pltpu.CompilerParams(dimension_semantics=sem, vmem_limit_bytes=VMEM_LIMIT)


def _layer_norm_rows(y, g, b):
    mu = jnp.mean(y, axis=-1, keepdims=True)
    yc = y - mu
    var = jnp.mean(yc * yc, axis=-1, keepdims=True)
    return yc * lax.rsqrt(var + LN_EPS) * g + b


def _block_transpose8(xs):
    blk = lax.shift_right_logical(lax.broadcasted_iota(jnp.int32, xs[0].shape, 1), 4)
    xs = list(xs)
    for k in range(3):
        d = 1 << k
        upper = (blk & d) != 0
        for i in range(8):
            if i & d:
                continue
            a, b = xs[i], xs[i + d]
            xs[i] = jnp.where(upper, pltpu.roll(b, S5_GROUP * d, axis=1), a)
            xs[i + d] = jnp.where(upper, b, pltpu.roll(a, 128 - S5_GROUP * d, axis=1))
    return xs


SEG_U, SEG_Q, SEG_K, SEG_V, SEG_GATE = range(5)


def _row_permutation(n_rows, src_of_row):
    r = lax.broadcasted_iota(jnp.int32, (n_rows, n_rows), 0)
    c = lax.broadcasted_iota(jnp.int32, (n_rows, n_rows), 1)
    return jnp.where(c == src_of_row(r), 1.0, 0.0).astype(BF16)


CAST_SLICES_PER_ROW_BLOCK = 4


def _mlp_weight_cast_specs(n_row_blocks, mlp_w1, mlp_w2, layer):
    pieces = n_row_blocks * CAST_SLICES_PER_ROW_BLOCK
    piece = lambda i, j: i * CAST_SLICES_PER_ROW_BLOCK + jnp.minimum(j, CAST_SLICES_PER_ROW_BLOCK - 1)
    ins, outs, shapes = [], [], []
    for w in (mlp_w1, mlp_w2):
        rows, cols = w.shape[1] // pieces, w.shape[2]
        ins.append(pl.BlockSpec((None, rows, cols), lambda i, j: (layer, piece(i, j), 0)))
        outs.append(pl.BlockSpec((rows, cols), lambda i, j: (piece(i, j), 0)))
        shapes.append(jax.ShapeDtypeStruct(w.shape[1:], BF16))
    return ins, outs, shapes


def _inproj_even_body(x_ref, w_ref, t1_ref, t2_ref, w1f_ref, w2f_ref, u_ref, o_ref, w1b_ref, w2b_ref,
                      xb_ref):
    j = pl.program_id(1)
    w1b_ref[...] = w1f_ref[...].astype(BF16)
    w2b_ref[...] = w2f_ref[...].astype(BF16)

    @pl.when(j == 0)
    def _():
        xb_ref[...] = x_ref[...].astype(BF16)
        u = jnp.dot(xb_ref[...], w_ref[...], preferred_element_type=F32).astype(BF16)
        tm = u.shape[0]
        nl = tm // S5_T
        shift = nl.bit_length() - 1
        perm = _row_permutation(tm, lambda r: (r & (nl - 1)) * S5_T + lax.shift_right_logical(r, shift))
        up = jnp.dot(perm, u, preferred_element_type=F32).astype(BF16)
        for t in range(S5_T):
            u_ref[t] = up[t * nl:(t + 1) * nl, :]

    @pl.when(j > 0)
    def _():
        acc = jnp.dot(xb_ref[...], w_ref[...], preferred_element_type=F32)
        rotate = (j == SEG_Q) | (j == SEG_K)
        scale = jnp.where(j == SEG_K, RET_HEAD_DIM ** -0.5, 1.0)
        t1 = jnp.where(rotate, t1_ref[...] * scale, 1.0)
        t2 = jnp.where(rotate, t2_ref[...] * scale, 0.0)
        is_gate = j == SEG_GATE
        half = RET_HEAD_DIM // 2
        for hh in range(RET_HEADS):
            lo = hh * RET_HEAD_DIM
            a = acc[:, lo:lo + half]
            b = acc[:, lo + half:lo + RET_HEAD_DIM]
            for off, r in ((0, a * t1 - b * t2), (half, b * t1 + a * t2)):
                r = jnp.where(is_gate, r * jax.nn.sigmoid(r), r)
                o_ref[:, lo + off:lo + off + half] = r.astype(BF16)


def _inproj_even(x, w_in_bf, t1, t2, mlp_w1, mlp_w2, layer, tm=1024):
    L = x.shape[0]
    tn = S5_WIDTH
    nseg = w_in_bf.shape[1] // tn
    assert nseg >= CAST_SLICES_PER_ROW_BLOCK
    tab = pl.BlockSpec((tm, RET_HEAD_DIM // 2), lambda i, j: (i, 0))
    cast_in, cast_out, cast_shapes = _mlp_weight_cast_specs(L // tm, mlp_w1, mlp_w2, layer)
    return pl.pallas_call(
        _inproj_even_body,
        grid=(L // tm, nseg),
        in_specs=[
            pl.BlockSpec((tm, D_MODEL), lambda i, j: (i, 0)),
            pl.BlockSpec((D_MODEL, tn), lambda i, j: (0, j)),
            tab, tab,
        ] + cast_in,
        out_specs=[pl.BlockSpec((S5_T, tm // S5_T, tn), lambda i, j: (0, i, 0)),
                   pl.BlockSpec((None, tm, tn), lambda i, j: (jnp.maximum(j - 1, 0), i, 0))] + cast_out,
        out_shape=[jax.ShapeDtypeStruct((S5_T, L // S5_T, tn), BF16),
                   jax.ShapeDtypeStruct((nseg - 1, L, tn), BF16)] + cast_shapes,
        scratch_shapes=[pltpu.VMEM((tm, D_MODEL), BF16)],
        compiler_params=_params(("parallel", "arbitrary")),
        name="inproj_even",
    )(x, w_in_bf, t1, t2, mlp_w1, mlp_w2)


def _cmul(ar, ai, br, bi):
    return ar * br - ai * bi, ar * bi + ai * br


(DISC_A1_RE, DISC_A1_IM, DISC_A2_RE, DISC_A2_IM, DISC_A4_RE, DISC_A4_IM, DISC_A8_RE, DISC_A8_IM,
 DISC_A16_RE, DISC_A16_IM, DISC_AT_RE, DISC_AT_IM, DISC_Z_RE, DISC_Z_IM, DISC_SCAN_A2, DISC_SCAN_A3) = range(16)


def _s5_disc_body(lr_ref, li_ref, ls_ref, sg_ref, o_ref):
    lr = jnp.minimum(lr_ref[...], -1e-4)
    li = li_ref[...]
    step = jnp.exp(ls_ref[...])
    mag = jnp.exp(lr * step)
    ar = mag * jnp.cos(li * step)
    ai = mag * jnp.sin(li * step)
    nr, ni = ar - 1.0, ai
    den = lr * lr + li * li
    o_ref[DISC_Z_RE] = (nr * lr + ni * li) / den
    o_ref[DISC_Z_IM] = (ni * lr - nr * li) / den
    pr, pi = ar, ai
    for k in range(6):
        o_ref[2 * k] = pr
        o_ref[2 * k + 1] = pi
        if k < 5:
            pr, pi = _cmul(pr, pi, pr, pi)
    o_ref[DISC_SCAN_A2] = pi * sg_ref[...]
    o_ref[DISC_SCAN_A3] = -pi * sg_ref[...]


def _s5_disc(lam_re, lam_im, log_step):
    assert S5_T == 32
    rows = 2 * S5_GROUPS
    two = lambda a: jnp.tile(a.astype(F32).reshape(rows, -1), (1, 2))
    lr = two(lam_re)
    li = two(lam_im)
    ls = jnp.broadcast_to(log_step.astype(F32).reshape(rows, 1), (rows, 2 * S5_STATE))
    sg = jnp.broadcast_to(jnp.concatenate([-jnp.ones((S5_STATE,), F32), jnp.ones((S5_STATE,), F32)])[None],
                          (rows, 2 * S5_STATE))
    return pl.pallas_call(
        _s5_disc_body,
        out_shape=jax.ShapeDtypeStruct((16, rows, 2 * S5_STATE), F32),
        name="s5_disc",
    )(lr, li, ls, sg)


def _s5_gen_body(col_ref, zrow_ref, bt_ref, btile_ref, ctile_ref, m_ref, ws_ref, wc_ref):
    P = S5_STATE
    hi = lax.Precision.HIGHEST
    tlo = lax.shift_right_logical(lax.broadcasted_iota(jnp.int32, (P, 128), 1), 4)
    lane = lax.broadcasted_iota(jnp.int32, (S5_GROUP, S5_TL), 1)
    ones = jnp.ones((P, 128), F32)
    zeros = jnp.zeros((P, 128), F32)

    def one_group(gi, carry):
        kt = []
        for d in range(2):
            col = col_ref[gi, d]
            c = lambda k: jnp.broadcast_to(col[:, k:k + 1], (P, 128))
            a1 = (c(DISC_A1_RE), c(DISC_A1_IM))
            a2 = (c(DISC_A2_RE), c(DISC_A2_IM))
            a4 = (c(DISC_A4_RE), c(DISC_A4_IM))
            a8 = (c(DISC_A8_RE), c(DISC_A8_IM))
            a16 = (c(DISC_A16_RE), c(DISC_A16_IM))
            blk = [None, a8, a16, _cmul(*a8, *a16)]

            def low_powers(reverse):
                xr, xi = ones, zeros
                for k, ak in enumerate((a1, a2, a4)):
                    bit = (lax.shift_right_logical(tlo, k) & 1) == (0 if reverse else 1)
                    yr, yi = _cmul(xr, xi, *ak)
                    xr = jnp.where(bit, yr, xr)
                    xi = jnp.where(bit, yi, xi)
                return xr, xi

            def expand(base, reverse):
                out = []
                for j in range(4):
                    f = blk[3 - j] if reverse else blk[j]
                    out.append(base if f is None else _cmul(*base, *f))
                return out

            ct = (ctile_ref[gi, d, 0], ctile_ref[gi, d, 1])
            bbar = _cmul(c(DISC_Z_RE), c(DISC_Z_IM), btile_ref[gi, d, 0], btile_ref[gi, d, 1])
            zrow = zrow_ref[gi, d]
            zr_row, zi_row = zrow[0:1, :], zrow[1:2, :]
            bbt_r, bbt_i = _cmul(zr_row, zi_row, bt_ref[gi, d, 0], bt_ref[gi, d, 1])
            ca = expand(_cmul(*ct, *low_powers(d == 1)), d == 1)
            ba = expand(_cmul(*bbar, *low_powers(d == 0)), d == 0)
            wcj = [_cmul(*x, *a1) for x in ca]
            cat = lambda parts, k: jnp.concatenate([x[k] for x in parts], axis=1)
            kt.append(jnp.dot(bbt_r, cat(ca, 0), precision=hi, preferred_element_type=F32)
                      - jnp.dot(bbt_i, cat(ca, 1), precision=hi, preferred_element_type=F32))
            ba_r, ba_i = cat(ba, 0).astype(BF16), cat(ba, 1).astype(BF16)
            for r, part in enumerate((ba_r, ba_i, ba_i, ba_r)):
                ws_ref[gi, (4 * d + r) * P:(4 * d + r + 1) * P, :] = part
            wc_ref[gi, 2 * d * P:(2 * d + 1) * P, :] = cat(wcj, 0).astype(BF16)
            wc_ref[gi, (2 * d + 1) * P:(2 * d + 2) * P, :] = (-cat(wcj, 1)).astype(BF16)
        ktf, ktb = kt
        for s in range(S5_T):
            lo, hi_lane = S5_GROUP * s, S5_GROUP * (s + 1)
            f = ktf if s == 0 else jnp.where(lane >= lo, pltpu.roll(ktf, lo, axis=1), 0.0)
            b = ktb if s == S5_T - 1 else jnp.where(lane < hi_lane, pltpu.roll(ktb, hi_lane, axis=1), 0.0)
            m_ref[gi, lo:hi_lane, :] = (f + b).astype(BF16)
        return carry

    lax.fori_loop(0, col_ref.shape[0], one_group, 0)


def _s5_gen(disc, b_re, b_im, c_re, c_im):
    G, P, Cg = S5_GROUPS, S5_STATE, S5_GROUP
    gb = S5_GEN_GB
    d4 = disc[:, :, :P].reshape(16, 2, G, P)
    col = d4.transpose(2, 1, 3, 0)
    zrow = d4[DISC_Z_RE:DISC_Z_IM + 1].transpose(2, 1, 0, 3)
    b = jnp.stack([b_re, b_im], axis=1).astype(F32)
    c = jnp.stack([c_re, c_im], axis=1).astype(F32)
    bt = b.transpose(2, 0, 1, 4, 3)
    btile = jnp.tile(b.transpose(2, 0, 1, 3, 4), (1, 1, 1, 1, 128 // Cg))
    ctile = jnp.tile(c.transpose(2, 0, 1, 4, 3), (1, 1, 1, 1, 128 // Cg))
    spec = lambda a: pl.BlockSpec((gb,) + a.shape[1:], lambda i: (i,) + (0,) * (a.ndim - 1))
    out = lambda rows: pl.BlockSpec((gb, rows, S5_TL), lambda i: (i, 0, 0))
    return pl.pallas_call(
        _s5_gen_body,
        grid=(G // gb,),
        in_specs=[spec(col), spec(zrow), spec(bt), spec(btile), spec(ctile)],
        out_specs=[out(S5_TL), out(8 * P), out(4 * P)],
        out_shape=[jax.ShapeDtypeStruct((G, S5_TL, S5_TL), BF16),
                   jax.ShapeDtypeStruct((G, 8 * P, S5_TL), BF16),
                   jax.ShapeDtypeStruct((G, 4 * P, S5_TL), BF16)],
        compiler_params=_params(("parallel",)),
        name="s5_gen",
    )(col, zrow, bt, btile, ctile)


def _s5_body(ut_ref, m_ref, ws_ref, wc_ref, dt_ref, a1_ref, a2_ref, a3_ref, y_ref,
             s_ref, h_ref, yg_ref, u_ref):
    nc = u_ref.shape[1]
    gb = u_ref.shape[0]
    w = gb * 128
    for jb in range(S5_T // 8):
        ys = _block_transpose8([ut_ref[8 * jb + t8].astype(F32) for t8 in range(8)])
        for g8 in range(8):
            u_ref[g8, :, jb * 128:(jb + 1) * 128] = ys[g8].astype(BF16)
    for gi in range(gb):
        s = lax.dot_general(u_ref[gi], ws_ref[gi], (((1,), (1,)), ((), ())),
                            preferred_element_type=F32)
        for r in range(4):
            s_ref[:, r * w + gi * 128:r * w + (gi + 1) * 128] = s[:, r * 128:(r + 1) * 128]

    a1f, a2f, a3f = a1_ref[0], a2_ref[0], a3_ref[0]
    a1b, a2b, a3b = a1_ref[1], a2_ref[1], a3_ref[1]

    def step(n, carry):
        hf, gf, hb, gb_ = carry
        m = nc - 1 - n
        h_ref[pl.ds(n, 1), 0:w] = hf
        h_ref[pl.ds(m, 1), w:2 * w] = hb
        sfh = s_ref[pl.ds(n, 1), 0:w]
        sfg = s_ref[pl.ds(n, 1), w:2 * w]
        sbh = s_ref[pl.ds(m, 1), 2 * w:3 * w]
        sbg = s_ref[pl.ds(m, 1), 3 * w:4 * w]
        hf2 = a1f * hf + a2f * gf + sfh
        gf2 = a1f * gf + a3f * hf + sfg
        hb2 = a1b * hb + a2b * gb_ + sbh
        gb2 = a1b * gb_ + a3b * hb + sbg
        return hf2, gf2, hb2, gb2

    z = jnp.zeros((1, w), F32)
    lax.fori_loop(0, nc, step, (z, z, z, z))

    for gi in range(gb):
        u = u_ref[gi]
        hcat = jnp.concatenate([h_ref[:, gi * 128:(gi + 1) * 128],
                                h_ref[:, w + gi * 128:w + (gi + 1) * 128]], axis=1).astype(BF16)
        y = (jnp.dot(u, m_ref[gi], preferred_element_type=F32)
             + jnp.dot(hcat, wc_ref[gi], preferred_element_type=F32)
             + dt_ref[gi] * u.astype(F32))
        yg_ref[gi] = jax.nn.gelu(y)

    for jb in range(S5_T // 8):
        zs = _block_transpose8([yg_ref[g8, :, jb * 128:(jb + 1) * 128] for g8 in range(8)])
        for t8 in range(8):
            y_ref[8 * jb + t8] = zs[t8].astype(BF16)


def _s5_mix(u_steps, m, ws_t, wc, d_tile, a1, a2, a3):
    _, nc, width = u_steps.shape
    G, TL = S5_GROUPS, S5_TL
    gb = S5_GB
    assert gb * S5_GROUP == 128
    w = gb * 128
    steps_spec = pl.BlockSpec((S5_T, nc, 128), lambda i: (0, 0, i))
    a_spec = pl.BlockSpec((2, 1, w), lambda i: (0, 0, i))
    return pl.pallas_call(
        _s5_body,
        grid=(G // gb,),
        in_specs=[
            steps_spec,
            pl.BlockSpec((gb, TL, TL), lambda i: (i, 0, 0)),
            pl.BlockSpec((gb, 512, TL), lambda i: (i, 0, 0)),
            pl.BlockSpec((gb, 256, TL), lambda i: (i, 0, 0)),
            pl.BlockSpec((gb, 1, TL), lambda i: (i, 0, 0)),
            a_spec, a_spec, a_spec,
        ],
        out_specs=steps_spec,
        out_shape=jax.ShapeDtypeStruct((S5_T, nc, width), BF16),
        scratch_shapes=[pltpu.VMEM((nc, 4 * w), F32), pltpu.VMEM((nc, 2 * w), F32),
                        pltpu.VMEM((gb, nc, TL), F32), pltpu.VMEM((gb, nc, TL), BF16)],
        compiler_params=_params(("parallel",)),
        name="s5_mix",
    )(u_steps, m, ws_t, wc, d_tile, a1, a2, a3)


def _row_index(n):
    return lax.broadcasted_iota(jnp.int32, (n, 1), 0).astype(F32)


def _log_decay(lg_ref, d, h):
    return -jnp.abs(jnp.full((1, 1), lg_ref[d, h], F32))


def _ret_bstate_body(lg_ref, k_ref, v_ref, sb_ref, st_ref):
    C = k_ref.shape[0]

    @pl.when(pl.program_id(0) == 0)
    def _():
        st_ref[...] = jnp.zeros_like(st_ref)

    jj = _row_index(C)
    for h in range(RET_HEADS):
        lo = h * RET_HEAD_DIM
        lgb = _log_decay(lg_ref, 1, h)
        sb_ref[h] = st_ref[h].astype(BF16)
        kd = (k_ref[:, lo:lo + RET_HEAD_DIM].astype(F32) * jnp.exp(jj * lgb)).astype(BF16)
        kv = lax.dot_general(kd, v_ref[:, lo:lo + RET_HEAD_DIM], (((0,), (0,)), ((), ())),
                             preferred_element_type=F32)
        st_ref[h] = st_ref[h] * jnp.exp(C * lgb) + kv


def _ret_bstate(lg, h5):
    L = h5.shape[1]
    C = RET_C
    nc = L // C
    return pl.pallas_call(
        _ret_bstate_body,
        grid=(nc,),
        in_specs=[
            pl.BlockSpec(memory_space=pltpu.SMEM),
            pl.BlockSpec((None, C, RET_WIDTH), lambda i: (SEG_K - 1, nc - 1 - i, 0)),
            pl.BlockSpec((None, C, RET_WIDTH), lambda i: (SEG_V - 1, nc - 1 - i, 0)),
        ],
        out_specs=pl.BlockSpec((None, RET_HEADS, RET_HEAD_DIM, RET_HEAD_DIM),
                               lambda i: (nc - 1 - i, 0, 0, 0)),
        out_shape=jax.ShapeDtypeStruct((nc, RET_HEADS, RET_HEAD_DIM, RET_HEAD_DIM), BF16),
        scratch_shapes=[pltpu.VMEM((RET_HEADS, RET_HEAD_DIM, RET_HEAD_DIM), F32)],
        compiler_params=_params(("arbitrary",)),
        name="ret_bstate",
    )(lg, h5, h5)


def _ret_main_body(lg_ref, q_ref, k_ref, v_ref, g_ref, sb_ref, o_ref, st_ref):
    C = q_ref.shape[0]

    @pl.when(pl.program_id(0) == 0)
    def _():
        st_ref[...] = jnp.zeros_like(st_ref)

    ii = _row_index(C)
    diff = (lax.broadcasted_iota(jnp.int32, (C, C), 0)
            - lax.broadcasted_iota(jnp.int32, (C, C), 1)).astype(F32)
    for h in range(RET_HEADS):
        lo = h * RET_HEAD_DIM
        lgf = _log_decay(lg_ref, 0, h)
        lgb = _log_decay(lg_ref, 1, h)
        q = q_ref[:, lo:lo + RET_HEAD_DIM]
        k = k_ref[:, lo:lo + RET_HEAD_DIM]
        v = v_ref[:, lo:lo + RET_HEAD_DIM]
        qf = q.astype(F32)
        s = lax.dot_general(q, k, (((1,), (1,)), ((), ())), preferred_element_type=F32)
        decay = jnp.where(diff >= 0, jnp.exp(lgf * jnp.maximum(diff, 0.0)),
                          jnp.exp(lgb * jnp.maximum(-diff, 0.0)))
        o = jnp.dot((s * decay).astype(BF16), v, preferred_element_type=F32)
        qdf = (qf * jnp.exp((ii + 1.0) * lgf)).astype(BF16)
        o = o + jnp.dot(qdf, st_ref[h].astype(BF16), preferred_element_type=F32)
        qdb = (qf * jnp.exp((C - ii) * lgb)).astype(BF16)
        o = o + jnp.dot(qdb, sb_ref[h], preferred_element_type=F32)
        mu = jnp.mean(o, axis=-1, keepdims=True)
        oc = o - mu
        var = jnp.mean(oc * oc, axis=-1, keepdims=True)
        on = oc * lax.rsqrt(var + HEAD_NORM_EPS)
        o_ref[:, lo:lo + RET_HEAD_DIM] = (on * g_ref[:, lo:lo + RET_HEAD_DIM].astype(F32)).astype(BF16)
        kd = (k.astype(F32) * jnp.exp((C - 1.0 - ii) * lgf)).astype(BF16)
        kv = lax.dot_general(kd, v, (((0,), (0,)), ((), ())), preferred_element_type=F32)
        st_ref[h] = st_ref[h] * jnp.exp(C * lgf) + kv


def _ret_main(lg, h5, sb):
    L = h5.shape[1]
    C = RET_C
    nc = L // C
    seg = lambda s: pl.BlockSpec((None, C, RET_WIDTH), lambda i: (s - 1, i, 0))
    return pl.pallas_call(
        _ret_main_body,
        grid=(nc,),
        in_specs=[
            pl.BlockSpec(memory_space=pltpu.SMEM),
            seg(SEG_Q), seg(SEG_K), seg(SEG_V), seg(SEG_GATE),
            pl.BlockSpec((None, RET_HEADS, RET_HEAD_DIM, RET_HEAD_DIM), lambda i: (i, 0, 0, 0)),
        ],
        out_specs=pl.BlockSpec((C, RET_WIDTH), lambda i: (i, 0)),
        out_shape=jax.ShapeDtypeStruct((L, RET_WIDTH), BF16),
        scratch_shapes=[pltpu.VMEM((RET_HEADS, RET_HEAD_DIM, RET_HEAD_DIM), F32)],
        compiler_params=_params(("arbitrary",)),
        name="ret_main",
    )(lg, h5, h5, h5, h5, sb)


def _outproj_even_body(y_ref, r_ref, x_ref, wg_ref, bg_ref, wo_ref, g_ref, b_ref, o_ref):
    nl = y_ref.shape[1]
    tm = S5_T * nl
    shift = S5_T.bit_length() - 1
    y_steps = y_ref[...].reshape(tm, y_ref.shape[2])
    perm = _row_permutation(tm, lambda r: (r & (S5_T - 1)) * nl + lax.shift_right_logical(r, shift))
    for r in range(0, tm, LN_ROWS):
        rows = slice(r, r + LN_ROWS)
        y = jnp.dot(perm[rows, :], y_steps, preferred_element_type=F32).astype(BF16)
        z = jnp.dot(y, wg_ref[...], preferred_element_type=F32) + bg_ref[...]
        s5 = (y.astype(F32) * jax.nn.sigmoid(z)).astype(BF16)
        mix = (jnp.dot(s5, wo_ref[0:S5_WIDTH, :], preferred_element_type=F32)
               + jnp.dot(r_ref[rows, :], wo_ref[S5_WIDTH:S5_WIDTH + RET_WIDTH, :],
                         preferred_element_type=F32))
        o_ref[rows, :] = _layer_norm_rows(DEEPNORM_ALPHA * x_ref[rows, :] + mix, g_ref[...], b_ref[...])


def _outproj_even(y_steps, ret, x, w_glu_bf, b_glu, w_out_bf, ln_g, ln_b, tm=512):
    L = x.shape[0]
    row = lambda n: pl.BlockSpec((tm, n), lambda i: (i, 0))
    full = lambda a: pl.BlockSpec(a.shape, lambda i: (0,) * a.ndim)
    return pl.pallas_call(
        _outproj_even_body,
        grid=(L // tm,),
        in_specs=[pl.BlockSpec((S5_T, tm // S5_T, S5_WIDTH), lambda i: (0, i, 0)),
                  row(RET_WIDTH), row(D_MODEL), full(w_glu_bf), full(b_glu),
                  full(w_out_bf), full(ln_g), full(ln_b)],
        out_specs=row(D_MODEL),
        out_shape=jax.ShapeDtypeStruct((L, D_MODEL), F32),
        compiler_params=_params(("parallel",)),
        name="outproj_even",
    )(y_steps, ret, x, w_glu_bf, b_glu, w_out_bf, ln_g, ln_b)


def _inproj_odd_body(x_ref, w_ref, c_ref, s1_ref, s2_ref, w1f_ref, w2f_ref, o_ref, w1b_ref, w2b_ref,
                     xb_ref):
    j = pl.program_id(1)
    w1b_ref[...] = w1f_ref[...].astype(BF16)
    w2b_ref[...] = w2f_ref[...].astype(BF16)

    @pl.when(j == 0)
    def _():
        xb_ref[...] = x_ref[...].astype(BF16)

    acc = jnp.dot(xb_ref[...], w_ref[...], preferred_element_type=F32)
    tn = acc.shape[1]
    nq = (ATT_HEADS * ATT_HEAD_DIM) // tn
    is_v = j > nq
    scale = jnp.where(j < nq, ATT_HEAD_DIM ** -0.5, 1.0)
    cc = jnp.where(is_v, 1.0, c_ref[...] * scale)
    s1 = jnp.where(is_v, 0.0, s1_ref[...] * scale)
    s2 = jnp.where(is_v, 0.0, s2_ref[...] * scale)
    for hh in range(tn // ATT_HEAD_DIM):
        lo = hh * ATT_HEAD_DIM
        a = acc[:, lo:lo + ATT_HEAD_DIM]
        up = pltpu.roll(a, ATT_HEAD_DIM - ROPE_DIM // 2, axis=1)
        dn = pltpu.roll(a, ROPE_DIM // 2, axis=1)
        o_ref[:, lo:lo + ATT_HEAD_DIM] = (a * cc + up * s1 + dn * s2).astype(BF16)


def _inproj_odd(x, w_in_bf, cc, s1, s2, mlp_w1, mlp_w2, layer, tm=1024, tn=512):
    L = x.shape[0]
    n_out = w_in_bf.shape[1]
    assert tn == ATT_KV_HEADS * ATT_HEAD_DIM and n_out // tn >= CAST_SLICES_PER_ROW_BLOCK
    tab = pl.BlockSpec((tm, ATT_HEAD_DIM), lambda i, j: (i, 0))
    cast_in, cast_out, cast_shapes = _mlp_weight_cast_specs(L // tm, mlp_w1, mlp_w2, layer)
    return pl.pallas_call(
        _inproj_odd_body,
        grid=(L // tm, n_out // tn),
        in_specs=[
            pl.BlockSpec((tm, D_MODEL), lambda i, j: (i, 0)),
            pl.BlockSpec((D_MODEL, tn), lambda i, j: (0, j)),
            tab, tab, tab,
        ] + cast_in,
        out_specs=[pl.BlockSpec((tm, tn), lambda i, j: (i, j))] + cast_out,
        out_shape=[jax.ShapeDtypeStruct((L, n_out), BF16)] + cast_shapes,
        scratch_shapes=[pltpu.VMEM((tm, D_MODEL), BF16)],
        compiler_params=_params(("parallel", "arbitrary")),
        name="inproj_odd",
    )(x, w_in_bf, cc, s1, s2, mlp_w1, mlp_w2)


def _attn_body(sink_ref, q_ref, kp_ref, kc_ref, kn_ref, vp_ref, vc_ref, vn_ref, o_ref):
    c = pl.program_id(0)
    nb = pl.num_programs(0)
    B = ATT_BLOCK
    hd = ATT_HEAD_DIM
    rows = ATT_GROUP * B
    r_i = lax.broadcasted_iota(jnp.int32, (rows, 3 * B), 0)
    s_i = lax.broadcasted_iota(jnp.int32, (rows, 3 * B), 1)
    rel = (r_i & (B - 1)) - s_i + B
    key_lo = jnp.where(c > 0, 0, B)
    key_hi = jnp.where(c < nb - 1, 3 * B, 2 * B)
    ok = (jnp.abs(rel) <= ATT_WINDOW) & (s_i >= key_lo) & (s_i < key_hi)
    head_of_row = lax.shift_right_logical(lax.broadcasted_iota(jnp.int32, (rows, 1), 0),
                                          int(math.log2(B)))
    for g in range(ATT_KV_HEADS):
        q = jnp.concatenate([q_ref[:, (g * ATT_GROUP + hh) * hd:(g * ATT_GROUP + hh + 1) * hd]
                             for hh in range(ATT_GROUP)], axis=0)
        ksl = slice(g * hd, (g + 1) * hd)
        k = jnp.concatenate([kp_ref[:, ksl], kc_ref[:, ksl], kn_ref[:, ksl]], axis=0)
        v = jnp.concatenate([vp_ref[:, ksl], vc_ref[:, ksl], vn_ref[:, ksl]], axis=0)
        s = lax.dot_general(q, k, (((1,), (1,)), ((), ())), preferred_element_type=F32)
        s = jnp.where(ok, s, NEG_INF)
        sink = jnp.zeros((rows, 1), F32)
        for hh in range(ATT_GROUP):
            sink = jnp.where(head_of_row == hh, sink_ref[g * ATT_GROUP + hh], sink)
        m = jnp.maximum(jnp.max(s, axis=-1, keepdims=True), sink)
        p = jnp.exp(s - m)
        den = jnp.sum(p, axis=-1, keepdims=True) + jnp.exp(sink - m)
        o = jnp.dot(p.astype(BF16), v, preferred_element_type=F32) / den
        for hh in range(ATT_GROUP):
            hcol = (g * ATT_GROUP + hh) * hd
            o_ref[:, hcol:hcol + hd] = o[hh * B:(hh + 1) * B, :].astype(BF16)


def _attention(sink, qkv):
    L = qkv.shape[0]
    B = ATT_BLOCK
    nb = L // B
    kvw = ATT_KV_HEADS * ATT_HEAD_DIM
    qw = ATT_HEADS * ATT_HEAD_DIM
    kcol = qw // kvw
    vcol = kcol + 1

    def kv_spec(col, off):
        return pl.BlockSpec((B, kvw), lambda i: (jnp.clip(i + off, 0, nb - 1), col))

    return pl.pallas_call(
        _attn_body,
        grid=(nb,),
        in_specs=[
            pl.BlockSpec(memory_space=pltpu.SMEM),
            pl.BlockSpec((B, qw), lambda i: (i, 0)),
            kv_spec(kcol, -1), kv_spec(kcol, 0), kv_spec(kcol, 1),
            kv_spec(vcol, -1), kv_spec(vcol, 0), kv_spec(vcol, 1),
        ],
        out_specs=pl.BlockSpec((B, qw), lambda i: (i, 0)),
        out_shape=jax.ShapeDtypeStruct((L, qw), BF16),
        compiler_params=_params(("parallel",)),
        name="attention",
    )(sink, qkv, qkv, qkv, qkv, qkv, qkv, qkv)


def _outproj_odd_body(a_ref, x_ref, wo_ref, g_ref, b_ref, o_ref):
    for r in range(0, a_ref.shape[0], LN_ROWS):
        rows = slice(r, r + LN_ROWS)
        mix = jnp.dot(a_ref[rows, :], wo_ref[...], preferred_element_type=F32)
        o_ref[rows, :] = _layer_norm_rows(DEEPNORM_ALPHA * x_ref[rows, :] + mix, g_ref[...], b_ref[...])


def _outproj_odd(a, x, w_out_bf, ln_g, ln_b, tm=512):
    L = x.shape[0]
    row = lambda n: pl.BlockSpec((tm, n), lambda i: (i, 0))
    full = lambda t: pl.BlockSpec(t.shape, lambda i: (0,) * t.ndim)
    return pl.pallas_call(
        _outproj_odd_body,
        grid=(L // tm,),
        in_specs=[row(a.shape[1]), row(D_MODEL), full(w_out_bf), full(ln_g), full(ln_b)],
        out_specs=row(D_MODEL),
        out_shape=jax.ShapeDtypeStruct((L, D_MODEL), F32),
        compiler_params=_params(("parallel",)),
        name="outproj_odd",
    )(a, x, w_out_bf, ln_g, ln_b)


def _mlp_body(x_ref, w1_ref, w2_ref, g_ref, b_ref, o_ref, xb_ref):
    f = pl.program_id(1)

    @pl.when(f == 0)
    def _():
        xb_ref[...] = x_ref[...].astype(BF16)
        o_ref[...] = jnp.zeros_like(o_ref)

    h = jnp.dot(xb_ref[...], w1_ref[...], preferred_element_type=F32)
    h = jnp.square(jnp.maximum(h, 0.0)).astype(BF16)
    o_ref[...] += jnp.dot(h, w2_ref[...], preferred_element_type=F32)

    @pl.when(f == pl.num_programs(1) - 1)
    def _():
        tm = x_ref.shape[0]
        for r in range(0, tm, LN_ROWS):
            y = DEEPNORM_ALPHA * x_ref[r:r + LN_ROWS, :] + o_ref[r:r + LN_ROWS, :]
            o_ref[r:r + LN_ROWS, :] = _layer_norm_rows(y, g_ref[...], b_ref[...])


def _mlp(x, w1, w2, ln_g, ln_b, tm=1024, tf=512):
    L = x.shape[0]
    return pl.pallas_call(
        _mlp_body,
        grid=(L // tm, D_FF // tf),
        in_specs=[
            pl.BlockSpec((tm, D_MODEL), lambda i, f: (i, 0), pipeline_mode=pl.Buffered(1)),
            pl.BlockSpec((D_MODEL, tf), lambda i, f: (0, f)),
            pl.BlockSpec((tf, D_MODEL), lambda i, f: (f, 0)),
            pl.BlockSpec((1, D_MODEL), lambda i, f: (0, 0)),
            pl.BlockSpec((1, D_MODEL), lambda i, f: (0, 0)),
        ],
        out_specs=pl.BlockSpec((tm, D_MODEL), lambda i, f: (i, 0)),
        out_shape=jax.ShapeDtypeStruct((L, D_MODEL), F32),
        scratch_shapes=[pltpu.VMEM((tm, D_MODEL), BF16)],
        compiler_params=_params(("parallel", "arbitrary")),
        name="mlp",
    )(x, w1, w2, ln_g, ln_b)


def _rotary_tables(L, rot_dim, theta):
    half = rot_dim // 2
    inv_freq = 1.0 / (theta ** (jnp.arange(half, dtype=F32) / half))
    ang = jnp.arange(L).astype(F32)[:, None] * inv_freq[None, :]
    return jnp.cos(ang), jnp.sin(ang)


def _even_layer(x, w_in, w_out, lam_re, lam_im, log_step, b_re, b_im, c_re, c_im,
                d_skip, w_glu, b_glu, ret_log_decay, ln_g, ln_b, mlp_w1, mlp_w2, layer):
    L = x.shape[0]
    cos, sin = _rotary_tables(L, RET_HEAD_DIM, RET_ROPE_THETA)
    u_steps, h5, w1_bf, w2_bf = _inproj_even(x, w_in.astype(BF16), cos, sin, mlp_w1, mlp_w2, layer)
    disc = _s5_disc(lam_re, lam_im, log_step)
    m, ws_t, wc = _s5_gen(disc, b_re, b_im, c_re, c_im)
    d_tile = jnp.tile(d_skip.astype(F32), (1, S5_T)).reshape(S5_GROUPS, 1, S5_TL)
    a1, a2, a3 = (disc[k].reshape(2, 1, S5_GROUPS * 128) for k in (DISC_AT_RE, DISC_SCAN_A2, DISC_SCAN_A3))
    y = _s5_mix(u_steps, m, ws_t, wc, d_tile, a1, a2, a3)
    lg = ret_log_decay.astype(F32)
    sb = _ret_bstate(lg, h5)
    ret = _ret_main(lg, h5, sb)
    x1 = _outproj_even(y, ret, x, w_glu.astype(BF16), b_glu.astype(F32).reshape(1, -1),
                       w_out.astype(BF16), ln_g.reshape(1, -1), ln_b.reshape(1, -1))
    return x1, w1_bf, w2_bf


def _odd_layer(x, w_in, w_out, sink, ln_g, ln_b, mlp_w1, mlp_w2, layer):
    L = x.shape[0]
    cos, sin = _rotary_tables(L, ROPE_DIM, ROPE_THETA)
    half = ROPE_DIM // 2
    pad = ATT_HEAD_DIM - ROPE_DIM
    cc = jnp.concatenate([cos, cos, jnp.ones((L, pad), F32)], axis=1)
    s1 = jnp.concatenate([-sin, jnp.zeros((L, ATT_HEAD_DIM - half), F32)], axis=1)
    s2 = jnp.concatenate([jnp.zeros((L, half), F32), sin, jnp.zeros((L, pad), F32)], axis=1)
    qkv, w1_bf, w2_bf = _inproj_odd(x, w_in.astype(BF16), cc, s1, s2, mlp_w1, mlp_w2, layer)
    att = _attention(sink.astype(F32), qkv)
    x1 = _outproj_odd(att, x, w_out.astype(BF16), ln_g.reshape(1, -1), ln_b.reshape(1, -1))
    return x1, w1_bf, w2_bf


def kernel(x, ln_g, ln_b, mlp_w1, mlp_w2, even_w_in, even_w_out, s5_lambda_re, s5_lambda_im, s5_log_step, s5_b_re, s5_b_im, s5_c_re, s5_c_im, s5_d, s5_w_glu, s5_b_glu, ret_log_decay, odd_w_in, odd_w_out, attn_sink):
    bsz = x.shape[0]
    outs = []
    for b in range(bsz):
        xb = x[b]
        for layer in range(DEPTH):
            if layer % 2 == 0:
                e = layer // 2
                xb, w1_bf, w2_bf = _even_layer(
                    xb, even_w_in[e], even_w_out[e], s5_lambda_re[e], s5_lambda_im[e],
                    s5_log_step[e], s5_b_re[e], s5_b_im[e], s5_c_re[e], s5_c_im[e],
                    s5_d[e], s5_w_glu[e], s5_b_glu[e], ret_log_decay[e],
                    ln_g[layer, 0], ln_b[layer, 0], mlp_w1, mlp_w2, layer)
            else:
                o = layer // 2
                xb, w1_bf, w2_bf = _odd_layer(xb, odd_w_in[o], odd_w_out[o], attn_sink[o],
                                              ln_g[layer, 0], ln_b[layer, 0], mlp_w1, mlp_w2, layer)
            xb = _mlp(xb, w1_bf, w2_bf,
                      ln_g[layer, 1].reshape(1, -1), ln_b[layer, 1].reshape(1, -1))
        outs.append(xb)
    return jnp.stack(outs, axis=0)
```

```python
import functools
import math

import jax
import jax.numpy as jnp
from jax import lax
from jax.experimental import pallas as pl
from jax.experimental.pallas import tpu as pltpu

F32 = jnp.float32
BF16 = jnp.bfloat16

D_MODEL = 2048
DEPTH = 2
S5_WIDTH = 1024
S5_GROUP = 16
S5_GROUPS = 64
S5_STATE = 64
RET_WIDTH = 1024
RET_HEADS = 4
RET_HEAD_DIM = 256
RET_ROPE_THETA = 10000.0
ATT_HEADS = 16
ATT_KV_HEADS = 4
ATT_HEAD_DIM = 128
ATT_GROUP = 4
ATT_WINDOW = 128
ATT_BLOCK = 128
ROPE_THETA = 500000.0
ROPE_DIM = 32
D_FF = 4 * D_MODEL
DEEPNORM_ALPHA = (2 * DEPTH) ** 0.25
LN_EPS = 1e-5
HEAD_NORM_EPS = 1e-6
NEG_INF = -1e30

V7X_VMEM_BYTES = 64 * 1024 * 1024
VMEM_LIMIT = V7X_VMEM_BYTES - 8 * 1024 * 1024

S5_T = 32
S5_TL = S5_T * S5_GROUP
S5_GB = 8
S5_GEN_GB = 8
RET_C = 256
LN_ROWS = 256


def _params(sem):
    return pltpu.CompilerParams(dimension_semantics=sem, vmem_limit_bytes=VMEM_LIMIT)


def _layer_norm_rows(y, g, b):
    mu = jnp.mean(y, axis=-1, keepdims=True)
    yc = y - mu
    var = jnp.mean(yc * yc, axis=-1, keepdims=True)
    return yc * lax.rsqrt(var + LN_EPS) * g + b


def _block_transpose8(xs):
    blk = lax.shift_right_logical(lax.broadcasted_iota(jnp.int32, xs[0].shape, 1), 4)
    xs = list(xs)
    for k in range(3):
        d = 1 << k
        upper = (blk & d) != 0
        for i in range(8):
            if i & d:
                continue
            a, b = xs[i], xs[i + d]
            xs[i] = jnp.where(upper, pltpu.roll(b, S5_GROUP * d, axis=1), a)
            xs[i + d] = jnp.where(upper, b, pltpu.roll(a, 128 - S5_GROUP * d, axis=1))
    return xs


SEG_U, SEG_Q, SEG_K, SEG_V, SEG_GATE = range(5)


def _row_permutation(n_rows, src_of_row):
    r = lax.broadcasted_iota(jnp.int32, (n_rows, n_rows), 0)
    c = lax.broadcasted_iota(jnp.int32, (n_rows, n_rows), 1)
    return jnp.where(c == src_of_row(r), 1.0, 0.0).astype(BF16)


CAST_SLICES_PER_ROW_BLOCK = 4


def _mlp_weight_cast_specs(n_row_blocks, mlp_w1, mlp_w2, layer):
    pieces = n_row_blocks * CAST_SLICES_PER_ROW_BLOCK
    piece = lambda i, j: i * CAST_SLICES_PER_ROW_BLOCK + jnp.minimum(j, CAST_SLICES_PER_ROW_BLOCK - 1)
    ins, outs, shapes = [], [], []
    for w in (mlp_w1, mlp_w2):
        rows, cols = w.shape[1] // pieces, w.shape[2]
        ins.append(pl.BlockSpec((None, rows, cols), lambda i, j: (layer, piece(i, j), 0)))
        outs.append(pl.BlockSpec((rows, cols), lambda i, j: (piece(i, j), 0)))
        shapes.append(jax.ShapeDtypeStruct(w.shape[1:], BF16))
    return ins, outs, shapes


def _inproj_even_body(x_ref, w_ref, t1_ref, t2_ref, w1f_ref, w2f_ref, u_ref, o_ref, w1b_ref, w2b_ref,
                      xb_ref):
    j = pl.program_id(1)
    w1b_ref[...] = w1f_ref[...].astype(BF16)
    w2b_ref[...] = w2f_ref[...].astype(BF16)

    @pl.when(j == 0)
    def _():
        xb_ref[...] = x_ref[...].astype(BF16)
        u = jnp.dot(xb_ref[...], w_ref[...], preferred_element_type=F32).astype(BF16)
        tm = u.shape[0]
        nl = tm // S5_T
        shift = nl.bit_length() - 1
        perm = _row_permutation(tm, lambda r: (r & (nl - 1)) * S5_T + lax.shift_right_logical(r, shift))
        up = jnp.dot(perm, u, preferred_element_type=F32).astype(BF16)
        for t in range(S5_T):
            u_ref[t] = up[t * nl:(t + 1) * nl, :]

    @pl.when(j > 0)
    def _():
        acc = jnp.dot(xb_ref[...], w_ref[...], preferred_element_type=F32)
        rotate = (j == SEG_Q) | (j == SEG_K)
        scale = jnp.where(j == SEG_K, RET_HEAD_DIM ** -0.5, 1.0)
        t1 = jnp.where(rotate, t1_ref[...] * scale, 1.0)
        t2 = jnp.where(rotate, t2_ref[...] * scale, 0.0)
        is_gate = j == SEG_GATE
        half = RET_HEAD_DIM // 2
        for hh in range(RET_HEADS):
            lo = hh * RET_HEAD_DIM
            a = acc[:, lo:lo + half]
            b = acc[:, lo + half:lo + RET_HEAD_DIM]
            for off, r in ((0, a * t1 - b * t2), (half, b * t1 + a * t2)):
                r = jnp.where(is_gate, r * jax.nn.sigmoid(r), r)
                o_ref[:, lo + off:lo + off + half] = r.astype(BF16)


def _inproj_even(x, w_in_bf, t1, t2, mlp_w1, mlp_w2, layer, tm=1024):
    L = x.shape[0]
    tn = S5_WIDTH
    nseg = w_in_bf.shape[1] // tn
    assert nseg >= CAST_SLICES_PER_ROW_BLOCK
    tab = pl.BlockSpec((tm, RET_HEAD_DIM // 2), lambda i, j: (i, 0))
    cast_in, cast_out, cast_shapes = _mlp_weight_cast_specs(L // tm, mlp_w1, mlp_w2, layer)
    return pl.pallas_call(
        _inproj_even_body,
        grid=(L // tm, nseg),
        in_specs=[
            pl.BlockSpec((tm, D_MODEL), lambda i, j: (i, 0)),
            pl.BlockSpec((D_MODEL, tn), lambda i, j: (0, j)),
            tab, tab,
        ] + cast_in,
        out_specs=[pl.BlockSpec((S5_T, tm // S5_T, tn), lambda i, j: (0, i, 0)),
                   pl.BlockSpec((None, tm, tn), lambda i, j: (jnp.maximum(j - 1, 0), i, 0))] + cast_out,
        out_shape=[jax.ShapeDtypeStruct((S5_T, L // S5_T, tn), BF16),
                   jax.ShapeDtypeStruct((nseg - 1, L, tn), BF16)] + cast_shapes,
        scratch_shapes=[pltpu.VMEM((tm, D_MODEL), BF16)],
        compiler_params=_params(("parallel", "arbitrary")),
        name="inproj_even",
    )(x, w_in_bf, t1, t2, mlp_w1, mlp_w2)


def _cmul(ar, ai, br, bi):
    return ar * br - ai * bi, ar * bi + ai * br


(DISC_A1_RE, DISC_A1_IM, DISC_A2_RE, DISC_A2_IM, DISC_A4_RE, DISC_A4_IM, DISC_A8_RE, DISC_A8_IM,
 DISC_A16_RE, DISC_A16_IM, DISC_AT_RE, DISC_AT_IM, DISC_Z_RE, DISC_Z_IM, DISC_SCAN_A2, DISC_SCAN_A3) = range(16)


def _s5_disc_body(lr_ref, li_ref, ls_ref, sg_ref, o_ref):
    lr = jnp.minimum(lr_ref[...], -1e-4)
    li = li_ref[...]
    step = jnp.exp(ls_ref[...])
    mag = jnp.exp(lr * step)
    ar = mag * jnp.cos(li * step)
    ai = mag * jnp.sin(li * step)
    nr, ni = ar - 1.0, ai
    den = lr * lr + li * li
    o_ref[DISC_Z_RE] = (nr * lr + ni * li) / den
    o_ref[DISC_Z_IM] = (ni * lr - nr * li) / den
    pr, pi = ar, ai
    for k in range(6):
        o_ref[2 * k] = pr
        o_ref[2 * k + 1] = pi
        if k < 5:
            pr, pi = _cmul(pr, pi, pr, pi)
    o_ref[DISC_SCAN_A2] = pi * sg_ref[...]
    o_ref[DISC_SCAN_A3] = -pi * sg_ref[...]


def _s5_disc(lam_re, lam_im, log_step):
    assert S5_T == 32
    rows = 2 * S5_GROUPS
    two = lambda a: jnp.tile(a.astype(F32).reshape(rows, -1), (1, 2))
    lr = two(lam_re)
    li = two(lam_im)
    ls = jnp.broadcast_to(log_step.astype(F32).reshape(rows, 1), (rows, 2 * S5_STATE))
    sg = jnp.broadcast_to(jnp.concatenate([-jnp.ones((S5_STATE,), F32), jnp.ones((S5_STATE,), F32)])[None],
                          (rows, 2 * S5_STATE))
    return pl.pallas_call(
        _s5_disc_body,
        out_shape=jax.ShapeDtypeStruct((16, rows, 2 * S5_STATE), F32),
        name="s5_disc",
    )(lr, li, ls, sg)


def _s5_gen_body(col_ref, zrow_ref, bt_ref, btile_ref, ctile_ref, m_ref, ws_ref, wc_ref):
    P = S5_STATE
    hi = lax.Precision.HIGHEST
    tlo = lax.shift_right_logical(lax.broadcasted_iota(jnp.int32, (P, 128), 1), 4)
    lane = lax.broadcasted_iota(jnp.int32, (S5_GROUP, S5_TL), 1)
    ones = jnp.ones((P, 128), F32)
    zeros = jnp.zeros((P, 128), F32)

    def one_group(gi, carry):
        kt = []
        for d in range(2):
            col = col_ref[gi, d]
            c = lambda k: jnp.broadcast_to(col[:, k:k + 1], (P, 128))
            a1 = (c(DISC_A1_RE), c(DISC_A1_IM))
            a2 = (c(DISC_A2_RE), c(DISC_A2_IM))
            a4 = (c(DISC_A4_RE), c(DISC_A4_IM))
            a8 = (c(DISC_A8_RE), c(DISC_A8_IM))
            a16 = (c(DISC_A16_RE), c(DISC_A16_IM))
            blk = [None, a8, a16, _cmul(*a8, *a16)]

            def low_powers(reverse):
                xr, xi = ones, zeros
                for k, ak in enumerate((a1, a2, a4)):
                    bit = (lax.shift_right_logical(tlo, k) & 1) == (0 if reverse else 1)
                    yr, yi = _cmul(xr, xi, *ak)
                    xr = jnp.where(bit, yr, xr)
                    xi = jnp.where(bit, yi, xi)
                return xr, xi

            def expand(base, reverse):
                out = []
                for j in range(4):
                    f = blk[3 - j] if reverse else blk[j]
                    out.append(base if f is None else _cmul(*base, *f))
                return out

            ct = (ctile_ref[gi, d, 0], ctile_ref[gi, d, 1])
            bbar = _cmul(c(DISC_Z_RE), c(DISC_Z_IM), btile_ref[gi, d, 0], btile_ref[gi, d, 1])
            zrow = zrow_ref[gi, d]
            zr_row, zi_row = zrow[0:1, :], zrow[1:2, :]
            bbt_r, bbt_i = _cmul(zr_row, zi_row, bt_ref[gi, d, 0], bt_ref[gi, d, 1])
            ca = expand(_cmul(*ct, *low_powers(d == 1)), d == 1)
            ba = expand(_cmul(*bbar, *low_powers(d == 0)), d == 0)
            wcj = [_cmul(*x, *a1) for x in ca]
            cat = lambda parts, k: jnp.concatenate([x[k] for x in parts], axis=1)
            kt.append(jnp.dot(bbt_r, cat(ca, 0), precision=hi, preferred_element_type=F32)
                      - jnp.dot(bbt_i, cat(ca, 1), precision=hi, preferred_element_type=F32))
            ba_r, ba_i = cat(ba, 0).astype(BF16), cat(ba, 1).astype(BF16)
            for r, part in enumerate((ba_r, ba_i, ba_i, ba_r)):
                ws_ref[gi, (4 * d + r) * P:(4 * d + r + 1) * P, :] = part
            wc_ref[gi, 2 * d * P:(2 * d + 1) * P, :] = cat(wcj, 0).astype(BF16)
            wc_ref[gi, (2 * d + 1) * P:(2 * d + 2) * P, :] = (-cat(wcj, 1)).astype(BF16)
        ktf, ktb = kt
        for s in range(S5_T):
            lo, hi_lane = S5_GROUP * s, S5_GROUP * (s + 1)
            f = ktf if s == 0 else jnp.where(lane >= lo, pltpu.roll(ktf, lo, axis=1), 0.0)
            b = ktb if s == S5_T - 1 else jnp.where(lane < hi_lane, pltpu.roll(ktb, hi_lane, axis=1), 0.0)
            m_ref[gi, lo:hi_lane, :] = (f + b).astype(BF16)
        return carry

    lax.fori_loop(0, col_ref.shape[0], one_group, 0)


def _s5_gen(disc, b_re, b_im, c_re, c_im):
    G, P, Cg = S5_GROUPS, S5_STATE, S5_GROUP
    gb = S5_GEN_GB
    d4 = disc[:, :, :P].reshape(16, 2, G, P)
    col = d4.transpose(2, 1, 3, 0)
    zrow = d4[DISC_Z_RE:DISC_Z_IM + 1].transpose(2, 1, 0, 3)
    b = jnp.stack([b_re, b_im], axis=1).astype(F32)
    c = jnp.stack([c_re, c_im], axis=1).astype(F32)
    bt = b.transpose(2, 0, 1, 4, 3)
    btile = jnp.tile(b.transpose(2, 0, 1, 3, 4), (1, 1, 1, 1, 128 // Cg))
    ctile = jnp.tile(c.transpose(2, 0, 1, 4, 3), (1, 1, 1, 1, 128 // Cg))
    spec = lambda a: pl.BlockSpec((gb,) + a.shape[1:], lambda i: (i,) + (0,) * (a.ndim - 1))
    out = lambda rows: pl.BlockSpec((gb, rows, S5_TL), lambda i: (i, 0, 0))
    return pl.pallas_call(
        _s5_gen_body,
        grid=(G // gb,),
        in_specs=[spec(col), spec(zrow), spec(bt), spec(btile), spec(ctile)],
        out_specs=[out(S5_TL), out(8 * P), out(4 * P)],
        out_shape=[jax.ShapeDtypeStruct((G, S5_TL, S5_TL), BF16),
                   jax.ShapeDtypeStruct((G, 8 * P, S5_TL), BF16),
                   jax.ShapeDtypeStruct((G, 4 * P, S5_TL), BF16)],
        compiler_params=_params(("parallel",)),
        name="s5_gen",
    )(col, zrow, bt, btile, ctile)


def _s5_body(ut_ref, m_ref, ws_ref, wc_ref, dt_ref, a1_ref, a2_ref, a3_ref, y_ref,
             s_ref, h_ref, yg_ref, u_ref):
    nc = u_ref.shape[1]
    gb = u_ref.shape[0]
    w = gb * 128
    for jb in range(S5_T // 8):
        ys = _block_transpose8([ut_ref[8 * jb + t8].astype(F32) for t8 in range(8)])
        for g8 in range(8):
            u_ref[g8, :, jb * 128:(jb + 1) * 128] = ys[g8].astype(BF16)
    for gi in range(gb):
        s = lax.dot_general(u_ref[gi], ws_ref[gi], (((1,), (1,)), ((), ())),
                            preferred_element_type=F32)
        for r in range(4):
            s_ref[:, r * w + gi * 128:r * w + (gi + 1) * 128] = s[:, r * 128:(r + 1) * 128]

    a1f, a2f, a3f = a1_ref[0], a2_ref[0], a3_ref[0]
    a1b, a2b, a3b = a1_ref[1], a2_ref[1], a3_ref[1]

    def step(n, carry):
        hf, gf, hb, gb_ = carry
        m = nc - 1 - n
        h_ref[pl.ds(n, 1), 0:w] = hf
        h_ref[pl.ds(m, 1), w:2 * w] = hb
        sfh = s_ref[pl.ds(n, 1), 0:w]
        sfg = s_ref[pl.ds(n, 1), w:2 * w]
        sbh = s_ref[pl.ds(m, 1), 2 * w:3 * w]
        sbg = s_ref[pl.ds(m, 1), 3 * w:4 * w]
        hf2 = a1f * hf + a2f * gf + sfh
        gf2 = a1f * gf + a3f * hf + sfg
        hb2 = a1b * hb + a2b * gb_ + sbh
        gb2 = a1b * gb_ + a3b * hb + sbg
        return hf2, gf2, hb2, gb2

    z = jnp.zeros((1, w), F32)
    lax.fori_loop(0, nc, step, (z, z, z, z))

    for gi in range(gb):
        u = u_ref[gi]
        hcat = jnp.concatenate([h_ref[:, gi * 128:(gi + 1) * 128],
                                h_ref[:, w + gi * 128:w + (gi + 1) * 128]], axis=1).astype(BF16)
        y = (jnp.dot(u, m_ref[gi], preferred_element_type=F32)
             + jnp.dot(hcat, wc_ref[gi], preferred_element_type=F32)
             + dt_ref[gi] * u.astype(F32))
        yg_ref[gi] = jax.nn.gelu(y)

    for jb in range(S5_T // 8):
        zs = _block_transpose8([yg_ref[g8, :, jb * 128:(jb + 1) * 128] for g8 in range(8)])
        for t8 in range(8):
            y_ref[8 * jb + t8] = zs[t8].astype(BF16)


def _s5_mix(u_steps, m, ws_t, wc, d_tile, a1, a2, a3):
    _, nc, width = u_steps.shape
    G, TL = S5_GROUPS, S5_TL
    gb = S5_GB
    assert gb * S5_GROUP == 128
    w = gb * 128
    steps_spec = pl.BlockSpec((S5_T, nc, 128), lambda i: (0, 0, i))
    a_spec = pl.BlockSpec((2, 1, w), lambda i: (0, 0, i))
    return pl.pallas_call(
        _s5_body,
        grid=(G // gb,),
        in_specs=[
            steps_spec,
            pl.BlockSpec((gb, TL, TL), lambda i: (i, 0, 0)),
            pl.BlockSpec((gb, 512, TL), lambda i: (i, 0, 0)),
            pl.BlockSpec((gb, 256, TL), lambda i: (i, 0, 0)),
            pl.BlockSpec((gb, 1, TL), lambda i: (i, 0, 0)),
            a_spec, a_spec, a_spec,
        ],
        out_specs=steps_spec,
        out_shape=jax.ShapeDtypeStruct((S5_T, nc, width), BF16),
        scratch_shapes=[pltpu.VMEM((nc, 4 * w), F32), pltpu.VMEM((nc, 2 * w), F32),
                        pltpu.VMEM((gb, nc, TL), F32), pltpu.VMEM((gb, nc, TL), BF16)],
        compiler_params=_params(("parallel",)),
        name="s5_mix",
    )(u_steps, m, ws_t, wc, d_tile, a1, a2, a3)


def _row_index(n):
    return lax.broadcasted_iota(jnp.int32, (n, 1), 0).astype(F32)


def _log_decay(lg_ref, d, h):
    return -jnp.abs(jnp.full((1, 1), lg_ref[d, h], F32))


def _ret_bstate_body(lg_ref, k_ref, v_ref, sb_ref, st_ref):
    C = k_ref.shape[0]

    @pl.when(pl.program_id(0) == 0)
    def _():
        st_ref[...] = jnp.zeros_like(st_ref)

    jj = _row_index(C)
    for h in range(RET_HEADS):
        lo = h * RET_HEAD_DIM
        lgb = _log_decay(lg_ref, 1, h)
        sb_ref[h] = st_ref[h].astype(BF16)
        kd = (k_ref[:, lo:lo + RET_HEAD_DIM].astype(F32) * jnp.exp(jj * lgb)).astype(BF16)
        kv = lax.dot_general(kd, v_ref[:, lo:lo + RET_HEAD_DIM], (((0,), (0,)), ((), ())),
                             preferred_element_type=F32)
        st_ref[h] = st_ref[h] * jnp.exp(C * lgb) + kv


def _ret_bstate(lg, h5):
    L = h5.shape[1]
    C = RET_C
    nc = L // C
    return pl.pallas_call(
        _ret_bstate_body,
        grid=(nc,),
        in_specs=[
            pl.BlockSpec(memory_space=pltpu.SMEM),
            pl.BlockSpec((None, C, RET_WIDTH), lambda i: (SEG_K - 1, nc - 1 - i, 0)),
            pl.BlockSpec((None, C, RET_WIDTH), lambda i: (SEG_V - 1, nc - 1 - i, 0)),
        ],
        out_specs=pl.BlockSpec((None, RET_HEADS, RET_HEAD_DIM, RET_HEAD_DIM),
                               lambda i: (nc - 1 - i, 0, 0, 0)),
        out_shape=jax.ShapeDtypeStruct((nc, RET_HEADS, RET_HEAD_DIM, RET_HEAD_DIM), BF16),
        scratch_shapes=[pltpu.VMEM((RET_HEADS, RET_HEAD_DIM, RET_HEAD_DIM), F32)],
        compiler_params=_params(("arbitrary",)),
        name="ret_bstate",
    )(lg, h5, h5)


def _ret_main_body(lg_ref, q_ref, k_ref, v_ref, g_ref, sb_ref, o_ref, st_ref, dec_ref):
    C = q_ref.shape[0]

    @pl.when(pl.program_id(0) == 0)
    def _():
        st_ref[...] = jnp.zeros_like(st_ref)
        diff = (lax.broadcasted_iota(jnp.int32, (C, C), 0)
                - lax.broadcasted_iota(jnp.int32, (C, C), 1)).astype(F32)
        for h in range(RET_HEADS):
            dec_ref[h] = jnp.where(diff >= 0, jnp.exp(_log_decay(lg_ref, 0, h) * jnp.maximum(diff, 0.0)),
                                   jnp.exp(_log_decay(lg_ref, 1, h) * jnp.maximum(-diff, 0.0)))

    ii = _row_index(C)
    for h in range(RET_HEADS):
        lo = h * RET_HEAD_DIM
        lgf = _log_decay(lg_ref, 0, h)
        lgb = _log_decay(lg_ref, 1, h)
        q = q_ref[:, lo:lo + RET_HEAD_DIM]
        k = k_ref[:, lo:lo + RET_HEAD_DIM]
        v = v_ref[:, lo:lo + RET_HEAD_DIM]
        qf = q.astype(F32)
        s = lax.dot_general(q, k, (((1,), (1,)), ((), ())), preferred_element_type=F32)
        o = jnp.dot((s * dec_ref[h]).astype(BF16), v, preferred_element_type=F32)
        qdf = (qf * jnp.exp((ii + 1.0) * lgf)).astype(BF16)
        o = o + jnp.dot(qdf, st_ref[h].astype(BF16), preferred_element_type=F32)
        qdb = (qf * jnp.exp((C - ii) * lgb)).astype(BF16)
        o = o + jnp.dot(qdb, sb_ref[h], preferred_element_type=F32)
        mu = jnp.mean(o, axis=-1, keepdims=True)
        oc = o - mu
        var = jnp.mean(oc * oc, axis=-1, keepdims=True)
        on = oc * lax.rsqrt(var + HEAD_NORM_EPS)
        o_ref[:, lo:lo + RET_HEAD_DIM] = (on * g_ref[:, lo:lo + RET_HEAD_DIM].astype(F32)).astype(BF16)
        kd = (k.astype(F32) * jnp.exp((C - 1.0 - ii) * lgf)).astype(BF16)
        kv = lax.dot_general(kd, v, (((0,), (0,)), ((), ())), preferred_element_type=F32)
        st_ref[h] = st_ref[h] * jnp.exp(C * lgf) + kv


def _ret_main(lg, h5, sb):
    L = h5.shape[1]
    C = RET_C
    nc = L // C
    seg = lambda s: pl.BlockSpec((None, C, RET_WIDTH), lambda i: (s - 1, i, 0))
    return pl.pallas_call(
        _ret_main_body,
        grid=(nc,),
        in_specs=[
            pl.BlockSpec(memory_space=pltpu.SMEM),
            seg(SEG_Q), seg(SEG_K), seg(SEG_V), seg(SEG_GATE),
            pl.BlockSpec((None, RET_HEADS, RET_HEAD_DIM, RET_HEAD_DIM), lambda i: (i, 0, 0, 0)),
        ],
        out_specs=pl.BlockSpec((C, RET_WIDTH), lambda i: (i, 0)),
        out_shape=jax.ShapeDtypeStruct((L, RET_WIDTH), BF16),
        scratch_shapes=[pltpu.VMEM((RET_HEADS, RET_HEAD_DIM, RET_HEAD_DIM), F32),
                        pltpu.VMEM((RET_HEADS, C, C), F32)],
        compiler_params=_params(("arbitrary",)),
        name="ret_main",
    )(lg, h5, h5, h5, h5, sb)


def _outproj_even_body(y_ref, r_ref, x_ref, wg_ref, bg_ref, wo_ref, g_ref, b_ref, o_ref):
    nl = y_ref.shape[1]
    tm = S5_T * nl
    shift = S5_T.bit_length() - 1
    y_steps = y_ref[...].reshape(tm, y_ref.shape[2])
    perm = _row_permutation(tm, lambda r: (r & (S5_T - 1)) * nl + lax.shift_right_logical(r, shift))
    for r in range(0, tm, LN_ROWS):
        rows = slice(r, r + LN_ROWS)
        y = jnp.dot(perm[rows, :], y_steps, preferred_element_type=F32).astype(BF16)
        z = jnp.dot(y, wg_ref[...], preferred_element_type=F32) + bg_ref[...]
        s5 = (y.astype(F32) * jax.nn.sigmoid(z)).astype(BF16)
        mix = (jnp.dot(s5, wo_ref[0:S5_WIDTH, :], preferred_element_type=F32)
               + jnp.dot(r_ref[rows, :], wo_ref[S5_WIDTH:S5_WIDTH + RET_WIDTH, :],
                         preferred_element_type=F32))
        o_ref[rows, :] = _layer_norm_rows(DEEPNORM_ALPHA * x_ref[rows, :] + mix, g_ref[...], b_ref[...])


def _outproj_even(y_steps, ret, x, w_glu_bf, b_glu, w_out_bf, ln_g, ln_b, tm=512):
    L = x.shape[0]
    row = lambda n: pl.BlockSpec((tm, n), lambda i: (i, 0))
    full = lambda a: pl.BlockSpec(a.shape, lambda i: (0,) * a.ndim)
    return pl.pallas_call(
        _outproj_even_body,
        grid=(L // tm,),
        in_specs=[pl.BlockSpec((S5_T, tm // S5_T, S5_WIDTH), lambda i: (0, i, 0)),
                  row(RET_WIDTH), row(D_MODEL), full(w_glu_bf), full(b_glu),
                  full(w_out_bf), full(ln_g), full(ln_b)],
        out_specs=row(D_MODEL),
        out_shape=jax.ShapeDtypeStruct((L, D_MODEL), F32),
        compiler_params=_params(("parallel",)),
        name="outproj_even",
    )(y_steps, ret, x, w_glu_bf, b_glu, w_out_bf, ln_g, ln_b)


def _inproj_odd_body(x_ref, w_ref, c_ref, s1_ref, s2_ref, w1f_ref, w2f_ref, o_ref, w1b_ref, w2b_ref,
                     xb_ref):
    j = pl.program_id(1)
    w1b_ref[...] = w1f_ref[...].astype(BF16)
    w2b_ref[...] = w2f_ref[...].astype(BF16)

    @pl.when(j == 0)
    def _():
        xb_ref[...] = x_ref[...].astype(BF16)

    acc = jnp.dot(xb_ref[...], w_ref[...], preferred_element_type=F32)
    tn = acc.shape[1]
    nq = (ATT_HEADS * ATT_HEAD_DIM) // tn
    is_v = j > nq
    scale = jnp.where(j < nq, ATT_HEAD_DIM ** -0.5, 1.0)
    cc = jnp.where(is_v, 1.0, c_ref[...] * scale)
    s1 = jnp.where(is_v, 0.0, s1_ref[...] * scale)
    s2 = jnp.where(is_v, 0.0, s2_ref[...] * scale)
    for hh in range(tn // ATT_HEAD_DIM):
        lo = hh * ATT_HEAD_DIM
        a = acc[:, lo:lo + ATT_HEAD_DIM]
        up = pltpu.roll(a, ATT_HEAD_DIM - ROPE_DIM // 2, axis=1)
        dn = pltpu.roll(a, ROPE_DIM // 2, axis=1)
        o_ref[:, lo:lo + ATT_HEAD_DIM] = (a * cc + up * s1 + dn * s2).astype(BF16)


def _inproj_odd(x, w_in_bf, cc, s1, s2, mlp_w1, mlp_w2, layer, tm=1024, tn=512):
    L = x.shape[0]
    n_out = w_in_bf.shape[1]
    assert tn == ATT_KV_HEADS * ATT_HEAD_DIM and n_out // tn >= CAST_SLICES_PER_ROW_BLOCK
    tab = pl.BlockSpec((tm, ATT_HEAD_DIM), lambda i, j: (i, 0))
    cast_in, cast_out, cast_shapes = _mlp_weight_cast_specs(L // tm, mlp_w1, mlp_w2, layer)
    return pl.pallas_call(
        _inproj_odd_body,
        grid=(L // tm, n_out // tn),
        in_specs=[
            pl.BlockSpec((tm, D_MODEL), lambda i, j: (i, 0)),
            pl.BlockSpec((D_MODEL, tn), lambda i, j: (0, j)),
            tab, tab, tab,
        ] + cast_in,
        out_specs=[pl.BlockSpec((tm, tn), lambda i, j: (i, j))] + cast_out,
        out_shape=[jax.ShapeDtypeStruct((L, n_out), BF16)] + cast_shapes,
        scratch_shapes=[pltpu.VMEM((tm, D_MODEL), BF16)],
        compiler_params=_params(("parallel", "arbitrary")),
        name="inproj_odd",
    )(x, w_in_bf, cc, s1, s2, mlp_w1, mlp_w2)


ATT_QB = 2


def _attn_body(sink_ref, q_ref, *refs):
    k_refs = refs[:ATT_QB + 2]
    v_refs = refs[ATT_QB + 2:2 * ATT_QB + 4]
    o_ref, bias_ref = refs[2 * ATT_QB + 4:]
    step = pl.program_id(0)
    nb = pl.num_programs(0) * ATT_QB
    B = ATT_BLOCK
    hd = ATT_HEAD_DIM
    rows = ATT_GROUP * B

    @pl.when(step == 0)
    def _():
        r_i = lax.broadcasted_iota(jnp.int32, (rows, 3 * B), 0)
        s_i = lax.broadcasted_iota(jnp.int32, (rows, 3 * B), 1)
        rel = (r_i & (B - 1)) - s_i + B
        bias_ref[...] = jnp.where(jnp.abs(rel) <= ATT_WINDOW, 0.0, NEG_INF)

    head_of_row = lax.shift_right_logical(lax.broadcasted_iota(jnp.int32, (rows, 1), 0),
                                          int(math.log2(B)))
    for qb in range(ATT_QB):
        c = step * ATT_QB + qb
        lo_bias = jnp.where(c > 0, 0.0, NEG_INF)
        hi_bias = jnp.where(c < nb - 1, 0.0, NEG_INF)
        for g in range(ATT_KV_HEADS):
            q = jnp.concatenate([q_ref[qb * B:(qb + 1) * B, (g * ATT_GROUP + hh) * hd:(g * ATT_GROUP + hh + 1) * hd]
                                 for hh in range(ATT_GROUP)], axis=0)
            ksl = slice(g * hd, (g + 1) * hd)
            k = jnp.concatenate([r[:, ksl] for r in k_refs[qb:qb + 3]], axis=0)
            v = jnp.concatenate([r[:, ksl] for r in v_refs[qb:qb + 3]], axis=0)
            s = lax.dot_general(q, k, (((1,), (1,)), ((), ())), preferred_element_type=F32) + bias_ref[...]
            s = jnp.concatenate([s[:, :B] + lo_bias, s[:, B:2 * B], s[:, 2 * B:] + hi_bias], axis=1)
            sink = jnp.zeros((rows, 1), F32)
            for hh in range(ATT_GROUP):
                sink = jnp.where(head_of_row == hh, sink_ref[g * ATT_GROUP + hh], sink)
            m = jnp.maximum(jnp.max(s, axis=-1, keepdims=True), sink)
            p = jnp.exp(s - m)
            den = jnp.sum(p, axis=-1, keepdims=True) + jnp.exp(sink - m)
            o = jnp.dot(p.astype(BF16), v, preferred_element_type=F32) / den
            for hh in range(ATT_GROUP):
                hcol = (g * ATT_GROUP + hh) * hd
                o_ref[qb * B:(qb + 1) * B, hcol:hcol + hd] = o[hh * B:(hh + 1) * B, :].astype(BF16)


def _attention(sink, qkv):
    L = qkv.shape[0]
    B = ATT_BLOCK
    nb = L // B
    kvw = ATT_KV_HEADS * ATT_HEAD_DIM
    qw = ATT_HEADS * ATT_HEAD_DIM
    kcol = qw // kvw
    vcol = kcol + 1

    def kv_spec(col, off):
        return pl.BlockSpec((B, kvw), lambda i: (jnp.clip(i * ATT_QB + off, 0, nb - 1), col))

    offs = range(-1, ATT_QB + 1)
    return pl.pallas_call(
        _attn_body,
        grid=(nb // ATT_QB,),
        in_specs=[
            pl.BlockSpec(memory_space=pltpu.SMEM),
            pl.BlockSpec((ATT_QB * B, qw), lambda i: (i, 0)),
        ] + [kv_spec(kcol, o) for o in offs] + [kv_spec(vcol, o) for o in offs],
        out_specs=pl.BlockSpec((ATT_QB * B, qw), lambda i: (i, 0)),
        out_shape=jax.ShapeDtypeStruct((L, qw), BF16),
        scratch_shapes=[pltpu.VMEM((ATT_GROUP * B, 3 * B), F32)],
        compiler_params=_params(("arbitrary",)),
        name="attention",
    )(sink, qkv, *([qkv] * (2 * ATT_QB + 4)))


def _outproj_odd_body(a_ref, x_ref, wo_ref, g_ref, b_ref, o_ref):
    for r in range(0, a_ref.shape[0], LN_ROWS):
        rows = slice(r, r + LN_ROWS)
        mix = jnp.dot(a_ref[rows, :], wo_ref[...], preferred_element_type=F32)
        o_ref[rows, :] = _layer_norm_rows(DEEPNORM_ALPHA * x_ref[rows, :] + mix, g_ref[...], b_ref[...])


def _outproj_odd(a, x, w_out_bf, ln_g, ln_b, tm=512):
    L = x.shape[0]
    row = lambda n: pl.BlockSpec((tm, n), lambda i: (i, 0))
    full = lambda t: pl.BlockSpec(t.shape, lambda i: (0,) * t.ndim)
    return pl.pallas_call(
        _outproj_odd_body,
        grid=(L // tm,),
        in_specs=[row(a.shape[1]), row(D_MODEL), full(w_out_bf), full(ln_g), full(ln_b)],
        out_specs=row(D_MODEL),
        out_shape=jax.ShapeDtypeStruct((L, D_MODEL), F32),
        compiler_params=_params(("parallel",)),
        name="outproj_odd",
    )(a, x, w_out_bf, ln_g, ln_b)


def _mlp_body(x_ref, w1_ref, w2_ref, g_ref, b_ref, o_ref, xb_ref, acc_ref):
    f = pl.program_id(1)

    @pl.when(f == 0)
    def _():
        xb_ref[...] = x_ref[...].astype(BF16)
        acc_ref[...] = jnp.zeros_like(acc_ref)

    def partial_sum(rows):
        h = jnp.dot(xb_ref[rows, :], w1_ref[...], preferred_element_type=F32)
        h = jnp.square(jnp.maximum(h, 0.0)).astype(BF16)
        return jnp.dot(h, w2_ref[...], preferred_element_type=F32)

    last = pl.num_programs(1) - 1

    @pl.when(f < last)
    def _():
        acc_ref[...] += partial_sum(slice(None))

    @pl.when(f == last)
    def _():
        for r in range(0, x_ref.shape[0], LN_ROWS):
            rows = slice(r, r + LN_ROWS)
            y = DEEPNORM_ALPHA * x_ref[rows, :] + (acc_ref[rows, :] + partial_sum(rows))
            o_ref[rows, :] = _layer_norm_rows(y, g_ref[...], b_ref[...])


def _mlp(x, w1, w2, ln_g, ln_b, tm=512, tf=1024):
    L = x.shape[0]
    return pl.pallas_call(
        _mlp_body,
        grid=(L // tm, D_FF // tf),
        in_specs=[
            pl.BlockSpec((tm, D_MODEL), lambda i, f: (i, 0)),
            pl.BlockSpec((D_MODEL, tf), lambda i, f: (0, f)),
            pl.BlockSpec((tf, D_MODEL), lambda i, f: (f, 0)),
            pl.BlockSpec((1, D_MODEL), lambda i, f: (0, 0)),
            pl.BlockSpec((1, D_MODEL), lambda i, f: (0, 0)),
        ],
        out_specs=pl.BlockSpec((tm, D_MODEL), lambda i, f: (i, 0)),
        out_shape=jax.ShapeDtypeStruct((L, D_MODEL), F32),
        scratch_shapes=[pltpu.VMEM((tm, D_MODEL), BF16), pltpu.VMEM((tm, D_MODEL), F32)],
        compiler_params=_params(("parallel", "arbitrary")),
        name="mlp",
    )(x, w1, w2, ln_g, ln_b)


def _rotary_tables(L, rot_dim, theta):
    half = rot_dim // 2
    inv_freq = 1.0 / (theta ** (jnp.arange(half, dtype=F32) / half))
    ang = jnp.arange(L).astype(F32)[:, None] * inv_freq[None, :]
    return jnp.cos(ang), jnp.sin(ang)


def _even_layer(x, w_in, w_out, lam_re, lam_im, log_step, b_re, b_im, c_re, c_im,
                d_skip, w_glu, b_glu, ret_log_decay, ln_g, ln_b, mlp_w1, mlp_w2, layer):
    L = x.shape[0]
    cos, sin = _rotary_tables(L, RET_HEAD_DIM, RET_ROPE_THETA)
    u_steps, h5, w1_bf, w2_bf = _inproj_even(x, w_in.astype(BF16), cos, sin, mlp_w1, mlp_w2, layer)
    disc = _s5_disc(lam_re, lam_im, log_step)
    m, ws_t, wc = _s5_gen(disc, b_re, b_im, c_re, c_im)
    d_tile = jnp.tile(d_skip.astype(F32), (1, S5_T)).reshape(S5_GROUPS, 1, S5_TL)
    a1, a2, a3 = (disc[k].reshape(2, 1, S5_GROUPS * 128) for k in (DISC_AT_RE, DISC_SCAN_A2, DISC_SCAN_A3))
    y = _s5_mix(u_steps, m, ws_t, wc, d_tile, a1, a2, a3)
    lg = ret_log_decay.astype(F32)
    sb = _ret_bstate(lg, h5)
    ret = _ret_main(lg, h5, sb)
    x1 = _outproj_even(y, ret, x, w_glu.astype(BF16), b_glu.astype(F32).reshape(1, -1),
                       w_out.astype(BF16), ln_g.reshape(1, -1), ln_b.reshape(1, -1))
    return x1, w1_bf, w2_bf


def _odd_layer(x, w_in, w_out, sink, ln_g, ln_b, mlp_w1, mlp_w2, layer):
    L = x.shape[0]
    cos, sin = _rotary_tables(L, ROPE_DIM, ROPE_THETA)
    half = ROPE_DIM // 2
    pad = ATT_HEAD_DIM - ROPE_DIM
    cc = jnp.concatenate([cos, cos, jnp.ones((L, pad), F32)], axis=1)
    s1 = jnp.concatenate([-sin, jnp.zeros((L, ATT_HEAD_DIM - half), F32)], axis=1)
    s2 = jnp.concatenate([jnp.zeros((L, half), F32), sin, jnp.zeros((L, pad), F32)], axis=1)
    qkv, w1_bf, w2_bf = _inproj_odd(x, w_in.astype(BF16), cc, s1, s2, mlp_w1, mlp_w2, layer)
    att = _attention(sink.astype(F32), qkv)
    x1 = _outproj_odd(att, x, w_out.astype(BF16), ln_g.reshape(1, -1), ln_b.reshape(1, -1))
    return x1, w1_bf, w2_bf


def kernel(x, ln_g, ln_b, mlp_w1, mlp_w2, even_w_in, even_w_out, s5_lambda_re, s5_lambda_im, s5_log_step, s5_b_re, s5_b_im, s5_c_re, s5_c_im, s5_d, s5_w_glu, s5_b_glu, ret_log_decay, odd_w_in, odd_w_out, attn_sink):
    bsz = x.shape[0]
    outs = []
    for b in range(bsz):
        xb = x[b]
        for layer in range(DEPTH):
            if layer % 2 == 0:
                e = layer // 2
                xb, w1_bf, w2_bf = _even_layer(
                    xb, even_w_in[e], even_w_out[e], s5_lambda_re[e], s5_lambda_im[e],
                    s5_log_step[e], s5_b_re[e], s5_b_im[e], s5_c_re[e], s5_c_im[e],
                    s5_d[e], s5_w_glu[e], s5_b_glu[e], ret_log_decay[e],
                    ln_g[layer, 0], ln_b[layer, 0], mlp_w1, mlp_w2, layer)
            else:
                o = layer // 2
                xb, w1_bf, w2_bf = _odd_layer(xb, odd_w_in[o], odd_w_out[o], attn_sink[o],
                                              ln_g[layer, 0], ln_b[layer, 0], mlp_w1, mlp_w2, layer)
            xb = _mlp(xb, w1_bf, w2_bf,
                      ln_g[layer, 1].reshape(1, -1), ln_b[layer, 1].reshape(1, -1))
        outs.append(xb)
    return jnp.stack(outs, axis=0)
```

```python
import functools
import math

import jax
import jax.numpy as jnp
from jax import lax
from jax.experimental import pallas as pl
from jax.experimental.pallas import tpu as pltpu

F32 = jnp.float32
BF16 = jnp.bfloat16

D_MODEL = 2048
DEPTH = 2
S5_WIDTH = 1024
S5_GROUP = 16
S5_GROUPS = 64
S5_STATE = 64
RET_WIDTH = 1024
RET_HEADS = 4
RET_HEAD_DIM = 256
RET_ROPE_THETA = 10000.0
ATT_HEADS = 16
ATT_KV_HEADS = 4
ATT_HEAD_DIM = 128
ATT_GROUP = 4
ATT_WINDOW = 128
ATT_BLOCK = 128
ROPE_THETA = 500000.0
ROPE_DIM = 32
D_FF = 4 * D_MODEL
DEEPNORM_ALPHA = (2 * DEPTH) ** 0.25
LN_EPS = 1e-5
HEAD_NORM_EPS = 1e-6
NEG_INF = -1e30

V7X_VMEM_BYTES = 64 * 1024 * 1024
VMEM_LIMIT = V7X_VMEM_BYTES - 8 * 1024 * 1024

S5_T = 32
S5_TL = S5_T * S5_GROUP
S5_GB = 8
S5_GEN_GB = 8
RET_C = 256
LN_ROWS = 256


def _params(sem):
    return pltpu.CompilerParams(dimension_semantics=sem, vmem_limit_bytes=VMEM_LIMIT)


def _layer_norm_rows(y, g, b):
    mu = jnp.mean(y, axis=-1, keepdims=True)
    yc = y - mu
    var = jnp.mean(yc * yc, axis=-1, keepdims=True)
    return yc * lax.rsqrt(var + LN_EPS) * g + b


def _block_transpose8(xs):
    blk = lax.shift_right_logical(lax.broadcasted_iota(jnp.int32, xs[0].shape, 1), 4)
    xs = list(xs)
    for k in range(3):
        d = 1 << k
        upper = (blk & d) != 0
        for i in range(8):
            if i & d:
                continue
            a, b = xs[i], xs[i + d]
            xs[i] = jnp.where(upper, pltpu.roll(b, S5_GROUP * d, axis=1), a)
            xs[i + d] = jnp.where(upper, b, pltpu.roll(a, 128 - S5_GROUP * d, axis=1))
    return xs


SEG_U, SEG_Q, SEG_K, SEG_V, SEG_GATE = range(5)


def _row_permutation(n_rows, src_of_row):
    r = lax.broadcasted_iota(jnp.int32, (n_rows, n_rows), 0)
    c = lax.broadcasted_iota(jnp.int32, (n_rows, n_rows), 1)
    return jnp.where(c == src_of_row(r), 1.0, 0.0).astype(BF16)


def _mlp_weight_cast_specs(n_steps, mlp_w1, mlp_w2, layer):
    ins, outs, shapes = [], [], []
    for w in (mlp_w1, mlp_w2):
        rows, cols = w.shape[1] // n_steps, w.shape[2]
        ins.append(pl.BlockSpec((None, rows, cols), lambda i: (layer, i, 0)))
        outs.append(pl.BlockSpec((rows, cols), lambda i: (i, 0)))
        shapes.append(jax.ShapeDtypeStruct(w.shape[1:], BF16))
    return ins, outs, shapes


def _inproj_even_body(x_ref, w_ref, cos_ref, sin_ref, w1f_ref, w2f_ref, u_ref, o_ref, w1b_ref, w2b_ref):
    w1b_ref[...] = w1f_ref[...].astype(BF16)
    w2b_ref[...] = w2f_ref[...].astype(BF16)
    xb = x_ref[...].astype(BF16)
    tm = xb.shape[0]
    tn = S5_WIDTH
    half = RET_HEAD_DIM // 2
    for seg in range(w_ref.shape[1] // tn):
        acc = jnp.dot(xb, w_ref[:, seg * tn:(seg + 1) * tn], preferred_element_type=F32)
        if seg == SEG_U:
            nl = tm // S5_T
            shift = nl.bit_length() - 1
            perm = _row_permutation(tm, lambda r: (r & (nl - 1)) * S5_T + lax.shift_right_logical(r, shift))
            up = jnp.dot(perm, acc.astype(BF16), preferred_element_type=F32)
            for t in range(S5_T):
                u_ref[t] = up[t * nl:(t + 1) * nl, :]
        elif seg in (SEG_Q, SEG_K):
            scale = RET_HEAD_DIM ** -0.5 if seg == SEG_K else 1.0
            cos = cos_ref[...] * scale
            sin = sin_ref[...] * scale
            for hh in range(RET_HEADS):
                lo = hh * RET_HEAD_DIM
                a = acc[:, lo:lo + half]
                b = acc[:, lo + half:lo + RET_HEAD_DIM]
                o_ref[seg - 1, :, lo:lo + half] = (a * cos - b * sin).astype(BF16)
                o_ref[seg - 1, :, lo + half:lo + RET_HEAD_DIM] = (b * cos + a * sin).astype(BF16)
        elif seg == SEG_GATE:
            o_ref[seg - 1] = (acc * jax.nn.sigmoid(acc)).astype(BF16)
        else:
            o_ref[seg - 1] = acc.astype(BF16)


def _inproj_even(x, w_in_bf, cos, sin, mlp_w1, mlp_w2, layer, tm=256):
    L = x.shape[0]
    tn = S5_WIDTH
    nseg = w_in_bf.shape[1] // tn
    row = lambda n: pl.BlockSpec((tm, n), lambda i: (i, 0))
    cast_in, cast_out, cast_shapes = _mlp_weight_cast_specs(L // tm, mlp_w1, mlp_w2, layer)
    return pl.pallas_call(
        _inproj_even_body,
        grid=(L // tm,),
        in_specs=[
            row(D_MODEL),
            pl.BlockSpec(w_in_bf.shape, lambda i: (0, 0), pipeline_mode=pl.Buffered(1)),
            row(RET_HEAD_DIM // 2), row(RET_HEAD_DIM // 2),
        ] + cast_in,
        out_specs=[pl.BlockSpec((S5_T, tm // S5_T, tn), lambda i: (0, i, 0)),
                   pl.BlockSpec((nseg - 1, tm, tn), lambda i: (0, i, 0))] + cast_out,
        out_shape=[jax.ShapeDtypeStruct((S5_T, L // S5_T, tn), F32),
                   jax.ShapeDtypeStruct((nseg - 1, L, tn), BF16)] + cast_shapes,
        compiler_params=_params(("parallel",)),
        name="inproj_even",
    )(x, w_in_bf, cos, sin, mlp_w1, mlp_w2)


def _cmul(ar, ai, br, bi):
    return ar * br - ai * bi, ar * bi + ai * br


(DISC_A1_RE, DISC_A1_IM, DISC_A2_RE, DISC_A2_IM, DISC_A4_RE, DISC_A4_IM, DISC_A8_RE, DISC_A8_IM,
 DISC_A16_RE, DISC_A16_IM, DISC_AT_RE, DISC_AT_IM, DISC_Z_RE, DISC_Z_IM, DISC_SCAN_A2, DISC_SCAN_A3) = range(16)


def _s5_disc_body(lr_ref, li_ref, ls_ref, sg_ref, o_ref):
    lr = jnp.minimum(lr_ref[...], -1e-4)
    li = li_ref[...]
    step = jnp.exp(ls_ref[...])
    mag = jnp.exp(lr * step)
    ar = mag * jnp.cos(li * step)
    ai = mag * jnp.sin(li * step)
    nr, ni = ar - 1.0, ai
    den = lr * lr + li * li
    o_ref[DISC_Z_RE] = (nr * lr + ni * li) / den
    o_ref[DISC_Z_IM] = (ni * lr - nr * li) / den
    pr, pi = ar, ai
    for k in range(6):
        o_ref[2 * k] = pr
        o_ref[2 * k + 1] = pi
        if k < 5:
            pr, pi = _cmul(pr, pi, pr, pi)
    o_ref[DISC_SCAN_A2] = pi * sg_ref[...]
    o_ref[DISC_SCAN_A3] = -pi * sg_ref[...]


def _s5_disc(lam_re, lam_im, log_step):
    assert S5_T == 32
    rows = 2 * S5_GROUPS
    two = lambda a: jnp.tile(a.astype(F32).reshape(rows, -1), (1, 2))
    lr = two(lam_re)
    li = two(lam_im)
    ls = jnp.broadcast_to(log_step.astype(F32).reshape(rows, 1), (rows, 2 * S5_STATE))
    sg = jnp.broadcast_to(jnp.concatenate([-jnp.ones((S5_STATE,), F32), jnp.ones((S5_STATE,), F32)])[None],
                          (rows, 2 * S5_STATE))
    return pl.pallas_call(
        _s5_disc_body,
        out_shape=jax.ShapeDtypeStruct((16, rows, 2 * S5_STATE), F32),
        name="s5_disc",
    )(lr, li, ls, sg)


def _s5_gen_body(col_ref, zrow_ref, bt_ref, btile_ref, ctile_ref, m_ref, ws_ref, wc_ref):
    P = S5_STATE
    hi = lax.Precision.HIGHEST
    tlo = lax.shift_right_logical(lax.broadcasted_iota(jnp.int32, (P, 128), 1), 4)
    lane = lax.broadcasted_iota(jnp.int32, (S5_GROUP, S5_TL), 1)
    ones = jnp.ones((P, 128), F32)
    zeros = jnp.zeros((P, 128), F32)

    def one_group(gi, carry):
        kt = []
        for d in range(2):
            col = col_ref[gi, d]
            c = lambda k: jnp.broadcast_to(col[:, k:k + 1], (P, 128))
            a1 = (c(DISC_A1_RE), c(DISC_A1_IM))
            a2 = (c(DISC_A2_RE), c(DISC_A2_IM))
            a4 = (c(DISC_A4_RE), c(DISC_A4_IM))
            a8 = (c(DISC_A8_RE), c(DISC_A8_IM))
            a16 = (c(DISC_A16_RE), c(DISC_A16_IM))
            blk = [None, a8, a16, _cmul(*a8, *a16)]

            def low_powers(reverse):
                xr, xi = ones, zeros
                for k, ak in enumerate((a1, a2, a4)):
                    bit = (lax.shift_right_logical(tlo, k) & 1) == (0 if reverse else 1)
                    yr, yi = _cmul(xr, xi, *ak)
                    xr = jnp.where(bit, yr, xr)
                    xi = jnp.where(bit, yi, xi)
                return xr, xi

            def expand(base, reverse):
                out = []
                for j in range(4):
                    f = blk[3 - j] if reverse else blk[j]
                    out.append(base if f is None else _cmul(*base, *f))
                return out

            ct = (ctile_ref[gi, d, 0], ctile_ref[gi, d, 1])
            bbar = _cmul(c(DISC_Z_RE), c(DISC_Z_IM), btile_ref[gi, d, 0], btile_ref[gi, d, 1])
            zrow = zrow_ref[gi, d]
            zr_row, zi_row = zrow[0:1, :], zrow[1:2, :]
            bbt_r, bbt_i = _cmul(zr_row, zi_row, bt_ref[gi, d, 0], bt_ref[gi, d, 1])
            ca = expand(_cmul(*ct, *low_powers(d == 1)), d == 1)
            ba = expand(_cmul(*bbar, *low_powers(d == 0)), d == 0)
            wcj = [_cmul(*x, *a1) for x in ca]
            cat = lambda parts, k: jnp.concatenate([x[k] for x in parts], axis=1)
            kt.append(jnp.dot(bbt_r, cat(ca, 0), precision=hi, preferred_element_type=F32)
                      - jnp.dot(bbt_i, cat(ca, 1), precision=hi, preferred_element_type=F32))
            ba_r, ba_i = cat(ba, 0).astype(BF16), cat(ba, 1).astype(BF16)
            for r, part in enumerate((ba_r, ba_i, ba_i, ba_r)):
                ws_ref[gi, (4 * d + r) * P:(4 * d + r + 1) * P, :] = part
            wc_ref[gi, 2 * d * P:(2 * d + 1) * P, :] = cat(wcj, 0).astype(BF16)
            wc_ref[gi, (2 * d + 1) * P:(2 * d + 2) * P, :] = (-cat(wcj, 1)).astype(BF16)
        ktf, ktb = kt
        for s in range(S5_T):
            lo, hi_lane = S5_GROUP * s, S5_GROUP * (s + 1)
            f = ktf if s == 0 else jnp.where(lane >= lo, pltpu.roll(ktf, lo, axis=1), 0.0)
            b = ktb if s == S5_T - 1 else jnp.where(lane < hi_lane, pltpu.roll(ktb, hi_lane, axis=1), 0.0)
            m_ref[gi, lo:hi_lane, :] = (f + b).astype(BF16)
        return carry

    lax.fori_loop(0, col_ref.shape[0], one_group, 0)


def _s5_gen(disc, b_re, b_im, c_re, c_im):
    G, P, Cg = S5_GROUPS, S5_STATE, S5_GROUP
    gb = S5_GEN_GB
    d4 = disc[:, :, :P].reshape(16, 2, G, P)
    col = d4.transpose(2, 1, 3, 0)
    zrow = d4[DISC_Z_RE:DISC_Z_IM + 1].transpose(2, 1, 0, 3)
    b = jnp.stack([b_re, b_im], axis=1).astype(F32)
    c = jnp.stack([c_re, c_im], axis=1).astype(F32)
    bt = b.transpose(2, 0, 1, 4, 3)
    btile = jnp.tile(b.transpose(2, 0, 1, 3, 4), (1, 1, 1, 1, 128 // Cg))
    ctile = jnp.tile(c.transpose(2, 0, 1, 4, 3), (1, 1, 1, 1, 128 // Cg))
    spec = lambda a: pl.BlockSpec((gb,) + a.shape[1:], lambda i: (i,) + (0,) * (a.ndim - 1))
    out = lambda rows: pl.BlockSpec((gb, rows, S5_TL), lambda i: (i, 0, 0))
    return pl.pallas_call(
        _s5_gen_body,
        grid=(G // gb,),
        in_specs=[spec(col), spec(zrow), spec(bt), spec(btile), spec(ctile)],
        out_specs=[out(S5_TL), out(8 * P), out(4 * P)],
        out_shape=[jax.ShapeDtypeStruct((G, S5_TL, S5_TL), BF16),
                   jax.ShapeDtypeStruct((G, 8 * P, S5_TL), BF16),
                   jax.ShapeDtypeStruct((G, 4 * P, S5_TL), BF16)],
        compiler_params=_params(("parallel",)),
        name="s5_gen",
    )(col, zrow, bt, btile, ctile)


def _s5_body(ut_ref, m_ref, ws_ref, wc_ref, dt_ref, a1_ref, a2_ref, a3_ref, y_ref,
             s_ref, h_ref, yg_ref, u_ref):
    nc = u_ref.shape[1]
    gb = u_ref.shape[0]
    w = gb * 128
    for jb in range(S5_T // 8):
        ys = _block_transpose8([ut_ref[8 * jb + t8] for t8 in range(8)])
        for g8 in range(8):
            u_ref[g8, :, jb * 128:(jb + 1) * 128] = ys[g8].astype(BF16)
    for gi in range(gb):
        s = lax.dot_general(u_ref[gi], ws_ref[gi], (((1,), (1,)), ((), ())),
                            preferred_element_type=F32)
        for r in range(4):
            s_ref[:, r * w + gi * 128:r * w + (gi + 1) * 128] = s[:, r * 128:(r + 1) * 128]

    a1f, a2f, a3f = a1_ref[0], a2_ref[0], a3_ref[0]
    a1b, a2b, a3b = a1_ref[1], a2_ref[1], a3_ref[1]

    def step(n, carry):
        hf, gf, hb, gb_ = carry
        m = nc - 1 - n
        h_ref[pl.ds(n, 1), 0:w] = hf
        h_ref[pl.ds(m, 1), w:2 * w] = hb
        sfh = s_ref[pl.ds(n, 1), 0:w]
        sfg = s_ref[pl.ds(n, 1), w:2 * w]
        sbh = s_ref[pl.ds(m, 1), 2 * w:3 * w]
        sbg = s_ref[pl.ds(m, 1), 3 * w:4 * w]
        hf2 = a1f * hf + a2f * gf + sfh
        gf2 = a1f * gf + a3f * hf + sfg
        hb2 = a1b * hb + a2b * gb_ + sbh
        gb2 = a1b * gb_ + a3b * hb + sbg
        return hf2, gf2, hb2, gb2

    z = jnp.zeros((1, w), F32)
    lax.fori_loop(0, nc, step, (z, z, z, z))

    for gi in range(gb):
        u = u_ref[gi]
        hcat = jnp.concatenate([h_ref[:, gi * 128:(gi + 1) * 128],
                                h_ref[:, w + gi * 128:w + (gi + 1) * 128]], axis=1).astype(BF16)
        y = (jnp.dot(u, m_ref[gi], preferred_element_type=F32)
             + jnp.dot(hcat, wc_ref[gi], preferred_element_type=F32)
             + dt_ref[gi] * u.astype(F32))
        yg_ref[gi] = jax.nn.gelu(y)

    for jb in range(S5_T // 8):
        zs = _block_transpose8([yg_ref[g8, :, jb * 128:(jb + 1) * 128] for g8 in range(8)])
        for t8 in range(8):
            y_ref[8 * jb + t8] = zs[t8].astype(BF16)


def _s5_mix(u_steps, m, ws_t, wc, d_tile, a1, a2, a3):
    _, nc, width = u_steps.shape
    G, TL = S5_GROUPS, S5_TL
    gb = S5_GB
    assert gb * S5_GROUP == 128
    w = gb * 128
    steps_spec = pl.BlockSpec((S5_T, nc, 128), lambda i: (0, 0, i))
    a_spec = pl.BlockSpec((2, 1, w), lambda i: (0, 0, i))
    return pl.pallas_call(
        _s5_body,
        grid=(G // gb,),
        in_specs=[
            steps_spec,
            pl.BlockSpec((gb, TL, TL), lambda i: (i, 0, 0)),
            pl.BlockSpec((gb, 512, TL), lambda i: (i, 0, 0)),
            pl.BlockSpec((gb, 256, TL), lambda i: (i, 0, 0)),
            pl.BlockSpec((gb, 1, TL), lambda i: (i, 0, 0)),
            a_spec, a_spec, a_spec,
        ],
        out_specs=steps_spec,
        out_shape=jax.ShapeDtypeStruct((S5_T, nc, width), BF16),
        scratch_shapes=[pltpu.VMEM((nc, 4 * w), F32), pltpu.VMEM((nc, 2 * w), F32),
                        pltpu.VMEM((gb, nc, TL), F32), pltpu.VMEM((gb, nc, TL), BF16)],
        compiler_params=_params(("parallel",)),
        name="s5_mix",
    )(u_steps, m, ws_t, wc, d_tile, a1, a2, a3)


def _row_index(n):
    return lax.broadcasted_iota(jnp.int32, (n, 1), 0).astype(F32)


def _log_decay(lg_ref, d, h):
    return -jnp.abs(jnp.full((1, 1), lg_ref[d, h], F32))


def _ret_bstate_body(lg_ref, k_ref, v_ref, sb_ref, st_ref):
    C = k_ref.shape[0]

    @pl.when(pl.program_id(0) == 0)
    def _():
        st_ref[...] = jnp.zeros_like(st_ref)

    jj = _row_index(C)
    for h in range(RET_HEADS):
        lo = h * RET_HEAD_DIM
        lgb = _log_decay(lg_ref, 1, h)
        sb_ref[h] = st_ref[h].astype(BF16)
        kd = (k_ref[:, lo:lo + RET_HEAD_DIM].astype(F32) * jnp.exp(jj * lgb)).astype(BF16)
        kv = lax.dot_general(kd, v_ref[:, lo:lo + RET_HEAD_DIM], (((0,), (0,)), ((), ())),
                             preferred_element_type=F32)
        st_ref[h] = st_ref[h] * jnp.exp(C * lgb) + kv


def _ret_bstate(lg, h5):
    L = h5.shape[1]
    C = RET_C
    nc = L // C
    return pl.pallas_call(
        _ret_bstate_body,
        grid=(nc,),
        in_specs=[
            pl.BlockSpec(memory_space=pltpu.SMEM),
            pl.BlockSpec((None, C, RET_WIDTH), lambda i: (SEG_K - 1, nc - 1 - i, 0)),
            pl.BlockSpec((None, C, RET_WIDTH), lambda i: (SEG_V - 1, nc - 1 - i, 0)),
        ],
        out_specs=pl.BlockSpec((None, RET_HEADS, RET_HEAD_DIM, RET_HEAD_DIM),
                               lambda i: (nc - 1 - i, 0, 0, 0)),
        out_shape=jax.ShapeDtypeStruct((nc, RET_HEADS, RET_HEAD_DIM, RET_HEAD_DIM), BF16),
        scratch_shapes=[pltpu.VMEM((RET_HEADS, RET_HEAD_DIM, RET_HEAD_DIM), F32)],
        compiler_params=_params(("arbitrary",)),
        name="ret_bstate",
    )(lg, h5, h5)


def _ret_main_body(lg_ref, q_ref, k_ref, v_ref, g_ref, sb_ref, o_ref, st_ref, dec_ref):
    C = q_ref.shape[0]

    @pl.when(pl.program_id(0) == 0)
    def _():
        st_ref[...] = jnp.zeros_like(st_ref)
        diff = (lax.broadcasted_iota(jnp.int32, (C, C), 0)
                - lax.broadcasted_iota(jnp.int32, (C, C), 1)).astype(F32)
        for h in range(RET_HEADS):
            dec_ref[h] = jnp.where(diff >= 0, jnp.exp(_log_decay(lg_ref, 0, h) * jnp.maximum(diff, 0.0)),
                                   jnp.exp(_log_decay(lg_ref, 1, h) * jnp.maximum(-diff, 0.0)))

    ii = _row_index(C)
    for h in range(RET_HEADS):
        lo = h * RET_HEAD_DIM
        lgf = _log_decay(lg_ref, 0, h)
        lgb = _log_decay(lg_ref, 1, h)
        q = q_ref[:, lo:lo + RET_HEAD_DIM]
        k = k_ref[:, lo:lo + RET_HEAD_DIM]
        v = v_ref[:, lo:lo + RET_HEAD_DIM]
        qf = q.astype(F32)
        s = lax.dot_general(q, k, (((1,), (1,)), ((), ())), preferred_element_type=F32)
        o = jnp.dot((s * dec_ref[h]).astype(BF16), v, preferred_element_type=F32)
        qdf = (qf * jnp.exp((ii + 1.0) * lgf)).astype(BF16)
        o = o + jnp.dot(qdf, st_ref[h].astype(BF16), preferred_element_type=F32)
        qdb = (qf * jnp.exp((C - ii) * lgb)).astype(BF16)
        o = o + jnp.dot(qdb, sb_ref[h], preferred_element_type=F32)
        mu = jnp.mean(o, axis=-1, keepdims=True)
        oc = o - mu
        var = jnp.mean(oc * oc, axis=-1, keepdims=True)
        on = oc * lax.rsqrt(var + HEAD_NORM_EPS)
        o_ref[:, lo:lo + RET_HEAD_DIM] = (on * g_ref[:, lo:lo + RET_HEAD_DIM].astype(F32)).astype(BF16)
        kd = (k.astype(F32) * jnp.exp((C - 1.0 - ii) * lgf)).astype(BF16)
        kv = lax.dot_general(kd, v, (((0,), (0,)), ((), ())), preferred_element_type=F32)
        st_ref[h] = st_ref[h] * jnp.exp(C * lgf) + kv


def _ret_main(lg, h5, sb):
    L = h5.shape[1]
    C = RET_C
    nc = L // C
    seg = lambda s: pl.BlockSpec((None, C, RET_WIDTH), lambda i: (s - 1, i, 0))
    return pl.pallas_call(
        _ret_main_body,
        grid=(nc,),
        in_specs=[
            pl.BlockSpec(memory_space=pltpu.SMEM),
            seg(SEG_Q), seg(SEG_K), seg(SEG_V), seg(SEG_GATE),
            pl.BlockSpec((None, RET_HEADS, RET_HEAD_DIM, RET_HEAD_DIM), lambda i: (i, 0, 0, 0)),
        ],
        out_specs=pl.BlockSpec((C, RET_WIDTH), lambda i: (i, 0)),
        out_shape=jax.ShapeDtypeStruct((L, RET_WIDTH), BF16),
        scratch_shapes=[pltpu.VMEM((RET_HEADS, RET_HEAD_DIM, RET_HEAD_DIM), F32),
                        pltpu.VMEM((RET_HEADS, C, C), F32)],
        compiler_params=_params(("arbitrary",)),
        name="ret_main",
    )(lg, h5, h5, h5, h5, sb)


def _outproj_even_body(y_ref, r_ref, x_ref, wg_ref, bg_ref, wo_ref, g_ref, b_ref, o_ref):
    nl = y_ref.shape[1]
    tm = S5_T * nl
    shift = S5_T.bit_length() - 1
    y_steps = y_ref[...].reshape(tm, y_ref.shape[2])
    perm = _row_permutation(tm, lambda r: (r & (S5_T - 1)) * nl + lax.shift_right_logical(r, shift))
    for r in range(0, tm, LN_ROWS):
        rows = slice(r, r + LN_ROWS)
        y = jnp.dot(perm[rows, :], y_steps, preferred_element_type=F32).astype(BF16)
        z = jnp.dot(y, wg_ref[...], preferred_element_type=F32) + bg_ref[...]
        s5 = (y.astype(F32) * jax.nn.sigmoid(z)).astype(BF16)
        mix = (jnp.dot(s5, wo_ref[0:S5_WIDTH, :], preferred_element_type=F32)
               + jnp.dot(r_ref[rows, :], wo_ref[S5_WIDTH:S5_WIDTH + RET_WIDTH, :],
                         preferred_element_type=F32))
        o_ref[rows, :] = _layer_norm_rows(DEEPNORM_ALPHA * x_ref[rows, :] + mix, g_ref[...], b_ref[...])


def _outproj_even(y_steps, ret, x, w_glu_bf, b_glu, w_out_bf, ln_g, ln_b, tm=512):
    L = x.shape[0]
    row = lambda n: pl.BlockSpec((tm, n), lambda i: (i, 0))
    full = lambda a: pl.BlockSpec(a.shape, lambda i: (0,) * a.ndim)
    return pl.pallas_call(
        _outproj_even_body,
        grid=(L // tm,),
        in_specs=[pl.BlockSpec((S5_T, tm // S5_T, S5_WIDTH), lambda i: (0, i, 0)),
                  row(RET_WIDTH), row(D_MODEL), full(w_glu_bf), full(b_glu),
                  full(w_out_bf), full(ln_g), full(ln_b)],
        out_specs=row(D_MODEL),
        out_shape=jax.ShapeDtypeStruct((L, D_MODEL), F32),
        compiler_params=_params(("parallel",)),
        name="outproj_even",
    )(y_steps, ret, x, w_glu_bf, b_glu, w_out_bf, ln_g, ln_b)


def _inproj_odd_body(x_ref, w_ref, c_ref, s1_ref, s2_ref, w1f_ref, w2f_ref, o_ref, w1b_ref, w2b_ref):
    w1b_ref[...] = w1f_ref[...].astype(BF16)
    w2b_ref[...] = w2f_ref[...].astype(BF16)
    xb = x_ref[...].astype(BF16)
    tn = ATT_KV_HEADS * ATT_HEAD_DIM
    nq = (ATT_HEADS * ATT_HEAD_DIM) // tn
    for cb in range(w_ref.shape[1] // tn):
        acc = jnp.dot(xb, w_ref[:, cb * tn:(cb + 1) * tn], preferred_element_type=F32)
        if cb > nq:
            o_ref[:, cb * tn:(cb + 1) * tn] = acc.astype(BF16)
            continue
        scale = ATT_HEAD_DIM ** -0.5 if cb < nq else 1.0
        cc = c_ref[...] * scale
        s1 = s1_ref[...] * scale
        s2 = s2_ref[...] * scale
        for hh in range(tn // ATT_HEAD_DIM):
            lo = hh * ATT_HEAD_DIM
            a = acc[:, lo:lo + ATT_HEAD_DIM]
            up = pltpu.roll(a, ATT_HEAD_DIM - ROPE_DIM // 2, axis=1)
            dn = pltpu.roll(a, ROPE_DIM // 2, axis=1)
            o_ref[:, cb * tn + lo:cb * tn + lo + ATT_HEAD_DIM] = (a * cc + up * s1 + dn * s2).astype(BF16)


def _inproj_odd(x, w_in_bf, cc, s1, s2, mlp_w1, mlp_w2, layer, tm=256):
    L = x.shape[0]
    n_out = w_in_bf.shape[1]
    row = lambda n: pl.BlockSpec((tm, n), lambda i: (i, 0))
    cast_in, cast_out, cast_shapes = _mlp_weight_cast_specs(L // tm, mlp_w1, mlp_w2, layer)
    return pl.pallas_call(
        _inproj_odd_body,
        grid=(L // tm,),
        in_specs=[
            row(D_MODEL),
            pl.BlockSpec(w_in_bf.shape, lambda i: (0, 0), pipeline_mode=pl.Buffered(1)),
            row(ATT_HEAD_DIM), row(ATT_HEAD_DIM), row(ATT_HEAD_DIM),
        ] + cast_in,
        out_specs=[row(n_out)] + cast_out,
        out_shape=[jax.ShapeDtypeStruct((L, n_out), BF16)] + cast_shapes,
        compiler_params=_params(("parallel",)),
        name="inproj_odd",
    )(x, w_in_bf, cc, s1, s2, mlp_w1, mlp_w2)


ATT_QB = 2


def _attn_body(sink_ref, q_ref, *refs):
    k_refs = refs[:ATT_QB + 2]
    v_refs = refs[ATT_QB + 2:2 * ATT_QB + 4]
    o_ref, bias_ref = refs[2 * ATT_QB + 4:]
    step = pl.program_id(0)
    nb = pl.num_programs(0) * ATT_QB
    B = ATT_BLOCK
    hd = ATT_HEAD_DIM
    rows = ATT_GROUP * B

    @pl.when(step == 0)
    def _():
        r_i = lax.broadcasted_iota(jnp.int32, (rows, 3 * B), 0)
        s_i = lax.broadcasted_iota(jnp.int32, (rows, 3 * B), 1)
        rel = (r_i & (B - 1)) - s_i + B
        bias_ref[...] = jnp.where(jnp.abs(rel) <= ATT_WINDOW, 0.0, NEG_INF)

    head_of_row = lax.shift_right_logical(lax.broadcasted_iota(jnp.int32, (rows, 1), 0),
                                          int(math.log2(B)))
    for qb in range(ATT_QB):
        c = step * ATT_QB + qb
        lo_bias = jnp.where(c > 0, 0.0, NEG_INF)
        hi_bias = jnp.where(c < nb - 1, 0.0, NEG_INF)
        for g in range(ATT_KV_HEADS):
            q = jnp.concatenate([q_ref[qb * B:(qb + 1) * B, (g * ATT_GROUP + hh) * hd:(g * ATT_GROUP + hh + 1) * hd]
                                 for hh in range(ATT_GROUP)], axis=0)
            ksl = slice(g * hd, (g + 1) * hd)
            k = jnp.concatenate([r[:, ksl] for r in k_refs[qb:qb + 3]], axis=0)
            v = jnp.concatenate([r[:, ksl] for r in v_refs[qb:qb + 3]], axis=0)
            s = lax.dot_general(q, k, (((1,), (1,)), ((), ())), preferred_element_type=F32) + bias_ref[...]
            s = jnp.concatenate([s[:, :B] + lo_bias, s[:, B:2 * B], s[:, 2 * B:] + hi_bias], axis=1)
            sink = jnp.zeros((rows, 1), F32)
            for hh in range(ATT_GROUP):
                sink = jnp.where(head_of_row == hh, sink_ref[g * ATT_GROUP + hh], sink)
            m = jnp.maximum(jnp.max(s, axis=-1, keepdims=True), sink)
            p = jnp.exp(s - m)
            den = jnp.sum(p, axis=-1, keepdims=True) + jnp.exp(sink - m)
            o = jnp.dot(p.astype(BF16), v, preferred_element_type=F32) / den
            for hh in range(ATT_GROUP):
                hcol = (g * ATT_GROUP + hh) * hd
                o_ref[qb * B:(qb + 1) * B, hcol:hcol + hd] = o[hh * B:(hh + 1) * B, :].astype(BF16)


def _attention(sink, qkv):
    L = qkv.shape[0]
    B = ATT_BLOCK
    nb = L // B
    kvw = ATT_KV_HEADS * ATT_HEAD_DIM
    qw = ATT_HEADS * ATT_HEAD_DIM
    kcol = qw // kvw
    vcol = kcol + 1

    def kv_spec(col, off):
        return pl.BlockSpec((B, kvw), lambda i: (jnp.clip(i * ATT_QB + off, 0, nb - 1), col))

    offs = range(-1, ATT_QB + 1)
    return pl.pallas_call(
        _attn_body,
        grid=(nb // ATT_QB,),
        in_specs=[
            pl.BlockSpec(memory_space=pltpu.SMEM),
            pl.BlockSpec((ATT_QB * B, qw), lambda i: (i, 0)),
        ] + [kv_spec(kcol, o) for o in offs] + [kv_spec(vcol, o) for o in offs],
        out_specs=pl.BlockSpec((ATT_QB * B, qw), lambda i: (i, 0)),
        out_shape=jax.ShapeDtypeStruct((L, qw), BF16),
        scratch_shapes=[pltpu.VMEM((ATT_GROUP * B, 3 * B), F32)],
        compiler_params=_params(("arbitrary",)),
        name="attention",
    )(sink, qkv, *([qkv] * (2 * ATT_QB + 4)))


def _outproj_odd_body(a_ref, x_ref, wo_ref, g_ref, b_ref, o_ref):
    for r in range(0, a_ref.shape[0], LN_ROWS):
        rows = slice(r, r + LN_ROWS)
        mix = jnp.dot(a_ref[rows, :], wo_ref[...], preferred_element_type=F32)
        o_ref[rows, :] = _layer_norm_rows(DEEPNORM_ALPHA * x_ref[rows, :] + mix, g_ref[...], b_ref[...])


def _outproj_odd(a, x, w_out_bf, ln_g, ln_b, tm=512):
    L = x.shape[0]
    row = lambda n: pl.BlockSpec((tm, n), lambda i: (i, 0))
    full = lambda t: pl.BlockSpec(t.shape, lambda i: (0,) * t.ndim)
    return pl.pallas_call(
        _outproj_odd_body,
        grid=(L // tm,),
        in_specs=[row(a.shape[1]), row(D_MODEL), full(w_out_bf), full(ln_g), full(ln_b)],
        out_specs=row(D_MODEL),
        out_shape=jax.ShapeDtypeStruct((L, D_MODEL), F32),
        compiler_params=_params(("parallel",)),
        name="outproj_odd",
    )(a, x, w_out_bf, ln_g, ln_b)


def _mlp_body(x_ref, w1_ref, w2_ref, g_ref, b_ref, o_ref, xb_ref, acc_ref):
    f = pl.program_id(1)

    @pl.when(f == 0)
    def _():
        xb_ref[...] = x_ref[...].astype(BF16)
        acc_ref[...] = jnp.zeros_like(acc_ref)

    def partial_sum(rows):
        h = jnp.dot(xb_ref[rows, :], w1_ref[...], preferred_element_type=F32)
        h = jnp.square(jnp.maximum(h, 0.0)).astype(BF16)
        return jnp.dot(h, w2_ref[...], preferred_element_type=F32)

    last = pl.num_programs(1) - 1

    @pl.when(f < last)
    def _():
        acc_ref[...] += partial_sum(slice(None))

    @pl.when(f == last)
    def _():
        for r in range(0, x_ref.shape[0], LN_ROWS):
            rows = slice(r, r + LN_ROWS)
            y = DEEPNORM_ALPHA * x_ref[rows, :] + (acc_ref[rows, :] + partial_sum(rows))
            o_ref[rows, :] = _layer_norm_rows(y, g_ref[...], b_ref[...])


def _mlp(x, w1, w2, ln_g, ln_b, tm=512, tf=1024):
    L = x.shape[0]
    return pl.pallas_call(
        _mlp_body,
        grid=(L // tm, D_FF // tf),
        in_specs=[
            pl.BlockSpec((tm, D_MODEL), lambda i, f: (i, 0)),
            pl.BlockSpec((D_MODEL, tf), lambda i, f: (0, f)),
            pl.BlockSpec((tf, D_MODEL), lambda i, f: (f, 0)),
            pl.BlockSpec((1, D_MODEL), lambda i, f: (0, 0)),
            pl.BlockSpec((1, D_MODEL), lambda i, f: (0, 0)),
        ],
        out_specs=pl.BlockSpec((tm, D_MODEL), lambda i, f: (i, 0)),
        out_shape=jax.ShapeDtypeStruct((L, D_MODEL), F32),
        scratch_shapes=[pltpu.VMEM((tm, D_MODEL), BF16), pltpu.VMEM((tm, D_MODEL), F32)],
        compiler_params=_params(("parallel", "arbitrary")),
        name="mlp",
    )(x, w1, w2, ln_g, ln_b)


def _rotary_tables(L, rot_dim, theta):
    half = rot_dim // 2
    inv_freq = 1.0 / (theta ** (jnp.arange(half, dtype=F32) / half))
    ang = jnp.arange(L).astype(F32)[:, None] * inv_freq[None, :]
    return jnp.cos(ang), jnp.sin(ang)


def _even_layer(x, w_in, w_out, lam_re, lam_im, log_step, b_re, b_im, c_re, c_im,
                d_skip, w_glu, b_glu, ret_log_decay, ln_g, ln_b, mlp_w1, mlp_w2, layer):
    L = x.shape[0]
    cos, sin = _rotary_tables(L, RET_HEAD_DIM, RET_ROPE_THETA)
    u_steps, h5, w1_bf, w2_bf = _inproj_even(x, w_in.astype(BF16), cos, sin, mlp_w1, mlp_w2, layer)
    disc = _s5_disc(lam_re, lam_im, log_step)
    m, ws_t, wc = _s5_gen(disc, b_re, b_im, c_re, c_im)
    d_tile = jnp.tile(d_skip.astype(F32), (1, S5_T)).reshape(S5_GROUPS, 1, S5_TL)
    a1, a2, a3 = (disc[k].reshape(2, 1, S5_GROUPS * 128) for k in (DISC_AT_RE, DISC_SCAN_A2, DISC_SCAN_A3))
    y = _s5_mix(u_steps, m, ws_t, wc, d_tile, a1, a2, a3)
    lg = ret_log_decay.astype(F32)
    sb = _ret_bstate(lg, h5)
    ret = _ret_main(lg, h5, sb)
    x1 = _outproj_even(y, ret, x, w_glu.astype(BF16), b_glu.astype(F32).reshape(1, -1),
                       w_out.astype(BF16), ln_g.reshape(1, -1), ln_b.reshape(1, -1))
    return x1, w1_bf, w2_bf


def _odd_layer(x, w_in, w_out, sink, ln_g, ln_b, mlp_w1, mlp_w2, layer):
    L = x.shape[0]
    cos, sin = _rotary_tables(L, ROPE_DIM, ROPE_THETA)
    half = ROPE_DIM // 2
    pad = ATT_HEAD_DIM - ROPE_DIM
    cc = jnp.concatenate([cos, cos, jnp.ones((L, pad), F32)], axis=1)
    s1 = jnp.concatenate([-sin, jnp.zeros((L, ATT_HEAD_DIM - half), F32)], axis=1)
    s2 = jnp.concatenate([jnp.zeros((L, half), F32), sin, jnp.zeros((L, pad), F32)], axis=1)
    qkv, w1_bf, w2_bf = _inproj_odd(x, w_in.astype(BF16), cc, s1, s2, mlp_w1, mlp_w2, layer)
    att = _attention(sink.astype(F32), qkv)
    x1 = _outproj_odd(att, x, w_out.astype(BF16), ln_g.reshape(1, -1), ln_b.reshape(1, -1))
    return x1, w1_bf, w2_bf


def kernel(x, ln_g, ln_b, mlp_w1, mlp_w2, even_w_in, even_w_out, s5_lambda_re, s5_lambda_im, s5_log_step, s5_b_re, s5_b_im, s5_c_re, s5_c_im, s5_d, s5_w_glu, s5_b_glu, ret_log_decay, odd_w_in, odd_w_out, attn_sink):
    bsz = x.shape[0]
    outs = []
    for b in range(bsz):
        xb = x[b]
        for layer in range(DEPTH):
            if layer % 2 == 0:
                e = layer // 2
                xb, w1_bf, w2_bf = _even_layer(
                    xb, even_w_in[e], even_w_out[e], s5_lambda_re[e], s5_lambda_im[e],
                    s5_log_step[e], s5_b_re[e], s5_b_im[e], s5_c_re[e], s5_c_im[e],
                    s5_d[e], s5_w_glu[e], s5_b_glu[e], ret_log_decay[e],
                    ln_g[layer, 0], ln_b[layer, 0], mlp_w1, mlp_w2, layer)
            else:
                o = layer // 2
                xb, w1_bf, w2_bf = _odd_layer(xb, odd_w_in[o], odd_w_out[o], attn_sink[o],
                                              ln_g[layer, 0], ln_b[layer, 0], mlp_w1, mlp_w2, layer)
            xb = _mlp(xb, w1_bf, w2_bf,
                      ln_g[layer, 1].reshape(1, -1), ln_b[layer, 1].reshape(1, -1))
        outs.append(xb)
    return jnp.stack(outs, axis=0)
```

```python
import functools
import math

import jax
import jax.numpy as jnp
from jax import lax
from jax.experimental import pallas as pl
from jax.experimental.pallas import tpu as pltpu

F32 = jnp.float32
BF16 = jnp.bfloat16

D_MODEL = 2048
DEPTH = 2
S5_WIDTH = 1024
S5_GROUP = 16
S5_GROUPS = 64
S5_STATE = 64
RET_WIDTH = 1024
RET_HEADS = 4
RET_HEAD_DIM = 256
RET_ROPE_THETA = 10000.0
ATT_HEADS = 16
ATT_KV_HEADS = 4
ATT_HEAD_DIM = 128
ATT_GROUP = 4
ATT_WINDOW = 128
ATT_BLOCK = 128
ROPE_THETA = 500000.0
ROPE_DIM = 32
D_FF = 4 * D_MODEL
DEEPNORM_ALPHA = (2 * DEPTH) ** 0.25
LN_EPS = 1e-5
HEAD_NORM_EPS = 1e-6
NEG_INF = -1e30

V7X_VMEM_BYTES = 64 * 1024 * 1024
VMEM_LIMIT = V7X_VMEM_BYTES - 8 * 1024 * 1024

S5_T = 32
S5_TL = S5_T * S5_GROUP
S5_GB = 8
S5_GEN_GB = 8
RET_C = 256
LN_ROWS = 256


def _params(sem):
    return pltpu.CompilerParams(dimension_semantics=sem, vmem_limit_bytes=VMEM_LIMIT)


def _layer_norm_rows(y, g, b):
    mu = jnp.mean(y, axis=-1, keepdims=True)
    yc = y - mu
    var = jnp.mean(yc * yc, axis=-1, keepdims=True)
    return yc * lax.rsqrt(var + LN_EPS) * g + b


def _block_transpose8(xs):
    blk = lax.shift_right_logical(lax.broadcasted_iota(jnp.int32, xs[0].shape, 1), 4)
    xs = list(xs)
    for k in range(3):
        d = 1 << k
        upper = (blk & d) != 0
        for i in range(8):
            if i & d:
                continue
            a, b = xs[i], xs[i + d]
            xs[i] = jnp.where(upper, pltpu.roll(b, S5_GROUP * d, axis=1), a)
            xs[i + d] = jnp.where(upper, b, pltpu.roll(a, 128 - S5_GROUP * d, axis=1))
    return xs


ROT_LO = 64


def _rotary_block(ca_ref, sa_ref, cb_ref, sb_ref, tm):
    cb = cb_ref[...]
    sb = sb_ref[...]
    n = tm // ROT_LO
    cos, sin = [], []
    for al in range(n):
        row = pl.ds(pl.program_id(0) * n + al, 1)
        ca = ca_ref[row, :]
        sa = sa_ref[row, :]
        cos.append(ca * cb - sa * sb)
        sin.append(sa * cb + ca * sb)
    return jnp.concatenate(cos, axis=0), jnp.concatenate(sin, axis=0)


SEG_U, SEG_Q, SEG_K, SEG_V, SEG_GATE = range(5)


def _row_permutation(n_rows, src_of_row):
    r = lax.broadcasted_iota(jnp.int32, (n_rows, n_rows), 0)
    c = lax.broadcasted_iota(jnp.int32, (n_rows, n_rows), 1)
    return jnp.where(c == src_of_row(r), 1.0, 0.0).astype(BF16)


def _mlp_weight_cast_specs(n_steps, mlp_w1, mlp_w2, layer):
    ins, outs, shapes = [], [], []
    for w in (mlp_w1, mlp_w2):
        rows, cols = w.shape[1] // n_steps, w.shape[2]
        ins.append(pl.BlockSpec((None, rows, cols), lambda i: (layer, i, 0)))
        outs.append(pl.BlockSpec((rows, cols), lambda i: (i, 0)))
        shapes.append(jax.ShapeDtypeStruct(w.shape[1:], BF16))
    return ins, outs, shapes


def _inproj_even_body(x_ref, w_ref, ca_ref, sa_ref, cb_ref, sb_ref, w1f_ref, w2f_ref,
                      u_ref, o_ref, w1b_ref, w2b_ref):
    w1b_ref[...] = w1f_ref[...].astype(BF16)
    w2b_ref[...] = w2f_ref[...].astype(BF16)
    xb = x_ref[...].astype(BF16)
    tm = xb.shape[0]
    tn = S5_WIDTH
    half = RET_HEAD_DIM // 2
    cos_blk, sin_blk = _rotary_block(ca_ref, sa_ref, cb_ref, sb_ref, tm)
    for seg in range(w_ref.shape[1] // tn):
        acc = jnp.dot(xb, w_ref[:, seg * tn:(seg + 1) * tn], preferred_element_type=F32)
        if seg == SEG_U:
            nl = tm // S5_T
            shift = nl.bit_length() - 1
            perm = _row_permutation(tm, lambda r: (r & (nl - 1)) * S5_T + lax.shift_right_logical(r, shift))
            up = jnp.dot(perm, acc.astype(BF16), preferred_element_type=F32)
            for t in range(S5_T):
                u_ref[t] = up[t * nl:(t + 1) * nl, :]
        elif seg in (SEG_Q, SEG_K):
            scale = RET_HEAD_DIM ** -0.5 if seg == SEG_K else 1.0
            cos = cos_blk * scale
            sin = sin_blk * scale
            for hh in range(RET_HEADS):
                lo = hh * RET_HEAD_DIM
                a = acc[:, lo:lo + half]
                b = acc[:, lo + half:lo + RET_HEAD_DIM]
                o_ref[seg - 1, :, lo:lo + half] = (a * cos - b * sin).astype(BF16)
                o_ref[seg - 1, :, lo + half:lo + RET_HEAD_DIM] = (b * cos + a * sin).astype(BF16)
        elif seg == SEG_GATE:
            o_ref[seg - 1] = (acc * jax.nn.sigmoid(acc)).astype(BF16)
        else:
            o_ref[seg - 1] = acc.astype(BF16)


def _inproj_even(x, w_in_bf, rot, mlp_w1, mlp_w2, layer, tm=256):
    L = x.shape[0]
    tn = S5_WIDTH
    nseg = w_in_bf.shape[1] // tn
    row = lambda n: pl.BlockSpec((tm, n), lambda i: (i, 0))
    cast_in, cast_out, cast_shapes = _mlp_weight_cast_specs(L // tm, mlp_w1, mlp_w2, layer)
    return pl.pallas_call(
        _inproj_even_body,
        grid=(L // tm,),
        in_specs=[
            row(D_MODEL),
            pl.BlockSpec(w_in_bf.shape, lambda i: (0, 0), pipeline_mode=pl.Buffered(1)),
        ] + [pl.BlockSpec(t.shape, lambda i: (0, 0)) for t in rot] + cast_in,
        out_specs=[pl.BlockSpec((S5_T, tm // S5_T, tn), lambda i: (0, i, 0)),
                   pl.BlockSpec((nseg - 1, tm, tn), lambda i: (0, i, 0))] + cast_out,
        out_shape=[jax.ShapeDtypeStruct((S5_T, L // S5_T, tn), F32),
                   jax.ShapeDtypeStruct((nseg - 1, L, tn), BF16)] + cast_shapes,
        compiler_params=_params(("parallel",)),
        name="inproj_even",
    )(x, w_in_bf, *rot, mlp_w1, mlp_w2)


def _cmul(ar, ai, br, bi):
    return ar * br - ai * bi, ar * bi + ai * br


(DISC_A1_RE, DISC_A1_IM, DISC_A2_RE, DISC_A2_IM, DISC_A4_RE, DISC_A4_IM, DISC_A8_RE, DISC_A8_IM,
 DISC_A16_RE, DISC_A16_IM, DISC_AT_RE, DISC_AT_IM, DISC_Z_RE, DISC_Z_IM, DISC_SCAN_A2, DISC_SCAN_A3) = range(16)


def _s5_disc_body(lr_ref, li_ref, ls_ref, sg_ref, o_ref):
    lr = jnp.minimum(lr_ref[...], -1e-4)
    li = li_ref[...]
    step = jnp.exp(ls_ref[...])
    mag = jnp.exp(lr * step)
    ar = mag * jnp.cos(li * step)
    ai = mag * jnp.sin(li * step)
    nr, ni = ar - 1.0, ai
    den = lr * lr + li * li
    o_ref[DISC_Z_RE] = (nr * lr + ni * li) / den
    o_ref[DISC_Z_IM] = (ni * lr - nr * li) / den
    pr, pi = ar, ai
    for k in range(6):
        o_ref[2 * k] = pr
        o_ref[2 * k + 1] = pi
        if k < 5:
            pr, pi = _cmul(pr, pi, pr, pi)
    o_ref[DISC_SCAN_A2] = pi * sg_ref[...]
    o_ref[DISC_SCAN_A3] = -pi * sg_ref[...]


def _s5_disc(lam_re, lam_im, log_step):
    assert S5_T == 32
    rows = 2 * S5_GROUPS
    two = lambda a: jnp.tile(a.astype(F32).reshape(rows, -1), (1, 2))
    lr = two(lam_re)
    li = two(lam_im)
    ls = jnp.broadcast_to(log_step.astype(F32).reshape(rows, 1), (rows, 2 * S5_STATE))
    sg = jnp.broadcast_to(jnp.concatenate([-jnp.ones((S5_STATE,), F32), jnp.ones((S5_STATE,), F32)])[None],
                          (rows, 2 * S5_STATE))
    return pl.pallas_call(
        _s5_disc_body,
        out_shape=jax.ShapeDtypeStruct((16, rows, 2 * S5_STATE), F32),
        name="s5_disc",
    )(lr, li, ls, sg)


def _s5_gen_body(col_ref, zrow_ref, bt_ref, btile_ref, ctile_ref, m_ref, ws_ref, wc_ref):
    P = S5_STATE
    hi = lax.Precision.HIGHEST
    tlo = lax.shift_right_logical(lax.broadcasted_iota(jnp.int32, (P, 128), 1), 4)
    lane = lax.broadcasted_iota(jnp.int32, (S5_GROUP, S5_TL), 1)
    ones = jnp.ones((P, 128), F32)
    zeros = jnp.zeros((P, 128), F32)

    def one_group(gi, carry):
        kt = []
        for d in range(2):
            col = col_ref[gi, d]
            c = lambda k: jnp.broadcast_to(col[:, k:k + 1], (P, 128))
            a1 = (c(DISC_A1_RE), c(DISC_A1_IM))
            a2 = (c(DISC_A2_RE), c(DISC_A2_IM))
            a4 = (c(DISC_A4_RE), c(DISC_A4_IM))
            a8 = (c(DISC_A8_RE), c(DISC_A8_IM))
            a16 = (c(DISC_A16_RE), c(DISC_A16_IM))
            blk = [None, a8, a16, _cmul(*a8, *a16)]

            def low_powers(reverse):
                xr, xi = ones, zeros
                for k, ak in enumerate((a1, a2, a4)):
                    bit = (lax.shift_right_logical(tlo, k) & 1) == (0 if reverse else 1)
                    yr, yi = _cmul(xr, xi, *ak)
                    xr = jnp.where(bit, yr, xr)
                    xi = jnp.where(bit, yi, xi)
                return xr, xi

            def expand(base, reverse):
                out = []
                for j in range(4):
                    f = blk[3 - j] if reverse else blk[j]
                    out.append(base if f is None else _cmul(*base, *f))
                return out

            ct = (ctile_ref[gi, d, 0], ctile_ref[gi, d, 1])
            bbar = _cmul(c(DISC_Z_RE), c(DISC_Z_IM), btile_ref[gi, d, 0], btile_ref[gi, d, 1])
            zrow = zrow_ref[gi, d]
            zr_row, zi_row = zrow[0:1, :], zrow[1:2, :]
            bbt_r, bbt_i = _cmul(zr_row, zi_row, bt_ref[gi, d, 0], bt_ref[gi, d, 1])
            ca = expand(_cmul(*ct, *low_powers(d == 1)), d == 1)
            ba = expand(_cmul(*bbar, *low_powers(d == 0)), d == 0)
            wcj = [_cmul(*x, *a1) for x in ca]
            cat = lambda parts, k: jnp.concatenate([x[k] for x in parts], axis=1)
            kt.append(jnp.dot(bbt_r, cat(ca, 0), precision=hi, preferred_element_type=F32)
                      - jnp.dot(bbt_i, cat(ca, 1), precision=hi, preferred_element_type=F32))
            ba_r, ba_i = cat(ba, 0).astype(BF16), cat(ba, 1).astype(BF16)
            for r, part in enumerate((ba_r, ba_i, ba_i, ba_r)):
                ws_ref[gi, (4 * d + r) * P:(4 * d + r + 1) * P, :] = part
            wc_ref[gi, 2 * d * P:(2 * d + 1) * P, :] = cat(wcj, 0).astype(BF16)
            wc_ref[gi, (2 * d + 1) * P:(2 * d + 2) * P, :] = (-cat(wcj, 1)).astype(BF16)
        ktf, ktb = kt
        for s in range(S5_T):
            lo, hi_lane = S5_GROUP * s, S5_GROUP * (s + 1)
            f = ktf if s == 0 else jnp.where(lane >= lo, pltpu.roll(ktf, lo, axis=1), 0.0)
            b = ktb if s == S5_T - 1 else jnp.where(lane < hi_lane, pltpu.roll(ktb, hi_lane, axis=1), 0.0)
            m_ref[gi, lo:hi_lane, :] = (f + b).astype(BF16)
        return carry

    lax.fori_loop(0, col_ref.shape[0], one_group, 0)


def _s5_gen(disc, b_re, b_im, c_re, c_im):
    G, P, Cg = S5_GROUPS, S5_STATE, S5_GROUP
    gb = S5_GEN_GB
    d4 = disc[:, :, :P].reshape(16, 2, G, P)
    col = d4.transpose(2, 1, 3, 0)
    zrow = d4[DISC_Z_RE:DISC_Z_IM + 1].transpose(2, 1, 0, 3)
    b = jnp.stack([b_re, b_im], axis=1).astype(F32)
    c = jnp.stack([c_re, c_im], axis=1).astype(F32)
    bt = b.transpose(2, 0, 1, 4, 3)
    btile = jnp.tile(b.transpose(2, 0, 1, 3, 4), (1, 1, 1, 1, 128 // Cg))
    ctile = jnp.tile(c.transpose(2, 0, 1, 4, 3), (1, 1, 1, 1, 128 // Cg))
    spec = lambda a: pl.BlockSpec((gb,) + a.shape[1:], lambda i: (i,) + (0,) * (a.ndim - 1))
    out = lambda rows: pl.BlockSpec((gb, rows, S5_TL), lambda i: (i, 0, 0))
    return pl.pallas_call(
        _s5_gen_body,
        grid=(G // gb,),
        in_specs=[spec(col), spec(zrow), spec(bt), spec(btile), spec(ctile)],
        out_specs=[out(S5_TL), out(8 * P), out(4 * P)],
        out_shape=[jax.ShapeDtypeStruct((G, S5_TL, S5_TL), BF16),
                   jax.ShapeDtypeStruct((G, 8 * P, S5_TL), BF16),
                   jax.ShapeDtypeStruct((G, 4 * P, S5_TL), BF16)],
        compiler_params=_params(("parallel",)),
        name="s5_gen",
    )(col, zrow, bt, btile, ctile)


def _s5_body(ut_ref, m_ref, ws_ref, wc_ref, dt_ref, a1_ref, a2_ref, a3_ref, y_ref,
             s_ref, h_ref, yg_ref, u_ref):
    nc = u_ref.shape[1]
    gb = u_ref.shape[0]
    w = gb * 128
    for jb in range(S5_T // 8):
        ys = _block_transpose8([ut_ref[8 * jb + t8] for t8 in range(8)])
        for g8 in range(8):
            u_ref[g8, :, jb * 128:(jb + 1) * 128] = ys[g8].astype(BF16)
    for gi in range(gb):
        s = lax.dot_general(u_ref[gi], ws_ref[gi], (((1,), (1,)), ((), ())),
                            preferred_element_type=F32)
        for r in range(4):
            s_ref[:, r * w + gi * 128:r * w + (gi + 1) * 128] = s[:, r * 128:(r + 1) * 128]

    a1f, a2f, a3f = a1_ref[0], a2_ref[0], a3_ref[0]
    a1b, a2b, a3b = a1_ref[1], a2_ref[1], a3_ref[1]

    def step(n, carry):
        hf, gf, hb, gb_ = carry
        m = nc - 1 - n
        h_ref[pl.ds(n, 1), 0:w] = hf
        h_ref[pl.ds(m, 1), w:2 * w] = hb
        sfh = s_ref[pl.ds(n, 1), 0:w]
        sfg = s_ref[pl.ds(n, 1), w:2 * w]
        sbh = s_ref[pl.ds(m, 1), 2 * w:3 * w]
        sbg = s_ref[pl.ds(m, 1), 3 * w:4 * w]
        hf2 = a1f * hf + a2f * gf + sfh
        gf2 = a1f * gf + a3f * hf + sfg
        hb2 = a1b * hb + a2b * gb_ + sbh
        gb2 = a1b * gb_ + a3b * hb + sbg
        return hf2, gf2, hb2, gb2

    z = jnp.zeros((1, w), F32)
    lax.fori_loop(0, nc, step, (z, z, z, z))

    for gi in range(gb):
        u = u_ref[gi]
        hcat = jnp.concatenate([h_ref[:, gi * 128:(gi + 1) * 128],
                                h_ref[:, w + gi * 128:w + (gi + 1) * 128]], axis=1).astype(BF16)
        y = (jnp.dot(u, m_ref[gi], preferred_element_type=F32)
             + jnp.dot(hcat, wc_ref[gi], preferred_element_type=F32)
             + dt_ref[gi] * u.astype(F32))
        yg_ref[gi] = jax.nn.gelu(y)

    for jb in range(S5_T // 8):
        zs = _block_transpose8([yg_ref[g8, :, jb * 128:(jb + 1) * 128] for g8 in range(8)])
        for t8 in range(8):
            y_ref[8 * jb + t8] = zs[t8].astype(BF16)


def _s5_mix(u_steps, m, ws_t, wc, d_tile, a1, a2, a3):
    _, nc, width = u_steps.shape
    G, TL = S5_GROUPS, S5_TL
    gb = S5_GB
    assert gb * S5_GROUP == 128
    w = gb * 128
    steps_spec = pl.BlockSpec((S5_T, nc, 128), lambda i: (0, 0, i))
    a_spec = pl.BlockSpec((2, 1, w), lambda i: (0, 0, i))
    return pl.pallas_call(
        _s5_body,
        grid=(G // gb,),
        in_specs=[
            steps_spec,
            pl.BlockSpec((gb, TL, TL), lambda i: (i, 0, 0)),
            pl.BlockSpec((gb, 512, TL), lambda i: (i, 0, 0)),
            pl.BlockSpec((gb, 256, TL), lambda i: (i, 0, 0)),
            pl.BlockSpec((gb, 1, TL), lambda i: (i, 0, 0)),
            a_spec, a_spec, a_spec,
        ],
        out_specs=steps_spec,
        out_shape=jax.ShapeDtypeStruct((S5_T, nc, width), BF16),
        scratch_shapes=[pltpu.VMEM((nc, 4 * w), F32), pltpu.VMEM((nc, 2 * w), F32),
                        pltpu.VMEM((gb, nc, TL), F32), pltpu.VMEM((gb, nc, TL), BF16)],
        compiler_params=_params(("parallel",)),
        name="s5_mix",
    )(u_steps, m, ws_t, wc, d_tile, a1, a2, a3)


def _row_index(n):
    return lax.broadcasted_iota(jnp.int32, (n, 1), 0).astype(F32)


def _log_decay(lg_ref, d, h):
    return -jnp.abs(jnp.full((1, 1), lg_ref[d, h], F32))


def _ret_bstate_body(lg_ref, k_ref, v_ref, sb_ref, st_ref):
    C = k_ref.shape[0]

    @pl.when(pl.program_id(0) == 0)
    def _():
        st_ref[...] = jnp.zeros_like(st_ref)

    jj = _row_index(C)
    for h in range(RET_HEADS):
        lo = h * RET_HEAD_DIM
        lgb = _log_decay(lg_ref, 1, h)
        sb_ref[h] = st_ref[h].astype(BF16)
        kd = (k_ref[:, lo:lo + RET_HEAD_DIM].astype(F32) * jnp.exp(jj * lgb)).astype(BF16)
        kv = lax.dot_general(kd, v_ref[:, lo:lo + RET_HEAD_DIM], (((0,), (0,)), ((), ())),
                             preferred_element_type=F32)
        st_ref[h] = st_ref[h] * jnp.exp(C * lgb) + kv


def _ret_bstate(lg, h5):
    L = h5.shape[1]
    C = RET_C
    nc = L // C
    return pl.pallas_call(
        _ret_bstate_body,
        grid=(nc,),
        in_specs=[
            pl.BlockSpec(memory_space=pltpu.SMEM),
            pl.BlockSpec((None, C, RET_WIDTH), lambda i: (SEG_K - 1, nc - 1 - i, 0)),
            pl.BlockSpec((None, C, RET_WIDTH), lambda i: (SEG_V - 1, nc - 1 - i, 0)),
        ],
        out_specs=pl.BlockSpec((None, RET_HEADS, RET_HEAD_DIM, RET_HEAD_DIM),
                               lambda i: (nc - 1 - i, 0, 0, 0)),
        out_shape=jax.ShapeDtypeStruct((nc, RET_HEADS, RET_HEAD_DIM, RET_HEAD_DIM), BF16),
        scratch_shapes=[pltpu.VMEM((RET_HEADS, RET_HEAD_DIM, RET_HEAD_DIM), F32)],
        compiler_params=_params(("arbitrary",)),
        name="ret_bstate",
    )(lg, h5, h5)


def _ret_main_body(lg_ref, q_ref, k_ref, v_ref, g_ref, sb_ref, o_ref, st_ref, dec_ref):
    C = q_ref.shape[0]

    @pl.when(pl.program_id(0) == 0)
    def _():
        st_ref[...] = jnp.zeros_like(st_ref)
        diff = (lax.broadcasted_iota(jnp.int32, (C, C), 0)
                - lax.broadcasted_iota(jnp.int32, (C, C), 1)).astype(F32)
        for h in range(RET_HEADS):
            dec_ref[h] = jnp.where(diff >= 0, jnp.exp(_log_decay(lg_ref, 0, h) * jnp.maximum(diff, 0.0)),
                                   jnp.exp(_log_decay(lg_ref, 1, h) * jnp.maximum(-diff, 0.0)))

    ii = _row_index(C)
    for h in range(RET_HEADS):
        lo = h * RET_HEAD_DIM
        lgf = _log_decay(lg_ref, 0, h)
        lgb = _log_decay(lg_ref, 1, h)
        q = q_ref[:, lo:lo + RET_HEAD_DIM]
        k = k_ref[:, lo:lo + RET_HEAD_DIM]
        v = v_ref[:, lo:lo + RET_HEAD_DIM]
        qf = q.astype(F32)
        s = lax.dot_general(q, k, (((1,), (1,)), ((), ())), preferred_element_type=F32)
        o = jnp.dot((s * dec_ref[h]).astype(BF16), v, preferred_element_type=F32)
        qdf = (qf * jnp.exp((ii + 1.0) * lgf)).astype(BF16)
        o = o + jnp.dot(qdf, st_ref[h].astype(BF16), preferred_element_type=F32)
        qdb = (qf * jnp.exp((C - ii) * lgb)).astype(BF16)
        o = o + jnp.dot(qdb, sb_ref[h], preferred_element_type=F32)
        mu = jnp.mean(o, axis=-1, keepdims=True)
        oc = o - mu
        var = jnp.mean(oc * oc, axis=-1, keepdims=True)
        on = oc * lax.rsqrt(var + HEAD_NORM_EPS)
        o_ref[:, lo:lo + RET_HEAD_DIM] = (on * g_ref[:, lo:lo + RET_HEAD_DIM].astype(F32)).astype(BF16)
        kd = (k.astype(F32) * jnp.exp((C - 1.0 - ii) * lgf)).astype(BF16)
        kv = lax.dot_general(kd, v, (((0,), (0,)), ((), ())), preferred_element_type=F32)
        st_ref[h] = st_ref[h] * jnp.exp(C * lgf) + kv


def _ret_main(lg, h5, sb):
    L = h5.shape[1]
    C = RET_C
    nc = L // C
    seg = lambda s: pl.BlockSpec((None, C, RET_WIDTH), lambda i: (s - 1, i, 0))
    return pl.pallas_call(
        _ret_main_body,
        grid=(nc,),
        in_specs=[
            pl.BlockSpec(memory_space=pltpu.SMEM),
            seg(SEG_Q), seg(SEG_K), seg(SEG_V), seg(SEG_GATE),
            pl.BlockSpec((None, RET_HEADS, RET_HEAD_DIM, RET_HEAD_DIM), lambda i: (i, 0, 0, 0)),
        ],
        out_specs=pl.BlockSpec((C, RET_WIDTH), lambda i: (i, 0)),
        out_shape=jax.ShapeDtypeStruct((L, RET_WIDTH), BF16),
        scratch_shapes=[pltpu.VMEM((RET_HEADS, RET_HEAD_DIM, RET_HEAD_DIM), F32),
                        pltpu.VMEM((RET_HEADS, C, C), F32)],
        compiler_params=_params(("arbitrary",)),
        name="ret_main",
    )(lg, h5, h5, h5, h5, sb)


def _outproj_even_body(y_ref, r_ref, x_ref, wg_ref, bg_ref, wo_ref, g_ref, b_ref, o_ref):
    nl = y_ref.shape[1]
    tm = S5_T * nl
    shift = S5_T.bit_length() - 1
    y_steps = y_ref[...].reshape(tm, y_ref.shape[2])
    perm = _row_permutation(tm, lambda r: (r & (S5_T - 1)) * nl + lax.shift_right_logical(r, shift))
    for r in range(0, tm, LN_ROWS):
        rows = slice(r, r + LN_ROWS)
        y = jnp.dot(perm[rows, :], y_steps, preferred_element_type=F32).astype(BF16)
        z = jnp.dot(y, wg_ref[...], preferred_element_type=F32) + bg_ref[...]
        s5 = (y.astype(F32) * jax.nn.sigmoid(z)).astype(BF16)
        mix = (jnp.dot(s5, wo_ref[0:S5_WIDTH, :], preferred_element_type=F32)
               + jnp.dot(r_ref[rows, :], wo_ref[S5_WIDTH:S5_WIDTH + RET_WIDTH, :],
                         preferred_element_type=F32))
        o_ref[rows, :] = _layer_norm_rows(DEEPNORM_ALPHA * x_ref[rows, :] + mix, g_ref[...], b_ref[...])


def _outproj_even(y_steps, ret, x, w_glu_bf, b_glu, w_out_bf, ln_g, ln_b, tm=512):
    L = x.shape[0]
    row = lambda n: pl.BlockSpec((tm, n), lambda i: (i, 0))
    full = lambda a: pl.BlockSpec(a.shape, lambda i: (0,) * a.ndim)
    return pl.pallas_call(
        _outproj_even_body,
        grid=(L // tm,),
        in_specs=[pl.BlockSpec((S5_T, tm // S5_T, S5_WIDTH), lambda i: (0, i, 0)),
                  row(RET_WIDTH), row(D_MODEL), full(w_glu_bf), full(b_glu),
                  full(w_out_bf), full(ln_g), full(ln_b)],
        out_specs=row(D_MODEL),
        out_shape=jax.ShapeDtypeStruct((L, D_MODEL), F32),
        compiler_params=_params(("parallel",)),
        name="outproj_even",
    )(y_steps, ret, x, w_glu_bf, b_glu, w_out_bf, ln_g, ln_b)


def _inproj_odd_body(x_ref, w_ref, ca_ref, sa_ref, cb_ref, sb_ref, w1f_ref, w2f_ref,
                     o_ref, w1b_ref, w2b_ref):
    w1b_ref[...] = w1f_ref[...].astype(BF16)
    w2b_ref[...] = w2f_ref[...].astype(BF16)
    xb = x_ref[...].astype(BF16)
    tn = ATT_KV_HEADS * ATT_HEAD_DIM
    nq = (ATT_HEADS * ATT_HEAD_DIM) // tn
    cc0, sin_blk = _rotary_block(ca_ref, sa_ref, cb_ref, sb_ref, xb.shape[0])
    lane = lax.broadcasted_iota(jnp.int32, sin_blk.shape, 1)
    s10 = jnp.where(lane < ROPE_DIM // 2, -sin_blk, 0.0)
    s20 = jnp.where((lane >= ROPE_DIM // 2) & (lane < ROPE_DIM), sin_blk, 0.0)
    for cb in range(w_ref.shape[1] // tn):
        acc = jnp.dot(xb, w_ref[:, cb * tn:(cb + 1) * tn], preferred_element_type=F32)
        if cb > nq:
            o_ref[:, cb * tn:(cb + 1) * tn] = acc.astype(BF16)
            continue
        scale = ATT_HEAD_DIM ** -0.5 if cb < nq else 1.0
        cc = cc0 * scale
        s1 = s10 * scale
        s2 = s20 * scale
        for hh in range(tn // ATT_HEAD_DIM):
            lo = hh * ATT_HEAD_DIM
            a = acc[:, lo:lo + ATT_HEAD_DIM]
            up = pltpu.roll(a, ATT_HEAD_DIM - ROPE_DIM // 2, axis=1)
            dn = pltpu.roll(a, ROPE_DIM // 2, axis=1)
            o_ref[:, cb * tn + lo:cb * tn + lo + ATT_HEAD_DIM] = (a * cc + up * s1 + dn * s2).astype(BF16)


def _inproj_odd(x, w_in_bf, rot, mlp_w1, mlp_w2, layer, tm=256):
    L = x.shape[0]
    n_out = w_in_bf.shape[1]
    row = lambda n: pl.BlockSpec((tm, n), lambda i: (i, 0))
    cast_in, cast_out, cast_shapes = _mlp_weight_cast_specs(L // tm, mlp_w1, mlp_w2, layer)
    return pl.pallas_call(
        _inproj_odd_body,
        grid=(L // tm,),
        in_specs=[
            row(D_MODEL),
            pl.BlockSpec(w_in_bf.shape, lambda i: (0, 0), pipeline_mode=pl.Buffered(1)),
        ] + [pl.BlockSpec(t.shape, lambda i: (0, 0)) for t in rot] + cast_in,
        out_specs=[row(n_out)] + cast_out,
        out_shape=[jax.ShapeDtypeStruct((L, n_out), BF16)] + cast_shapes,
        compiler_params=_params(("parallel",)),
        name="inproj_odd",
    )(x, w_in_bf, *rot, mlp_w1, mlp_w2)


ATT_QB = 2


def _attn_body(sink_ref, q_ref, *refs):
    k_refs = refs[:ATT_QB + 2]
    v_refs = refs[ATT_QB + 2:2 * ATT_QB + 4]
    o_ref, bias_ref = refs[2 * ATT_QB + 4:]
    step = pl.program_id(0)
    nb = pl.num_programs(0) * ATT_QB
    B = ATT_BLOCK
    hd = ATT_HEAD_DIM
    rows = ATT_GROUP * B

    @pl.when(step == 0)
    def _():
        r_i = lax.broadcasted_iota(jnp.int32, (rows, 3 * B), 0)
        s_i = lax.broadcasted_iota(jnp.int32, (rows, 3 * B), 1)
        rel = (r_i & (B - 1)) - s_i + B
        in_win = jnp.abs(rel) <= ATT_WINDOW
        bias_ref[0] = jnp.where(in_win & (s_i >= B), 0.0, NEG_INF)
        bias_ref[1] = jnp.where(in_win, 0.0, NEG_INF)
        bias_ref[2] = jnp.where(in_win & (s_i < 2 * B), 0.0, NEG_INF)

    head_of_row = lax.shift_right_logical(lax.broadcasted_iota(jnp.int32, (rows, 1), 0),
                                          int(math.log2(B)))
    for qb in range(ATT_QB):
        c = step * ATT_QB + qb
        bias = bias_ref[jnp.where(c == 0, 0, jnp.where(c == nb - 1, 2, 1))]
        for g in range(ATT_KV_HEADS):
            q = jnp.concatenate([q_ref[qb * B:(qb + 1) * B, (g * ATT_GROUP + hh) * hd:(g * ATT_GROUP + hh + 1) * hd]
                                 for hh in range(ATT_GROUP)], axis=0)
            ksl = slice(g * hd, (g + 1) * hd)
            k = jnp.concatenate([r[:, ksl] for r in k_refs[qb:qb + 3]], axis=0)
            v = jnp.concatenate([r[:, ksl] for r in v_refs[qb:qb + 3]], axis=0)
            s = lax.dot_general(q, k, (((1,), (1,)), ((), ())), preferred_element_type=F32) + bias
            sink = jnp.zeros((rows, 1), F32)
            for hh in range(ATT_GROUP):
                sink = jnp.where(head_of_row == hh, sink_ref[g * ATT_GROUP + hh], sink)
            m = jnp.maximum(jnp.max(s, axis=-1, keepdims=True), sink)
            p = jnp.exp(s - m)
            den = jnp.sum(p, axis=-1, keepdims=True) + jnp.exp(sink - m)
            o = jnp.dot(p.astype(BF16), v, preferred_element_type=F32) / den
            for hh in range(ATT_GROUP):
                hcol = (g * ATT_GROUP + hh) * hd
                o_ref[qb * B:(qb + 1) * B, hcol:hcol + hd] = o[hh * B:(hh + 1) * B, :].astype(BF16)


def _attention(sink, qkv):
    L = qkv.shape[0]
    B = ATT_BLOCK
    nb = L // B
    assert nb % ATT_QB == 0 and nb >= 2
    kvw = ATT_KV_HEADS * ATT_HEAD_DIM
    qw = ATT_HEADS * ATT_HEAD_DIM
    kcol = qw // kvw
    vcol = kcol + 1

    def kv_spec(col, off):
        return pl.BlockSpec((B, kvw), lambda i: (jnp.clip(i * ATT_QB + off, 0, nb - 1), col))

    offs = range(-1, ATT_QB + 1)
    return pl.pallas_call(
        _attn_body,
        grid=(nb // ATT_QB,),
        in_specs=[
            pl.BlockSpec(memory_space=pltpu.SMEM),
            pl.BlockSpec((ATT_QB * B, qw), lambda i: (i, 0)),
        ] + [kv_spec(kcol, o) for o in offs] + [kv_spec(vcol, o) for o in offs],
        out_specs=pl.BlockSpec((ATT_QB * B, qw), lambda i: (i, 0)),
        out_shape=jax.ShapeDtypeStruct((L, qw), BF16),
        scratch_shapes=[pltpu.VMEM((3, ATT_GROUP * B, 3 * B), F32)],
        compiler_params=_params(("arbitrary",)),
        name="attention",
    )(sink, qkv, *([qkv] * (2 * ATT_QB + 4)))


def _outproj_odd_body(a_ref, x_ref, wo_ref, g_ref, b_ref, o_ref):
    for r in range(0, a_ref.shape[0], LN_ROWS):
        rows = slice(r, r + LN_ROWS)
        mix = jnp.dot(a_ref[rows, :], wo_ref[...], preferred_element_type=F32)
        o_ref[rows, :] = _layer_norm_rows(DEEPNORM_ALPHA * x_ref[rows, :] + mix, g_ref[...], b_ref[...])


def _outproj_odd(a, x, w_out_bf, ln_g, ln_b, tm=512):
    L = x.shape[0]
    row = lambda n: pl.BlockSpec((tm, n), lambda i: (i, 0))
    full = lambda t: pl.BlockSpec(t.shape, lambda i: (0,) * t.ndim)
    return pl.pallas_call(
        _outproj_odd_body,
        grid=(L // tm,),
        in_specs=[row(a.shape[1]), row(D_MODEL), full(w_out_bf), full(ln_g), full(ln_b)],
        out_specs=row(D_MODEL),
        out_shape=jax.ShapeDtypeStruct((L, D_MODEL), F32),
        compiler_params=_params(("parallel",)),
        name="outproj_odd",
    )(a, x, w_out_bf, ln_g, ln_b)


def _mlp_body(x_ref, w1_ref, w2_ref, g_ref, b_ref, o_ref, xb_ref, acc_ref):
    f = pl.program_id(1)

    @pl.when(f == 0)
    def _():
        xb_ref[...] = x_ref[...].astype(BF16)
        acc_ref[...] = jnp.zeros_like(acc_ref)

    def partial_sum(rows):
        h = jnp.dot(xb_ref[rows, :], w1_ref[...], preferred_element_type=F32)
        h = jnp.square(jnp.maximum(h, 0.0)).astype(BF16)
        return jnp.dot(h, w2_ref[...], preferred_element_type=F32)

    last = pl.num_programs(1) - 1

    @pl.when(f < last)
    def _():
        acc_ref[...] += partial_sum(slice(None))

    @pl.when(f == last)
    def _():
        for r in range(0, x_ref.shape[0], LN_ROWS):
            rows = slice(r, r + LN_ROWS)
            y = DEEPNORM_ALPHA * x_ref[rows, :] + (acc_ref[rows, :] + partial_sum(rows))
            o_ref[rows, :] = _layer_norm_rows(y, g_ref[...], b_ref[...])


def _mlp(x, w1, w2, ln_g, ln_b, tm=512, tf=1024):
    L = x.shape[0]
    return pl.pallas_call(
        _mlp_body,
        grid=(L // tm, D_FF // tf),
        in_specs=[
            pl.BlockSpec((tm, D_MODEL), lambda i, f: (i, 0)),
            pl.BlockSpec((D_MODEL, tf), lambda i, f: (0, f)),
            pl.BlockSpec((tf, D_MODEL), lambda i, f: (f, 0)),
            pl.BlockSpec((1, D_MODEL), lambda i, f: (0, 0)),
            pl.BlockSpec((1, D_MODEL), lambda i, f: (0, 0)),
        ],
        out_specs=pl.BlockSpec((tm, D_MODEL), lambda i, f: (i, 0)),
        out_shape=jax.ShapeDtypeStruct((L, D_MODEL), F32),
        scratch_shapes=[pltpu.VMEM((tm, D_MODEL), BF16), pltpu.VMEM((tm, D_MODEL), F32)],
        compiler_params=_params(("parallel", "arbitrary")),
        name="mlp",
    )(x, w1, w2, ln_g, ln_b)


def _rotary_tables(L, rot_dim, theta):
    half = rot_dim // 2
    inv_freq = 1.0 / (theta ** (jnp.arange(half, dtype=F32) / half))
    ang_a = (jnp.arange(L // ROT_LO) * ROT_LO).astype(F32)[:, None] * inv_freq[None, :]
    ang_b = jnp.arange(ROT_LO).astype(F32)[:, None] * inv_freq[None, :]
    return jnp.cos(ang_a), jnp.sin(ang_a), jnp.cos(ang_b), jnp.sin(ang_b)


def _even_layer(x, w_in, w_out, lam_re, lam_im, log_step, b_re, b_im, c_re, c_im,
                d_skip, w_glu, b_glu, ret_log_decay, ln_g, ln_b, mlp_w1, mlp_w2, layer):
    L = x.shape[0]
    rot = _rotary_tables(L, RET_HEAD_DIM, RET_ROPE_THETA)
    u_steps, h5, w1_bf, w2_bf = _inproj_even(x, w_in.astype(BF16), rot, mlp_w1, mlp_w2, layer)
    disc = _s5_disc(lam_re, lam_im, log_step)
    m, ws_t, wc = _s5_gen(disc, b_re, b_im, c_re, c_im)
    d_tile = jnp.tile(d_skip.astype(F32), (1, S5_T)).reshape(S5_GROUPS, 1, S5_TL)
    a1, a2, a3 = (disc[k].reshape(2, 1, S5_GROUPS * 128) for k in (DISC_AT_RE, DISC_SCAN_A2, DISC_SCAN_A3))
    y = _s5_mix(u_steps, m, ws_t, wc, d_tile, a1, a2, a3)
    lg = ret_log_decay.astype(F32)
    sb = _ret_bstate(lg, h5)
    ret = _ret_main(lg, h5, sb)
    x1 = _outproj_even(y, ret, x, w_glu.astype(BF16), b_glu.astype(F32).reshape(1, -1),
                       w_out.astype(BF16), ln_g.reshape(1, -1), ln_b.reshape(1, -1))
    return x1, w1_bf, w2_bf


def _odd_layer(x, w_in, w_out, sink, ln_g, ln_b, mlp_w1, mlp_w2, layer):
    L = x.shape[0]
    pad = ATT_HEAD_DIM - ROPE_DIM
    widen = lambda t, fill: jnp.concatenate([t, t, jnp.full((t.shape[0], pad), fill, F32)], axis=1)
    ca, sa, cb, sb = _rotary_tables(L, ROPE_DIM, ROPE_THETA)
    rot = (widen(ca, 1.0), widen(sa, 0.0), widen(cb, 1.0), widen(sb, 0.0))
    qkv, w1_bf, w2_bf = _inproj_odd(x, w_in.astype(BF16), rot, mlp_w1, mlp_w2, layer)
    att = _attention(sink.astype(F32), qkv)
    x1 = _outproj_odd(att, x, w_out.astype(BF16), ln_g.reshape(1, -1), ln_b.reshape(1, -1))
    return x1, w1_bf, w2_bf


def kernel(x, ln_g, ln_b, mlp_w1, mlp_w2, even_w_in, even_w_out, s5_lambda_re, s5_lambda_im, s5_log_step, s5_b_re, s5_b_im, s5_c_re, s5_c_im, s5_d, s5_w_glu, s5_b_glu, ret_log_decay, odd_w_in, odd_w_out, attn_sink):
    bsz = x.shape[0]
    outs = []
    for b in range(bsz):
        xb = x[b]
        for layer in range(DEPTH):
            if layer % 2 == 0:
                e = layer // 2
                xb, w1_bf, w2_bf = _even_layer(
                    xb, even_w_in[e], even_w_out[e], s5_lambda_re[e], s5_lambda_im[e],
                    s5_log_step[e], s5_b_re[e], s5_b_im[e], s5_c_re[e], s5_c_im[e],
                    s5_d[e], s5_w_glu[e], s5_b_glu[e], ret_log_decay[e],
                    ln_g[layer, 0], ln_b[layer, 0], mlp_w1, mlp_w2, layer)
            else:
                o = layer // 2
                xb, w1_bf, w2_bf = _odd_layer(xb, odd_w_in[o], odd_w_out[o], attn_sink[o],
                                              ln_g[layer, 0], ln_b[layer, 0], mlp_w1, mlp_w2, layer)
            xb = _mlp(xb, w1_bf, w2_bf,
                      ln_g[layer, 1].reshape(1, -1), ln_b[layer, 1].reshape(1, -1))
        outs.append(xb)
    return jnp.stack(outs, axis=0)
```

```python
import functools
import math

import jax
import jax.numpy as jnp
from jax import lax
from jax.experimental import pallas as pl
from jax.experimental.pallas import tpu as pltpu

F32 = jnp.float32
BF16 = jnp.bfloat16

D_MODEL = 2048
DEPTH = 2
S5_WIDTH = 1024
S5_GROUP = 16
S5_GROUPS = 64
S5_STATE = 64
RET_WIDTH = 1024
RET_HEADS = 4
RET_HEAD_DIM = 256
RET_ROPE_THETA = 10000.0
ATT_HEADS = 16
ATT_KV_HEADS = 4
ATT_HEAD_DIM = 128
ATT_GROUP = 4
ATT_WINDOW = 128
ATT_BLOCK = 128
ROPE_THETA = 500000.0
ROPE_DIM = 32
D_FF = 4 * D_MODEL
DEEPNORM_ALPHA = (2 * DEPTH) ** 0.25
LN_EPS = 1e-5
HEAD_NORM_EPS = 1e-6
NEG_INF = -1e30

V7X_VMEM_BYTES = 64 * 1024 * 1024
VMEM_LIMIT = V7X_VMEM_BYTES - 8 * 1024 * 1024

S5_T = 32
S5_TL = S5_T * S5_GROUP
S5_GB = 8
S5_GEN_GB = 8
RET_C = 256
LN_ROWS = 256


def _params(sem):
    return pltpu.CompilerParams(dimension_semantics=sem, vmem_limit_bytes=VMEM_LIMIT)


def _layer_norm_rows(y, g, b):
    mu = jnp.mean(y, axis=-1, keepdims=True)
    yc = y - mu
    var = jnp.mean(yc * yc, axis=-1, keepdims=True)
    return yc * lax.rsqrt(var + LN_EPS) * g + b


def _block_transpose8(xs):
    blk = lax.shift_right_logical(lax.broadcasted_iota(jnp.int32, xs[0].shape, 1), 4)
    xs = list(xs)
    for k in range(3):
        d = 1 << k
        upper = (blk & d) != 0
        for i in range(8):
            if i & d:
                continue
            a, b = xs[i], xs[i + d]
            xs[i] = jnp.where(upper, pltpu.roll(b, S5_GROUP * d, axis=1), a)
            xs[i + d] = jnp.where(upper, b, pltpu.roll(a, 128 - S5_GROUP * d, axis=1))
    return xs


ROT_LO = 64


def _rotary_block(ca_ref, sa_ref, cb_ref, sb_ref, tm):
    cb = cb_ref[...]
    sb = sb_ref[...]
    n = tm // ROT_LO
    cos, sin = [], []
    for al in range(n):
        row = pl.ds(pl.program_id(0) * n + al, 1)
        ca = ca_ref[row, :]
        sa = sa_ref[row, :]
        cos.append(ca * cb - sa * sb)
        sin.append(sa * cb + ca * sb)
    return jnp.concatenate(cos, axis=0), jnp.concatenate(sin, axis=0)


SEG_U, SEG_Q, SEG_K, SEG_V, SEG_GATE = range(5)


def _row_permutation(n_rows, src_of_row):
    r = lax.broadcasted_iota(jnp.int32, (n_rows, n_rows), 0)
    c = lax.broadcasted_iota(jnp.int32, (n_rows, n_rows), 1)
    return jnp.where(c == src_of_row(r), 1.0, 0.0).astype(BF16)


def _mlp_weight_cast_specs(n_steps, mlp_w1, mlp_w2, layer):
    ins, outs, shapes = [], [], []
    for w in (mlp_w1, mlp_w2):
        rows, cols = w.shape[1] // n_steps, w.shape[2]
        ins.append(pl.BlockSpec((None, rows, cols), lambda i: (layer, i, 0)))
        outs.append(pl.BlockSpec((rows, cols), lambda i: (i, 0)))
        shapes.append(jax.ShapeDtypeStruct(w.shape[1:], BF16))
    return ins, outs, shapes


def _inproj_even_body(x_ref, w_ref, ca_ref, sa_ref, cb_ref, sb_ref, w1f_ref, w2f_ref,
                      u_ref, o_ref, w1b_ref, w2b_ref):
    w1b_ref[...] = w1f_ref[...].astype(BF16)
    w2b_ref[...] = w2f_ref[...].astype(BF16)
    xb = x_ref[...].astype(BF16)
    tm = xb.shape[0]
    tn = S5_WIDTH
    half = RET_HEAD_DIM // 2
    cos_blk, sin_blk = _rotary_block(ca_ref, sa_ref, cb_ref, sb_ref, tm)
    for seg in range(w_ref.shape[1] // tn):
        acc = jnp.dot(xb, w_ref[:, seg * tn:(seg + 1) * tn], preferred_element_type=F32)
        if seg == SEG_U:
            nl = tm // S5_T
            shift = nl.bit_length() - 1
            perm = _row_permutation(tm, lambda r: (r & (nl - 1)) * S5_T + lax.shift_right_logical(r, shift))
            up = jnp.dot(perm, acc.astype(BF16), preferred_element_type=F32)
            for t in range(S5_T):
                u_ref[t] = up[t * nl:(t + 1) * nl, :]
        elif seg in (SEG_Q, SEG_K):
            scale = RET_HEAD_DIM ** -0.5 if seg == SEG_K else 1.0
            cos = cos_blk * scale
            sin = sin_blk * scale
            for hh in range(RET_HEADS):
                lo = hh * RET_HEAD_DIM
                a = acc[:, lo:lo + half]
                b = acc[:, lo + half:lo + RET_HEAD_DIM]
                o_ref[seg - 1, :, lo:lo + half] = (a * cos - b * sin).astype(BF16)
                o_ref[seg - 1, :, lo + half:lo + RET_HEAD_DIM] = (b * cos + a * sin).astype(BF16)
        elif seg == SEG_GATE:
            o_ref[seg - 1] = (acc * jax.nn.sigmoid(acc)).astype(BF16)
        else:
            o_ref[seg - 1] = acc.astype(BF16)


def _inproj_even(x, w_in_bf, rot, mlp_w1, mlp_w2, layer, tm=256):
    L = x.shape[0]
    tn = S5_WIDTH
    nseg = w_in_bf.shape[1] // tn
    row = lambda n: pl.BlockSpec((tm, n), lambda i: (i, 0))
    cast_in, cast_out, cast_shapes = _mlp_weight_cast_specs(L // tm, mlp_w1, mlp_w2, layer)
    return pl.pallas_call(
        _inproj_even_body,
        grid=(L // tm,),
        in_specs=[
            row(D_MODEL),
            pl.BlockSpec(w_in_bf.shape, lambda i: (0, 0), pipeline_mode=pl.Buffered(1)),
        ] + [pl.BlockSpec(t.shape, lambda i: (0, 0)) for t in rot] + cast_in,
        out_specs=[pl.BlockSpec((S5_T, tm // S5_T, tn), lambda i: (0, i, 0)),
                   pl.BlockSpec((nseg - 1, tm, tn), lambda i: (0, i, 0))] + cast_out,
        out_shape=[jax.ShapeDtypeStruct((S5_T, L // S5_T, tn), F32),
                   jax.ShapeDtypeStruct((nseg - 1, L, tn), BF16)] + cast_shapes,
        compiler_params=_params(("parallel",)),
        name="inproj_even",
    )(x, w_in_bf, *rot, mlp_w1, mlp_w2)


def _cmul(ar, ai, br, bi):
    return ar * br - ai * bi, ar * bi + ai * br


(DISC_A1_RE, DISC_A1_IM, DISC_A2_RE, DISC_A2_IM, DISC_A4_RE, DISC_A4_IM, DISC_A8_RE, DISC_A8_IM,
 DISC_A16_RE, DISC_A16_IM, DISC_AT_RE, DISC_AT_IM, DISC_Z_RE, DISC_Z_IM, DISC_SCAN_A2, DISC_SCAN_A3) = range(16)


def _s5_disc_body(lr_ref, li_ref, ls_ref, sg_ref, o_ref):
    lr = jnp.minimum(lr_ref[...], -1e-4)
    li = li_ref[...]
    step = jnp.exp(ls_ref[...])
    mag = jnp.exp(lr * step)
    ar = mag * jnp.cos(li * step)
    ai = mag * jnp.sin(li * step)
    nr, ni = ar - 1.0, ai
    den = lr * lr + li * li
    o_ref[DISC_Z_RE] = (nr * lr + ni * li) / den
    o_ref[DISC_Z_IM] = (ni * lr - nr * li) / den
    pr, pi = ar, ai
    for k in range(6):
        o_ref[2 * k] = pr
        o_ref[2 * k + 1] = pi
        if k < 5:
            pr, pi = _cmul(pr, pi, pr, pi)
    o_ref[DISC_SCAN_A2] = pi * sg_ref[...]
    o_ref[DISC_SCAN_A3] = -pi * sg_ref[...]


def _s5_disc(lam_re, lam_im, log_step):
    assert S5_T == 32
    rows = 2 * S5_GROUPS
    two = lambda a: jnp.tile(a.astype(F32).reshape(rows, -1), (1, 2))
    lr = two(lam_re)
    li = two(lam_im)
    ls = jnp.broadcast_to(log_step.astype(F32).reshape(rows, 1), (rows, 2 * S5_STATE))
    sg = jnp.broadcast_to(jnp.concatenate([-jnp.ones((S5_STATE,), F32), jnp.ones((S5_STATE,), F32)])[None],
                          (rows, 2 * S5_STATE))
    return pl.pallas_call(
        _s5_disc_body,
        out_shape=jax.ShapeDtypeStruct((16, rows, 2 * S5_STATE), F32),
        name="s5_disc",
    )(lr, li, ls, sg)


def _s5_gen_body(n_round, col_ref, bre_ref, bim_ref, cre_ref, cim_ref, *refs):
    m_ref, ws_ref, wc_ref = refs[n_round:n_round + 3]
    for src, dst in zip(refs[:n_round], refs[n_round + 3:]):
        dst[...] = src[...].astype(BF16)
    P = S5_STATE
    hi = lax.Precision.HIGHEST
    tlo = lax.shift_right_logical(lax.broadcasted_iota(jnp.int32, (P, 128), 1), 4)
    lane = lax.broadcasted_iota(jnp.int32, (S5_GROUP, S5_TL), 1)
    ones = jnp.ones((P, 128), F32)
    zeros = jnp.zeros((P, 128), F32)
    spread = jnp.where((lax.broadcasted_iota(jnp.int32, (S5_GROUP, 128), 1) & (S5_GROUP - 1))
                       == lax.broadcasted_iota(jnp.int32, (S5_GROUP, 128), 0), 1.0, 0.0)
    contract0 = (((0,), (0,)), ((), ()))

    def one_group(gi, carry):
        kt = []
        for d in range(2):
            col = col_ref[gi, d]
            c = lambda k: jnp.broadcast_to(col[:, k:k + 1], (P, 128))
            a1 = (c(DISC_A1_RE), c(DISC_A1_IM))
            a2 = (c(DISC_A2_RE), c(DISC_A2_IM))
            a4 = (c(DISC_A4_RE), c(DISC_A4_IM))
            a8 = (c(DISC_A8_RE), c(DISC_A8_IM))
            a16 = (c(DISC_A16_RE), c(DISC_A16_IM))
            blk = [None, a8, a16, _cmul(*a8, *a16)]

            def low_powers(reverse):
                xr, xi = ones, zeros
                for k, ak in enumerate((a1, a2, a4)):
                    bit = (lax.shift_right_logical(tlo, k) & 1) == (0 if reverse else 1)
                    yr, yi = _cmul(xr, xi, *ak)
                    xr = jnp.where(bit, yr, xr)
                    xi = jnp.where(bit, yi, xi)
                return xr, xi

            def expand(base, reverse):
                out = []
                for j in range(4):
                    f = blk[3 - j] if reverse else blk[j]
                    out.append(base if f is None else _cmul(*base, *f))
                return out

            ct = tuple(lax.dot_general(r[d, gi], spread, contract0, precision=hi, preferred_element_type=F32)
                       for r in (cre_ref, cim_ref))
            bt = tuple(jnp.dot(r[d, gi], spread, precision=hi, preferred_element_type=F32)
                       for r in (bre_ref, bim_ref))
            bbar = _cmul(c(DISC_Z_RE), c(DISC_Z_IM), *bt)
            ca = expand(_cmul(*ct, *low_powers(d == 1)), d == 1)
            ba = expand(_cmul(*bbar, *low_powers(d == 0)), d == 0)
            wcj = [_cmul(*x, *a1) for x in ca]
            cat = lambda parts, k: jnp.concatenate([x[k] for x in parts], axis=1)
            kt.append(lax.dot_general(bbar[0][:, :S5_GROUP], cat(ca, 0), contract0, precision=hi,
                                      preferred_element_type=F32)
                      - lax.dot_general(bbar[1][:, :S5_GROUP], cat(ca, 1), contract0, precision=hi,
                                        preferred_element_type=F32))
            ba_r, ba_i = cat(ba, 0).astype(BF16), cat(ba, 1).astype(BF16)
            for r, part in enumerate((ba_r, ba_i, ba_i, ba_r)):
                ws_ref[gi, (4 * d + r) * P:(4 * d + r + 1) * P, :] = part
            wc_ref[gi, 2 * d * P:(2 * d + 1) * P, :] = cat(wcj, 0).astype(BF16)
            wc_ref[gi, (2 * d + 1) * P:(2 * d + 2) * P, :] = (-cat(wcj, 1)).astype(BF16)
        ktf, ktb = kt
        for s in range(S5_T):
            lo, hi_lane = S5_GROUP * s, S5_GROUP * (s + 1)
            f = ktf if s == 0 else jnp.where(lane >= lo, pltpu.roll(ktf, lo, axis=1), 0.0)
            b = ktb if s == S5_T - 1 else jnp.where(lane < hi_lane, pltpu.roll(ktb, hi_lane, axis=1), 0.0)
            m_ref[gi, lo:hi_lane, :] = (f + b).astype(BF16)
        return carry

    lax.fori_loop(0, col_ref.shape[0], one_group, 0)


def _s5_gen(disc, b_re, b_im, c_re, c_im, to_round):
    G, P, Cg = S5_GROUPS, S5_STATE, S5_GROUP
    gb = S5_GEN_GB
    steps = G // gb
    col = disc[:, :, :P].reshape(16, 2, G, P).transpose(2, 1, 3, 0)
    per_dir = lambda a: pl.BlockSpec((2, gb) + a.shape[2:], lambda i: (0, i, 0, 0))
    out = lambda rows: pl.BlockSpec((gb, rows, S5_TL), lambda i: (i, 0, 0))
    row_slice = lambda w: pl.BlockSpec((w.shape[0] // steps, w.shape[1]), lambda i: (i, 0))
    res = pl.pallas_call(
        functools.partial(_s5_gen_body, len(to_round)),
        grid=(steps,),
        in_specs=[pl.BlockSpec((gb,) + col.shape[1:], lambda i: (i, 0, 0, 0)),
                  per_dir(b_re), per_dir(b_im), per_dir(c_re), per_dir(c_im)]
                 + [row_slice(w) for w in to_round],
        out_specs=[out(S5_TL), out(8 * P), out(4 * P)] + [row_slice(w) for w in to_round],
        out_shape=[jax.ShapeDtypeStruct((G, S5_TL, S5_TL), BF16),
                   jax.ShapeDtypeStruct((G, 8 * P, S5_TL), BF16),
                   jax.ShapeDtypeStruct((G, 4 * P, S5_TL), BF16)]
                  + [jax.ShapeDtypeStruct(w.shape, BF16) for w in to_round],
        compiler_params=_params(("parallel",)),
        name="s5_gen",
    )(col, b_re.astype(F32), b_im.astype(F32), c_re.astype(F32), c_im.astype(F32), *to_round)
    return res[0], res[1], res[2], tuple(res[3:])


def _s5_body(ut_ref, m_ref, ws_ref, wc_ref, dt_ref, a1_ref, a2_ref, a3_ref, y_ref,
             s_ref, h_ref, yg_ref, u_ref):
    nc = u_ref.shape[1]
    gb = u_ref.shape[0]
    w = gb * 128
    for jb in range(S5_T // 8):
        ys = _block_transpose8([ut_ref[8 * jb + t8] for t8 in range(8)])
        for g8 in range(8):
            u_ref[g8, :, jb * 128:(jb + 1) * 128] = ys[g8].astype(BF16)
    for gi in range(gb):
        s = lax.dot_general(u_ref[gi], ws_ref[gi], (((1,), (1,)), ((), ())),
                            preferred_element_type=F32)
        for r in range(4):
            s_ref[:, r * w + gi * 128:r * w + (gi + 1) * 128] = s[:, r * 128:(r + 1) * 128]

    a1f, a2f, a3f = a1_ref[0], a2_ref[0], a3_ref[0]
    a1b, a2b, a3b = a1_ref[1], a2_ref[1], a3_ref[1]

    def step(n, carry):
        hf, gf, hb, gb_ = carry
        m = nc - 1 - n
        h_ref[pl.ds(n, 1), 0:w] = hf
        h_ref[pl.ds(m, 1), w:2 * w] = hb
        sfh = s_ref[pl.ds(n, 1), 0:w]
        sfg = s_ref[pl.ds(n, 1), w:2 * w]
        sbh = s_ref[pl.ds(m, 1), 2 * w:3 * w]
        sbg = s_ref[pl.ds(m, 1), 3 * w:4 * w]
        hf2 = a1f * hf + a2f * gf + sfh
        gf2 = a1f * gf + a3f * hf + sfg
        hb2 = a1b * hb + a2b * gb_ + sbh
        gb2 = a1b * gb_ + a3b * hb + sbg
        return hf2, gf2, hb2, gb2

    z = jnp.zeros((1, w), F32)
    lax.fori_loop(0, nc, step, (z, z, z, z))

    for gi in range(gb):
        u = u_ref[gi]
        hcat = jnp.concatenate([h_ref[:, gi * 128:(gi + 1) * 128],
                                h_ref[:, w + gi * 128:w + (gi + 1) * 128]], axis=1).astype(BF16)
        y = (jnp.dot(u, m_ref[gi], preferred_element_type=F32)
             + jnp.dot(hcat, wc_ref[gi], preferred_element_type=F32)
             + dt_ref[gi] * u.astype(F32))
        yg_ref[gi] = jax.nn.gelu(y)

    for jb in range(S5_T // 8):
        zs = _block_transpose8([yg_ref[g8, :, jb * 128:(jb + 1) * 128] for g8 in range(8)])
        for t8 in range(8):
            y_ref[8 * jb + t8] = zs[t8].astype(BF16)


def _s5_mix(u_steps, m, ws_t, wc, d_tile, a1, a2, a3):
    _, nc, width = u_steps.shape
    G, TL = S5_GROUPS, S5_TL
    gb = S5_GB
    assert gb * S5_GROUP == 128
    w = gb * 128
    steps_spec = pl.BlockSpec((S5_T, nc, 128), lambda i: (0, 0, i))
    a_spec = pl.BlockSpec((2, 1, w), lambda i: (0, 0, i))
    return pl.pallas_call(
        _s5_body,
        grid=(G // gb,),
        in_specs=[
            steps_spec,
            pl.BlockSpec((gb, TL, TL), lambda i: (i, 0, 0)),
            pl.BlockSpec((gb, 512, TL), lambda i: (i, 0, 0)),
            pl.BlockSpec((gb, 256, TL), lambda i: (i, 0, 0)),
            pl.BlockSpec((gb, 1, TL), lambda i: (i, 0, 0)),
            a_spec, a_spec, a_spec,
        ],
        out_specs=steps_spec,
        out_shape=jax.ShapeDtypeStruct((S5_T, nc, width), BF16),
        scratch_shapes=[pltpu.VMEM((nc, 4 * w), F32), pltpu.VMEM((nc, 2 * w), F32),
                        pltpu.VMEM((gb, nc, TL), F32), pltpu.VMEM((gb, nc, TL), BF16)],
        compiler_params=_params(("parallel",)),
        name="s5_mix",
    )(u_steps, m, ws_t, wc, d_tile, a1, a2, a3)


def _row_index(n):
    return lax.broadcasted_iota(jnp.int32, (n, 1), 0).astype(F32)


def _log_decay(lg_ref, d, h):
    return -jnp.abs(jnp.full((1, 1), lg_ref[d, h], F32))


def _ret_bstate_body(lg_ref, k_ref, v_ref, sb_ref, st_ref):
    C = k_ref.shape[0]

    @pl.when(pl.program_id(0) == 0)
    def _():
        st_ref[...] = jnp.zeros_like(st_ref)

    jj = _row_index(C)
    for h in range(RET_HEADS):
        lo = h * RET_HEAD_DIM
        lgb = _log_decay(lg_ref, 1, h)
        sb_ref[h] = st_ref[h].astype(BF16)
        kd = (k_ref[:, lo:lo + RET_HEAD_DIM].astype(F32) * jnp.exp(jj * lgb)).astype(BF16)
        kv = lax.dot_general(kd, v_ref[:, lo:lo + RET_HEAD_DIM], (((0,), (0,)), ((), ())),
                             preferred_element_type=F32)
        st_ref[h] = st_ref[h] * jnp.exp(C * lgb) + kv


def _ret_bstate(lg, h5):
    L = h5.shape[1]
    C = RET_C
    nc = L // C
    return pl.pallas_call(
        _ret_bstate_body,
        grid=(nc,),
        in_specs=[
            pl.BlockSpec(memory_space=pltpu.SMEM),
            pl.BlockSpec((None, C, RET_WIDTH), lambda i: (SEG_K - 1, nc - 1 - i, 0)),
            pl.BlockSpec((None, C, RET_WIDTH), lambda i: (SEG_V - 1, nc - 1 - i, 0)),
        ],
        out_specs=pl.BlockSpec((None, RET_HEADS, RET_HEAD_DIM, RET_HEAD_DIM),
                               lambda i: (nc - 1 - i, 0, 0, 0)),
        out_shape=jax.ShapeDtypeStruct((nc, RET_HEADS, RET_HEAD_DIM, RET_HEAD_DIM), BF16),
        scratch_shapes=[pltpu.VMEM((RET_HEADS, RET_HEAD_DIM, RET_HEAD_DIM), F32)],
        compiler_params=_params(("arbitrary",)),
        name="ret_bstate",
    )(lg, h5, h5)


def _ret_main_body(lg_ref, q_ref, k_ref, v_ref, g_ref, sb_ref, o_ref, st_ref, dec_ref):
    C = q_ref.shape[0]

    @pl.when(pl.program_id(0) == 0)
    def _():
        st_ref[...] = jnp.zeros_like(st_ref)
        diff = (lax.broadcasted_iota(jnp.int32, (C, C), 0)
                - lax.broadcasted_iota(jnp.int32, (C, C), 1)).astype(F32)
        for h in range(RET_HEADS):
            dec_ref[h] = jnp.where(diff >= 0, jnp.exp(_log_decay(lg_ref, 0, h) * jnp.maximum(diff, 0.0)),
                                   jnp.exp(_log_decay(lg_ref, 1, h) * jnp.maximum(-diff, 0.0)))

    ii = _row_index(C)
    for h in range(RET_HEADS):
        lo = h * RET_HEAD_DIM
        lgf = _log_decay(lg_ref, 0, h)
        lgb = _log_decay(lg_ref, 1, h)
        q = q_ref[:, lo:lo + RET_HEAD_DIM]
        k = k_ref[:, lo:lo + RET_HEAD_DIM]
        v = v_ref[:, lo:lo + RET_HEAD_DIM]
        qf = q.astype(F32)
        s = lax.dot_general(q, k, (((1,), (1,)), ((), ())), preferred_element_type=F32)
        o = jnp.dot((s * dec_ref[h]).astype(BF16), v, preferred_element_type=F32)
        qdf = (qf * jnp.exp((ii + 1.0) * lgf)).astype(BF16)
        o = o + jnp.dot(qdf, st_ref[h].astype(BF16), preferred_element_type=F32)
        qdb = (qf * jnp.exp((C - ii) * lgb)).astype(BF16)
        o = o + jnp.dot(qdb, sb_ref[h], preferred_element_type=F32)
        mu = jnp.mean(o, axis=-1, keepdims=True)
        oc = o - mu
        var = jnp.mean(oc * oc, axis=-1, keepdims=True)
        on = oc * lax.rsqrt(var + HEAD_NORM_EPS)
        o_ref[:, lo:lo + RET_HEAD_DIM] = (on * g_ref[:, lo:lo + RET_HEAD_DIM].astype(F32)).astype(BF16)
        kd = (k.astype(F32) * jnp.exp((C - 1.0 - ii) * lgf)).astype(BF16)
        kv = lax.dot_general(kd, v, (((0,), (0,)), ((), ())), preferred_element_type=F32)
        st_ref[h] = st_ref[h] * jnp.exp(C * lgf) + kv


def _ret_main(lg, h5, sb):
    L = h5.shape[1]
    C = RET_C
    nc = L // C
    seg = lambda s: pl.BlockSpec((None, C, RET_WIDTH), lambda i: (s - 1, i, 0))
    return pl.pallas_call(
        _ret_main_body,
        grid=(nc,),
        in_specs=[
            pl.BlockSpec(memory_space=pltpu.SMEM),
            seg(SEG_Q), seg(SEG_K), seg(SEG_V), seg(SEG_GATE),
            pl.BlockSpec((None, RET_HEADS, RET_HEAD_DIM, RET_HEAD_DIM), lambda i: (i, 0, 0, 0)),
        ],
        out_specs=pl.BlockSpec((C, RET_WIDTH), lambda i: (i, 0)),
        out_shape=jax.ShapeDtypeStruct((L, RET_WIDTH), BF16),
        scratch_shapes=[pltpu.VMEM((RET_HEADS, RET_HEAD_DIM, RET_HEAD_DIM), F32),
                        pltpu.VMEM((RET_HEADS, C, C), F32)],
        compiler_params=_params(("arbitrary",)),
        name="ret_main",
    )(lg, h5, h5, h5, h5, sb)


def _outproj_even_body(y_ref, r_ref, x_ref, wg_ref, bg_ref, wo_ref, g_ref, b_ref, o_ref):
    nl = y_ref.shape[1]
    tm = S5_T * nl
    shift = S5_T.bit_length() - 1
    y_steps = y_ref[...].reshape(tm, y_ref.shape[2])
    perm = _row_permutation(tm, lambda r: (r & (S5_T - 1)) * nl + lax.shift_right_logical(r, shift))
    for r in range(0, tm, LN_ROWS):
        rows = slice(r, r + LN_ROWS)
        y = jnp.dot(perm[rows, :], y_steps, preferred_element_type=F32).astype(BF16)
        z = jnp.dot(y, wg_ref[...], preferred_element_type=F32) + bg_ref[...]
        s5 = (y.astype(F32) * jax.nn.sigmoid(z)).astype(BF16)
        mix = (jnp.dot(s5, wo_ref[0:S5_WIDTH, :], preferred_element_type=F32)
               + jnp.dot(r_ref[rows, :], wo_ref[S5_WIDTH:S5_WIDTH + RET_WIDTH, :],
                         preferred_element_type=F32))
        o_ref[rows, :] = _layer_norm_rows(DEEPNORM_ALPHA * x_ref[rows, :] + mix, g_ref[...], b_ref[...])


def _outproj_even(y_steps, ret, x, w_glu_bf, b_glu, w_out_bf, ln_g, ln_b, tm=512):
    L = x.shape[0]
    row = lambda n: pl.BlockSpec((tm, n), lambda i: (i, 0))
    full = lambda a: pl.BlockSpec(a.shape, lambda i: (0,) * a.ndim)
    return pl.pallas_call(
        _outproj_even_body,
        grid=(L // tm,),
        in_specs=[pl.BlockSpec((S5_T, tm // S5_T, S5_WIDTH), lambda i: (0, i, 0)),
                  row(RET_WIDTH), row(D_MODEL), full(w_glu_bf), full(b_glu),
                  full(w_out_bf), full(ln_g), full(ln_b)],
        out_specs=row(D_MODEL),
        out_shape=jax.ShapeDtypeStruct((L, D_MODEL), F32),
        compiler_params=_params(("parallel",)),
        name="outproj_even",
    )(y_steps, ret, x, w_glu_bf, b_glu, w_out_bf, ln_g, ln_b)


def _inproj_odd_body(x_ref, w_ref, ca_ref, sa_ref, cb_ref, sb_ref, w1f_ref, w2f_ref,
                     o_ref, w1b_ref, w2b_ref):
    w1b_ref[...] = w1f_ref[...].astype(BF16)
    w2b_ref[...] = w2f_ref[...].astype(BF16)
    xb = x_ref[...].astype(BF16)
    tn = ATT_KV_HEADS * ATT_HEAD_DIM
    nq = (ATT_HEADS * ATT_HEAD_DIM) // tn
    cc0, sin_blk = _rotary_block(ca_ref, sa_ref, cb_ref, sb_ref, xb.shape[0])
    lane = lax.broadcasted_iota(jnp.int32, sin_blk.shape, 1)
    s10 = jnp.where(lane < ROPE_DIM // 2, -sin_blk, 0.0)
    s20 = jnp.where((lane >= ROPE_DIM // 2) & (lane < ROPE_DIM), sin_blk, 0.0)
    for cb in range(w_ref.shape[1] // tn):
        acc = jnp.dot(xb, w_ref[:, cb * tn:(cb + 1) * tn], preferred_element_type=F32)
        if cb > nq:
            o_ref[:, cb * tn:(cb + 1) * tn] = acc.astype(BF16)
            continue
        scale = ATT_HEAD_DIM ** -0.5 if cb < nq else 1.0
        cc = cc0 * scale
        s1 = s10 * scale
        s2 = s20 * scale
        for hh in range(tn // ATT_HEAD_DIM):
            lo = hh * ATT_HEAD_DIM
            a = acc[:, lo:lo + ATT_HEAD_DIM]
            up = pltpu.roll(a, ATT_HEAD_DIM - ROPE_DIM // 2, axis=1)
            dn = pltpu.roll(a, ROPE_DIM // 2, axis=1)
            o_ref[:, cb * tn + lo:cb * tn + lo + ATT_HEAD_DIM] = (a * cc + up * s1 + dn * s2).astype(BF16)


def _inproj_odd(x, w_in_bf, rot, mlp_w1, mlp_w2, layer, tm=256):
    L = x.shape[0]
    n_out = w_in_bf.shape[1]
    row = lambda n: pl.BlockSpec((tm, n), lambda i: (i, 0))
    cast_in, cast_out, cast_shapes = _mlp_weight_cast_specs(L // tm, mlp_w1, mlp_w2, layer)
    return pl.pallas_call(
        _inproj_odd_body,
        grid=(L // tm,),
        in_specs=[
            row(D_MODEL),
            pl.BlockSpec(w_in_bf.shape, lambda i: (0, 0), pipeline_mode=pl.Buffered(1)),
        ] + [pl.BlockSpec(t.shape, lambda i: (0, 0)) for t in rot] + cast_in,
        out_specs=[row(n_out)] + cast_out,
        out_shape=[jax.ShapeDtypeStruct((L, n_out), BF16)] + cast_shapes,
        compiler_params=_params(("parallel",)),
        name="inproj_odd",
    )(x, w_in_bf, *rot, mlp_w1, mlp_w2)


ATT_QB = 2


def _attn_body(sink_ref, q_ref, *refs):
    k_refs = refs[:ATT_QB + 2]
    v_refs = refs[ATT_QB + 2:2 * ATT_QB + 4]
    o_ref, bias_ref = refs[2 * ATT_QB + 4:]
    step = pl.program_id(0)
    nb = pl.num_programs(0) * ATT_QB
    B = ATT_BLOCK
    hd = ATT_HEAD_DIM
    rows = ATT_GROUP * B

    @pl.when(step == 0)
    def _():
        r_i = lax.broadcasted_iota(jnp.int32, (rows, 3 * B), 0)
        s_i = lax.broadcasted_iota(jnp.int32, (rows, 3 * B), 1)
        rel = (r_i & (B - 1)) - s_i + B
        in_win = jnp.abs(rel) <= ATT_WINDOW
        bias_ref[0] = jnp.where(in_win & (s_i >= B), 0.0, NEG_INF)
        bias_ref[1] = jnp.where(in_win, 0.0, NEG_INF)
        bias_ref[2] = jnp.where(in_win & (s_i < 2 * B), 0.0, NEG_INF)

    head_of_row = lax.shift_right_logical(lax.broadcasted_iota(jnp.int32, (rows, 1), 0),
                                          int(math.log2(B)))
    for qb in range(ATT_QB):
        c = step * ATT_QB + qb
        bias = bias_ref[jnp.where(c == 0, 0, jnp.where(c == nb - 1, 2, 1))]
        for g in range(ATT_KV_HEADS):
            q = jnp.concatenate([q_ref[qb * B:(qb + 1) * B, (g * ATT_GROUP + hh) * hd:(g * ATT_GROUP + hh + 1) * hd]
                                 for hh in range(ATT_GROUP)], axis=0)
            ksl = slice(g * hd, (g + 1) * hd)
            k = jnp.concatenate([r[:, ksl] for r in k_refs[qb:qb + 3]], axis=0)
            v = jnp.concatenate([r[:, ksl] for r in v_refs[qb:qb + 3]], axis=0)
            s = lax.dot_general(q, k, (((1,), (1,)), ((), ())), preferred_element_type=F32) + bias
            sink = jnp.zeros((rows, 1), F32)
            for hh in range(ATT_GROUP):
                sink = jnp.where(head_of_row == hh, sink_ref[g * ATT_GROUP + hh], sink)
            m = jnp.maximum(jnp.max(s, axis=-1, keepdims=True), sink)
            p = jnp.exp(s - m)
            den = jnp.sum(p, axis=-1, keepdims=True) + jnp.exp(sink - m)
            o = jnp.dot(p.astype(BF16), v, preferred_element_type=F32) / den
            for hh in range(ATT_GROUP):
                hcol = (g * ATT_GROUP + hh) * hd
                o_ref[qb * B:(qb + 1) * B, hcol:hcol + hd] = o[hh * B:(hh + 1) * B, :].astype(BF16)


def _attention(sink, qkv):
    L = qkv.shape[0]
    B = ATT_BLOCK
    nb = L // B
    assert nb % ATT_QB == 0 and nb >= 2
    kvw = ATT_KV_HEADS * ATT_HEAD_DIM
    qw = ATT_HEADS * ATT_HEAD_DIM
    kcol = qw // kvw
    vcol = kcol + 1

    def kv_spec(col, off):
        return pl.BlockSpec((B, kvw), lambda i: (jnp.clip(i * ATT_QB + off, 0, nb - 1), col))

    offs = range(-1, ATT_QB + 1)
    return pl.pallas_call(
        _attn_body,
        grid=(nb // ATT_QB,),
        in_specs=[
            pl.BlockSpec(memory_space=pltpu.SMEM),
            pl.BlockSpec((ATT_QB * B, qw), lambda i: (i, 0)),
        ] + [kv_spec(kcol, o) for o in offs] + [kv_spec(vcol, o) for o in offs],
        out_specs=pl.BlockSpec((ATT_QB * B, qw), lambda i: (i, 0)),
        out_shape=jax.ShapeDtypeStruct((L, qw), BF16),
        scratch_shapes=[pltpu.VMEM((3, ATT_GROUP * B, 3 * B), F32)],
        compiler_params=_params(("arbitrary",)),
        name="attention",
    )(sink, qkv, *([qkv] * (2 * ATT_QB + 4)))


def _outproj_odd_body(a_ref, x_ref, wo_ref, g_ref, b_ref, o_ref):
    for r in range(0, a_ref.shape[0], LN_ROWS):
        rows = slice(r, r + LN_ROWS)
        mix = jnp.dot(a_ref[rows, :], wo_ref[...], preferred_element_type=F32)
        o_ref[rows, :] = _layer_norm_rows(DEEPNORM_ALPHA * x_ref[rows, :] + mix, g_ref[...], b_ref[...])


def _outproj_odd(a, x, w_out_bf, ln_g, ln_b, tm=512):
    L = x.shape[0]
    row = lambda n: pl.BlockSpec((tm, n), lambda i: (i, 0))
    full = lambda t: pl.BlockSpec(t.shape, lambda i: (0,) * t.ndim)
    return pl.pallas_call(
        _outproj_odd_body,
        grid=(L // tm,),
        in_specs=[row(a.shape[1]), row(D_MODEL), full(w_out_bf), full(ln_g), full(ln_b)],
        out_specs=row(D_MODEL),
        out_shape=jax.ShapeDtypeStruct((L, D_MODEL), F32),
        compiler_params=_params(("parallel",)),
        name="outproj_odd",
    )(a, x, w_out_bf, ln_g, ln_b)


def _mlp_body(x_ref, w1_ref, w2_ref, g_ref, b_ref, o_ref, xb_ref, acc_ref):
    f = pl.program_id(1)

    @pl.when(f == 0)
    def _():
        xb_ref[...] = x_ref[...].astype(BF16)
        acc_ref[...] = jnp.zeros_like(acc_ref)

    def partial_sum(rows):
        h = jnp.dot(xb_ref[rows, :], w1_ref[...], preferred_element_type=F32)
        h = jnp.square(jnp.maximum(h, 0.0)).astype(BF16)
        return jnp.dot(h, w2_ref[...], preferred_element_type=F32)

    last = pl.num_programs(1) - 1

    @pl.when(f < last)
    def _():
        acc_ref[...] += partial_sum(slice(None))

    @pl.when(f == last)
    def _():
        for r in range(0, x_ref.shape[0], LN_ROWS):
            rows = slice(r, r + LN_ROWS)
            y = DEEPNORM_ALPHA * x_ref[rows, :] + (acc_ref[rows, :] + partial_sum(rows))
            o_ref[rows, :] = _layer_norm_rows(y, g_ref[...], b_ref[...])


def _mlp(x, w1, w2, ln_g, ln_b, tm=512, tf=1024):
    L = x.shape[0]
    return pl.pallas_call(
        _mlp_body,
        grid=(L // tm, D_FF // tf),
        in_specs=[
            pl.BlockSpec((tm, D_MODEL), lambda i, f: (i, 0)),
            pl.BlockSpec((D_MODEL, tf), lambda i, f: (0, f)),
            pl.BlockSpec((tf, D_MODEL), lambda i, f: (f, 0)),
            pl.BlockSpec((1, D_MODEL), lambda i, f: (0, 0)),
            pl.BlockSpec((1, D_MODEL), lambda i, f: (0, 0)),
        ],
        out_specs=pl.BlockSpec((tm, D_MODEL), lambda i, f: (i, 0)),
        out_shape=jax.ShapeDtypeStruct((L, D_MODEL), F32),
        scratch_shapes=[pltpu.VMEM((tm, D_MODEL), BF16), pltpu.VMEM((tm, D_MODEL), F32)],
        compiler_params=_params(("parallel", "arbitrary")),
        name="mlp",
    )(x, w1, w2, ln_g, ln_b)


def _rotary_tables(L, rot_dim, theta):
    half = rot_dim // 2
    inv_freq = 1.0 / (theta ** (jnp.arange(half, dtype=F32) / half))
    ang_a = (jnp.arange(L // ROT_LO) * ROT_LO).astype(F32)[:, None] * inv_freq[None, :]
    ang_b = jnp.arange(ROT_LO).astype(F32)[:, None] * inv_freq[None, :]
    return jnp.cos(ang_a), jnp.sin(ang_a), jnp.cos(ang_b), jnp.sin(ang_b)


def _even_layer(x, w_in, w_out, lam_re, lam_im, log_step, b_re, b_im, c_re, c_im,
                d_skip, w_glu, b_glu, ret_log_decay, ln_g, ln_b, mlp_w1, mlp_w2, layer, later_weights):
    L = x.shape[0]
    rot = _rotary_tables(L, RET_HEAD_DIM, RET_ROPE_THETA)
    u_steps, h5, w1_bf, w2_bf = _inproj_even(x, w_in.astype(BF16), rot, mlp_w1, mlp_w2, layer)
    disc = _s5_disc(lam_re, lam_im, log_step)
    m, ws_t, wc, (w_glu_bf, w_out_bf, *later_bf) = _s5_gen(disc, b_re, b_im, c_re, c_im,
                                                           (w_glu, w_out) + tuple(later_weights))
    d_tile = jnp.tile(d_skip.astype(F32), (1, S5_T)).reshape(S5_GROUPS, 1, S5_TL)
    a1, a2, a3 = (disc[k].reshape(2, 1, S5_GROUPS * 128) for k in (DISC_AT_RE, DISC_SCAN_A2, DISC_SCAN_A3))
    y = _s5_mix(u_steps, m, ws_t, wc, d_tile, a1, a2, a3)
    lg = ret_log_decay.astype(F32)
    sb = _ret_bstate(lg, h5)
    ret = _ret_main(lg, h5, sb)
    x1 = _outproj_even(y, ret, x, w_glu_bf, b_glu.astype(F32).reshape(1, -1),
                       w_out_bf, ln_g.reshape(1, -1), ln_b.reshape(1, -1))
    return x1, w1_bf, w2_bf, later_bf


def _odd_layer(x, w_in_bf, w_out_bf, sink, ln_g, ln_b, mlp_w1, mlp_w2, layer):
    L = x.shape[0]
    pad = ATT_HEAD_DIM - ROPE_DIM
    widen = lambda t, fill: jnp.concatenate([t, t, jnp.full((t.shape[0], pad), fill, F32)], axis=1)
    ca, sa, cb, sb = _rotary_tables(L, ROPE_DIM, ROPE_THETA)
    rot = (widen(ca, 1.0), widen(sa, 0.0), widen(cb, 1.0), widen(sb, 0.0))
    qkv, w1_bf, w2_bf = _inproj_odd(x, w_in_bf, rot, mlp_w1, mlp_w2, layer)
    att = _attention(sink.astype(F32), qkv)
    x1 = _outproj_odd(att, x, w_out_bf, ln_g.reshape(1, -1), ln_b.reshape(1, -1))
    return x1, w1_bf, w2_bf


def kernel(x, ln_g, ln_b, mlp_w1, mlp_w2, even_w_in, even_w_out, s5_lambda_re, s5_lambda_im, s5_log_step, s5_b_re, s5_b_im, s5_c_re, s5_c_im, s5_d, s5_w_glu, s5_b_glu, ret_log_decay, odd_w_in, odd_w_out, attn_sink):
    bsz = x.shape[0]
    outs = []
    for b in range(bsz):
        xb = x[b]
        for layer in range(DEPTH):
            if layer % 2 == 0:
                e = layer // 2
                later = (odd_w_in[e], odd_w_out[e]) if layer + 1 < DEPTH else ()
                xb, w1_bf, w2_bf, later_bf = _even_layer(
                    xb, even_w_in[e], even_w_out[e], s5_lambda_re[e], s5_lambda_im[e],
                    s5_log_step[e], s5_b_re[e], s5_b_im[e], s5_c_re[e], s5_c_im[e],
                    s5_d[e], s5_w_glu[e], s5_b_glu[e], ret_log_decay[e],
                    ln_g[layer, 0], ln_b[layer, 0], mlp_w1, mlp_w2, layer, later)
            else:
                o = layer // 2
                xb, w1_bf, w2_bf = _odd_layer(xb, later_bf[0], later_bf[1], attn_sink[o],
                                              ln_g[layer, 0], ln_b[layer, 0], mlp_w1, mlp_w2, layer)
            xb = _mlp(xb, w1_bf, w2_bf,
                      ln_g[layer, 1].reshape(1, -1), ln_b[layer, 1].reshape(1, -1))
        outs.append(xb)
    return jnp.stack(outs, axis=0)
```

```python
import functools
import math

import jax
import jax.numpy as jnp
from jax import lax
from jax.experimental import pallas as pl
from jax.experimental.pallas import tpu as pltpu

F32 = jnp.float32
BF16 = jnp.bfloat16

D_MODEL = 2048
DEPTH = 2
S5_WIDTH = 1024
S5_GROUP = 16
S5_GROUPS = 64
S5_STATE = 64
RET_WIDTH = 1024
RET_HEADS = 4
RET_HEAD_DIM = 256
RET_ROPE_THETA = 10000.0
ATT_HEADS = 16
ATT_KV_HEADS = 4
ATT_HEAD_DIM = 128
ATT_GROUP = 4
ATT_WINDOW = 128
ATT_BLOCK = 128
ROPE_THETA = 500000.0
ROPE_DIM = 32
D_FF = 4 * D_MODEL
DEEPNORM_ALPHA = (2 * DEPTH) ** 0.25
LN_EPS = 1e-5
HEAD_NORM_EPS = 1e-6
NEG_INF = -1e30
LOG2_E = math.log2(math.e)

V7X_VMEM_BYTES = 64 * 1024 * 1024
VMEM_LIMIT = V7X_VMEM_BYTES - 8 * 1024 * 1024

S5_T = 32
S5_TL = S5_T * S5_GROUP
S5_GB = 8
S5_GEN_GB = 8
RET_C = 256
LN_ROWS = 256


def _params(sem):
    return pltpu.CompilerParams(dimension_semantics=sem, vmem_limit_bytes=VMEM_LIMIT)


def _layer_norm_rows(y, g, b):
    mu = jnp.mean(y, axis=-1, keepdims=True)
    yc = y - mu
    var = jnp.mean(yc * yc, axis=-1, keepdims=True)
    return yc * lax.rsqrt(var + LN_EPS) * g + b


def _block_transpose8(xs):
    blk = lax.shift_right_logical(lax.broadcasted_iota(jnp.int32, xs[0].shape, 1), 4)
    xs = list(xs)
    for k in range(3):
        d = 1 << k
        upper = (blk & d) != 0
        for i in range(8):
            if i & d:
                continue
            a, b = xs[i], xs[i + d]
            xs[i] = jnp.where(upper, pltpu.roll(b, S5_GROUP * d, axis=1), a)
            xs[i + d] = jnp.where(upper, b, pltpu.roll(a, 128 - S5_GROUP * d, axis=1))
    return xs


ROT_LO = 64


def _rotary_block(ca_ref, sa_ref, cb_ref, sb_ref, tm):
    cb = cb_ref[...]
    sb = sb_ref[...]
    n = tm // ROT_LO
    cos, sin = [], []
    for al in range(n):
        row = pl.ds(pl.program_id(0) * n + al, 1)
        ca = ca_ref[row, :]
        sa = sa_ref[row, :]
        cos.append(ca * cb - sa * sb)
        sin.append(sa * cb + ca * sb)
    return jnp.concatenate(cos, axis=0), jnp.concatenate(sin, axis=0)


SEG_U, SEG_Q, SEG_K, SEG_V, SEG_GATE = range(5)


def _row_permutation(n_rows, src_of_row):
    r = lax.broadcasted_iota(jnp.int32, (n_rows, n_rows), 0)
    c = lax.broadcasted_iota(jnp.int32, (n_rows, n_rows), 1)
    return jnp.where(c == src_of_row(r), 1.0, 0.0).astype(BF16)


def _mlp_weight_cast_specs(n_steps, mlp_w1, mlp_w2, layer):
    ins, outs, shapes = [], [], []
    for w in (mlp_w1, mlp_w2):
        rows, cols = w.shape[1] // n_steps, w.shape[2]
        ins.append(pl.BlockSpec((None, rows, cols), lambda i: (layer, i, 0)))
        outs.append(pl.BlockSpec((rows, cols), lambda i: (i, 0)))
        shapes.append(jax.ShapeDtypeStruct(w.shape[1:], BF16))
    return ins, outs, shapes


def _inproj_even_body(x_ref, w_ref, ca_ref, sa_ref, cb_ref, sb_ref, w1f_ref, w2f_ref,
                      u_ref, o_ref, w1b_ref, w2b_ref):
    w1b_ref[...] = w1f_ref[...].astype(BF16)
    w2b_ref[...] = w2f_ref[...].astype(BF16)
    xb = x_ref[...].astype(BF16)
    tm = xb.shape[0]
    tn = S5_WIDTH
    half = RET_HEAD_DIM // 2
    cos_blk, sin_blk = _rotary_block(ca_ref, sa_ref, cb_ref, sb_ref, tm)
    for seg in range(w_ref.shape[1] // tn):
        acc = jnp.dot(xb, w_ref[:, seg * tn:(seg + 1) * tn], preferred_element_type=F32)
        if seg == SEG_U:
            nl = tm // S5_T
            shift = nl.bit_length() - 1
            perm = _row_permutation(tm, lambda r: (r & (nl - 1)) * S5_T + lax.shift_right_logical(r, shift))
            up = jnp.dot(perm, acc.astype(BF16), preferred_element_type=F32)
            for t in range(S5_T):
                u_ref[t] = up[t * nl:(t + 1) * nl, :]
        elif seg in (SEG_Q, SEG_K):
            scale = RET_HEAD_DIM ** -0.5 if seg == SEG_K else 1.0
            cos = cos_blk * scale
            sin = sin_blk * scale
            for hh in range(RET_HEADS):
                lo = hh * RET_HEAD_DIM
                a = acc[:, lo:lo + half]
                b = acc[:, lo + half:lo + RET_HEAD_DIM]
                o_ref[seg - 1, :, lo:lo + half] = (a * cos - b * sin).astype(BF16)
                o_ref[seg - 1, :, lo + half:lo + RET_HEAD_DIM] = (b * cos + a * sin).astype(BF16)
        elif seg == SEG_GATE:
            o_ref[seg - 1] = (acc * jax.nn.sigmoid(acc)).astype(BF16)
        else:
            o_ref[seg - 1] = acc.astype(BF16)


def _inproj_even(x, w_in_bf, rot, mlp_w1, mlp_w2, layer, tm=256):
    L = x.shape[0]
    tn = S5_WIDTH
    nseg = w_in_bf.shape[1] // tn
    row = lambda n: pl.BlockSpec((tm, n), lambda i: (i, 0))
    cast_in, cast_out, cast_shapes = _mlp_weight_cast_specs(L // tm, mlp_w1, mlp_w2, layer)
    return pl.pallas_call(
        _inproj_even_body,
        grid=(L // tm,),
        in_specs=[
            row(D_MODEL),
            pl.BlockSpec(w_in_bf.shape, lambda i: (0, 0), pipeline_mode=pl.Buffered(1)),
        ] + [pl.BlockSpec(t.shape, lambda i: (0, 0)) for t in rot] + cast_in,
        out_specs=[pl.BlockSpec((S5_T, tm // S5_T, tn), lambda i: (0, i, 0)),
                   pl.BlockSpec((nseg - 1, tm, tn), lambda i: (0, i, 0))] + cast_out,
        out_shape=[jax.ShapeDtypeStruct((S5_T, L // S5_T, tn), F32),
                   jax.ShapeDtypeStruct((nseg - 1, L, tn), BF16)] + cast_shapes,
        compiler_params=_params(("parallel",)),
        name="inproj_even",
    )(x, w_in_bf, *rot, mlp_w1, mlp_w2)


def _cmul(ar, ai, br, bi):
    return ar * br - ai * bi, ar * bi + ai * br


(DISC_A1_RE, DISC_A1_IM, DISC_A2_RE, DISC_A2_IM, DISC_A4_RE, DISC_A4_IM, DISC_A8_RE, DISC_A8_IM,
 DISC_A16_RE, DISC_A16_IM, DISC_AT_RE, DISC_AT_IM, DISC_Z_RE, DISC_Z_IM, DISC_SCAN_A2, DISC_SCAN_A3) = range(16)


def _s5_disc_body(lr_ref, li_ref, ls_ref, sg_ref, o_ref):
    lr = jnp.minimum(lr_ref[...], -1e-4)
    li = li_ref[...]
    step = jnp.exp(ls_ref[...])
    mag = jnp.exp(lr * step)
    ar = mag * jnp.cos(li * step)
    ai = mag * jnp.sin(li * step)
    nr, ni = ar - 1.0, ai
    den = lr * lr + li * li
    o_ref[DISC_Z_RE] = (nr * lr + ni * li) / den
    o_ref[DISC_Z_IM] = (ni * lr - nr * li) / den
    pr, pi = ar, ai
    for k in range(6):
        o_ref[2 * k] = pr
        o_ref[2 * k + 1] = pi
        if k < 5:
            pr, pi = _cmul(pr, pi, pr, pi)
    o_ref[DISC_SCAN_A2] = pi * sg_ref[...]
    o_ref[DISC_SCAN_A3] = -pi * sg_ref[...]


def _s5_disc(lam_re, lam_im, log_step):
    assert S5_T == 32
    rows = 2 * S5_GROUPS
    two = lambda a: jnp.tile(a.astype(F32).reshape(rows, -1), (1, 2))
    lr = two(lam_re)
    li = two(lam_im)
    ls = jnp.broadcast_to(log_step.astype(F32).reshape(rows, 1), (rows, 2 * S5_STATE))
    sg = jnp.broadcast_to(jnp.concatenate([-jnp.ones((S5_STATE,), F32), jnp.ones((S5_STATE,), F32)])[None],
                          (rows, 2 * S5_STATE))
    return pl.pallas_call(
        _s5_disc_body,
        out_shape=jax.ShapeDtypeStruct((16, rows, 2 * S5_STATE), F32),
        name="s5_disc",
    )(lr, li, ls, sg)


def _s5_gen_body(n_round, col_ref, bre_ref, bim_ref, cre_ref, cim_ref, *refs):
    m_ref, ws_ref, wc_ref = refs[n_round:n_round + 3]
    for src, dst in zip(refs[:n_round], refs[n_round + 3:]):
        dst[...] = src[...].astype(BF16)
    P = S5_STATE
    hi = lax.Precision.HIGHEST
    tlo = lax.shift_right_logical(lax.broadcasted_iota(jnp.int32, (P, 128), 1), 4)
    lane = lax.broadcasted_iota(jnp.int32, (S5_GROUP, S5_TL), 1)
    ones = jnp.ones((P, 128), F32)
    zeros = jnp.zeros((P, 128), F32)
    spread = jnp.where((lax.broadcasted_iota(jnp.int32, (S5_GROUP, 128), 1) & (S5_GROUP - 1))
                       == lax.broadcasted_iota(jnp.int32, (S5_GROUP, 128), 0), 1.0, 0.0)
    contract0 = (((0,), (0,)), ((), ()))

    def one_group(gi, carry):
        kt = []
        for d in range(2):
            col = col_ref[gi, d]
            c = lambda k: jnp.broadcast_to(col[:, k:k + 1], (P, 128))
            a1 = (c(DISC_A1_RE), c(DISC_A1_IM))
            a2 = (c(DISC_A2_RE), c(DISC_A2_IM))
            a4 = (c(DISC_A4_RE), c(DISC_A4_IM))
            a8 = (c(DISC_A8_RE), c(DISC_A8_IM))
            a16 = (c(DISC_A16_RE), c(DISC_A16_IM))
            blk = [None, a8, a16, _cmul(*a8, *a16)]

            def low_powers(reverse):
                xr, xi = ones, zeros
                for k, ak in enumerate((a1, a2, a4)):
                    bit = (lax.shift_right_logical(tlo, k) & 1) == (0 if reverse else 1)
                    yr, yi = _cmul(xr, xi, *ak)
                    xr = jnp.where(bit, yr, xr)
                    xi = jnp.where(bit, yi, xi)
                return xr, xi

            def expand(base, reverse):
                out = []
                for j in range(4):
                    f = blk[3 - j] if reverse else blk[j]
                    out.append(base if f is None else _cmul(*base, *f))
                return out

            ct = tuple(lax.dot_general(r[d, gi], spread, contract0, precision=hi, preferred_element_type=F32)
                       for r in (cre_ref, cim_ref))
            bt = tuple(jnp.dot(r[d, gi], spread, precision=hi, preferred_element_type=F32)
                       for r in (bre_ref, bim_ref))
            bbar = _cmul(c(DISC_Z_RE), c(DISC_Z_IM), *bt)
            ca = expand(_cmul(*ct, *low_powers(d == 1)), d == 1)
            ba = expand(_cmul(*bbar, *low_powers(d == 0)), d == 0)
            wcj = [_cmul(*x, *a1) for x in ca]
            cat = lambda parts, k: jnp.concatenate([x[k] for x in parts], axis=1)
            kt.append(lax.dot_general(bbar[0][:, :S5_GROUP], cat(ca, 0), contract0, precision=hi,
                                      preferred_element_type=F32)
                      - lax.dot_general(bbar[1][:, :S5_GROUP], cat(ca, 1), contract0, precision=hi,
                                        preferred_element_type=F32))
            ba_r, ba_i = cat(ba, 0).astype(BF16), cat(ba, 1).astype(BF16)
            for r, part in enumerate((ba_r, ba_i, ba_i, ba_r)):
                ws_ref[gi, (4 * d + r) * P:(4 * d + r + 1) * P, :] = part
            wc_ref[gi, 2 * d * P:(2 * d + 1) * P, :] = cat(wcj, 0).astype(BF16)
            wc_ref[gi, (2 * d + 1) * P:(2 * d + 2) * P, :] = (-cat(wcj, 1)).astype(BF16)
        ktf, ktb = kt
        for s in range(S5_T):
            lo, hi_lane = S5_GROUP * s, S5_GROUP * (s + 1)
            f = ktf if s == 0 else jnp.where(lane >= lo, pltpu.roll(ktf, lo, axis=1), 0.0)
            b = ktb if s == S5_T - 1 else jnp.where(lane < hi_lane, pltpu.roll(ktb, hi_lane, axis=1), 0.0)
            m_ref[gi, lo:hi_lane, :] = (f + b).astype(BF16)
        return carry

    lax.fori_loop(0, col_ref.shape[0], one_group, 0, unroll=4)


def _s5_gen(disc, b_re, b_im, c_re, c_im, to_round):
    G, P, Cg = S5_GROUPS, S5_STATE, S5_GROUP
    gb = S5_GEN_GB
    steps = G // gb
    col = disc[:, :, :P].reshape(16, 2, G, P).transpose(2, 1, 3, 0)
    per_dir = lambda a: pl.BlockSpec((2, gb) + a.shape[2:], lambda i: (0, i, 0, 0))
    out = lambda rows: pl.BlockSpec((gb, rows, S5_TL), lambda i: (i, 0, 0))
    row_slice = lambda w: pl.BlockSpec((w.shape[0] // steps, w.shape[1]), lambda i: (i, 0))
    res = pl.pallas_call(
        functools.partial(_s5_gen_body, len(to_round)),
        grid=(steps,),
        in_specs=[pl.BlockSpec((gb,) + col.shape[1:], lambda i: (i, 0, 0, 0)),
                  per_dir(b_re), per_dir(b_im), per_dir(c_re), per_dir(c_im)]
                 + [row_slice(w) for w in to_round],
        out_specs=[out(S5_TL), out(8 * P), out(4 * P)] + [row_slice(w) for w in to_round],
        out_shape=[jax.ShapeDtypeStruct((G, S5_TL, S5_TL), BF16),
                   jax.ShapeDtypeStruct((G, 8 * P, S5_TL), BF16),
                   jax.ShapeDtypeStruct((G, 4 * P, S5_TL), BF16)]
                  + [jax.ShapeDtypeStruct(w.shape, BF16) for w in to_round],
        compiler_params=_params(("parallel",)),
        name="s5_gen",
    )(col, b_re.astype(F32), b_im.astype(F32), c_re.astype(F32), c_im.astype(F32), *to_round)
    return res[0], res[1], res[2], tuple(res[3:])


def _s5_body(ut_ref, m_ref, ws_ref, wc_ref, dt_ref, a1_ref, a2_ref, a3_ref, y_ref,
             s_ref, h_ref, yg_ref, u_ref):
    nc = u_ref.shape[1]
    gb = u_ref.shape[0]
    w = gb * 128
    for jb in range(S5_T // 8):
        ys = _block_transpose8([ut_ref[8 * jb + t8] for t8 in range(8)])
        for g8 in range(8):
            u_ref[g8, :, jb * 128:(jb + 1) * 128] = ys[g8].astype(BF16)
    for gi in range(gb):
        s = lax.dot_general(u_ref[gi], ws_ref[gi], (((1,), (1,)), ((), ())),
                            preferred_element_type=F32)
        for r in range(4):
            s_ref[:, r * w + gi * 128:r * w + (gi + 1) * 128] = s[:, r * 128:(r + 1) * 128]

    a1f, a2f, a3f = a1_ref[0], a2_ref[0], a3_ref[0]
    a1b, a2b, a3b = a1_ref[1], a2_ref[1], a3_ref[1]

    def step(n, carry):
        hf, gf, hb, gb_ = carry
        m = nc - 1 - n
        h_ref[pl.ds(n, 1), 0:w] = hf
        h_ref[pl.ds(m, 1), w:2 * w] = hb
        sfh = s_ref[pl.ds(n, 1), 0:w]
        sfg = s_ref[pl.ds(n, 1), w:2 * w]
        sbh = s_ref[pl.ds(m, 1), 2 * w:3 * w]
        sbg = s_ref[pl.ds(m, 1), 3 * w:4 * w]
        hf2 = a1f * hf + a2f * gf + sfh
        gf2 = a1f * gf + a3f * hf + sfg
        hb2 = a1b * hb + a2b * gb_ + sbh
        gb2 = a1b * gb_ + a3b * hb + sbg
        return hf2, gf2, hb2, gb2

    z = jnp.zeros((1, w), F32)
    lax.fori_loop(0, nc, step, (z, z, z, z), unroll=4)

    for gi in range(gb):
        u = u_ref[gi]
        hcat = jnp.concatenate([h_ref[:, gi * 128:(gi + 1) * 128],
                                h_ref[:, w + gi * 128:w + (gi + 1) * 128]], axis=1).astype(BF16)
        y = (jnp.dot(u, m_ref[gi], preferred_element_type=F32)
             + jnp.dot(hcat, wc_ref[gi], preferred_element_type=F32)
             + dt_ref[gi] * u.astype(F32))
        yg_ref[gi] = jax.nn.gelu(y)

    for jb in range(S5_T // 8):
        zs = _block_transpose8([yg_ref[g8, :, jb * 128:(jb + 1) * 128] for g8 in range(8)])
        for t8 in range(8):
            y_ref[8 * jb + t8] = zs[t8].astype(BF16)


def _s5_mix(u_steps, m, ws_t, wc, d_tile, a1, a2, a3):
    _, nc, width = u_steps.shape
    G, TL = S5_GROUPS, S5_TL
    gb = S5_GB
    assert gb * S5_GROUP == 128
    w = gb * 128
    steps_spec = pl.BlockSpec((S5_T, nc, 128), lambda i: (0, 0, i))
    a_spec = pl.BlockSpec((2, 1, w), lambda i: (0, 0, i))
    return pl.pallas_call(
        _s5_body,
        grid=(G // gb,),
        in_specs=[
            steps_spec,
            pl.BlockSpec((gb, TL, TL), lambda i: (i, 0, 0)),
            pl.BlockSpec((gb, 512, TL), lambda i: (i, 0, 0)),
            pl.BlockSpec((gb, 256, TL), lambda i: (i, 0, 0)),
            pl.BlockSpec((gb, 1, TL), lambda i: (i, 0, 0)),
            a_spec, a_spec, a_spec,
        ],
        out_specs=steps_spec,
        out_shape=jax.ShapeDtypeStruct((S5_T, nc, width), BF16),
        scratch_shapes=[pltpu.VMEM((nc, 4 * w), F32), pltpu.VMEM((nc, 2 * w), F32),
                        pltpu.VMEM((gb, nc, TL), F32), pltpu.VMEM((gb, nc, TL), BF16)],
        compiler_params=_params(("parallel",)),
        name="s5_mix",
    )(u_steps, m, ws_t, wc, d_tile, a1, a2, a3)


def _row_index(n):
    return lax.broadcasted_iota(jnp.int32, (n, 1), 0).astype(F32)


def _log_decay(lg_ref, d, h):
    return -jnp.abs(jnp.full((1, 1), lg_ref[d, h], F32))


def _ret_bstate_body(lg_ref, k_ref, v_ref, sb_ref, st_ref):
    C = k_ref.shape[0]

    @pl.when(pl.program_id(0) == 0)
    def _():
        st_ref[...] = jnp.zeros_like(st_ref)

    jj = _row_index(C)
    for h in range(RET_HEADS):
        lo = h * RET_HEAD_DIM
        lgb = _log_decay(lg_ref, 1, h)
        sb_ref[h] = st_ref[h].astype(BF16)
        kd = (k_ref[:, lo:lo + RET_HEAD_DIM].astype(F32) * jnp.exp(jj * lgb)).astype(BF16)
        kv = lax.dot_general(kd, v_ref[:, lo:lo + RET_HEAD_DIM], (((0,), (0,)), ((), ())),
                             preferred_element_type=F32)
        st_ref[h] = st_ref[h] * jnp.exp(C * lgb) + kv


def _ret_bstate(lg, h5):
    L = h5.shape[1]
    C = RET_C
    nc = L // C
    return pl.pallas_call(
        _ret_bstate_body,
        grid=(nc,),
        in_specs=[
            pl.BlockSpec(memory_space=pltpu.SMEM),
            pl.BlockSpec((None, C, RET_WIDTH), lambda i: (SEG_K - 1, nc - 1 - i, 0)),
            pl.BlockSpec((None, C, RET_WIDTH), lambda i: (SEG_V - 1, nc - 1 - i, 0)),
        ],
        out_specs=pl.BlockSpec((None, RET_HEADS, RET_HEAD_DIM, RET_HEAD_DIM),
                               lambda i: (nc - 1 - i, 0, 0, 0)),
        out_shape=jax.ShapeDtypeStruct((nc, RET_HEADS, RET_HEAD_DIM, RET_HEAD_DIM), BF16),
        scratch_shapes=[pltpu.VMEM((RET_HEADS, RET_HEAD_DIM, RET_HEAD_DIM), F32)],
        compiler_params=_params(("arbitrary",)),
        name="ret_bstate",
    )(lg, h5, h5)


def _ret_main_body(lg_ref, q_ref, k_ref, v_ref, g_ref, sb_ref, o_ref, st_ref, dec_ref):
    C = q_ref.shape[0]

    @pl.when(pl.program_id(0) == 0)
    def _():
        st_ref[...] = jnp.zeros_like(st_ref)
        diff = (lax.broadcasted_iota(jnp.int32, (C, C), 0)
                - lax.broadcasted_iota(jnp.int32, (C, C), 1)).astype(F32)
        for h in range(RET_HEADS):
            dec_ref[h] = jnp.where(diff >= 0, jnp.exp(_log_decay(lg_ref, 0, h) * jnp.maximum(diff, 0.0)),
                                   jnp.exp(_log_decay(lg_ref, 1, h) * jnp.maximum(-diff, 0.0)))

    ii = _row_index(C)
    for h in range(RET_HEADS):
        lo = h * RET_HEAD_DIM
        lgf = _log_decay(lg_ref, 0, h)
        lgb = _log_decay(lg_ref, 1, h)
        q = q_ref[:, lo:lo + RET_HEAD_DIM]
        k = k_ref[:, lo:lo + RET_HEAD_DIM]
        v = v_ref[:, lo:lo + RET_HEAD_DIM]
        qf = q.astype(F32)
        s = lax.dot_general(q, k, (((1,), (1,)), ((), ())), preferred_element_type=F32)
        o = jnp.dot((s * dec_ref[h]).astype(BF16), v, preferred_element_type=F32)
        qdf = (qf * jnp.exp((ii + 1.0) * lgf)).astype(BF16)
        o = o + jnp.dot(qdf, st_ref[h].astype(BF16), preferred_element_type=F32)
        qdb = (qf * jnp.exp((C - ii) * lgb)).astype(BF16)
        o = o + jnp.dot(qdb, sb_ref[h], preferred_element_type=F32)
        mu = jnp.mean(o, axis=-1, keepdims=True)
        oc = o - mu
        var = jnp.mean(oc * oc, axis=-1, keepdims=True)
        on = oc * lax.rsqrt(var + HEAD_NORM_EPS)
        o_ref[:, lo:lo + RET_HEAD_DIM] = (on * g_ref[:, lo:lo + RET_HEAD_DIM].astype(F32)).astype(BF16)
        kd = (k.astype(F32) * jnp.exp((C - 1.0 - ii) * lgf)).astype(BF16)
        kv = lax.dot_general(kd, v, (((0,), (0,)), ((), ())), preferred_element_type=F32)
        st_ref[h] = st_ref[h] * jnp.exp(C * lgf) + kv


def _ret_main(lg, h5, sb):
    L = h5.shape[1]
    C = RET_C
    nc = L // C
    seg = lambda s: pl.BlockSpec((None, C, RET_WIDTH), lambda i: (s - 1, i, 0))
    return pl.pallas_call(
        _ret_main_body,
        grid=(nc,),
        in_specs=[
            pl.BlockSpec(memory_space=pltpu.SMEM),
            seg(SEG_Q), seg(SEG_K), seg(SEG_V), seg(SEG_GATE),
            pl.BlockSpec((None, RET_HEADS, RET_HEAD_DIM, RET_HEAD_DIM), lambda i: (i, 0, 0, 0)),
        ],
        out_specs=pl.BlockSpec((C, RET_WIDTH), lambda i: (i, 0)),
        out_shape=jax.ShapeDtypeStruct((L, RET_WIDTH), BF16),
        scratch_shapes=[pltpu.VMEM((RET_HEADS, RET_HEAD_DIM, RET_HEAD_DIM), F32),
                        pltpu.VMEM((RET_HEADS, C, C), F32)],
        compiler_params=_params(("arbitrary",)),
        name="ret_main",
    )(lg, h5, h5, h5, h5, sb)


def _outproj_even_body(y_ref, r_ref, x_ref, wg_ref, bg_ref, wo_ref, g_ref, b_ref, o_ref):
    nl = y_ref.shape[1]
    tm = S5_T * nl
    shift = S5_T.bit_length() - 1
    y_steps = y_ref[...].reshape(tm, y_ref.shape[2])
    perm = _row_permutation(tm, lambda r: (r & (S5_T - 1)) * nl + lax.shift_right_logical(r, shift))
    for r in range(0, tm, LN_ROWS):
        rows = slice(r, r + LN_ROWS)
        y = jnp.dot(perm[rows, :], y_steps, preferred_element_type=F32).astype(BF16)
        z = jnp.dot(y, wg_ref[...], preferred_element_type=F32) + bg_ref[...]
        s5 = (y.astype(F32) * jax.nn.sigmoid(z)).astype(BF16)
        mix = (jnp.dot(s5, wo_ref[0:S5_WIDTH, :], preferred_element_type=F32)
               + jnp.dot(r_ref[rows, :], wo_ref[S5_WIDTH:S5_WIDTH + RET_WIDTH, :],
                         preferred_element_type=F32))
        o_ref[rows, :] = _layer_norm_rows(DEEPNORM_ALPHA * x_ref[rows, :] + mix, g_ref[...], b_ref[...])


def _outproj_even(y_steps, ret, x, w_glu_bf, b_glu, w_out_bf, ln_g, ln_b, tm=512):
    L = x.shape[0]
    row = lambda n: pl.BlockSpec((tm, n), lambda i: (i, 0))
    full = lambda a: pl.BlockSpec(a.shape, lambda i: (0,) * a.ndim)
    return pl.pallas_call(
        _outproj_even_body,
        grid=(L // tm,),
        in_specs=[pl.BlockSpec((S5_T, tm // S5_T, S5_WIDTH), lambda i: (0, i, 0)),
                  row(RET_WIDTH), row(D_MODEL), full(w_glu_bf), full(b_glu),
                  full(w_out_bf), full(ln_g), full(ln_b)],
        out_specs=row(D_MODEL),
        out_shape=jax.ShapeDtypeStruct((L, D_MODEL), F32),
        compiler_params=_params(("parallel",)),
        name="outproj_even",
    )(y_steps, ret, x, w_glu_bf, b_glu, w_out_bf, ln_g, ln_b)


def _inproj_odd_body(x_ref, w_ref, ca_ref, sa_ref, cb_ref, sb_ref, w1f_ref, w2f_ref,
                     o_ref, w1b_ref, w2b_ref):
    w1b_ref[...] = w1f_ref[...].astype(BF16)
    w2b_ref[...] = w2f_ref[...].astype(BF16)
    xb = x_ref[...].astype(BF16)
    tn = ATT_KV_HEADS * ATT_HEAD_DIM
    nq = (ATT_HEADS * ATT_HEAD_DIM) // tn
    cc0, sin_blk = _rotary_block(ca_ref, sa_ref, cb_ref, sb_ref, xb.shape[0])
    lane = lax.broadcasted_iota(jnp.int32, sin_blk.shape, 1)
    s10 = jnp.where(lane < ROPE_DIM // 2, -sin_blk, 0.0)
    s20 = jnp.where((lane >= ROPE_DIM // 2) & (lane < ROPE_DIM), sin_blk, 0.0)
    for cb in range(w_ref.shape[1] // tn):
        acc = jnp.dot(xb, w_ref[:, cb * tn:(cb + 1) * tn], preferred_element_type=F32)
        if cb > nq:
            o_ref[:, cb * tn:(cb + 1) * tn] = acc.astype(BF16)
            continue
        scale = ATT_HEAD_DIM ** -0.5 * LOG2_E if cb < nq else 1.0
        cc = cc0 * scale
        s1 = s10 * scale
        s2 = s20 * scale
        for hh in range(tn // ATT_HEAD_DIM):
            lo = hh * ATT_HEAD_DIM
            a = acc[:, lo:lo + ATT_HEAD_DIM]
            up = pltpu.roll(a, ATT_HEAD_DIM - ROPE_DIM // 2, axis=1)
            dn = pltpu.roll(a, ROPE_DIM // 2, axis=1)
            o_ref[:, cb * tn + lo:cb * tn + lo + ATT_HEAD_DIM] = (a * cc + up * s1 + dn * s2).astype(BF16)


def _inproj_odd(x, w_in_bf, rot, mlp_w1, mlp_w2, layer, tm=256):
    L = x.shape[0]
    n_out = w_in_bf.shape[1]
    row = lambda n: pl.BlockSpec((tm, n), lambda i: (i, 0))
    cast_in, cast_out, cast_shapes = _mlp_weight_cast_specs(L // tm, mlp_w1, mlp_w2, layer)
    return pl.pallas_call(
        _inproj_odd_body,
        grid=(L // tm,),
        in_specs=[
            row(D_MODEL),
            pl.BlockSpec(w_in_bf.shape, lambda i: (0, 0), pipeline_mode=pl.Buffered(1)),
        ] + [pl.BlockSpec(t.shape, lambda i: (0, 0)) for t in rot] + cast_in,
        out_specs=[row(n_out)] + cast_out,
        out_shape=[jax.ShapeDtypeStruct((L, n_out), BF16)] + cast_shapes,
        compiler_params=_params(("parallel",)),
        name="inproj_odd",
    )(x, w_in_bf, *rot, mlp_w1, mlp_w2)


ATT_QB = 2


def _attn_body(sink_ref, q_ref, *refs):
    k_refs = refs[:ATT_QB + 2]
    v_refs = refs[ATT_QB + 2:2 * ATT_QB + 4]
    o_ref, bias_ref = refs[2 * ATT_QB + 4:]
    step = pl.program_id(0)
    nb = pl.num_programs(0) * ATT_QB
    B = ATT_BLOCK
    hd = ATT_HEAD_DIM
    rows = ATT_GROUP * B

    @pl.when(step == 0)
    def _():
        r_i = lax.broadcasted_iota(jnp.int32, (rows, 3 * B), 0)
        s_i = lax.broadcasted_iota(jnp.int32, (rows, 3 * B), 1)
        rel = (r_i & (B - 1)) - s_i + B
        in_win = jnp.abs(rel) <= ATT_WINDOW
        bias_ref[0] = jnp.where(in_win & (s_i >= B), 0.0, NEG_INF)
        bias_ref[1] = jnp.where(in_win, 0.0, NEG_INF)
        bias_ref[2] = jnp.where(in_win & (s_i < 2 * B), 0.0, NEG_INF)

    head_of_row = lax.shift_right_logical(lax.broadcasted_iota(jnp.int32, (rows, 1), 0),
                                          int(math.log2(B)))
    for qb in range(ATT_QB):
        c = step * ATT_QB + qb
        bias = bias_ref[jnp.where(c == 0, 0, jnp.where(c == nb - 1, 2, 1))]
        for g in range(ATT_KV_HEADS):
            q = jnp.concatenate([q_ref[qb * B:(qb + 1) * B, (g * ATT_GROUP + hh) * hd:(g * ATT_GROUP + hh + 1) * hd]
                                 for hh in range(ATT_GROUP)], axis=0)
            ksl = slice(g * hd, (g + 1) * hd)
            k = jnp.concatenate([r[:, ksl] for r in k_refs[qb:qb + 3]], axis=0)
            v = jnp.concatenate([r[:, ksl] for r in v_refs[qb:qb + 3]], axis=0)
            s = lax.dot_general(q, k, (((1,), (1,)), ((), ())), preferred_element_type=F32) + bias
            sink = jnp.zeros((rows, 1), F32)
            for hh in range(ATT_GROUP):
                sink = jnp.where(head_of_row == hh, sink_ref[g * ATT_GROUP + hh], sink)
            sink = sink * LOG2_E
            m = jnp.maximum(jnp.max(s, axis=-1, keepdims=True), sink)
            p = jnp.exp2(s - m)
            den = jnp.sum(p, axis=-1, keepdims=True) + jnp.exp2(sink - m)
            o = jnp.dot(p.astype(BF16), v, preferred_element_type=F32) / den
            for hh in range(ATT_GROUP):
                hcol = (g * ATT_GROUP + hh) * hd
                o_ref[qb * B:(qb + 1) * B, hcol:hcol + hd] = o[hh * B:(hh + 1) * B, :].astype(BF16)


def _attention(sink, qkv):
    L = qkv.shape[0]
    B = ATT_BLOCK
    nb = L // B
    assert nb % ATT_QB == 0 and nb >= 2
    kvw = ATT_KV_HEADS * ATT_HEAD_DIM
    qw = ATT_HEADS * ATT_HEAD_DIM
    kcol = qw // kvw
    vcol = kcol + 1

    def kv_spec(col, off):
        return pl.BlockSpec((B, kvw), lambda i: (jnp.clip(i * ATT_QB + off, 0, nb - 1), col))

    offs = range(-1, ATT_QB + 1)
    return pl.pallas_call(
        _attn_body,
        grid=(nb // ATT_QB,),
        in_specs=[
            pl.BlockSpec(memory_space=pltpu.SMEM),
            pl.BlockSpec((ATT_QB * B, qw), lambda i: (i, 0)),
        ] + [kv_spec(kcol, o) for o in offs] + [kv_spec(vcol, o) for o in offs],
        out_specs=pl.BlockSpec((ATT_QB * B, qw), lambda i: (i, 0)),
        out_shape=jax.ShapeDtypeStruct((L, qw), BF16),
        scratch_shapes=[pltpu.VMEM((3, ATT_GROUP * B, 3 * B), F32)],
        compiler_params=_params(("arbitrary",)),
        name="attention",
    )(sink, qkv, *([qkv] * (2 * ATT_QB + 4)))


def _outproj_odd_body(a_ref, x_ref, wo_ref, g_ref, b_ref, o_ref):
    for r in range(0, a_ref.shape[0], LN_ROWS):
        rows = slice(r, r + LN_ROWS)
        mix = jnp.dot(a_ref[rows, :], wo_ref[...], preferred_element_type=F32)
        o_ref[rows, :] = _layer_norm_rows(DEEPNORM_ALPHA * x_ref[rows, :] + mix, g_ref[...], b_ref[...])


def _outproj_odd(a, x, w_out_bf, ln_g, ln_b, tm=1024):
    L = x.shape[0]
    row = lambda n: pl.BlockSpec((tm, n), lambda i: (i, 0))
    full = lambda t: pl.BlockSpec(t.shape, lambda i: (0,) * t.ndim)
    return pl.pallas_call(
        _outproj_odd_body,
        grid=(L // tm,),
        in_specs=[row(a.shape[1]), row(D_MODEL),
                  pl.BlockSpec(w_out_bf.shape, lambda i: (0, 0), pipeline_mode=pl.Buffered(1)),
                  full(ln_g), full(ln_b)],
        out_specs=row(D_MODEL),
        out_shape=jax.ShapeDtypeStruct((L, D_MODEL), F32),
        compiler_params=_params(("parallel",)),
        name="outproj_odd",
    )(a, x, w_out_bf, ln_g, ln_b)


def _mlp_body(x_ref, w1_ref, w2_ref, g_ref, b_ref, o_ref, xb_ref, acc_ref):
    f = pl.program_id(1)

    def partial_sum(rows):
        h = jnp.dot(xb_ref[rows, :], w1_ref[...], preferred_element_type=F32)
        h = jnp.square(jnp.maximum(h, 0.0)).astype(BF16)
        return jnp.dot(h, w2_ref[...], preferred_element_type=F32)

    last = pl.num_programs(1) - 1

    @pl.when(f == 0)
    def _():
        xb_ref[...] = x_ref[...].astype(BF16)
        acc_ref[...] = partial_sum(slice(None))

    @pl.when((f > 0) & (f < last))
    def _():
        acc_ref[...] += partial_sum(slice(None))

    @pl.when(f == last)
    def _():
        for r in range(0, x_ref.shape[0], LN_ROWS):
            rows = slice(r, r + LN_ROWS)
            y = DEEPNORM_ALPHA * x_ref[rows, :] + (acc_ref[rows, :] + partial_sum(rows))
            o_ref[rows, :] = _layer_norm_rows(y, g_ref[...], b_ref[...])


def _mlp(x, w1, w2, ln_g, ln_b, tm=512, tf=1024):
    L = x.shape[0]
    return pl.pallas_call(
        _mlp_body,
        grid=(L // tm, D_FF // tf),
        in_specs=[
            pl.BlockSpec((tm, D_MODEL), lambda i, f: (i, 0)),
            pl.BlockSpec((D_MODEL, tf), lambda i, f: (0, f)),
            pl.BlockSpec((tf, D_MODEL), lambda i, f: (f, 0)),
            pl.BlockSpec((1, D_MODEL), lambda i, f: (0, 0)),
            pl.BlockSpec((1, D_MODEL), lambda i, f: (0, 0)),
        ],
        out_specs=pl.BlockSpec((tm, D_MODEL), lambda i, f: (i, 0)),
        out_shape=jax.ShapeDtypeStruct((L, D_MODEL), F32),
        scratch_shapes=[pltpu.VMEM((tm, D_MODEL), BF16), pltpu.VMEM((tm, D_MODEL), F32)],
        compiler_params=_params(("parallel", "arbitrary")),
        name="mlp",
    )(x, w1, w2, ln_g, ln_b)


def _rotary_tables(L, rot_dim, theta):
    half = rot_dim // 2
    inv_freq = 1.0 / (theta ** (jnp.arange(half, dtype=F32) / half))
    ang_a = (jnp.arange(L // ROT_LO) * ROT_LO).astype(F32)[:, None] * inv_freq[None, :]
    ang_b = jnp.arange(ROT_LO).astype(F32)[:, None] * inv_freq[None, :]
    return jnp.cos(ang_a), jnp.sin(ang_a), jnp.cos(ang_b), jnp.sin(ang_b)


def _even_layer(x, w_in, w_out, lam_re, lam_im, log_step, b_re, b_im, c_re, c_im,
                d_skip, w_glu, b_glu, ret_log_decay, ln_g, ln_b, mlp_w1, mlp_w2, layer, later_weights):
    L = x.shape[0]
    rot = _rotary_tables(L, RET_HEAD_DIM, RET_ROPE_THETA)
    u_steps, h5, w1_bf, w2_bf = _inproj_even(x, w_in.astype(BF16), rot, mlp_w1, mlp_w2, layer)
    disc = _s5_disc(lam_re, lam_im, log_step)
    m, ws_t, wc, (w_glu_bf, w_out_bf, *later_bf) = _s5_gen(disc, b_re, b_im, c_re, c_im,
                                                           (w_glu, w_out) + tuple(later_weights))
    d_tile = jnp.tile(d_skip.astype(F32), (1, S5_T)).reshape(S5_GROUPS, 1, S5_TL)
    a1, a2, a3 = (disc[k].reshape(2, 1, S5_GROUPS * 128) for k in (DISC_AT_RE, DISC_SCAN_A2, DISC_SCAN_A3))
    y = _s5_mix(u_steps, m, ws_t, wc, d_tile, a1, a2, a3)
    lg = ret_log_decay.astype(F32)
    sb = _ret_bstate(lg, h5)
    ret = _ret_main(lg, h5, sb)
    x1 = _outproj_even(y, ret, x, w_glu_bf, b_glu.astype(F32).reshape(1, -1),
                       w_out_bf, ln_g.reshape(1, -1), ln_b.reshape(1, -1))
    return x1, w1_bf, w2_bf, later_bf


def _odd_layer(x, w_in_bf, w_out_bf, sink, ln_g, ln_b, mlp_w1, mlp_w2, layer):
    L = x.shape[0]
    pad = ATT_HEAD_DIM - ROPE_DIM
    widen = lambda t, fill: jnp.concatenate([t, t, jnp.full((t.shape[0], pad), fill, F32)], axis=1)
    ca, sa, cb, sb = _rotary_tables(L, ROPE_DIM, ROPE_THETA)
    rot = (widen(ca, 1.0), widen(sa, 0.0), widen(cb, 1.0), widen(sb, 0.0))
    qkv, w1_bf, w2_bf = _inproj_odd(x, w_in_bf, rot, mlp_w1, mlp_w2, layer)
    att = _attention(sink.astype(F32), qkv)
    x1 = _outproj_odd(att, x, w_out_bf, ln_g.reshape(1, -1), ln_b.reshape(1, -1))
    return x1, w1_bf, w2_bf


def kernel(x, ln_g, ln_b, mlp_w1, mlp_w2, even_w_in, even_w_out, s5_lambda_re, s5_lambda_im, s5_log_step, s5_b_re, s5_b_im, s5_c_re, s5_c_im, s5_d, s5_w_glu, s5_b_glu, ret_log_decay, odd_w_in, odd_w_out, attn_sink):
    bsz = x.shape[0]
    outs = []
    for b in range(bsz):
        xb = x[b]
        for layer in range(DEPTH):
            if layer % 2 == 0:
                e = layer // 2
                later = (odd_w_in[e], odd_w_out[e]) if layer + 1 < DEPTH else ()
                xb, w1_bf, w2_bf, later_bf = _even_layer(
                    xb, even_w_in[e], even_w_out[e], s5_lambda_re[e], s5_lambda_im[e],
                    s5_log_step[e], s5_b_re[e], s5_b_im[e], s5_c_re[e], s5_c_im[e],
                    s5_d[e], s5_w_glu[e], s5_b_glu[e], ret_log_decay[e],
                    ln_g[layer, 0], ln_b[layer, 0], mlp_w1, mlp_w2, layer, later)
            else:
                o = layer // 2
                xb, w1_bf, w2_bf = _odd_layer(xb, later_bf[0], later_bf[1], attn_sink[o],
                                              ln_g[layer, 0], ln_b[layer, 0], mlp_w1, mlp_w2, layer)
            xb = _mlp(xb, w1_bf, w2_bf,
                      ln_g[layer, 1].reshape(1, -1), ln_b[layer, 1].reshape(1, -1))
        outs.append(xb)
    return jnp.stack(outs, axis=0)
```

```python
import functools
import math

import jax
import jax.numpy as jnp
from jax import lax
from jax.experimental import pallas as pl
from jax.experimental.pallas import tpu as pltpu

F32 = jnp.float32
BF16 = jnp.bfloat16

D_MODEL = 2048
DEPTH = 2
S5_WIDTH = 1024
S5_GROUP = 16
S5_GROUPS = 64
S5_STATE = 64
RET_WIDTH = 1024
RET_HEADS = 4
RET_HEAD_DIM = 256
RET_ROPE_THETA = 10000.0
ATT_HEADS = 16
ATT_KV_HEADS = 4
ATT_HEAD_DIM = 128
ATT_GROUP = 4
ATT_WINDOW = 128
ATT_BLOCK = 128
ROPE_THETA = 500000.0
ROPE_DIM = 32
D_FF = 4 * D_MODEL
DEEPNORM_ALPHA = (2 * DEPTH) ** 0.25
LN_EPS = 1e-5
HEAD_NORM_EPS = 1e-6
NEG_INF = -1e30
LOG2_E = math.log2(math.e)

V7X_VMEM_BYTES = 64 * 1024 * 1024
VMEM_LIMIT = V7X_VMEM_BYTES - 8 * 1024 * 1024

S5_T = 32
S5_TL = S5_T * S5_GROUP
S5_GB = 8
S5_GEN_GB = 4
RET_C = 256
LN_ROWS = 256


def _params(sem):
    return pltpu.CompilerParams(dimension_semantics=sem, vmem_limit_bytes=VMEM_LIMIT)


def _layer_norm_rows(y, g, b):
    mu = jnp.mean(y, axis=-1, keepdims=True)
    yc = y - mu
    var = jnp.mean(yc * yc, axis=-1, keepdims=True)
    return yc * lax.rsqrt(var + LN_EPS) * g + b


def _block_transpose8(xs):
    blk = lax.shift_right_logical(lax.broadcasted_iota(jnp.int32, xs[0].shape, 1), 4)
    xs = list(xs)
    for k in range(3):
        d = 1 << k
        upper = (blk & d) != 0
        for i in range(8):
            if i & d:
                continue
            a, b = xs[i], xs[i + d]
            xs[i] = jnp.where(upper, pltpu.roll(b, S5_GROUP * d, axis=1), a)
            xs[i + d] = jnp.where(upper, b, pltpu.roll(a, 128 - S5_GROUP * d, axis=1))
    return xs


ROT_LO = 64


def _rotary_block(ca_ref, sa_ref, cb_ref, sb_ref, tm):
    cb = cb_ref[...]
    sb = sb_ref[...]
    n = tm // ROT_LO
    cos, sin = [], []
    for al in range(n):
        row = pl.ds(pl.program_id(0) * n + al, 1)
        ca = ca_ref[row, :]
        sa = sa_ref[row, :]
        cos.append(ca * cb - sa * sb)
        sin.append(sa * cb + ca * sb)
    return jnp.concatenate(cos, axis=0), jnp.concatenate(sin, axis=0)


SEG_U, SEG_Q, SEG_K, SEG_V, SEG_GATE = range(5)


def _row_permutation(n_rows, src_of_row):
    r = lax.broadcasted_iota(jnp.int32, (n_rows, n_rows), 0)
    c = lax.broadcasted_iota(jnp.int32, (n_rows, n_rows), 1)
    return jnp.where(c == src_of_row(r), 1.0, 0.0).astype(BF16)


def _mlp_weight_cast_specs(n_steps, mlp_w1, mlp_w2, layer):
    ins, outs, shapes = [], [], []
    for w in (mlp_w1, mlp_w2):
        rows, cols = w.shape[1] // n_steps, w.shape[2]
        ins.append(pl.BlockSpec((None, rows, cols), lambda i: (layer, i, 0)))
        outs.append(pl.BlockSpec((rows, cols), lambda i: (i, 0)))
        shapes.append(jax.ShapeDtypeStruct(w.shape[1:], BF16))
    return ins, outs, shapes


def _inproj_even_body(x_ref, w_ref, ca_ref, sa_ref, cb_ref, sb_ref, w1f_ref, w2f_ref,
                      u_ref, o_ref, w1b_ref, w2b_ref):
    w1b_ref[...] = w1f_ref[...].astype(BF16)
    w2b_ref[...] = w2f_ref[...].astype(BF16)
    xb = x_ref[...].astype(BF16)
    tm = xb.shape[0]
    tn = S5_WIDTH
    half = RET_HEAD_DIM // 2
    cos_blk, sin_blk = _rotary_block(ca_ref, sa_ref, cb_ref, sb_ref, tm)
    for seg in range(w_ref.shape[1] // tn):
        acc = jnp.dot(xb, w_ref[:, seg * tn:(seg + 1) * tn], preferred_element_type=F32)
        if seg == SEG_U:
            nl = tm // S5_T
            shift = nl.bit_length() - 1
            perm = _row_permutation(tm, lambda r: (r & (nl - 1)) * S5_T + lax.shift_right_logical(r, shift))
            up = jnp.dot(perm, acc.astype(BF16), preferred_element_type=F32)
            for t in range(S5_T):
                u_ref[t] = up[t * nl:(t + 1) * nl, :]
        elif seg in (SEG_Q, SEG_K):
            scale = RET_HEAD_DIM ** -0.5 if seg == SEG_K else 1.0
            cos = cos_blk * scale
            sin = sin_blk * scale
            for hh in range(RET_HEADS):
                lo = hh * RET_HEAD_DIM
                a = acc[:, lo:lo + half]
                b = acc[:, lo + half:lo + RET_HEAD_DIM]
                o_ref[seg - 1, :, lo:lo + half] = (a * cos - b * sin).astype(BF16)
                o_ref[seg - 1, :, lo + half:lo + RET_HEAD_DIM] = (b * cos + a * sin).astype(BF16)
        elif seg == SEG_GATE:
            o_ref[seg - 1] = (acc * jax.nn.sigmoid(acc)).astype(BF16)
        else:
            o_ref[seg - 1] = acc.astype(BF16)


def _inproj_even(x, w_in_bf, rot, mlp_w1, mlp_w2, layer, tm=256):
    L = x.shape[0]
    tn = S5_WIDTH
    nseg = w_in_bf.shape[1] // tn
    row = lambda n: pl.BlockSpec((tm, n), lambda i: (i, 0))
    cast_in, cast_out, cast_shapes = _mlp_weight_cast_specs(L // tm, mlp_w1, mlp_w2, layer)
    return pl.pallas_call(
        _inproj_even_body,
        grid=(L // tm,),
        in_specs=[
            row(D_MODEL),
            pl.BlockSpec(w_in_bf.shape, lambda i: (0, 0), pipeline_mode=pl.Buffered(1)),
        ] + [pl.BlockSpec(t.shape, lambda i: (0, 0)) for t in rot] + cast_in,
        out_specs=[pl.BlockSpec((S5_T, tm // S5_T, tn), lambda i: (0, i, 0)),
                   pl.BlockSpec((nseg - 1, tm, tn), lambda i: (0, i, 0))] + cast_out,
        out_shape=[jax.ShapeDtypeStruct((S5_T, L // S5_T, tn), F32),
                   jax.ShapeDtypeStruct((nseg - 1, L, tn), BF16)] + cast_shapes,
        compiler_params=_params(("parallel",)),
        name="inproj_even",
    )(x, w_in_bf, *rot, mlp_w1, mlp_w2)


def _cmul(ar, ai, br, bi):
    return ar * br - ai * bi, ar * bi + ai * br


(DISC_A1_RE, DISC_A1_IM, DISC_A2_RE, DISC_A2_IM, DISC_A4_RE, DISC_A4_IM, DISC_A8_RE, DISC_A8_IM,
 DISC_A16_RE, DISC_A16_IM, DISC_AT_RE, DISC_AT_IM, DISC_Z_RE, DISC_Z_IM, DISC_SCAN_A2, DISC_SCAN_A3) = range(16)


def _s5_disc_body(lr_ref, li_ref, ls_ref, sg_ref, o_ref):
    lr = jnp.minimum(lr_ref[...], -1e-4)
    li = li_ref[...]
    step = jnp.exp(ls_ref[...])
    mag = jnp.exp(lr * step)
    ar = mag * jnp.cos(li * step)
    ai = mag * jnp.sin(li * step)
    nr, ni = ar - 1.0, ai
    den = lr * lr + li * li
    o_ref[DISC_Z_RE] = (nr * lr + ni * li) / den
    o_ref[DISC_Z_IM] = (ni * lr - nr * li) / den
    pr, pi = ar, ai
    for k in range(6):
        o_ref[2 * k] = pr
        o_ref[2 * k + 1] = pi
        if k < 5:
            pr, pi = _cmul(pr, pi, pr, pi)
    o_ref[DISC_SCAN_A2] = pi * sg_ref[...]
    o_ref[DISC_SCAN_A3] = -pi * sg_ref[...]


def _s5_disc(lam_re, lam_im, log_step):
    assert S5_T == 32
    rows = 2 * S5_GROUPS
    two = lambda a: jnp.tile(a.astype(F32).reshape(rows, -1), (1, 2))
    lr = two(lam_re)
    li = two(lam_im)
    ls = jnp.broadcast_to(log_step.astype(F32).reshape(rows, 1), (rows, 2 * S5_STATE))
    sg = jnp.broadcast_to(jnp.concatenate([-jnp.ones((S5_STATE,), F32), jnp.ones((S5_STATE,), F32)])[None],
                          (rows, 2 * S5_STATE))
    return pl.pallas_call(
        _s5_disc_body,
        out_shape=jax.ShapeDtypeStruct((16, rows, 2 * S5_STATE), F32),
        name="s5_disc",
    )(lr, li, ls, sg)


def _s5_gen_body(n_round, col_ref, bre_ref, bim_ref, cre_ref, cim_ref, *refs):
    m_ref, ws_ref, wc_ref = refs[n_round:n_round + 3]
    for src, dst in zip(refs[:n_round], refs[n_round + 3:]):
        dst[...] = src[...].astype(BF16)
    P = S5_STATE
    hi = lax.Precision.HIGHEST
    tlo = lax.shift_right_logical(lax.broadcasted_iota(jnp.int32, (P, 128), 1), 4)
    lane = lax.broadcasted_iota(jnp.int32, (S5_GROUP, S5_TL), 1)
    ones = jnp.ones((P, 128), F32)
    zeros = jnp.zeros((P, 128), F32)
    spread = jnp.where((lax.broadcasted_iota(jnp.int32, (S5_GROUP, 128), 1) & (S5_GROUP - 1))
                       == lax.broadcasted_iota(jnp.int32, (S5_GROUP, 128), 0), 1.0, 0.0)
    contract0 = (((0,), (0,)), ((), ()))

    def one_group(gi, carry):
        kt = []
        for d in range(2):
            col = col_ref[gi, d]
            c = lambda k: jnp.broadcast_to(col[:, k:k + 1], (P, 128))
            a1 = (c(DISC_A1_RE), c(DISC_A1_IM))
            a2 = (c(DISC_A2_RE), c(DISC_A2_IM))
            a4 = (c(DISC_A4_RE), c(DISC_A4_IM))
            a8 = (c(DISC_A8_RE), c(DISC_A8_IM))
            a16 = (c(DISC_A16_RE), c(DISC_A16_IM))
            blk = [None, a8, a16, _cmul(*a8, *a16)]

            def low_powers(reverse):
                xr, xi = ones, zeros
                for k, ak in enumerate((a1, a2, a4)):
                    bit = (lax.shift_right_logical(tlo, k) & 1) == (0 if reverse else 1)
                    yr, yi = _cmul(xr, xi, *ak)
                    xr = jnp.where(bit, yr, xr)
                    xi = jnp.where(bit, yi, xi)
                return xr, xi

            def expand(base, reverse):
                out = []
                for j in range(4):
                    f = blk[3 - j] if reverse else blk[j]
                    out.append(base if f is None else _cmul(*base, *f))
                return out

            ct = tuple(lax.dot_general(r[d, gi], spread, contract0, precision=hi, preferred_element_type=F32)
                       for r in (cre_ref, cim_ref))
            bt = tuple(jnp.dot(r[d, gi], spread, precision=hi, preferred_element_type=F32)
                       for r in (bre_ref, bim_ref))
            bbar = _cmul(c(DISC_Z_RE), c(DISC_Z_IM), *bt)
            ca = expand(_cmul(*ct, *low_powers(d == 1)), d == 1)
            ba = expand(_cmul(*bbar, *low_powers(d == 0)), d == 0)
            wcj = [_cmul(*x, *a1) for x in ca]
            cat = lambda parts, k: jnp.concatenate([x[k] for x in parts], axis=1)
            kt.append(lax.dot_general(bbar[0][:, :S5_GROUP], cat(ca, 0), contract0, precision=hi,
                                      preferred_element_type=F32)
                      - lax.dot_general(bbar[1][:, :S5_GROUP], cat(ca, 1), contract0, precision=hi,
                                        preferred_element_type=F32))
            ba_r, ba_i = cat(ba, 0).astype(BF16), cat(ba, 1).astype(BF16)
            for r, part in enumerate((ba_r, ba_i, ba_i, ba_r)):
                ws_ref[gi, (4 * d + r) * P:(4 * d + r + 1) * P, :] = part
            wc_ref[gi, 2 * d * P:(2 * d + 1) * P, :] = cat(wcj, 0).astype(BF16)
            wc_ref[gi, (2 * d + 1) * P:(2 * d + 2) * P, :] = (-cat(wcj, 1)).astype(BF16)
        ktf, ktb = kt
        for s in range(S5_T):
            lo, hi_lane = S5_GROUP * s, S5_GROUP * (s + 1)
            f = ktf if s == 0 else jnp.where(lane >= lo, pltpu.roll(ktf, lo, axis=1), 0.0)
            b = ktb if s == S5_T - 1 else jnp.where(lane < hi_lane, pltpu.roll(ktb, hi_lane, axis=1), 0.0)
            m_ref[gi, lo:hi_lane, :] = (f + b).astype(BF16)
        return carry

    lax.fori_loop(0, col_ref.shape[0], one_group, 0, unroll=4)


def _s5_gen(disc, b_re, b_im, c_re, c_im, to_round):
    G, P, Cg = S5_GROUPS, S5_STATE, S5_GROUP
    gb = S5_GEN_GB
    steps = G // gb
    col = disc[:, :, :P].reshape(16, 2, G, P).transpose(2, 1, 3, 0)
    per_dir = lambda a: pl.BlockSpec((2, gb) + a.shape[2:], lambda i: (0, i, 0, 0))
    out = lambda rows: pl.BlockSpec((gb, rows, S5_TL), lambda i: (i, 0, 0))
    row_slice = lambda w: pl.BlockSpec((w.shape[0] // steps, w.shape[1]), lambda i: (i, 0))
    res = pl.pallas_call(
        functools.partial(_s5_gen_body, len(to_round)),
        grid=(steps,),
        in_specs=[pl.BlockSpec((gb,) + col.shape[1:], lambda i: (i, 0, 0, 0)),
                  per_dir(b_re), per_dir(b_im), per_dir(c_re), per_dir(c_im)]
                 + [row_slice(w) for w in to_round],
        out_specs=[out(S5_TL), out(8 * P), out(4 * P)] + [row_slice(w) for w in to_round],
        out_shape=[jax.ShapeDtypeStruct((G, S5_TL, S5_TL), BF16),
                   jax.ShapeDtypeStruct((G, 8 * P, S5_TL), BF16),
                   jax.ShapeDtypeStruct((G, 4 * P, S5_TL), BF16)]
                  + [jax.ShapeDtypeStruct(w.shape, BF16) for w in to_round],
        compiler_params=_params(("parallel",)),
        name="s5_gen",
    )(col, b_re.astype(F32), b_im.astype(F32), c_re.astype(F32), c_im.astype(F32), *to_round)
    return res[0], res[1], res[2], tuple(res[3:])


def _s5_body(ut_ref, m_ref, ws_ref, wc_ref, dt_ref, a1_ref, a2_ref, a3_ref, y_ref,
             s_ref, h_ref, yg_ref, u_ref):
    nc = u_ref.shape[1]
    gb = u_ref.shape[0]
    w = gb * 128
    for jb in range(S5_T // 8):
        ys = _block_transpose8([ut_ref[8 * jb + t8] for t8 in range(8)])
        for g8 in range(8):
            u_ref[g8, :, jb * 128:(jb + 1) * 128] = ys[g8].astype(BF16)
    for gi in range(gb):
        s = lax.dot_general(u_ref[gi], ws_ref[gi], (((1,), (1,)), ((), ())),
                            preferred_element_type=F32)
        for r in range(4):
            s_ref[:, r * w + gi * 128:r * w + (gi + 1) * 128] = s[:, r * 128:(r + 1) * 128]

    a1f, a2f, a3f = a1_ref[0], a2_ref[0], a3_ref[0]
    a1b, a2b, a3b = a1_ref[1], a2_ref[1], a3_ref[1]

    def step(n, carry):
        hf, gf, hb, gb_ = carry
        m = nc - 1 - n
        h_ref[pl.ds(n, 1), 0:w] = hf
        h_ref[pl.ds(m, 1), w:2 * w] = hb
        sfh = s_ref[pl.ds(n, 1), 0:w]
        sfg = s_ref[pl.ds(n, 1), w:2 * w]
        sbh = s_ref[pl.ds(m, 1), 2 * w:3 * w]
        sbg = s_ref[pl.ds(m, 1), 3 * w:4 * w]
        hf2 = a1f * hf + a2f * gf + sfh
        gf2 = a1f * gf + a3f * hf + sfg
        hb2 = a1b * hb + a2b * gb_ + sbh
        gb2 = a1b * gb_ + a3b * hb + sbg
        return hf2, gf2, hb2, gb2

    z = jnp.zeros((1, w), F32)
    lax.fori_loop(0, nc, step, (z, z, z, z), unroll=4)

    for gi in range(gb):
        u = u_ref[gi]
        hcat = jnp.concatenate([h_ref[:, gi * 128:(gi + 1) * 128],
                                h_ref[:, w + gi * 128:w + (gi + 1) * 128]], axis=1).astype(BF16)
        y = (jnp.dot(u, m_ref[gi], preferred_element_type=F32)
             + jnp.dot(hcat, wc_ref[gi], preferred_element_type=F32)
             + dt_ref[gi] * u.astype(F32))
        yg_ref[gi] = jax.nn.gelu(y)

    for jb in range(S5_T // 8):
        zs = _block_transpose8([yg_ref[g8, :, jb * 128:(jb + 1) * 128] for g8 in range(8)])
        for t8 in range(8):
            y_ref[8 * jb + t8] = zs[t8].astype(BF16)


def _s5_mix(u_steps, m, ws_t, wc, d_tile, a1, a2, a3):
    _, nc, width = u_steps.shape
    G, TL = S5_GROUPS, S5_TL
    gb = S5_GB
    assert gb * S5_GROUP == 128
    w = gb * 128
    steps_spec = pl.BlockSpec((S5_T, nc, 128), lambda i: (0, 0, i))
    a_spec = pl.BlockSpec((2, 1, w), lambda i: (0, 0, i))
    return pl.pallas_call(
        _s5_body,
        grid=(G // gb,),
        in_specs=[
            steps_spec,
            pl.BlockSpec((gb, TL, TL), lambda i: (i, 0, 0)),
            pl.BlockSpec((gb, 512, TL), lambda i: (i, 0, 0)),
            pl.BlockSpec((gb, 256, TL), lambda i: (i, 0, 0)),
            pl.BlockSpec((gb, 1, TL), lambda i: (i, 0, 0)),
            a_spec, a_spec, a_spec,
        ],
        out_specs=steps_spec,
        out_shape=jax.ShapeDtypeStruct((S5_T, nc, width), BF16),
        scratch_shapes=[pltpu.VMEM((nc, 4 * w), F32), pltpu.VMEM((nc, 2 * w), F32),
                        pltpu.VMEM((gb, nc, TL), F32), pltpu.VMEM((gb, nc, TL), BF16)],
        compiler_params=_params(("parallel",)),
        name="s5_mix",
    )(u_steps, m, ws_t, wc, d_tile, a1, a2, a3)


def _row_index(n):
    return lax.broadcasted_iota(jnp.int32, (n, 1), 0).astype(F32)


def _log_decay(lg_ref, d, h):
    return -jnp.abs(jnp.full((1, 1), lg_ref[d, h], F32))


RET_BSTATE_CHUNKS = 4


def _ret_bstate_body(lg_ref, k_ref, v_ref, sb_ref, st_ref):
    C = RET_C

    @pl.when(pl.program_id(0) == 0)
    def _():
        st_ref[...] = jnp.zeros_like(st_ref)

    jj = _row_index(C)
    for sub in reversed(range(k_ref.shape[0] // C)):
        rows = slice(sub * C, (sub + 1) * C)
        for h in range(RET_HEADS):
            lo = h * RET_HEAD_DIM
            lgb = _log_decay(lg_ref, 1, h)
            sb_ref[sub, h] = st_ref[h].astype(BF16)
            kd = (k_ref[rows, lo:lo + RET_HEAD_DIM].astype(F32) * jnp.exp(jj * lgb)).astype(BF16)
            kv = lax.dot_general(kd, v_ref[rows, lo:lo + RET_HEAD_DIM], (((0,), (0,)), ((), ())),
                                 preferred_element_type=F32)
            st_ref[h] = st_ref[h] * jnp.exp(C * lgb) + kv


def _ret_bstate(lg, h5):
    L = h5.shape[1]
    C = RET_C
    nc = L // C
    per = RET_BSTATE_CHUNKS
    steps = nc // per
    return pl.pallas_call(
        _ret_bstate_body,
        grid=(steps,),
        in_specs=[
            pl.BlockSpec(memory_space=pltpu.SMEM),
            pl.BlockSpec((None, per * C, RET_WIDTH), lambda i: (SEG_K - 1, steps - 1 - i, 0)),
            pl.BlockSpec((None, per * C, RET_WIDTH), lambda i: (SEG_V - 1, steps - 1 - i, 0)),
        ],
        out_specs=pl.BlockSpec((per, RET_HEADS, RET_HEAD_DIM, RET_HEAD_DIM),
                               lambda i: (steps - 1 - i, 0, 0, 0)),
        out_shape=jax.ShapeDtypeStruct((nc, RET_HEADS, RET_HEAD_DIM, RET_HEAD_DIM), BF16),
        scratch_shapes=[pltpu.VMEM((RET_HEADS, RET_HEAD_DIM, RET_HEAD_DIM), F32)],
        compiler_params=_params(("arbitrary",)),
        name="ret_bstate",
    )(lg, h5, h5)


def _ret_main_body(lg_ref, q_ref, k_ref, v_ref, g_ref, sb_ref, o_ref, st_ref, dec_ref):
    C = q_ref.shape[0]

    @pl.when(pl.program_id(0) == 0)
    def _():
        st_ref[...] = jnp.zeros_like(st_ref)
        diff = (lax.broadcasted_iota(jnp.int32, (C, C), 0)
                - lax.broadcasted_iota(jnp.int32, (C, C), 1)).astype(F32)
        for h in range(RET_HEADS):
            dec_ref[h] = jnp.where(diff >= 0, jnp.exp(_log_decay(lg_ref, 0, h) * jnp.maximum(diff, 0.0)),
                                   jnp.exp(_log_decay(lg_ref, 1, h) * jnp.maximum(-diff, 0.0)))

    ii = _row_index(C)
    for h in range(RET_HEADS):
        lo = h * RET_HEAD_DIM
        lgf = _log_decay(lg_ref, 0, h)
        lgb = _log_decay(lg_ref, 1, h)
        q = q_ref[:, lo:lo + RET_HEAD_DIM]
        k = k_ref[:, lo:lo + RET_HEAD_DIM]
        v = v_ref[:, lo:lo + RET_HEAD_DIM]
        qf = q.astype(F32)
        s = lax.dot_general(q, k, (((1,), (1,)), ((), ())), preferred_element_type=F32)
        o = jnp.dot((s * dec_ref[h]).astype(BF16), v, preferred_element_type=F32)
        qdf = (qf * jnp.exp((ii + 1.0) * lgf)).astype(BF16)
        o = o + jnp.dot(qdf, st_ref[h].astype(BF16), preferred_element_type=F32)
        qdb = (qf * jnp.exp((C - ii) * lgb)).astype(BF16)
        o = o + jnp.dot(qdb, sb_ref[h], preferred_element_type=F32)
        mu = jnp.mean(o, axis=-1, keepdims=True)
        oc = o - mu
        var = jnp.mean(oc * oc, axis=-1, keepdims=True)
        on = oc * lax.rsqrt(var + HEAD_NORM_EPS)
        o_ref[:, lo:lo + RET_HEAD_DIM] = (on * g_ref[:, lo:lo + RET_HEAD_DIM].astype(F32)).astype(BF16)
        kd = (k.astype(F32) * jnp.exp((C - 1.0 - ii) * lgf)).astype(BF16)
        kv = lax.dot_general(kd, v, (((0,), (0,)), ((), ())), preferred_element_type=F32)
        st_ref[h] = st_ref[h] * jnp.exp(C * lgf) + kv


def _ret_main(lg, h5, sb):
    L = h5.shape[1]
    C = RET_C
    nc = L // C
    seg = lambda s: pl.BlockSpec((None, C, RET_WIDTH), lambda i: (s - 1, i, 0))
    return pl.pallas_call(
        _ret_main_body,
        grid=(nc,),
        in_specs=[
            pl.BlockSpec(memory_space=pltpu.SMEM),
            seg(SEG_Q), seg(SEG_K), seg(SEG_V), seg(SEG_GATE),
            pl.BlockSpec((None, RET_HEADS, RET_HEAD_DIM, RET_HEAD_DIM), lambda i: (i, 0, 0, 0)),
        ],
        out_specs=pl.BlockSpec((C, RET_WIDTH), lambda i: (i, 0)),
        out_shape=jax.ShapeDtypeStruct((L, RET_WIDTH), BF16),
        scratch_shapes=[pltpu.VMEM((RET_HEADS, RET_HEAD_DIM, RET_HEAD_DIM), F32),
                        pltpu.VMEM((RET_HEADS, C, C), F32)],
        compiler_params=_params(("arbitrary",)),
        name="ret_main",
    )(lg, h5, h5, h5, h5, sb)


def _outproj_even_body(y_ref, r_ref, x_ref, wg_ref, bg_ref, wo_ref, g_ref, b_ref, o_ref):
    nl = y_ref.shape[1]
    tm = S5_T * nl
    shift = S5_T.bit_length() - 1
    y_steps = y_ref[...].reshape(tm, y_ref.shape[2])
    perm = _row_permutation(tm, lambda r: (r & (S5_T - 1)) * nl + lax.shift_right_logical(r, shift))
    for r in range(0, tm, LN_ROWS):
        rows = slice(r, r + LN_ROWS)
        y = jnp.dot(perm[rows, :], y_steps, preferred_element_type=F32).astype(BF16)
        z = jnp.dot(y, wg_ref[...], preferred_element_type=F32) + bg_ref[...]
        s5 = (y.astype(F32) * jax.nn.sigmoid(z)).astype(BF16)
        mix = (jnp.dot(s5, wo_ref[0:S5_WIDTH, :], preferred_element_type=F32)
               + jnp.dot(r_ref[rows, :], wo_ref[S5_WIDTH:S5_WIDTH + RET_WIDTH, :],
                         preferred_element_type=F32))
        o_ref[rows, :] = _layer_norm_rows(DEEPNORM_ALPHA * x_ref[rows, :] + mix, g_ref[...], b_ref[...])


def _outproj_even(y_steps, ret, x, w_glu_bf, b_glu, w_out_bf, ln_g, ln_b, tm=512):
    L = x.shape[0]
    row = lambda n: pl.BlockSpec((tm, n), lambda i: (i, 0))
    full = lambda a: pl.BlockSpec(a.shape, lambda i: (0,) * a.ndim)
    return pl.pallas_call(
        _outproj_even_body,
        grid=(L // tm,),
        in_specs=[pl.BlockSpec((S5_T, tm // S5_T, S5_WIDTH), lambda i: (0, i, 0)),
                  row(RET_WIDTH), row(D_MODEL), full(w_glu_bf), full(b_glu),
                  full(w_out_bf), full(ln_g), full(ln_b)],
        out_specs=row(D_MODEL),
        out_shape=jax.ShapeDtypeStruct((L, D_MODEL), F32),
        compiler_params=_params(("parallel",)),
        name="outproj_even",
    )(y_steps, ret, x, w_glu_bf, b_glu, w_out_bf, ln_g, ln_b)


def _inproj_odd_body(x_ref, w_ref, ca_ref, sa_ref, cb_ref, sb_ref, w1f_ref, w2f_ref,
                     o_ref, w1b_ref, w2b_ref):
    w1b_ref[...] = w1f_ref[...].astype(BF16)
    w2b_ref[...] = w2f_ref[...].astype(BF16)
    xb = x_ref[...].astype(BF16)
    tn = ATT_KV_HEADS * ATT_HEAD_DIM
    nq = (ATT_HEADS * ATT_HEAD_DIM) // tn
    cc0, sin_blk = _rotary_block(ca_ref, sa_ref, cb_ref, sb_ref, xb.shape[0])
    lane = lax.broadcasted_iota(jnp.int32, sin_blk.shape, 1)
    s10 = jnp.where(lane < ROPE_DIM // 2, -sin_blk, 0.0)
    s20 = jnp.where((lane >= ROPE_DIM // 2) & (lane < ROPE_DIM), sin_blk, 0.0)
    for cb in range(w_ref.shape[1] // tn):
        acc = jnp.dot(xb, w_ref[:, cb * tn:(cb + 1) * tn], preferred_element_type=F32)
        if cb > nq:
            o_ref[:, cb * tn:(cb + 1) * tn] = acc.astype(BF16)
            continue
        scale = ATT_HEAD_DIM ** -0.5 * LOG2_E if cb < nq else 1.0
        cc = cc0 * scale
        s1 = s10 * scale
        s2 = s20 * scale
        for hh in range(tn // ATT_HEAD_DIM):
            lo = hh * ATT_HEAD_DIM
            a = acc[:, lo:lo + ATT_HEAD_DIM]
            up = pltpu.roll(a, ATT_HEAD_DIM - ROPE_DIM // 2, axis=1)
            dn = pltpu.roll(a, ROPE_DIM // 2, axis=1)
            o_ref[:, cb * tn + lo:cb * tn + lo + ATT_HEAD_DIM] = (a * cc + up * s1 + dn * s2).astype(BF16)


def _inproj_odd(x, w_in_bf, rot, mlp_w1, mlp_w2, layer, tm=256):
    L = x.shape[0]
    n_out = w_in_bf.shape[1]
    row = lambda n: pl.BlockSpec((tm, n), lambda i: (i, 0))
    cast_in, cast_out, cast_shapes = _mlp_weight_cast_specs(L // tm, mlp_w1, mlp_w2, layer)
    return pl.pallas_call(
        _inproj_odd_body,
        grid=(L // tm,),
        in_specs=[
            row(D_MODEL),
            pl.BlockSpec(w_in_bf.shape, lambda i: (0, 0), pipeline_mode=pl.Buffered(1)),
        ] + [pl.BlockSpec(t.shape, lambda i: (0, 0)) for t in rot] + cast_in,
        out_specs=[row(n_out)] + cast_out,
        out_shape=[jax.ShapeDtypeStruct((L, n_out), BF16)] + cast_shapes,
        compiler_params=_params(("parallel",)),
        name="inproj_odd",
    )(x, w_in_bf, *rot, mlp_w1, mlp_w2)


ATT_QB = 4


def _attn_body(sink_ref, q_ref, *refs):
    k_refs = refs[:ATT_QB + 2]
    v_refs = refs[ATT_QB + 2:2 * ATT_QB + 4]
    o_ref, bias_ref = refs[2 * ATT_QB + 4:]
    step = pl.program_id(0)
    nb = pl.num_programs(0) * ATT_QB
    B = ATT_BLOCK
    hd = ATT_HEAD_DIM
    rows = ATT_GROUP * B

    @pl.when(step == 0)
    def _():
        r_i = lax.broadcasted_iota(jnp.int32, (rows, 3 * B), 0)
        s_i = lax.broadcasted_iota(jnp.int32, (rows, 3 * B), 1)
        rel = (r_i & (B - 1)) - s_i + B
        in_win = jnp.abs(rel) <= ATT_WINDOW
        bias_ref[0] = jnp.where(in_win & (s_i >= B), 0.0, NEG_INF)
        bias_ref[1] = jnp.where(in_win, 0.0, NEG_INF)
        bias_ref[2] = jnp.where(in_win & (s_i < 2 * B), 0.0, NEG_INF)

    head_of_row = lax.shift_right_logical(lax.broadcasted_iota(jnp.int32, (rows, 1), 0),
                                          int(math.log2(B)))
    for qb in range(ATT_QB):
        c = step * ATT_QB + qb
        bias = bias_ref[jnp.where(c == 0, 0, jnp.where(c == nb - 1, 2, 1))]
        for g in range(ATT_KV_HEADS):
            q = jnp.concatenate([q_ref[qb * B:(qb + 1) * B, (g * ATT_GROUP + hh) * hd:(g * ATT_GROUP + hh + 1) * hd]
                                 for hh in range(ATT_GROUP)], axis=0)
            ksl = slice(g * hd, (g + 1) * hd)
            k = jnp.concatenate([r[:, ksl] for r in k_refs[qb:qb + 3]], axis=0)
            v = jnp.concatenate([r[:, ksl] for r in v_refs[qb:qb + 3]], axis=0)
            s = lax.dot_general(q, k, (((1,), (1,)), ((), ())), preferred_element_type=F32) + bias
            sink = jnp.zeros((rows, 1), F32)
            for hh in range(ATT_GROUP):
                sink = jnp.where(head_of_row == hh, sink_ref[g * ATT_GROUP + hh], sink)
            sink = sink * LOG2_E
            m = jnp.maximum(jnp.max(s, axis=-1, keepdims=True), sink)
            p = jnp.exp2(s - m)
            den = jnp.sum(p, axis=-1, keepdims=True) + jnp.exp2(sink - m)
            o = jnp.dot(p.astype(BF16), v, preferred_element_type=F32) / den
            for hh in range(ATT_GROUP):
                hcol = (g * ATT_GROUP + hh) * hd
                o_ref[qb * B:(qb + 1) * B, hcol:hcol + hd] = o[hh * B:(hh + 1) * B, :].astype(BF16)


def _attention(sink, qkv):
    L = qkv.shape[0]
    B = ATT_BLOCK
    nb = L // B
    assert nb % ATT_QB == 0 and nb >= 2
    kvw = ATT_KV_HEADS * ATT_HEAD_DIM
    qw = ATT_HEADS * ATT_HEAD_DIM
    kcol = qw // kvw
    vcol = kcol + 1

    def kv_spec(col, off):
        return pl.BlockSpec((B, kvw), lambda i: (jnp.clip(i * ATT_QB + off, 0, nb - 1), col))

    offs = range(-1, ATT_QB + 1)
    return pl.pallas_call(
        _attn_body,
        grid=(nb // ATT_QB,),
        in_specs=[
            pl.BlockSpec(memory_space=pltpu.SMEM),
            pl.BlockSpec((ATT_QB * B, qw), lambda i: (i, 0)),
        ] + [kv_spec(kcol, o) for o in offs] + [kv_spec(vcol, o) for o in offs],
        out_specs=pl.BlockSpec((ATT_QB * B, qw), lambda i: (i, 0)),
        out_shape=jax.ShapeDtypeStruct((L, qw), BF16),
        scratch_shapes=[pltpu.VMEM((3, ATT_GROUP * B, 3 * B), F32)],
        compiler_params=_params(("arbitrary",)),
        name="attention",
    )(sink, qkv, *([qkv] * (2 * ATT_QB + 4)))


def _outproj_odd_body(a_ref, x_ref, wo_ref, g_ref, b_ref, o_ref):
    for r in range(0, a_ref.shape[0], LN_ROWS):
        rows = slice(r, r + LN_ROWS)
        mix = jnp.dot(a_ref[rows, :], wo_ref[...], preferred_element_type=F32)
        o_ref[rows, :] = _layer_norm_rows(DEEPNORM_ALPHA * x_ref[rows, :] + mix, g_ref[...], b_ref[...])


def _outproj_odd(a, x, w_out_bf, ln_g, ln_b, tm=512):
    L = x.shape[0]
    row = lambda n: pl.BlockSpec((tm, n), lambda i: (i, 0))
    full = lambda t: pl.BlockSpec(t.shape, lambda i: (0,) * t.ndim)
    return pl.pallas_call(
        _outproj_odd_body,
        grid=(L // tm,),
        in_specs=[row(a.shape[1]), row(D_MODEL),
                  pl.BlockSpec(w_out_bf.shape, lambda i: (0, 0), pipeline_mode=pl.Buffered(1)),
                  full(ln_g), full(ln_b)],
        out_specs=row(D_MODEL),
        out_shape=jax.ShapeDtypeStruct((L, D_MODEL), F32),
        compiler_params=_params(("parallel",)),
        name="outproj_odd",
    )(a, x, w_out_bf, ln_g, ln_b)


def _mlp_body(x_ref, w1_ref, w2_ref, g_ref, b_ref, o_ref, xb_ref, acc_ref):
    f = pl.program_id(1)

    def partial_sum(rows):
        h = jnp.dot(xb_ref[rows, :], w1_ref[...], preferred_element_type=F32)
        h = jnp.square(jnp.maximum(h, 0.0)).astype(BF16)
        return jnp.dot(h, w2_ref[...], preferred_element_type=F32)

    last = pl.num_programs(1) - 1

    @pl.when(f == 0)
    def _():
        xb_ref[...] = x_ref[...].astype(BF16)
        acc_ref[...] = partial_sum(slice(None))

    @pl.when((f > 0) & (f < last))
    def _():
        acc_ref[...] += partial_sum(slice(None))

    @pl.when(f == last)
    def _():
        for r in range(0, x_ref.shape[0], LN_ROWS):
            rows = slice(r, r + LN_ROWS)
            y = DEEPNORM_ALPHA * x_ref[rows, :] + (acc_ref[rows, :] + partial_sum(rows))
            o_ref[rows, :] = _layer_norm_rows(y, g_ref[...], b_ref[...])


def _mlp(x, w1, w2, ln_g, ln_b, tm=512, tf=1024):
    L = x.shape[0]
    return pl.pallas_call(
        _mlp_body,
        grid=(L // tm, D_FF // tf),
        in_specs=[
            pl.BlockSpec((tm, D_MODEL), lambda i, f: (i, 0)),
            pl.BlockSpec((D_MODEL, tf), lambda i, f: (0, f)),
            pl.BlockSpec((tf, D_MODEL), lambda i, f: (f, 0)),
            pl.BlockSpec((1, D_MODEL), lambda i, f: (0, 0)),
            pl.BlockSpec((1, D_MODEL), lambda i, f: (0, 0)),
        ],
        out_specs=pl.BlockSpec((tm, D_MODEL), lambda i, f: (i, 0)),
        out_shape=jax.ShapeDtypeStruct((L, D_MODEL), F32),
        scratch_shapes=[pltpu.VMEM((tm, D_MODEL), BF16), pltpu.VMEM((tm, D_MODEL), F32)],
        compiler_params=_params(("parallel", "arbitrary")),
        name="mlp",
    )(x, w1, w2, ln_g, ln_b)


def _rotary_tables(L, rot_dim, theta):
    half = rot_dim // 2
    inv_freq = 1.0 / (theta ** (jnp.arange(half, dtype=F32) / half))
    ang_a = (jnp.arange(L // ROT_LO) * ROT_LO).astype(F32)[:, None] * inv_freq[None, :]
    ang_b = jnp.arange(ROT_LO).astype(F32)[:, None] * inv_freq[None, :]
    return jnp.cos(ang_a), jnp.sin(ang_a), jnp.cos(ang_b), jnp.sin(ang_b)


def _even_layer(x, w_in, w_out, lam_re, lam_im, log_step, b_re, b_im, c_re, c_im,
                d_skip, w_glu, b_glu, ret_log_decay, ln_g, ln_b, mlp_w1, mlp_w2, layer, later_weights):
    L = x.shape[0]
    rot = _rotary_tables(L, RET_HEAD_DIM, RET_ROPE_THETA)
    disc = _s5_disc(lam_re, lam_im, log_step)
    m, ws_t, wc, (w_in_bf, w_glu_bf, w_out_bf, *later_bf) = _s5_gen(
        disc, b_re, b_im, c_re, c_im, (w_in, w_glu, w_out) + tuple(later_weights))
    u_steps, h5, w1_bf, w2_bf = _inproj_even(x, w_in_bf, rot, mlp_w1, mlp_w2, layer)
    d_tile = jnp.tile(d_skip.astype(F32), (1, S5_T)).reshape(S5_GROUPS, 1, S5_TL)
    a1, a2, a3 = (disc[k].reshape(2, 1, S5_GROUPS * 128) for k in (DISC_AT_RE, DISC_SCAN_A2, DISC_SCAN_A3))
    y = _s5_mix(u_steps, m, ws_t, wc, d_tile, a1, a2, a3)
    lg = ret_log_decay.astype(F32)
    sb = _ret_bstate(lg, h5)
    ret = _ret_main(lg, h5, sb)
    x1 = _outproj_even(y, ret, x, w_glu_bf, b_glu.astype(F32).reshape(1, -1),
                       w_out_bf, ln_g.reshape(1, -1), ln_b.reshape(1, -1))
    return x1, w1_bf, w2_bf, later_bf


def _odd_layer(x, w_in_bf, w_out_bf, sink, ln_g, ln_b, mlp_w1, mlp_w2, layer):
    L = x.shape[0]
    pad = ATT_HEAD_DIM - ROPE_DIM
    widen = lambda t, fill: jnp.concatenate([t, t, jnp.full((t.shape[0], pad), fill, F32)], axis=1)
    ca, sa, cb, sb = _rotary_tables(L, ROPE_DIM, ROPE_THETA)
    rot = (widen(ca, 1.0), widen(sa, 0.0), widen(cb, 1.0), widen(sb, 0.0))
    qkv, w1_bf, w2_bf = _inproj_odd(x, w_in_bf, rot, mlp_w1, mlp_w2, layer)
    att = _attention(sink.astype(F32), qkv)
    x1 = _outproj_odd(att, x, w_out_bf, ln_g.reshape(1, -1), ln_b.reshape(1, -1))
    return x1, w1_bf, w2_bf


def kernel(x, ln_g, ln_b, mlp_w1, mlp_w2, even_w_in, even_w_out, s5_lambda_re, s5_lambda_im, s5_log_step, s5_b_re, s5_b_im, s5_c_re, s5_c_im, s5_d, s5_w_glu, s5_b_glu, ret_log_decay, odd_w_in, odd_w_out, attn_sink):
    bsz = x.shape[0]
    outs = []
    for b in range(bsz):
        xb = x[b]
        for layer in range(DEPTH):
            if layer % 2 == 0:
                e = layer // 2
                later = (odd_w_in[e], odd_w_out[e]) if layer + 1 < DEPTH else ()
                xb, w1_bf, w2_bf, later_bf = _even_layer(
                    xb, even_w_in[e], even_w_out[e], s5_lambda_re[e], s5_lambda_im[e],
                    s5_log_step[e], s5_b_re[e], s5_b_im[e], s5_c_re[e], s5_c_im[e],
                    s5_d[e], s5_w_glu[e], s5_b_glu[e], ret_log_decay[e],
                    ln_g[layer, 0], ln_b[layer, 0], mlp_w1, mlp_w2, layer, later)
            else:
                o = layer // 2
                xb, w1_bf, w2_bf = _odd_layer(xb, later_bf[0], later_bf[1], attn_sink[o],
                                              ln_g[layer, 0], ln_b[layer, 0], mlp_w1, mlp_w2, layer)
            xb = _mlp(xb, w1_bf, w2_bf,
                      ln_g[layer, 1].reshape(1, -1), ln_b[layer, 1].reshape(1, -1))
        outs.append(xb)
    return jnp.stack(outs, axis=0)
```

```python
import functools
import math

import jax
import jax.numpy as jnp
from jax import lax
from jax.experimental import pallas as pl
from jax.experimental.pallas import tpu as pltpu

F32 = jnp.float32
BF16 = jnp.bfloat16

D_MODEL = 2048
DEPTH = 2
S5_WIDTH = 1024
S5_GROUP = 16
S5_GROUPS = 64
S5_STATE = 64
RET_WIDTH = 1024
RET_HEADS = 4
RET_HEAD_DIM = 256
RET_ROPE_THETA = 10000.0
ATT_HEADS = 16
ATT_KV_HEADS = 4
ATT_HEAD_DIM = 128
ATT_GROUP = 4
ATT_WINDOW = 128
ATT_BLOCK = 128
ROPE_THETA = 500000.0
ROPE_DIM = 32
D_FF = 4 * D_MODEL
DEEPNORM_ALPHA = (2 * DEPTH) ** 0.25
LN_EPS = 1e-5
HEAD_NORM_EPS = 1e-6
NEG_INF = -1e30
LOG2_E = math.log2(math.e)

V7X_VMEM_BYTES = 64 * 1024 * 1024
VMEM_LIMIT = V7X_VMEM_BYTES - 8 * 1024 * 1024

S5_T = 32
S5_TL = S5_T * S5_GROUP
S5_GB = 8
S5_GEN_GB = 4
RET_C = 256
LN_ROWS = 256
MLP_TF = 1024


def _params(sem):
    return pltpu.CompilerParams(dimension_semantics=sem, vmem_limit_bytes=VMEM_LIMIT)


def _layer_norm_rows(y, g, b):
    mu = jnp.mean(y, axis=-1, keepdims=True)
    yc = y - mu
    var = jnp.mean(yc * yc, axis=-1, keepdims=True)
    return yc * lax.rsqrt(var + LN_EPS) * g + b


def _block_transpose8(xs):
    blk = lax.shift_right_logical(lax.broadcasted_iota(jnp.int32, xs[0].shape, 1), 4)
    xs = list(xs)
    for k in range(3):
        d = 1 << k
        upper = (blk & d) != 0
        for i in range(8):
            if i & d:
                continue
            a, b = xs[i], xs[i + d]
            xs[i] = jnp.where(upper, pltpu.roll(b, S5_GROUP * d, axis=1), a)
            xs[i + d] = jnp.where(upper, b, pltpu.roll(a, 128 - S5_GROUP * d, axis=1))
    return xs


ROT_LO = 64


def _rotary_block(ca_ref, sa_ref, cb_ref, sb_ref, tm):
    cb = cb_ref[...]
    sb = sb_ref[...]
    n = tm // ROT_LO
    cos, sin = [], []
    for al in range(n):
        row = pl.ds(pl.program_id(0) * n + al, 1)
        ca = ca_ref[row, :]
        sa = sa_ref[row, :]
        cos.append(ca * cb - sa * sb)
        sin.append(sa * cb + ca * sb)
    return jnp.concatenate(cos, axis=0), jnp.concatenate(sin, axis=0)


SEG_U, SEG_Q, SEG_K, SEG_V, SEG_GATE = range(5)


def _row_permutation(n_rows, src_of_row):
    r = lax.broadcasted_iota(jnp.int32, (n_rows, n_rows), 0)
    c = lax.broadcasted_iota(jnp.int32, (n_rows, n_rows), 1)
    return jnp.where(c == src_of_row(r), 1.0, 0.0).astype(BF16)


def _mlp_weight_cast_specs(n_steps, mlp_w1, mlp_w2, layer):
    ins, outs, shapes = [], [], []
    for w in (mlp_w1, mlp_w2):
        rows, cols = w.shape[1] // n_steps, w.shape[2]
        ins.append(pl.BlockSpec((None, rows, cols), lambda i: (layer, i, 0)))
    d, f = mlp_w1.shape[1:]
    outs.append(pl.BlockSpec((f // MLP_TF, d // n_steps, MLP_TF), lambda i: (0, i, 0)))
    shapes.append(jax.ShapeDtypeStruct((f // MLP_TF, d, MLP_TF), BF16))
    outs.append(pl.BlockSpec((mlp_w2.shape[1] // n_steps, mlp_w2.shape[2]), lambda i: (i, 0)))
    shapes.append(jax.ShapeDtypeStruct(mlp_w2.shape[1:], BF16))
    return ins, outs, shapes


def _round_mlp_weight_slices(w1f_ref, w2f_ref, w1b_ref, w2b_ref):
    for c in range(w1b_ref.shape[0]):
        w1b_ref[c] = w1f_ref[:, c * MLP_TF:(c + 1) * MLP_TF].astype(BF16)
    w2b_ref[...] = w2f_ref[...].astype(BF16)


def _inproj_even_body(x_ref, w_ref, ca_ref, sa_ref, cb_ref, sb_ref, w1f_ref, w2f_ref,
                      u_ref, o_ref, w1b_ref, w2b_ref):
    _round_mlp_weight_slices(w1f_ref, w2f_ref, w1b_ref, w2b_ref)
    xb = x_ref[...].astype(BF16)
    tm = xb.shape[0]
    tn = S5_WIDTH
    half = RET_HEAD_DIM // 2
    cos_blk, sin_blk = _rotary_block(ca_ref, sa_ref, cb_ref, sb_ref, tm)
    for seg in range(w_ref.shape[1] // tn):
        acc = jnp.dot(xb, w_ref[:, seg * tn:(seg + 1) * tn], preferred_element_type=F32)
        if seg == SEG_U:
            nl = tm // S5_T
            shift = nl.bit_length() - 1
            perm = _row_permutation(tm, lambda r: (r & (nl - 1)) * S5_T + lax.shift_right_logical(r, shift))
            up = jnp.dot(perm, acc.astype(BF16), preferred_element_type=F32)
            for t in range(S5_T):
                u_ref[t] = up[t * nl:(t + 1) * nl, :]
        elif seg in (SEG_Q, SEG_K):
            scale = RET_HEAD_DIM ** -0.5 if seg == SEG_K else 1.0
            cos = cos_blk * scale
            sin = sin_blk * scale
            for hh in range(RET_HEADS):
                lo = hh * RET_HEAD_DIM
                a = acc[:, lo:lo + half]
                b = acc[:, lo + half:lo + RET_HEAD_DIM]
                o_ref[seg - 1, :, lo:lo + half] = (a * cos - b * sin).astype(BF16)
                o_ref[seg - 1, :, lo + half:lo + RET_HEAD_DIM] = (b * cos + a * sin).astype(BF16)
        elif seg == SEG_GATE:
            o_ref[seg - 1] = (acc * jax.nn.sigmoid(acc)).astype(BF16)
        else:
            o_ref[seg - 1] = acc.astype(BF16)


def _inproj_even(x, w_in_bf, rot, mlp_w1, mlp_w2, layer, tm=256):
    L = x.shape[0]
    tn = S5_WIDTH
    nseg = w_in_bf.shape[1] // tn
    row = lambda n: pl.BlockSpec((tm, n), lambda i: (i, 0))
    cast_in, cast_out, cast_shapes = _mlp_weight_cast_specs(L // tm, mlp_w1, mlp_w2, layer)
    return pl.pallas_call(
        _inproj_even_body,
        grid=(L // tm,),
        in_specs=[
            row(D_MODEL),
            pl.BlockSpec(w_in_bf.shape, lambda i: (0, 0), pipeline_mode=pl.Buffered(1)),
        ] + [pl.BlockSpec(t.shape, lambda i: (0, 0)) for t in rot] + cast_in,
        out_specs=[pl.BlockSpec((S5_T, tm // S5_T, tn), lambda i: (0, i, 0)),
                   pl.BlockSpec((nseg - 1, tm, tn), lambda i: (0, i, 0))] + cast_out,
        out_shape=[jax.ShapeDtypeStruct((S5_T, L // S5_T, tn), F32),
                   jax.ShapeDtypeStruct((nseg - 1, L, tn), BF16)] + cast_shapes,
        compiler_params=_params(("parallel",)),
        name="inproj_even",
    )(x, w_in_bf, *rot, mlp_w1, mlp_w2)


def _cmul(ar, ai, br, bi):
    return ar * br - ai * bi, ar * bi + ai * br


(DISC_A1_RE, DISC_A1_IM, DISC_A2_RE, DISC_A2_IM, DISC_A4_RE, DISC_A4_IM, DISC_A8_RE, DISC_A8_IM,
 DISC_A16_RE, DISC_A16_IM, DISC_AT_RE, DISC_AT_IM, DISC_Z_RE, DISC_Z_IM, DISC_SCAN_A2, DISC_SCAN_A3) = range(16)


def _s5_disc_body(lr_ref, li_ref, ls_ref, sg_ref, o_ref):
    lr = jnp.minimum(lr_ref[...], -1e-4)
    li = li_ref[...]
    step = jnp.exp(ls_ref[...])
    mag = jnp.exp(lr * step)
    ar = mag * jnp.cos(li * step)
    ai = mag * jnp.sin(li * step)
    nr, ni = ar - 1.0, ai
    den = lr * lr + li * li
    o_ref[DISC_Z_RE] = (nr * lr + ni * li) / den
    o_ref[DISC_Z_IM] = (ni * lr - nr * li) / den
    pr, pi = ar, ai
    for k in range(6):
        o_ref[2 * k] = pr
        o_ref[2 * k + 1] = pi
        if k < 5:
            pr, pi = _cmul(pr, pi, pr, pi)
    o_ref[DISC_SCAN_A2] = pi * sg_ref[...]
    o_ref[DISC_SCAN_A3] = -pi * sg_ref[...]


def _s5_disc(lam_re, lam_im, log_step):
    assert S5_T == 32
    rows = 2 * S5_GROUPS
    two = lambda a: jnp.tile(a.astype(F32).reshape(rows, -1), (1, 2))
    lr = two(lam_re)
    li = two(lam_im)
    ls = jnp.broadcast_to(log_step.astype(F32).reshape(rows, 1), (rows, 2 * S5_STATE))
    sg = jnp.broadcast_to(jnp.concatenate([-jnp.ones((S5_STATE,), F32), jnp.ones((S5_STATE,), F32)])[None],
                          (rows, 2 * S5_STATE))
    return pl.pallas_call(
        _s5_disc_body,
        out_shape=jax.ShapeDtypeStruct((16, rows, 2 * S5_STATE), F32),
        name="s5_disc",
    )(lr, li, ls, sg)


def _s5_gen_body(n_round, col_ref, bre_ref, bim_ref, cre_ref, cim_ref, *refs):
    m_ref, ws_ref, wc_ref = refs[n_round:n_round + 3]
    for src, dst in zip(refs[:n_round], refs[n_round + 3:]):
        dst[...] = src[...].astype(BF16)
    P = S5_STATE
    hi = lax.Precision.HIGHEST
    tlo = lax.shift_right_logical(lax.broadcasted_iota(jnp.int32, (P, 128), 1), 4)
    lane = lax.broadcasted_iota(jnp.int32, (S5_GROUP, S5_TL), 1)
    ones = jnp.ones((P, 128), F32)
    zeros = jnp.zeros((P, 128), F32)
    spread = jnp.where((lax.broadcasted_iota(jnp.int32, (S5_GROUP, 128), 1) & (S5_GROUP - 1))
                       == lax.broadcasted_iota(jnp.int32, (S5_GROUP, 128), 0), 1.0, 0.0)
    contract0 = (((0,), (0,)), ((), ()))

    def one_group(gi, carry):
        kt = []
        for d in range(2):
            col = col_ref[gi, d]
            c = lambda k: jnp.broadcast_to(col[:, k:k + 1], (P, 128))
            a1 = (c(DISC_A1_RE), c(DISC_A1_IM))
            a2 = (c(DISC_A2_RE), c(DISC_A2_IM))
            a4 = (c(DISC_A4_RE), c(DISC_A4_IM))
            a8 = (c(DISC_A8_RE), c(DISC_A8_IM))
            a16 = (c(DISC_A16_RE), c(DISC_A16_IM))
            blk = [None, a8, a16, _cmul(*a8, *a16)]

            def low_powers(reverse):
                xr, xi = ones, zeros
                for k, ak in enumerate((a1, a2, a4)):
                    bit = (lax.shift_right_logical(tlo, k) & 1) == (0 if reverse else 1)
                    yr, yi = _cmul(xr, xi, *ak)
                    xr = jnp.where(bit, yr, xr)
                    xi = jnp.where(bit, yi, xi)
                return xr, xi

            def expand(base, reverse):
                out = []
                for j in range(4):
                    f = blk[3 - j] if reverse else blk[j]
                    out.append(base if f is None else _cmul(*base, *f))
                return out

            ct = tuple(lax.dot_general(r[d, gi], spread, contract0, precision=hi, preferred_element_type=F32)
                       for r in (cre_ref, cim_ref))
            bt = tuple(jnp.dot(r[d, gi], spread, precision=hi, preferred_element_type=F32)
                       for r in (bre_ref, bim_ref))
            bbar = _cmul(c(DISC_Z_RE), c(DISC_Z_IM), *bt)
            ca = expand(_cmul(*ct, *low_powers(d == 1)), d == 1)
            ba = expand(_cmul(*bbar, *low_powers(d == 0)), d == 0)
            wcj = [_cmul(*x, *a1) for x in ca]
            cat = lambda parts, k: jnp.concatenate([x[k] for x in parts], axis=1)
            kt.append(lax.dot_general(bbar[0][:, :S5_GROUP], cat(ca, 0), contract0, precision=hi,
                                      preferred_element_type=F32)
                      - lax.dot_general(bbar[1][:, :S5_GROUP], cat(ca, 1), contract0, precision=hi,
                                        preferred_element_type=F32))
            ba_r, ba_i = cat(ba, 0).astype(BF16), cat(ba, 1).astype(BF16)
            for r, part in enumerate((ba_r, ba_i, ba_i, ba_r)):
                ws_ref[gi, (4 * d + r) * P:(4 * d + r + 1) * P, :] = part
            wc_ref[gi, 2 * d * P:(2 * d + 1) * P, :] = cat(wcj, 0).astype(BF16)
            wc_ref[gi, (2 * d + 1) * P:(2 * d + 2) * P, :] = (-cat(wcj, 1)).astype(BF16)
        ktf, ktb = kt
        for s in range(S5_T):
            lo, hi_lane = S5_GROUP * s, S5_GROUP * (s + 1)
            f = ktf if s == 0 else jnp.where(lane >= lo, pltpu.roll(ktf, lo, axis=1), 0.0)
            b = ktb if s == S5_T - 1 else jnp.where(lane < hi_lane, pltpu.roll(ktb, hi_lane, axis=1), 0.0)
            m_ref[gi, lo:hi_lane, :] = (f + b).astype(BF16)
        return carry

    lax.fori_loop(0, col_ref.shape[0], one_group, 0, unroll=4)


def _s5_gen(disc, b_re, b_im, c_re, c_im, to_round):
    G, P, Cg = S5_GROUPS, S5_STATE, S5_GROUP
    gb = S5_GEN_GB
    steps = G // gb
    col = disc[:, :, :P].reshape(16, 2, G, P).transpose(2, 1, 3, 0)
    per_dir = lambda a: pl.BlockSpec((2, gb) + a.shape[2:], lambda i: (0, i, 0, 0))
    out = lambda rows: pl.BlockSpec((gb, rows, S5_TL), lambda i: (i, 0, 0))
    row_slice = lambda w: pl.BlockSpec((w.shape[0] // steps, w.shape[1]), lambda i: (i, 0))
    res = pl.pallas_call(
        functools.partial(_s5_gen_body, len(to_round)),
        grid=(steps,),
        in_specs=[pl.BlockSpec((gb,) + col.shape[1:], lambda i: (i, 0, 0, 0)),
                  per_dir(b_re), per_dir(b_im), per_dir(c_re), per_dir(c_im)]
                 + [row_slice(w) for w in to_round],
        out_specs=[out(S5_TL), out(8 * P), out(4 * P)] + [row_slice(w) for w in to_round],
        out_shape=[jax.ShapeDtypeStruct((G, S5_TL, S5_TL), BF16),
                   jax.ShapeDtypeStruct((G, 8 * P, S5_TL), BF16),
                   jax.ShapeDtypeStruct((G, 4 * P, S5_TL), BF16)]
                  + [jax.ShapeDtypeStruct(w.shape, BF16) for w in to_round],
        compiler_params=_params(("parallel",)),
        name="s5_gen",
    )(col, b_re.astype(F32), b_im.astype(F32), c_re.astype(F32), c_im.astype(F32), *to_round)
    return res[0], res[1], res[2], tuple(res[3:])


def _s5_body(ut_ref, m_ref, ws_ref, wc_ref, dt_ref, a1_ref, a2_ref, a3_ref, y_ref,
             s_ref, h_ref, yg_ref, u_ref):
    nc = u_ref.shape[1]
    gb = u_ref.shape[0]
    w = gb * 128
    for jb in range(S5_T // 8):
        ys = _block_transpose8([ut_ref[8 * jb + t8] for t8 in range(8)])
        for g8 in range(8):
            u_ref[g8, :, jb * 128:(jb + 1) * 128] = ys[g8].astype(BF16)
    for gi in range(gb):
        s = lax.dot_general(u_ref[gi], ws_ref[gi], (((1,), (1,)), ((), ())),
                            preferred_element_type=F32)
        for r in range(4):
            s_ref[:, r * w + gi * 128:r * w + (gi + 1) * 128] = s[:, r * 128:(r + 1) * 128]

    a1f, a2f, a3f = a1_ref[0], a2_ref[0], a3_ref[0]
    a1b, a2b, a3b = a1_ref[1], a2_ref[1], a3_ref[1]

    def step(n, carry):
        hf, gf, hb, gb_ = carry
        m = nc - 1 - n
        h_ref[pl.ds(n, 1), 0:w] = hf
        h_ref[pl.ds(m, 1), w:2 * w] = hb
        sfh = s_ref[pl.ds(n, 1), 0:w]
        sfg = s_ref[pl.ds(n, 1), w:2 * w]
        sbh = s_ref[pl.ds(m, 1), 2 * w:3 * w]
        sbg = s_ref[pl.ds(m, 1), 3 * w:4 * w]
        hf2 = a1f * hf + a2f * gf + sfh
        gf2 = a1f * gf + a3f * hf + sfg
        hb2 = a1b * hb + a2b * gb_ + sbh
        gb2 = a1b * gb_ + a3b * hb + sbg
        return hf2, gf2, hb2, gb2

    z = jnp.zeros((1, w), F32)
    lax.fori_loop(0, nc, step, (z, z, z, z), unroll=4)

    for gi in range(gb):
        u = u_ref[gi]
        hcat = jnp.concatenate([h_ref[:, gi * 128:(gi + 1) * 128],
                                h_ref[:, w + gi * 128:w + (gi + 1) * 128]], axis=1).astype(BF16)
        y = (jnp.dot(u, m_ref[gi], preferred_element_type=F32)
             + jnp.dot(hcat, wc_ref[gi], preferred_element_type=F32)
             + dt_ref[gi] * u.astype(F32))
        yg_ref[gi] = jax.nn.gelu(y)

    for jb in range(S5_T // 8):
        zs = _block_transpose8([yg_ref[g8, :, jb * 128:(jb + 1) * 128] for g8 in range(8)])
        for t8 in range(8):
            y_ref[8 * jb + t8] = zs[t8].astype(BF16)


def _s5_mix(u_steps, m, ws_t, wc, d_tile, a1, a2, a3):
    _, nc, width = u_steps.shape
    G, TL = S5_GROUPS, S5_TL
    gb = S5_GB
    assert gb * S5_GROUP == 128
    w = gb * 128
    steps_spec = pl.BlockSpec((S5_T, nc, 128), lambda i: (0, 0, i))
    a_spec = pl.BlockSpec((2, 1, w), lambda i: (0, 0, i))
    return pl.pallas_call(
        _s5_body,
        grid=(G // gb,),
        in_specs=[
            steps_spec,
            pl.BlockSpec((gb, TL, TL), lambda i: (i, 0, 0)),
            pl.BlockSpec((gb, 512, TL), lambda i: (i, 0, 0)),
            pl.BlockSpec((gb, 256, TL), lambda i: (i, 0, 0)),
            pl.BlockSpec((gb, 1, TL), lambda i: (i, 0, 0)),
            a_spec, a_spec, a_spec,
        ],
        out_specs=steps_spec,
        out_shape=jax.ShapeDtypeStruct((S5_T, nc, width), BF16),
        scratch_shapes=[pltpu.VMEM((nc, 4 * w), F32), pltpu.VMEM((nc, 2 * w), F32),
                        pltpu.VMEM((gb, nc, TL), F32), pltpu.VMEM((gb, nc, TL), BF16)],
        compiler_params=_params(("parallel",)),
        name="s5_mix",
    )(u_steps, m, ws_t, wc, d_tile, a1, a2, a3)


def _row_index(n):
    return lax.broadcasted_iota(jnp.int32, (n, 1), 0).astype(F32)


def _log_decay(lg_ref, d, h):
    return -jnp.abs(jnp.full((1, 1), lg_ref[d, h], F32))


RET_BSTATE_CHUNKS = 4


def _ret_bstate_body(lg_ref, k_ref, v_ref, sb_ref, st_ref):
    C = RET_C

    @pl.when(pl.program_id(0) == 0)
    def _():
        st_ref[...] = jnp.zeros_like(st_ref)

    jj = _row_index(C)
    for sub in reversed(range(k_ref.shape[0] // C)):
        rows = slice(sub * C, (sub + 1) * C)
        for h in range(RET_HEADS):
            lo = h * RET_HEAD_DIM
            lgb = _log_decay(lg_ref, 1, h)
            sb_ref[sub, h] = st_ref[h].astype(BF16)
            kd = (k_ref[rows, lo:lo + RET_HEAD_DIM].astype(F32) * jnp.exp(jj * lgb)).astype(BF16)
            kv = lax.dot_general(kd, v_ref[rows, lo:lo + RET_HEAD_DIM], (((0,), (0,)), ((), ())),
                                 preferred_element_type=F32)
            st_ref[h] = st_ref[h] * jnp.exp(C * lgb) + kv


def _ret_bstate(lg, h5):
    L = h5.shape[1]
    C = RET_C
    nc = L // C
    per = RET_BSTATE_CHUNKS
    steps = nc // per
    return pl.pallas_call(
        _ret_bstate_body,
        grid=(steps,),
        in_specs=[
            pl.BlockSpec(memory_space=pltpu.SMEM),
            pl.BlockSpec((None, per * C, RET_WIDTH), lambda i: (SEG_K - 1, steps - 1 - i, 0)),
            pl.BlockSpec((None, per * C, RET_WIDTH), lambda i: (SEG_V - 1, steps - 1 - i, 0)),
        ],
        out_specs=pl.BlockSpec((per, RET_HEADS, RET_HEAD_DIM, RET_HEAD_DIM),
                               lambda i: (steps - 1 - i, 0, 0, 0)),
        out_shape=jax.ShapeDtypeStruct((nc, RET_HEADS, RET_HEAD_DIM, RET_HEAD_DIM), BF16),
        scratch_shapes=[pltpu.VMEM((RET_HEADS, RET_HEAD_DIM, RET_HEAD_DIM), F32)],
        compiler_params=_params(("arbitrary",)),
        name="ret_bstate",
    )(lg, h5, h5)


RET_MAIN_CHUNKS = 2


def _ret_main_body(lg_ref, q_ref, k_ref, v_ref, g_ref, sb_ref, o_ref, st_ref, dec_ref):
    C = RET_C

    @pl.when(pl.program_id(0) == 0)
    def _():
        st_ref[...] = jnp.zeros_like(st_ref)
        diff = (lax.broadcasted_iota(jnp.int32, (C, C), 0)
                - lax.broadcasted_iota(jnp.int32, (C, C), 1)).astype(F32)
        for h in range(RET_HEADS):
            dec_ref[h] = jnp.where(diff >= 0, jnp.exp(_log_decay(lg_ref, 0, h) * jnp.maximum(diff, 0.0)),
                                   jnp.exp(_log_decay(lg_ref, 1, h) * jnp.maximum(-diff, 0.0)))

    ii = _row_index(C)
    for sub in range(q_ref.shape[0] // C):
        rows = slice(sub * C, (sub + 1) * C)
        for h in range(RET_HEADS):
            lo = h * RET_HEAD_DIM
            cols = slice(lo, lo + RET_HEAD_DIM)
            lgf = _log_decay(lg_ref, 0, h)
            lgb = _log_decay(lg_ref, 1, h)
            q = q_ref[rows, cols]
            k = k_ref[rows, cols]
            v = v_ref[rows, cols]
            qf = q.astype(F32)
            s = lax.dot_general(q, k, (((1,), (1,)), ((), ())), preferred_element_type=F32)
            o = jnp.dot((s * dec_ref[h]).astype(BF16), v, preferred_element_type=F32)
            qdf = (qf * jnp.exp((ii + 1.0) * lgf)).astype(BF16)
            o = o + jnp.dot(qdf, st_ref[h].astype(BF16), preferred_element_type=F32)
            qdb = (qf * jnp.exp((C - ii) * lgb)).astype(BF16)
            o = o + jnp.dot(qdb, sb_ref[sub, h], preferred_element_type=F32)
            mu = jnp.mean(o, axis=-1, keepdims=True)
            oc = o - mu
            var = jnp.mean(oc * oc, axis=-1, keepdims=True)
            on = oc * lax.rsqrt(var + HEAD_NORM_EPS)
            o_ref[rows, cols] = (on * g_ref[rows, cols].astype(F32)).astype(BF16)
            kd = (k.astype(F32) * jnp.exp((C - 1.0 - ii) * lgf)).astype(BF16)
            kv = lax.dot_general(kd, v, (((0,), (0,)), ((), ())), preferred_element_type=F32)
            st_ref[h] = st_ref[h] * jnp.exp(C * lgf) + kv


def _ret_main(lg, h5, sb):
    L = h5.shape[1]
    C = RET_C
    per = RET_MAIN_CHUNKS
    steps = L // (per * C)
    seg = lambda s: pl.BlockSpec((None, per * C, RET_WIDTH), lambda i: (s - 1, i, 0))
    return pl.pallas_call(
        _ret_main_body,
        grid=(steps,),
        in_specs=[
            pl.BlockSpec(memory_space=pltpu.SMEM),
            seg(SEG_Q), seg(SEG_K), seg(SEG_V), seg(SEG_GATE),
            pl.BlockSpec((per, RET_HEADS, RET_HEAD_DIM, RET_HEAD_DIM), lambda i: (i, 0, 0, 0)),
        ],
        out_specs=pl.BlockSpec((per * C, RET_WIDTH), lambda i: (i, 0)),
        out_shape=jax.ShapeDtypeStruct((L, RET_WIDTH), BF16),
        scratch_shapes=[pltpu.VMEM((RET_HEADS, RET_HEAD_DIM, RET_HEAD_DIM), F32),
                        pltpu.VMEM((RET_HEADS, C, C), F32)],
        compiler_params=_params(("arbitrary",)),
        name="ret_main",
    )(lg, h5, h5, h5, h5, sb)


def _outproj_even_body(y_ref, r_ref, x_ref, wg_ref, bg_ref, wo_ref, g_ref, b_ref, o_ref):
    nl = y_ref.shape[1]
    tm = S5_T * nl
    shift = S5_T.bit_length() - 1
    y_steps = y_ref[...].reshape(tm, y_ref.shape[2])
    perm = _row_permutation(tm, lambda r: (r & (S5_T - 1)) * nl + lax.shift_right_logical(r, shift))
    for r in range(0, tm, LN_ROWS):
        rows = slice(r, r + LN_ROWS)
        y = jnp.dot(perm[rows, :], y_steps, preferred_element_type=F32).astype(BF16)
        z = jnp.dot(y, wg_ref[...], preferred_element_type=F32) + bg_ref[...]
        s5 = (y.astype(F32) * jax.nn.sigmoid(z)).astype(BF16)
        mix = (jnp.dot(s5, wo_ref[0:S5_WIDTH, :], preferred_element_type=F32)
               + jnp.dot(r_ref[rows, :], wo_ref[S5_WIDTH:S5_WIDTH + RET_WIDTH, :],
                         preferred_element_type=F32))
        o_ref[rows, :] = _layer_norm_rows(DEEPNORM_ALPHA * x_ref[rows, :] + mix, g_ref[...], b_ref[...])


def _outproj_even(y_steps, ret, x, w_glu_bf, b_glu, w_out_bf, ln_g, ln_b, tm=512):
    L = x.shape[0]
    row = lambda n: pl.BlockSpec((tm, n), lambda i: (i, 0))
    full = lambda a: pl.BlockSpec(a.shape, lambda i: (0,) * a.ndim)
    return pl.pallas_call(
        _outproj_even_body,
        grid=(L // tm,),
        in_specs=[pl.BlockSpec((S5_T, tm // S5_T, S5_WIDTH), lambda i: (0, i, 0)),
                  row(RET_WIDTH), row(D_MODEL), full(w_glu_bf), full(b_glu),
                  full(w_out_bf), full(ln_g), full(ln_b)],
        out_specs=row(D_MODEL),
        out_shape=jax.ShapeDtypeStruct((L, D_MODEL), F32),
        compiler_params=_params(("parallel",)),
        name="outproj_even",
    )(y_steps, ret, x, w_glu_bf, b_glu, w_out_bf, ln_g, ln_b)


def _inproj_odd_body(x_ref, w_ref, ca_ref, sa_ref, cb_ref, sb_ref, w1f_ref, w2f_ref,
                     o_ref, w1b_ref, w2b_ref):
    _round_mlp_weight_slices(w1f_ref, w2f_ref, w1b_ref, w2b_ref)
    xb = x_ref[...].astype(BF16)
    tn = ATT_KV_HEADS * ATT_HEAD_DIM
    nq = (ATT_HEADS * ATT_HEAD_DIM) // tn
    cc0, sin_blk = _rotary_block(ca_ref, sa_ref, cb_ref, sb_ref, xb.shape[0])
    lane = lax.broadcasted_iota(jnp.int32, sin_blk.shape, 1)
    s10 = jnp.where(lane < ROPE_DIM // 2, -sin_blk, 0.0)
    s20 = jnp.where((lane >= ROPE_DIM // 2) & (lane < ROPE_DIM), sin_blk, 0.0)
    for cb in range(w_ref.shape[1] // tn):
        acc = jnp.dot(xb, w_ref[:, cb * tn:(cb + 1) * tn], preferred_element_type=F32)
        if cb > nq:
            o_ref[:, cb * tn:(cb + 1) * tn] = acc.astype(BF16)
            continue
        scale = ATT_HEAD_DIM ** -0.5 * LOG2_E if cb < nq else 1.0
        cc = cc0 * scale
        s1 = s10 * scale
        s2 = s20 * scale
        for hh in range(tn // ATT_HEAD_DIM):
            lo = hh * ATT_HEAD_DIM
            a = acc[:, lo:lo + ATT_HEAD_DIM]
            up = pltpu.roll(a, ATT_HEAD_DIM - ROPE_DIM // 2, axis=1)
            dn = pltpu.roll(a, ROPE_DIM // 2, axis=1)
            o_ref[:, cb * tn + lo:cb * tn + lo + ATT_HEAD_DIM] = (a * cc + up * s1 + dn * s2).astype(BF16)


def _inproj_odd(x, w_in_bf, rot, mlp_w1, mlp_w2, layer, tm=256):
    L = x.shape[0]
    n_out = w_in_bf.shape[1]
    row = lambda n: pl.BlockSpec((tm, n), lambda i: (i, 0))
    cast_in, cast_out, cast_shapes = _mlp_weight_cast_specs(L // tm, mlp_w1, mlp_w2, layer)
    return pl.pallas_call(
        _inproj_odd_body,
        grid=(L // tm,),
        in_specs=[
            row(D_MODEL),
            pl.BlockSpec(w_in_bf.shape, lambda i: (0, 0), pipeline_mode=pl.Buffered(1)),
        ] + [pl.BlockSpec(t.shape, lambda i: (0, 0)) for t in rot] + cast_in,
        out_specs=[row(n_out)] + cast_out,
        out_shape=[jax.ShapeDtypeStruct((L, n_out), BF16)] + cast_shapes,
        compiler_params=_params(("parallel",)),
        name="inproj_odd",
    )(x, w_in_bf, *rot, mlp_w1, mlp_w2)


ATT_QB = 4


def _attn_body(sink_ref, q_ref, *refs):
    k_refs = refs[:ATT_QB + 2]
    v_refs = refs[ATT_QB + 2:2 * ATT_QB + 4]
    o_ref, bias_ref = refs[2 * ATT_QB + 4:]
    step = pl.program_id(0)
    nb = pl.num_programs(0) * ATT_QB
    B = ATT_BLOCK
    hd = ATT_HEAD_DIM
    rows = ATT_GROUP * B

    @pl.when(step == 0)
    def _():
        r_i = lax.broadcasted_iota(jnp.int32, (rows, 3 * B), 0)
        s_i = lax.broadcasted_iota(jnp.int32, (rows, 3 * B), 1)
        rel = (r_i & (B - 1)) - s_i + B
        in_win = jnp.abs(rel) <= ATT_WINDOW
        bias_ref[0] = jnp.where(in_win & (s_i >= B), 0.0, NEG_INF)
        bias_ref[1] = jnp.where(in_win, 0.0, NEG_INF)
        bias_ref[2] = jnp.where(in_win & (s_i < 2 * B), 0.0, NEG_INF)

    head_of_row = lax.shift_right_logical(lax.broadcasted_iota(jnp.int32, (rows, 1), 0),
                                          int(math.log2(B)))
    for qb in range(ATT_QB):
        c = step * ATT_QB + qb
        bias = bias_ref[jnp.where(c == 0, 0, jnp.where(c == nb - 1, 2, 1))]
        for g in range(ATT_KV_HEADS):
            q = jnp.concatenate([q_ref[qb * B:(qb + 1) * B, (g * ATT_GROUP + hh) * hd:(g * ATT_GROUP + hh + 1) * hd]
                                 for hh in range(ATT_GROUP)], axis=0)
            ksl = slice(g * hd, (g + 1) * hd)
            k = jnp.concatenate([r[:, ksl] for r in k_refs[qb:qb + 3]], axis=0)
            v = jnp.concatenate([r[:, ksl] for r in v_refs[qb:qb + 3]], axis=0)
            s = lax.dot_general(q, k, (((1,), (1,)), ((), ())), preferred_element_type=F32) + bias
            sink = jnp.zeros((rows, 1), F32)
            for hh in range(ATT_GROUP):
                sink = jnp.where(head_of_row == hh, sink_ref[g * ATT_GROUP + hh], sink)
            sink = sink * LOG2_E
            m = jnp.maximum(jnp.max(s, axis=-1, keepdims=True), sink)
            p = jnp.exp2(s - m)
            den = jnp.sum(p, axis=-1, keepdims=True) + jnp.exp2(sink - m)
            o = jnp.dot(p.astype(BF16), v, preferred_element_type=F32) / den
            for hh in range(ATT_GROUP):
                hcol = (g * ATT_GROUP + hh) * hd
                o_ref[qb * B:(qb + 1) * B, hcol:hcol + hd] = o[hh * B:(hh + 1) * B, :].astype(BF16)


def _attention(sink, qkv):
    L = qkv.shape[0]
    B = ATT_BLOCK
    nb = L // B
    assert nb % ATT_QB == 0 and nb >= 2
    kvw = ATT_KV_HEADS * ATT_HEAD_DIM
    qw = ATT_HEADS * ATT_HEAD_DIM
    kcol = qw // kvw
    vcol = kcol + 1

    def kv_spec(col, off):
        return pl.BlockSpec((B, kvw), lambda i: (jnp.clip(i * ATT_QB + off, 0, nb - 1), col))

    offs = range(-1, ATT_QB + 1)
    return pl.pallas_call(
        _attn_body,
        grid=(nb // ATT_QB,),
        in_specs=[
            pl.BlockSpec(memory_space=pltpu.SMEM),
            pl.BlockSpec((ATT_QB * B, qw), lambda i: (i, 0)),
        ] + [kv_spec(kcol, o) for o in offs] + [kv_spec(vcol, o) for o in offs],
        out_specs=pl.BlockSpec((ATT_QB * B, qw), lambda i: (i, 0)),
        out_shape=jax.ShapeDtypeStruct((L, qw), BF16),
        scratch_shapes=[pltpu.VMEM((3, ATT_GROUP * B, 3 * B), F32)],
        compiler_params=_params(("arbitrary",)),
        name="attention",
    )(sink, qkv, *([qkv] * (2 * ATT_QB + 4)))


def _tapered_row_blocks(tm):
    starts = list(range(0, tm - LN_ROWS, LN_ROWS)) + [tm - LN_ROWS, tm - LN_ROWS // 2]
    return [slice(a, b) for a, b in zip(starts, starts[1:] + [tm])]


def _outproj_odd_body(a_ref, x_ref, wo_ref, g_ref, b_ref, o_ref):
    for rows in _tapered_row_blocks(a_ref.shape[0]):
        mix = jnp.dot(a_ref[rows, :], wo_ref[...], preferred_element_type=F32)
        o_ref[rows, :] = _layer_norm_rows(DEEPNORM_ALPHA * x_ref[rows, :] + mix, g_ref[...], b_ref[...])


def _outproj_odd(a, x, w_out_bf, ln_g, ln_b, tm=512):
    L = x.shape[0]
    row = lambda n: pl.BlockSpec((tm, n), lambda i: (i, 0))
    full = lambda t: pl.BlockSpec(t.shape, lambda i: (0,) * t.ndim)
    return pl.pallas_call(
        _outproj_odd_body,
        grid=(L // tm,),
        in_specs=[row(a.shape[1]), row(D_MODEL),
                  pl.BlockSpec(w_out_bf.shape, lambda i: (0, 0), pipeline_mode=pl.Buffered(1)),
                  full(ln_g), full(ln_b)],
        out_specs=row(D_MODEL),
        out_shape=jax.ShapeDtypeStruct((L, D_MODEL), F32),
        compiler_params=_params(("parallel",)),
        name="outproj_odd",
    )(a, x, w_out_bf, ln_g, ln_b)


def _mlp_body(x_ref, w1_ref, w2_ref, g_ref, b_ref, o_ref, xb_ref, acc_ref):
    f = pl.program_id(1)

    def partial_sum(rows):
        h = jnp.dot(xb_ref[rows, :], w1_ref[...], preferred_element_type=F32)
        h = jnp.square(jnp.maximum(h, 0.0)).astype(BF16)
        return jnp.dot(h, w2_ref[...], preferred_element_type=F32)

    last = pl.num_programs(1) - 1

    @pl.when(f == 0)
    def _():
        xb_ref[...] = x_ref[...].astype(BF16)
        acc_ref[...] = partial_sum(slice(None))

    @pl.when((f > 0) & (f < last))
    def _():
        acc_ref[...] += partial_sum(slice(None))

    @pl.when(f == last)
    def _():
        for r in range(0, x_ref.shape[0], LN_ROWS):
            rows = slice(r, r + LN_ROWS)
            y = DEEPNORM_ALPHA * x_ref[rows, :] + (acc_ref[rows, :] + partial_sum(rows))
            o_ref[rows, :] = _layer_norm_rows(y, g_ref[...], b_ref[...])


def _mlp(x, w1, w2, ln_g, ln_b, tm=512):
    tf = MLP_TF
    L = x.shape[0]
    return pl.pallas_call(
        _mlp_body,
        grid=(L // tm, D_FF // tf),
        in_specs=[
            pl.BlockSpec((tm, D_MODEL), lambda i, f: (i, 0)),
            pl.BlockSpec((None, D_MODEL, tf), lambda i, f: (f, 0, 0)),
            pl.BlockSpec((tf, D_MODEL), lambda i, f: (f, 0)),
            pl.BlockSpec((1, D_MODEL), lambda i, f: (0, 0)),
            pl.BlockSpec((1, D_MODEL), lambda i, f: (0, 0)),
        ],
        out_specs=pl.BlockSpec((tm, D_MODEL), lambda i, f: (i, 0)),
        out_shape=jax.ShapeDtypeStruct((L, D_MODEL), F32),
        scratch_shapes=[pltpu.VMEM((tm, D_MODEL), BF16), pltpu.VMEM((tm, D_MODEL), F32)],
        compiler_params=_params(("parallel", "arbitrary")),
        name="mlp",
    )(x, w1, w2, ln_g, ln_b)


def _rotary_tables(L, rot_dim, theta):
    half = rot_dim // 2
    inv_freq = 1.0 / (theta ** (jnp.arange(half, dtype=F32) / half))
    ang_a = (jnp.arange(L // ROT_LO) * ROT_LO).astype(F32)[:, None] * inv_freq[None, :]
    ang_b = jnp.arange(ROT_LO).astype(F32)[:, None] * inv_freq[None, :]
    return jnp.cos(ang_a), jnp.sin(ang_a), jnp.cos(ang_b), jnp.sin(ang_b)


def _even_layer(x, w_in, w_out, lam_re, lam_im, log_step, b_re, b_im, c_re, c_im,
                d_skip, w_glu, b_glu, ret_log_decay, ln_g, ln_b, mlp_w1, mlp_w2, layer, later_weights):
    L = x.shape[0]
    rot = _rotary_tables(L, RET_HEAD_DIM, RET_ROPE_THETA)
    disc = _s5_disc(lam_re, lam_im, log_step)
    m, ws_t, wc, (w_in_bf, w_glu_bf, w_out_bf, *later_bf) = _s5_gen(
        disc, b_re, b_im, c_re, c_im, (w_in, w_glu, w_out) + tuple(later_weights))
    u_steps, h5, w1_bf, w2_bf = _inproj_even(x, w_in_bf, rot, mlp_w1, mlp_w2, layer)
    d_tile = jnp.tile(d_skip.astype(F32), (1, S5_T)).reshape(S5_GROUPS, 1, S5_TL)
    a1, a2, a3 = (disc[k].reshape(2, 1, S5_GROUPS * 128) for k in (DISC_AT_RE, DISC_SCAN_A2, DISC_SCAN_A3))
    y = _s5_mix(u_steps, m, ws_t, wc, d_tile, a1, a2, a3)
    lg = ret_log_decay.astype(F32)
    sb = _ret_bstate(lg, h5)
    ret = _ret_main(lg, h5, sb)
    x1 = _outproj_even(y, ret, x, w_glu_bf, b_glu.astype(F32).reshape(1, -1),
                       w_out_bf, ln_g.reshape(1, -1), ln_b.reshape(1, -1))
    return x1, w1_bf, w2_bf, later_bf


def _odd_layer(x, w_in_bf, w_out_bf, sink, ln_g, ln_b, mlp_w1, mlp_w2, layer):
    L = x.shape[0]
    pad = ATT_HEAD_DIM - ROPE_DIM
    widen = lambda t, fill: jnp.concatenate([t, t, jnp.full((t.shape[0], pad), fill, F32)], axis=1)
    ca, sa, cb, sb = _rotary_tables(L, ROPE_DIM, ROPE_THETA)
    rot = (widen(ca, 1.0), widen(sa, 0.0), widen(cb, 1.0), widen(sb, 0.0))
    qkv, w1_bf, w2_bf = _inproj_odd(x, w_in_bf, rot, mlp_w1, mlp_w2, layer)
    att = _attention(sink.astype(F32), qkv)
    x1 = _outproj_odd(att, x, w_out_bf, ln_g.reshape(1, -1), ln_b.reshape(1, -1))
    return x1, w1_bf, w2_bf


def kernel(x, ln_g, ln_b, mlp_w1, mlp_w2, even_w_in, even_w_out, s5_lambda_re, s5_lambda_im, s5_log_step, s5_b_re, s5_b_im, s5_c_re, s5_c_im, s5_d, s5_w_glu, s5_b_glu, ret_log_decay, odd_w_in, odd_w_out, attn_sink):
    bsz = x.shape[0]
    outs = []
    for b in range(bsz):
        xb = x[b]
        for layer in range(DEPTH):
            if layer % 2 == 0:
                e = layer // 2
                later = (odd_w_in[e], odd_w_out[e]) if layer + 1 < DEPTH else ()
                xb, w1_bf, w2_bf, later_bf = _even_layer(
                    xb, even_w_in[e], even_w_out[e], s5_lambda_re[e], s5_lambda_im[e],
                    s5_log_step[e], s5_b_re[e], s5_b_im[e], s5_c_re[e], s5_c_im[e],
                    s5_d[e], s5_w_glu[e], s5_b_glu[e], ret_log_decay[e],
                    ln_g[layer, 0], ln_b[layer, 0], mlp_w1, mlp_w2, layer, later)
            else:
                o = layer // 2
                xb, w1_bf, w2_bf = _odd_layer(xb, later_bf[0], later_bf[1], attn_sink[o],
                                              ln_g[layer, 0], ln_b[layer, 0], mlp_w1, mlp_w2, layer)
            xb = _mlp(xb, w1_bf, w2_bf,
                      ln_g[layer, 1].reshape(1, -1), ln_b[layer, 1].reshape(1, -1))
        outs.append(xb)
    return jnp.stack(outs, axis=0)
```

```python
import functools
import math

import jax
import jax.numpy as jnp
from jax import lax
from jax.experimental import pallas as pl
from jax.experimental.pallas import tpu as pltpu

F32 = jnp.float32
BF16 = jnp.bfloat16

D_MODEL = 2048
DEPTH = 2
S5_WIDTH = 1024
S5_GROUP = 16
S5_GROUPS = 64
S5_STATE = 64
RET_WIDTH = 1024
RET_HEADS = 4
RET_HEAD_DIM = 256
RET_ROPE_THETA = 10000.0
ATT_HEADS = 16
ATT_KV_HEADS = 4
ATT_HEAD_DIM = 128
ATT_GROUP = 4
ATT_WINDOW = 128
ATT_BLOCK = 128
ROPE_THETA = 500000.0
ROPE_DIM = 32
D_FF = 4 * D_MODEL
DEEPNORM_ALPHA = (2 * DEPTH) ** 0.25
LN_EPS = 1e-5
HEAD_NORM_EPS = 1e-6
NEG_INF = -1e30
LOG2_E = math.log2(math.e)

V7X_VMEM_BYTES = 64 * 1024 * 1024
VMEM_LIMIT = V7X_VMEM_BYTES - 8 * 1024 * 1024

S5_T = 32
S5_TL = S5_T * S5_GROUP
S5_GB = 8
S5_GEN_GB = 4
RET_C = 256
LN_ROWS = 256
MLP_TF = 1024


def _params(sem):
    return pltpu.CompilerParams(dimension_semantics=sem, vmem_limit_bytes=VMEM_LIMIT)


def _layer_norm_rows(y, g, b):
    mu = jnp.mean(y, axis=-1, keepdims=True)
    yc = y - mu
    var = jnp.mean(yc * yc, axis=-1, keepdims=True)
    return yc * lax.rsqrt(var + LN_EPS) * g + b


def _block_transpose8(xs):
    blk = lax.shift_right_logical(lax.broadcasted_iota(jnp.int32, xs[0].shape, 1), 4)
    xs = list(xs)
    for k in range(3):
        d = 1 << k
        upper = (blk & d) != 0
        for i in range(8):
            if i & d:
                continue
            a, b = xs[i], xs[i + d]
            xs[i] = jnp.where(upper, pltpu.roll(b, S5_GROUP * d, axis=1), a)
            xs[i + d] = jnp.where(upper, b, pltpu.roll(a, 128 - S5_GROUP * d, axis=1))
    return xs


ROT_LO = 64


def _rotary_block(ca_ref, sa_ref, cb_ref, sb_ref, tm):
    cb = cb_ref[...]
    sb = sb_ref[...]
    n = tm // ROT_LO
    cos, sin = [], []
    for al in range(n):
        row = pl.ds(pl.program_id(0) * n + al, 1)
        ca = ca_ref[row, :]
        sa = sa_ref[row, :]
        cos.append(ca * cb - sa * sb)
        sin.append(sa * cb + ca * sb)
    return jnp.concatenate(cos, axis=0), jnp.concatenate(sin, axis=0)


SEG_U, SEG_Q, SEG_K, SEG_V, SEG_GATE = range(5)


def _row_permutation(n_rows, src_of_row):
    r = lax.broadcasted_iota(jnp.int32, (n_rows, n_rows), 0)
    c = lax.broadcasted_iota(jnp.int32, (n_rows, n_rows), 1)
    return jnp.where(c == src_of_row(r), 1.0, 0.0).astype(BF16)


def _mlp_weight_cast_specs(n_steps, mlp_w1, mlp_w2, layer):
    ins, outs, shapes = [], [], []
    for w in (mlp_w1, mlp_w2):
        rows, cols = w.shape[1] // n_steps, w.shape[2]
        ins.append(pl.BlockSpec((None, rows, cols), lambda i: (layer, i, 0)))
    d, f = mlp_w1.shape[1:]
    outs.append(pl.BlockSpec((f // MLP_TF, d // n_steps, MLP_TF), lambda i: (0, i, 0)))
    shapes.append(jax.ShapeDtypeStruct((f // MLP_TF, d, MLP_TF), BF16))
    outs.append(pl.BlockSpec((mlp_w2.shape[1] // n_steps, mlp_w2.shape[2]), lambda i: (i, 0)))
    shapes.append(jax.ShapeDtypeStruct(mlp_w2.shape[1:], BF16))
    return ins, outs, shapes


def _round_mlp_weight_slices(w1f_ref, w2f_ref, w1b_ref, w2b_ref):
    for c in range(w1b_ref.shape[0]):
        w1b_ref[c] = w1f_ref[:, c * MLP_TF:(c + 1) * MLP_TF].astype(BF16)
    w2b_ref[...] = w2f_ref[...].astype(BF16)


def _inproj_even_body(x_ref, w_ref, ca_ref, sa_ref, cb_ref, sb_ref, w1f_ref, w2f_ref,
                      u_ref, o_ref, w1b_ref, w2b_ref):
    _round_mlp_weight_slices(w1f_ref, w2f_ref, w1b_ref, w2b_ref)
    xb = x_ref[...].astype(BF16)
    tm = xb.shape[0]
    tn = S5_WIDTH
    half = RET_HEAD_DIM // 2
    cos_blk, sin_blk = _rotary_block(ca_ref, sa_ref, cb_ref, sb_ref, tm)
    for seg in range(w_ref.shape[1] // tn):
        acc = jnp.dot(xb, w_ref[:, seg * tn:(seg + 1) * tn], preferred_element_type=F32)
        if seg == SEG_U:
            nl = tm // S5_T
            shift = nl.bit_length() - 1
            perm = _row_permutation(tm, lambda r: (r & (nl - 1)) * S5_T + lax.shift_right_logical(r, shift))
            up = jnp.dot(perm, acc.astype(BF16), preferred_element_type=F32)
            for t in range(S5_T):
                u_ref[t] = up[t * nl:(t + 1) * nl, :]
        elif seg in (SEG_Q, SEG_K):
            scale = RET_HEAD_DIM ** -0.5 if seg == SEG_K else 1.0
            cos = cos_blk * scale
            sin = sin_blk * scale
            for hh in range(RET_HEADS):
                lo = hh * RET_HEAD_DIM
                a = acc[:, lo:lo + half]
                b = acc[:, lo + half:lo + RET_HEAD_DIM]
                o_ref[seg - 1, :, lo:lo + half] = (a * cos - b * sin).astype(BF16)
                o_ref[seg - 1, :, lo + half:lo + RET_HEAD_DIM] = (b * cos + a * sin).astype(BF16)
        elif seg == SEG_GATE:
            o_ref[seg - 1] = (acc * jax.nn.sigmoid(acc)).astype(BF16)
        else:
            o_ref[seg - 1] = acc.astype(BF16)


def _inproj_even(x, w_in_bf, rot, mlp_w1, mlp_w2, layer, tm=256):
    L = x.shape[0]
    tn = S5_WIDTH
    nseg = w_in_bf.shape[1] // tn
    row = lambda n: pl.BlockSpec((tm, n), lambda i: (i, 0))
    cast_in, cast_out, cast_shapes = _mlp_weight_cast_specs(L // tm, mlp_w1, mlp_w2, layer)
    return pl.pallas_call(
        _inproj_even_body,
        grid=(L // tm,),
        in_specs=[
            row(D_MODEL),
            pl.BlockSpec(w_in_bf.shape, lambda i: (0, 0), pipeline_mode=pl.Buffered(1)),
        ] + [pl.BlockSpec(t.shape, lambda i: (0, 0)) for t in rot] + cast_in,
        out_specs=[pl.BlockSpec((S5_T, tm // S5_T, tn), lambda i: (0, i, 0)),
                   pl.BlockSpec((nseg - 1, tm, tn), lambda i: (0, i, 0))] + cast_out,
        out_shape=[jax.ShapeDtypeStruct((S5_T, L // S5_T, tn), F32),
                   jax.ShapeDtypeStruct((nseg - 1, L, tn), BF16)] + cast_shapes,
        compiler_params=_params(("parallel",)),
        name="inproj_even",
    )(x, w_in_bf, *rot, mlp_w1, mlp_w2)


def _cmul(ar, ai, br, bi):
    return ar * br - ai * bi, ar * bi + ai * br


(DISC_A1_RE, DISC_A1_IM, DISC_A2_RE, DISC_A2_IM, DISC_A4_RE, DISC_A4_IM, DISC_A8_RE, DISC_A8_IM,
 DISC_A16_RE, DISC_A16_IM, DISC_AT_RE, DISC_AT_IM, DISC_Z_RE, DISC_Z_IM, DISC_SCAN_A2, DISC_SCAN_A3) = range(16)


def _s5_disc_body(lr_ref, li_ref, ls_ref, sg_ref, o_ref):
    lr = jnp.minimum(lr_ref[...], -1e-4)
    li = li_ref[...]
    step = jnp.exp(ls_ref[...])
    mag = jnp.exp(lr * step)
    ar = mag * jnp.cos(li * step)
    ai = mag * jnp.sin(li * step)
    nr, ni = ar - 1.0, ai
    den = lr * lr + li * li
    o_ref[DISC_Z_RE] = (nr * lr + ni * li) / den
    o_ref[DISC_Z_IM] = (ni * lr - nr * li) / den
    pr, pi = ar, ai
    for k in range(6):
        o_ref[2 * k] = pr
        o_ref[2 * k + 1] = pi
        if k < 5:
            pr, pi = _cmul(pr, pi, pr, pi)
    o_ref[DISC_SCAN_A2] = pi * sg_ref[...]
    o_ref[DISC_SCAN_A3] = -pi * sg_ref[...]


def _s5_disc(lam_re, lam_im, log_step):
    assert S5_T == 32
    rows = 2 * S5_GROUPS
    two = lambda a: jnp.tile(a.astype(F32).reshape(rows, -1), (1, 2))
    lr = two(lam_re)
    li = two(lam_im)
    ls = jnp.broadcast_to(log_step.astype(F32).reshape(rows, 1), (rows, 2 * S5_STATE))
    sg = jnp.broadcast_to(jnp.concatenate([-jnp.ones((S5_STATE,), F32), jnp.ones((S5_STATE,), F32)])[None],
                          (rows, 2 * S5_STATE))
    return pl.pallas_call(
        _s5_disc_body,
        out_shape=jax.ShapeDtypeStruct((16, rows, 2 * S5_STATE), F32),
        name="s5_disc",
    )(lr, li, ls, sg)


def _s5_gen_body(n_round, col_ref, bre_ref, bim_ref, cre_ref, cim_ref, *refs):
    m_ref, ws_ref, wc_ref = refs[n_round:n_round + 3]
    for src, dst in zip(refs[:n_round], refs[n_round + 3:]):
        dst[...] = src[...].astype(BF16)
    P = S5_STATE
    hi = lax.Precision.HIGHEST
    tlo = lax.shift_right_logical(lax.broadcasted_iota(jnp.int32, (P, 128), 1), 4)
    lane = lax.broadcasted_iota(jnp.int32, (S5_GROUP, S5_TL), 1)
    ones = jnp.ones((P, 128), F32)
    zeros = jnp.zeros((P, 128), F32)
    spread = jnp.where((lax.broadcasted_iota(jnp.int32, (S5_GROUP, 128), 1) & (S5_GROUP - 1))
                       == lax.broadcasted_iota(jnp.int32, (S5_GROUP, 128), 0), 1.0, 0.0)
    contract0 = (((0,), (0,)), ((), ()))

    def one_group(gi, carry):
        kt = []
        for d in range(2):
            col = col_ref[gi, d]
            c = lambda k: jnp.broadcast_to(col[:, k:k + 1], (P, 128))
            a1 = (c(DISC_A1_RE), c(DISC_A1_IM))
            a2 = (c(DISC_A2_RE), c(DISC_A2_IM))
            a4 = (c(DISC_A4_RE), c(DISC_A4_IM))
            a8 = (c(DISC_A8_RE), c(DISC_A8_IM))
            a16 = (c(DISC_A16_RE), c(DISC_A16_IM))
            blk = [None, a8, a16, _cmul(*a8, *a16)]

            def low_powers(reverse):
                xr, xi = ones, zeros
                for k, ak in enumerate((a1, a2, a4)):
                    bit = (lax.shift_right_logical(tlo, k) & 1) == (0 if reverse else 1)
                    yr, yi = _cmul(xr, xi, *ak)
                    xr = jnp.where(bit, yr, xr)
                    xi = jnp.where(bit, yi, xi)
                return xr, xi

            def expand(base, reverse):
                out = []
                for j in range(4):
                    f = blk[3 - j] if reverse else blk[j]
                    out.append(base if f is None else _cmul(*base, *f))
                return out

            ct = tuple(lax.dot_general(r[d, gi], spread, contract0, precision=hi, preferred_element_type=F32)
                       for r in (cre_ref, cim_ref))
            bt = tuple(jnp.dot(r[d, gi], spread, precision=hi, preferred_element_type=F32)
                       for r in (bre_ref, bim_ref))
            bbar = _cmul(c(DISC_Z_RE), c(DISC_Z_IM), *bt)
            ca = expand(_cmul(*ct, *low_powers(d == 1)), d == 1)
            ba = expand(_cmul(*bbar, *low_powers(d == 0)), d == 0)
            wcj = [_cmul(*x, *a1) for x in ca]
            cat = lambda parts, k: jnp.concatenate([x[k] for x in parts], axis=1)
            kt.append(lax.dot_general(bbar[0][:, :S5_GROUP], cat(ca, 0), contract0, precision=hi,
                                      preferred_element_type=F32)
                      - lax.dot_general(bbar[1][:, :S5_GROUP], cat(ca, 1), contract0, precision=hi,
                                        preferred_element_type=F32))
            ba_r, ba_i = cat(ba, 0).astype(BF16), cat(ba, 1).astype(BF16)
            for r, part in enumerate((ba_r, ba_i, ba_i, ba_r)):
                ws_ref[gi, (4 * d + r) * P:(4 * d + r + 1) * P, :] = part
            wc_ref[gi, 2 * d * P:(2 * d + 1) * P, :] = cat(wcj, 0).astype(BF16)
            wc_ref[gi, (2 * d + 1) * P:(2 * d + 2) * P, :] = (-cat(wcj, 1)).astype(BF16)
        ktf, ktb = kt
        for s in range(S5_T):
            lo, hi_lane = S5_GROUP * s, S5_GROUP * (s + 1)
            f = ktf if s == 0 else jnp.where(lane >= lo, pltpu.roll(ktf, lo, axis=1), 0.0)
            b = ktb if s == S5_T - 1 else jnp.where(lane < hi_lane, pltpu.roll(ktb, hi_lane, axis=1), 0.0)
            m_ref[gi, lo:hi_lane, :] = (f + b).astype(BF16)
        return carry

    lax.fori_loop(0, col_ref.shape[0], one_group, 0, unroll=4)


def _s5_gen(disc, b_re, b_im, c_re, c_im, to_round):
    G, P, Cg = S5_GROUPS, S5_STATE, S5_GROUP
    gb = S5_GEN_GB
    steps = G // gb
    col = disc[:, :, :P].reshape(16, 2, G, P).transpose(2, 1, 3, 0)
    per_dir = lambda a: pl.BlockSpec((2, gb) + a.shape[2:], lambda i: (0, i, 0, 0))
    out = lambda rows: pl.BlockSpec((gb, rows, S5_TL), lambda i: (i, 0, 0))
    row_slice = lambda w: pl.BlockSpec((w.shape[0] // steps, w.shape[1]), lambda i: (i, 0))
    res = pl.pallas_call(
        functools.partial(_s5_gen_body, len(to_round)),
        grid=(steps,),
        in_specs=[pl.BlockSpec((gb,) + col.shape[1:], lambda i: (i, 0, 0, 0)),
                  per_dir(b_re), per_dir(b_im), per_dir(c_re), per_dir(c_im)]
                 + [row_slice(w) for w in to_round],
        out_specs=[out(S5_TL), out(8 * P), out(4 * P)] + [row_slice(w) for w in to_round],
        out_shape=[jax.ShapeDtypeStruct((G, S5_TL, S5_TL), BF16),
                   jax.ShapeDtypeStruct((G, 8 * P, S5_TL), BF16),
                   jax.ShapeDtypeStruct((G, 4 * P, S5_TL), BF16)]
                  + [jax.ShapeDtypeStruct(w.shape, BF16) for w in to_round],
        compiler_params=_params(("parallel",)),
        name="s5_gen",
    )(col, b_re.astype(F32), b_im.astype(F32), c_re.astype(F32), c_im.astype(F32), *to_round)
    return res[0], res[1], res[2], tuple(res[3:])


def _s5_body(ut_ref, m_ref, ws_ref, wc_ref, dt_ref, a1_ref, a2_ref, a3_ref, y_ref,
             s_ref, h_ref, yg_ref, u_ref):
    nc = u_ref.shape[1]
    gb = u_ref.shape[0]
    w = gb * 128
    for jb in range(S5_T // 8):
        ys = _block_transpose8([ut_ref[8 * jb + t8] for t8 in range(8)])
        for g8 in range(8):
            u_ref[g8, :, jb * 128:(jb + 1) * 128] = ys[g8].astype(BF16)
    for gi in range(gb):
        s = lax.dot_general(u_ref[gi], ws_ref[gi], (((1,), (1,)), ((), ())),
                            preferred_element_type=F32)
        for r in range(4):
            s_ref[:, r * w + gi * 128:r * w + (gi + 1) * 128] = s[:, r * 128:(r + 1) * 128]

    a1f, a2f, a3f = a1_ref[0], a2_ref[0], a3_ref[0]
    a1b, a2b, a3b = a1_ref[1], a2_ref[1], a3_ref[1]

    def step(n, carry):
        hf, gf, hb, gb_ = carry
        m = nc - 1 - n
        h_ref[pl.ds(n, 1), 0:w] = hf
        h_ref[pl.ds(m, 1), w:2 * w] = hb
        sfh = s_ref[pl.ds(n, 1), 0:w]
        sfg = s_ref[pl.ds(n, 1), w:2 * w]
        sbh = s_ref[pl.ds(m, 1), 2 * w:3 * w]
        sbg = s_ref[pl.ds(m, 1), 3 * w:4 * w]
        hf2 = a1f * hf + a2f * gf + sfh
        gf2 = a1f * gf + a3f * hf + sfg
        hb2 = a1b * hb + a2b * gb_ + sbh
        gb2 = a1b * gb_ + a3b * hb + sbg
        return hf2, gf2, hb2, gb2

    z = jnp.zeros((1, w), F32)
    lax.fori_loop(0, nc, step, (z, z, z, z), unroll=4)

    for gi in range(gb):
        u = u_ref[gi]
        hcat = jnp.concatenate([h_ref[:, gi * 128:(gi + 1) * 128],
                                h_ref[:, w + gi * 128:w + (gi + 1) * 128]], axis=1).astype(BF16)
        y = (jnp.dot(u, m_ref[gi], preferred_element_type=F32)
             + jnp.dot(hcat, wc_ref[gi], preferred_element_type=F32)
             + dt_ref[gi] * u.astype(F32))
        yg_ref[gi] = jax.nn.gelu(y)

    for jb in range(S5_T // 8):
        zs = _block_transpose8([yg_ref[g8, :, jb * 128:(jb + 1) * 128] for g8 in range(8)])
        for t8 in range(8):
            y_ref[8 * jb + t8] = zs[t8].astype(BF16)


def _s5_mix(u_steps, m, ws_t, wc, d_tile, a1, a2, a3):
    _, nc, width = u_steps.shape
    G, TL = S5_GROUPS, S5_TL
    gb = S5_GB
    assert gb * S5_GROUP == 128
    w = gb * 128
    steps_spec = pl.BlockSpec((S5_T, nc, 128), lambda i: (0, 0, i))
    a_spec = pl.BlockSpec((2, 1, w), lambda i: (0, 0, i))
    return pl.pallas_call(
        _s5_body,
        grid=(G // gb,),
        in_specs=[
            steps_spec,
            pl.BlockSpec((gb, TL, TL), lambda i: (i, 0, 0)),
            pl.BlockSpec((gb, 512, TL), lambda i: (i, 0, 0)),
            pl.BlockSpec((gb, 256, TL), lambda i: (i, 0, 0)),
            pl.BlockSpec((gb, 1, TL), lambda i: (i, 0, 0)),
            a_spec, a_spec, a_spec,
        ],
        out_specs=steps_spec,
        out_shape=jax.ShapeDtypeStruct((S5_T, nc, width), BF16),
        scratch_shapes=[pltpu.VMEM((nc, 4 * w), F32), pltpu.VMEM((nc, 2 * w), F32),
                        pltpu.VMEM((gb, nc, TL), F32), pltpu.VMEM((gb, nc, TL), BF16)],
        compiler_params=_params(("parallel",)),
        name="s5_mix",
    )(u_steps, m, ws_t, wc, d_tile, a1, a2, a3)


def _row_index(n):
    return lax.broadcasted_iota(jnp.int32, (n, 1), 0).astype(F32)


def _log_decay(lg_ref, d, h):
    return -jnp.abs(jnp.full((1, 1), lg_ref[d, h], F32))


RET_BSTATE_CHUNKS = 4


def _ret_bstate_body(lg_ref, k_ref, v_ref, sb_ref, st_ref):
    C = RET_C

    @pl.when(pl.program_id(0) == 0)
    def _():
        st_ref[...] = jnp.zeros_like(st_ref)

    jj = _row_index(C)
    for sub in reversed(range(k_ref.shape[0] // C)):
        rows = slice(sub * C, (sub + 1) * C)
        for h in range(RET_HEADS):
            lo = h * RET_HEAD_DIM
            lgb = _log_decay(lg_ref, 1, h)
            sb_ref[sub, h] = st_ref[h].astype(BF16)
            kd = (k_ref[rows, lo:lo + RET_HEAD_DIM].astype(F32) * jnp.exp(jj * lgb)).astype(BF16)
            kv = lax.dot_general(kd, v_ref[rows, lo:lo + RET_HEAD_DIM], (((0,), (0,)), ((), ())),
                                 preferred_element_type=F32)
            st_ref[h] = st_ref[h] * jnp.exp(C * lgb) + kv


def _ret_bstate(lg, h5):
    L = h5.shape[1]
    C = RET_C
    nc = L // C
    per = RET_BSTATE_CHUNKS
    steps = nc // per
    return pl.pallas_call(
        _ret_bstate_body,
        grid=(steps,),
        in_specs=[
            pl.BlockSpec(memory_space=pltpu.SMEM),
            pl.BlockSpec((None, per * C, RET_WIDTH), lambda i: (SEG_K - 1, steps - 1 - i, 0)),
            pl.BlockSpec((None, per * C, RET_WIDTH), lambda i: (SEG_V - 1, steps - 1 - i, 0)),
        ],
        out_specs=pl.BlockSpec((per, RET_HEADS, RET_HEAD_DIM, RET_HEAD_DIM),
                               lambda i: (steps - 1 - i, 0, 0, 0)),
        out_shape=jax.ShapeDtypeStruct((nc, RET_HEADS, RET_HEAD_DIM, RET_HEAD_DIM), BF16),
        scratch_shapes=[pltpu.VMEM((RET_HEADS, RET_HEAD_DIM, RET_HEAD_DIM), F32)],
        compiler_params=_params(("arbitrary",)),
        name="ret_bstate",
    )(lg, h5, h5)


RET_MAIN_CHUNKS = 2


def _ret_main_body(lg_ref, q_ref, k_ref, v_ref, g_ref, sb_ref, o_ref, st_ref, dec_ref):
    C = RET_C

    @pl.when(pl.program_id(0) == 0)
    def _():
        st_ref[...] = jnp.zeros_like(st_ref)
        diff = (lax.broadcasted_iota(jnp.int32, (C, C), 0)
                - lax.broadcasted_iota(jnp.int32, (C, C), 1)).astype(F32)
        for h in range(RET_HEADS):
            dec_ref[h] = jnp.where(diff >= 0, jnp.exp(_log_decay(lg_ref, 0, h) * jnp.maximum(diff, 0.0)),
                                   jnp.exp(_log_decay(lg_ref, 1, h) * jnp.maximum(-diff, 0.0)))

    ii = _row_index(C)
    for sub in range(q_ref.shape[0] // C):
        rows = slice(sub * C, (sub + 1) * C)
        for h in range(RET_HEADS):
            lo = h * RET_HEAD_DIM
            cols = slice(lo, lo + RET_HEAD_DIM)
            lgf = _log_decay(lg_ref, 0, h)
            lgb = _log_decay(lg_ref, 1, h)
            q = q_ref[rows, cols]
            k = k_ref[rows, cols]
            v = v_ref[rows, cols]
            qf = q.astype(F32)
            s = lax.dot_general(q, k, (((1,), (1,)), ((), ())), preferred_element_type=F32)
            o = jnp.dot((s * dec_ref[h]).astype(BF16), v, preferred_element_type=F32)
            qdf = (qf * jnp.exp((ii + 1.0) * lgf)).astype(BF16)
            o = o + jnp.dot(qdf, st_ref[h].astype(BF16), preferred_element_type=F32)
            qdb = (qf * jnp.exp((C - ii) * lgb)).astype(BF16)
            o = o + jnp.dot(qdb, sb_ref[sub, h], preferred_element_type=F32)
            mu = jnp.mean(o, axis=-1, keepdims=True)
            oc = o - mu
            var = jnp.mean(oc * oc, axis=-1, keepdims=True)
            on = oc * lax.rsqrt(var + HEAD_NORM_EPS)
            o_ref[rows, cols] = (on * g_ref[rows, cols].astype(F32)).astype(BF16)
            kd = (k.astype(F32) * jnp.exp((C - 1.0 - ii) * lgf)).astype(BF16)
            kv = lax.dot_general(kd, v, (((0,), (0,)), ((), ())), preferred_element_type=F32)
            st_ref[h] = st_ref[h] * jnp.exp(C * lgf) + kv


def _ret_main(lg, h5, sb):
    L = h5.shape[1]
    C = RET_C
    per = RET_MAIN_CHUNKS
    steps = L // (per * C)
    seg = lambda s: pl.BlockSpec((None, per * C, RET_WIDTH), lambda i: (s - 1, i, 0))
    return pl.pallas_call(
        _ret_main_body,
        grid=(steps,),
        in_specs=[
            pl.BlockSpec(memory_space=pltpu.SMEM),
            seg(SEG_Q), seg(SEG_K), seg(SEG_V), seg(SEG_GATE),
            pl.BlockSpec((per, RET_HEADS, RET_HEAD_DIM, RET_HEAD_DIM), lambda i: (i, 0, 0, 0)),
        ],
        out_specs=pl.BlockSpec((per * C, RET_WIDTH), lambda i: (i, 0)),
        out_shape=jax.ShapeDtypeStruct((L, RET_WIDTH), BF16),
        scratch_shapes=[pltpu.VMEM((RET_HEADS, RET_HEAD_DIM, RET_HEAD_DIM), F32),
                        pltpu.VMEM((RET_HEADS, C, C), F32)],
        compiler_params=_params(("arbitrary",)),
        name="ret_main",
    )(lg, h5, h5, h5, h5, sb)


def _outproj_even_body(y_ref, r_ref, x_ref, wg_ref, bg_ref, wo_ref, g_ref, b_ref, o_ref):
    nl = y_ref.shape[1]
    tm = S5_T * nl
    shift = S5_T.bit_length() - 1
    y_steps = y_ref[...].reshape(tm, y_ref.shape[2])
    perm = _row_permutation(tm, lambda r: (r & (S5_T - 1)) * nl + lax.shift_right_logical(r, shift))
    for r in range(0, tm, LN_ROWS):
        rows = slice(r, r + LN_ROWS)
        y = jnp.dot(perm[rows, :], y_steps, preferred_element_type=F32).astype(BF16)
        z = jnp.dot(y, wg_ref[...], preferred_element_type=F32) + bg_ref[...]
        s5 = (y.astype(F32) * jax.nn.sigmoid(z)).astype(BF16)
        mix = (jnp.dot(s5, wo_ref[0:S5_WIDTH, :], preferred_element_type=F32)
               + jnp.dot(r_ref[rows, :], wo_ref[S5_WIDTH:S5_WIDTH + RET_WIDTH, :],
                         preferred_element_type=F32))
        o_ref[rows, :] = _layer_norm_rows(DEEPNORM_ALPHA * x_ref[rows, :] + mix, g_ref[...], b_ref[...])


def _outproj_even(y_steps, ret, x, w_glu_bf, b_glu, w_out_bf, ln_g, ln_b, tm=512):
    L = x.shape[0]
    row = lambda n: pl.BlockSpec((tm, n), lambda i: (i, 0))
    full = lambda a: pl.BlockSpec(a.shape, lambda i: (0,) * a.ndim)
    return pl.pallas_call(
        _outproj_even_body,
        grid=(L // tm,),
        in_specs=[pl.BlockSpec((S5_T, tm // S5_T, S5_WIDTH), lambda i: (0, i, 0)),
                  row(RET_WIDTH), row(D_MODEL), full(w_glu_bf), full(b_glu),
                  full(w_out_bf), full(ln_g), full(ln_b)],
        out_specs=row(D_MODEL),
        out_shape=jax.ShapeDtypeStruct((L, D_MODEL), F32),
        compiler_params=_params(("parallel",)),
        name="outproj_even",
    )(y_steps, ret, x, w_glu_bf, b_glu, w_out_bf, ln_g, ln_b)


def _inproj_odd_body(x_ref, w_ref, ca_ref, sa_ref, cb_ref, sb_ref, w1f_ref, w2f_ref,
                     o_ref, w1b_ref, w2b_ref):
    _round_mlp_weight_slices(w1f_ref, w2f_ref, w1b_ref, w2b_ref)
    xb = x_ref[...].astype(BF16)
    tn = ATT_KV_HEADS * ATT_HEAD_DIM
    nq = (ATT_HEADS * ATT_HEAD_DIM) // tn
    cc0, sin_blk = _rotary_block(ca_ref, sa_ref, cb_ref, sb_ref, xb.shape[0])
    lane = lax.broadcasted_iota(jnp.int32, sin_blk.shape, 1)
    s10 = jnp.where(lane < ROPE_DIM // 2, -sin_blk, 0.0)
    s20 = jnp.where((lane >= ROPE_DIM // 2) & (lane < ROPE_DIM), sin_blk, 0.0)
    for cb in range(w_ref.shape[1] // tn):
        acc = jnp.dot(xb, w_ref[:, cb * tn:(cb + 1) * tn], preferred_element_type=F32)
        if cb > nq:
            o_ref[:, cb * tn:(cb + 1) * tn] = acc.astype(BF16)
            continue
        scale = ATT_HEAD_DIM ** -0.5 * LOG2_E if cb < nq else 1.0
        cc = cc0 * scale
        s1 = s10 * scale
        s2 = s20 * scale
        for hh in range(tn // ATT_HEAD_DIM):
            lo = hh * ATT_HEAD_DIM
            a = acc[:, lo:lo + ATT_HEAD_DIM]
            up = pltpu.roll(a, ATT_HEAD_DIM - ROPE_DIM // 2, axis=1)
            dn = pltpu.roll(a, ROPE_DIM // 2, axis=1)
            o_ref[:, cb * tn + lo:cb * tn + lo + ATT_HEAD_DIM] = (a * cc + up * s1 + dn * s2).astype(BF16)


def _inproj_odd(x, w_in_bf, rot, mlp_w1, mlp_w2, layer, tm=256):
    L = x.shape[0]
    n_out = w_in_bf.shape[1]
    row = lambda n: pl.BlockSpec((tm, n), lambda i: (i, 0))
    cast_in, cast_out, cast_shapes = _mlp_weight_cast_specs(L // tm, mlp_w1, mlp_w2, layer)
    return pl.pallas_call(
        _inproj_odd_body,
        grid=(L // tm,),
        in_specs=[
            row(D_MODEL),
            pl.BlockSpec(w_in_bf.shape, lambda i: (0, 0), pipeline_mode=pl.Buffered(1)),
        ] + [pl.BlockSpec(t.shape, lambda i: (0, 0)) for t in rot] + cast_in,
        out_specs=[row(n_out)] + cast_out,
        out_shape=[jax.ShapeDtypeStruct((L, n_out), BF16)] + cast_shapes,
        compiler_params=_params(("parallel",)),
        name="inproj_odd",
    )(x, w_in_bf, *rot, mlp_w1, mlp_w2)


ATT_QB = 4


def _attn_body(sink_ref, q_ref, *refs):
    k_refs = refs[:ATT_QB + 2]
    v_refs = refs[ATT_QB + 2:2 * ATT_QB + 4]
    o_ref, bias_ref = refs[2 * ATT_QB + 4:]
    step = pl.program_id(0)
    nb = pl.num_programs(0) * ATT_QB
    B = ATT_BLOCK
    hd = ATT_HEAD_DIM
    rows = ATT_GROUP * B

    @pl.when(step == 0)
    def _():
        r_i = lax.broadcasted_iota(jnp.int32, (rows, 3 * B), 0)
        s_i = lax.broadcasted_iota(jnp.int32, (rows, 3 * B), 1)
        rel = (r_i & (B - 1)) - s_i + B
        in_win = jnp.abs(rel) <= ATT_WINDOW
        bias_ref[0] = jnp.where(in_win & (s_i >= B), 0.0, NEG_INF)
        bias_ref[1] = jnp.where(in_win, 0.0, NEG_INF)
        bias_ref[2] = jnp.where(in_win & (s_i < 2 * B), 0.0, NEG_INF)

    head_of_row = lax.shift_right_logical(lax.broadcasted_iota(jnp.int32, (rows, 1), 0),
                                          int(math.log2(B)))
    for qb in range(ATT_QB):
        c = step * ATT_QB + qb
        bias = bias_ref[jnp.where(c == 0, 0, jnp.where(c == nb - 1, 2, 1))]
        for g in range(ATT_KV_HEADS):
            q = jnp.concatenate([q_ref[qb * B:(qb + 1) * B, (g * ATT_GROUP + hh) * hd:(g * ATT_GROUP + hh + 1) * hd]
                                 for hh in range(ATT_GROUP)], axis=0)
            ksl = slice(g * hd, (g + 1) * hd)
            k = jnp.concatenate([r[:, ksl] for r in k_refs[qb:qb + 3]], axis=0)
            v = jnp.concatenate([r[:, ksl] for r in v_refs[qb:qb + 3]], axis=0)
            s = lax.dot_general(q, k, (((1,), (1,)), ((), ())), preferred_element_type=F32) + bias
            sink = jnp.zeros((rows, 1), F32)
            for hh in range(ATT_GROUP):
                sink = jnp.where(head_of_row == hh, sink_ref[g * ATT_GROUP + hh], sink)
            sink = sink * LOG2_E
            m = jnp.maximum(jnp.max(s, axis=-1, keepdims=True), sink)
            p = jnp.exp2(s - m)
            den = jnp.sum(p, axis=-1, keepdims=True) + jnp.exp2(sink - m)
            o = jnp.dot(p.astype(BF16), v, preferred_element_type=F32) / den
            for hh in range(ATT_GROUP):
                hcol = (g * ATT_GROUP + hh) * hd
                o_ref[qb * B:(qb + 1) * B, hcol:hcol + hd] = o[hh * B:(hh + 1) * B, :].astype(BF16)


def _attention(sink, qkv):
    L = qkv.shape[0]
    B = ATT_BLOCK
    nb = L // B
    assert nb % ATT_QB == 0 and nb >= 2
    kvw = ATT_KV_HEADS * ATT_HEAD_DIM
    qw = ATT_HEADS * ATT_HEAD_DIM
    kcol = qw // kvw
    vcol = kcol + 1

    def kv_spec(col, off):
        return pl.BlockSpec((B, kvw), lambda i: (jnp.clip(i * ATT_QB + off, 0, nb - 1), col))

    offs = range(-1, ATT_QB + 1)
    return pl.pallas_call(
        _attn_body,
        grid=(nb // ATT_QB,),
        in_specs=[
            pl.BlockSpec(memory_space=pltpu.SMEM),
            pl.BlockSpec((ATT_QB * B, qw), lambda i: (i, 0)),
        ] + [kv_spec(kcol, o) for o in offs] + [kv_spec(vcol, o) for o in offs],
        out_specs=pl.BlockSpec((ATT_QB * B, qw), lambda i: (i, 0)),
        out_shape=jax.ShapeDtypeStruct((L, qw), BF16),
        scratch_shapes=[pltpu.VMEM((3, ATT_GROUP * B, 3 * B), F32)],
        compiler_params=_params(("arbitrary",)),
        name="attention",
    )(sink, qkv, *([qkv] * (2 * ATT_QB + 4)))


def _tapered_row_blocks(tm):
    starts = list(range(0, tm - LN_ROWS, LN_ROWS)) + [tm - LN_ROWS, tm - LN_ROWS // 2]
    return [slice(a, b) for a, b in zip(starts, starts[1:] + [tm])]


def _outproj_odd_body(a_ref, x_ref, wo_ref, g_ref, b_ref, o_ref):
    for rows in _tapered_row_blocks(a_ref.shape[0]):
        mix = jnp.dot(a_ref[rows, :], wo_ref[...], preferred_element_type=F32)
        o_ref[rows, :] = _layer_norm_rows(DEEPNORM_ALPHA * x_ref[rows, :] + mix, g_ref[...], b_ref[...])


def _outproj_odd(a, x, w_out_bf, ln_g, ln_b, tm=512):
    L = x.shape[0]
    row = lambda n: pl.BlockSpec((tm, n), lambda i: (i, 0))
    full = lambda t: pl.BlockSpec(t.shape, lambda i: (0,) * t.ndim)
    return pl.pallas_call(
        _outproj_odd_body,
        grid=(L // tm,),
        in_specs=[row(a.shape[1]), row(D_MODEL),
                  pl.BlockSpec(w_out_bf.shape, lambda i: (0, 0), pipeline_mode=pl.Buffered(1)),
                  full(ln_g), full(ln_b)],
        out_specs=row(D_MODEL),
        out_shape=jax.ShapeDtypeStruct((L, D_MODEL), F32),
        compiler_params=_params(("parallel",)),
        name="outproj_odd",
    )(a, x, w_out_bf, ln_g, ln_b)


def _mlp_body(x_ref, w1_hbm, w2_hbm, g_ref, b_ref, o_ref, xb_ref, acc_ref, w1buf, w2buf, sem):
    n_chunks = w1_hbm.shape[0]
    assert n_chunks % 2 == 0
    step = pl.program_id(0)

    def chunk_copies(c):
        slot = c % 2
        return (pltpu.make_async_copy(w1_hbm.at[c], w1buf.at[slot], sem.at[0, slot]),
                pltpu.make_async_copy(w2_hbm.at[pl.ds(c * MLP_TF, MLP_TF)], w2buf.at[slot], sem.at[1, slot]))

    def start_chunk(c):
        for cp in chunk_copies(c):
            cp.start()

    def partial_sum(rows, w1, w2):
        h = jnp.dot(xb_ref[rows, :], w1, preferred_element_type=F32)
        h = jnp.square(jnp.maximum(h, 0.0)).astype(BF16)
        return jnp.dot(h, w2, preferred_element_type=F32)

    pl.when(step == 0)(lambda: start_chunk(0))
    xb_ref[...] = x_ref[...].astype(BF16)
    for c in range(n_chunks):
        slot = c % 2
        if c + 1 < n_chunks:
            start_chunk(c + 1)
        else:
            pl.when(step + 1 < pl.num_programs(0))(lambda: start_chunk(0))
        for cp in chunk_copies(c):
            cp.wait()
        if c == 0:
            acc_ref[...] = partial_sum(slice(None), w1buf[slot], w2buf[slot])
        elif c + 1 < n_chunks:
            acc_ref[...] += partial_sum(slice(None), w1buf[slot], w2buf[slot])
        else:
            for r in range(0, x_ref.shape[0], LN_ROWS):
                rows = slice(r, r + LN_ROWS)
                y = DEEPNORM_ALPHA * x_ref[rows, :] + (acc_ref[rows, :]
                                                       + partial_sum(rows, w1buf[slot], w2buf[slot]))
                o_ref[rows, :] = _layer_norm_rows(y, g_ref[...], b_ref[...])


def _mlp(x, w1, w2, ln_g, ln_b, tm=512):
    tf = MLP_TF
    L = x.shape[0]
    return pl.pallas_call(
        _mlp_body,
        grid=(L // tm,),
        in_specs=[
            pl.BlockSpec((tm, D_MODEL), lambda i: (i, 0)),
            pl.BlockSpec(memory_space=pl.ANY),
            pl.BlockSpec(memory_space=pl.ANY),
            pl.BlockSpec((1, D_MODEL), lambda i: (0, 0)),
            pl.BlockSpec((1, D_MODEL), lambda i: (0, 0)),
        ],
        out_specs=pl.BlockSpec((tm, D_MODEL), lambda i: (i, 0)),
        out_shape=jax.ShapeDtypeStruct((L, D_MODEL), F32),
        scratch_shapes=[pltpu.VMEM((tm, D_MODEL), BF16), pltpu.VMEM((tm, D_MODEL), F32),
                        pltpu.VMEM((2, D_MODEL, tf), BF16), pltpu.VMEM((2, tf, D_MODEL), BF16),
                        pltpu.SemaphoreType.DMA((2, 2))],
        compiler_params=_params(("arbitrary",)),
        name="mlp",
    )(x, w1, w2, ln_g, ln_b)


def _rotary_tables(L, rot_dim, theta):
    half = rot_dim // 2
    inv_freq = 1.0 / (theta ** (jnp.arange(half, dtype=F32) / half))
    ang_a = (jnp.arange(L // ROT_LO) * ROT_LO).astype(F32)[:, None] * inv_freq[None, :]
    ang_b = jnp.arange(ROT_LO).astype(F32)[:, None] * inv_freq[None, :]
    return jnp.cos(ang_a), jnp.sin(ang_a), jnp.cos(ang_b), jnp.sin(ang_b)


def _even_layer(x, w_in, w_out, lam_re, lam_im, log_step, b_re, b_im, c_re, c_im,
                d_skip, w_glu, b_glu, ret_log_decay, ln_g, ln_b, mlp_w1, mlp_w2, layer, later_weights):
    L = x.shape[0]
    rot = _rotary_tables(L, RET_HEAD_DIM, RET_ROPE_THETA)
    disc = _s5_disc(lam_re, lam_im, log_step)
    m, ws_t, wc, (w_in_bf, w_glu_bf, w_out_bf, *later_bf) = _s5_gen(
        disc, b_re, b_im, c_re, c_im, (w_in, w_glu, w_out) + tuple(later_weights))
    u_steps, h5, w1_bf, w2_bf = _inproj_even(x, w_in_bf, rot, mlp_w1, mlp_w2, layer)
    d_tile = jnp.tile(d_skip.astype(F32), (1, S5_T)).reshape(S5_GROUPS, 1, S5_TL)
    a1, a2, a3 = (disc[k].reshape(2, 1, S5_GROUPS * 128) for k in (DISC_AT_RE, DISC_SCAN_A2, DISC_SCAN_A3))
    y = _s5_mix(u_steps, m, ws_t, wc, d_tile, a1, a2, a3)
    lg = ret_log_decay.astype(F32)
    sb = _ret_bstate(lg, h5)
    ret = _ret_main(lg, h5, sb)
    x1 = _outproj_even(y, ret, x, w_glu_bf, b_glu.astype(F32).reshape(1, -1),
                       w_out_bf, ln_g.reshape(1, -1), ln_b.reshape(1, -1))
    return x1, w1_bf, w2_bf, later_bf


def _odd_layer(x, w_in_bf, w_out_bf, sink, ln_g, ln_b, mlp_w1, mlp_w2, layer):
    L = x.shape[0]
    pad = ATT_HEAD_DIM - ROPE_DIM
    widen = lambda t, fill: jnp.concatenate([t, t, jnp.full((t.shape[0], pad), fill, F32)], axis=1)
    ca, sa, cb, sb = _rotary_tables(L, ROPE_DIM, ROPE_THETA)
    rot = (widen(ca, 1.0), widen(sa, 0.0), widen(cb, 1.0), widen(sb, 0.0))
    qkv, w1_bf, w2_bf = _inproj_odd(x, w_in_bf, rot, mlp_w1, mlp_w2, layer)
    att = _attention(sink.astype(F32), qkv)
    x1 = _outproj_odd(att, x, w_out_bf, ln_g.reshape(1, -1), ln_b.reshape(1, -1))
    return x1, w1_bf, w2_bf


def kernel(x, ln_g, ln_b, mlp_w1, mlp_w2, even_w_in, even_w_out, s5_lambda_re, s5_lambda_im, s5_log_step, s5_b_re, s5_b_im, s5_c_re, s5_c_im, s5_d, s5_w_glu, s5_b_glu, ret_log_decay, odd_w_in, odd_w_out, attn_sink):
    bsz = x.shape[0]
    outs = []
    for b in range(bsz):
        xb = x[b]
        for layer in range(DEPTH):
            if layer % 2 == 0:
                e = layer // 2
                later = (odd_w_in[e], odd_w_out[e]) if layer + 1 < DEPTH else ()
                xb, w1_bf, w2_bf, later_bf = _even_layer(
                    xb, even_w_in[e], even_w_out[e], s5_lambda_re[e], s5_lambda_im[e],
                    s5_log_step[e], s5_b_re[e], s5_b_im[e], s5_c_re[e], s5_c_im[e],
                    s5_d[e], s5_w_glu[e], s5_b_glu[e], ret_log_decay[e],
                    ln_g[layer, 0], ln_b[layer, 0], mlp_w1, mlp_w2, layer, later)
            else:
                o = layer // 2
                xb, w1_bf, w2_bf = _odd_layer(xb, later_bf[0], later_bf[1], attn_sink[o],
                                              ln_g[layer, 0], ln_b[layer, 0], mlp_w1, mlp_w2, layer)
            xb = _mlp(xb, w1_bf, w2_bf,
                      ln_g[layer, 1].reshape(1, -1), ln_b[layer, 1].reshape(1, -1))
        outs.append(xb)
    return jnp.stack(outs, axis=0)
```

```python
import functools
import math

import jax
import jax.numpy as jnp
from jax import lax
from jax.experimental import pallas as pl
from jax.experimental.pallas import tpu as pltpu

F32 = jnp.float32
BF16 = jnp.bfloat16

D_MODEL = 2048
DEPTH = 2
S5_WIDTH = 1024
S5_GROUP = 16
S5_GROUPS = 64
S5_STATE = 64
RET_WIDTH = 1024
RET_HEADS = 4
RET_HEAD_DIM = 256
RET_ROPE_THETA = 10000.0
ATT_HEADS = 16
ATT_KV_HEADS = 4
ATT_HEAD_DIM = 128
ATT_GROUP = 4
ATT_WINDOW = 128
ATT_BLOCK = 128
ROPE_THETA = 500000.0
ROPE_DIM = 32
D_FF = 4 * D_MODEL
DEEPNORM_ALPHA = (2 * DEPTH) ** 0.25
LN_EPS = 1e-5
HEAD_NORM_EPS = 1e-6
NEG_INF = -1e30
LOG2_E = math.log2(math.e)

V7X_VMEM_BYTES = 64 * 1024 * 1024
VMEM_LIMIT = V7X_VMEM_BYTES - 8 * 1024 * 1024

S5_T = 32
S5_TL = S5_T * S5_GROUP
S5_GB = 8
S5_GEN_GB = 4
RET_C = 256
LN_ROWS = 256
MLP_TF = 1024


def _params(sem):
    return pltpu.CompilerParams(dimension_semantics=sem, vmem_limit_bytes=VMEM_LIMIT)


def _layer_norm_rows(y, g, b):
    mu = jnp.mean(y, axis=-1, keepdims=True)
    yc = y - mu
    var = jnp.mean(yc * yc, axis=-1, keepdims=True)
    return yc * lax.rsqrt(var + LN_EPS) * g + b


def _block_transpose8(xs):
    blk = lax.shift_right_logical(lax.broadcasted_iota(jnp.int32, xs[0].shape, 1), 4)
    xs = list(xs)
    for k in range(3):
        d = 1 << k
        upper = (blk & d) != 0
        for i in range(8):
            if i & d:
                continue
            a, b = xs[i], xs[i + d]
            xs[i] = jnp.where(upper, pltpu.roll(b, S5_GROUP * d, axis=1), a)
            xs[i + d] = jnp.where(upper, b, pltpu.roll(a, 128 - S5_GROUP * d, axis=1))
    return xs


ROT_LO = 64


def _rotary_block(ca_ref, sa_ref, cb_ref, sb_ref, tm):
    cb = cb_ref[...]
    sb = sb_ref[...]
    n = tm // ROT_LO
    cos, sin = [], []
    for al in range(n):
        row = pl.ds(pl.program_id(0) * n + al, 1)
        ca = ca_ref[row, :]
        sa = sa_ref[row, :]
        cos.append(ca * cb - sa * sb)
        sin.append(sa * cb + ca * sb)
    return jnp.concatenate(cos, axis=0), jnp.concatenate(sin, axis=0)


SEG_U, SEG_Q, SEG_K, SEG_V, SEG_GATE = range(5)


def _row_permutation(n_rows, src_of_row):
    r = lax.broadcasted_iota(jnp.int32, (n_rows, n_rows), 0)
    c = lax.broadcasted_iota(jnp.int32, (n_rows, n_rows), 1)
    return jnp.where(c == src_of_row(r), 1.0, 0.0).astype(BF16)


def _mlp_weight_cast_specs(n_steps, mlp_w1, mlp_w2, layer):
    ins, outs, shapes = [], [], []
    for w in (mlp_w1, mlp_w2):
        rows, cols = w.shape[1] // n_steps, w.shape[2]
        ins.append(pl.BlockSpec((None, rows, cols), lambda i: (layer, i, 0)))
    d, f = mlp_w1.shape[1:]
    outs.append(pl.BlockSpec((f // MLP_TF, d // n_steps, MLP_TF), lambda i: (0, i, 0)))
    shapes.append(jax.ShapeDtypeStruct((f // MLP_TF, d, MLP_TF), BF16))
    outs.append(pl.BlockSpec((mlp_w2.shape[1] // n_steps, mlp_w2.shape[2]), lambda i: (i, 0)))
    shapes.append(jax.ShapeDtypeStruct(mlp_w2.shape[1:], BF16))
    return ins, outs, shapes


def _round_mlp_weight_slices(w1f_ref, w2f_ref, w1b_ref, w2b_ref):
    for c in range(w1b_ref.shape[0]):
        w1b_ref[c] = w1f_ref[:, c * MLP_TF:(c + 1) * MLP_TF].astype(BF16)
    w2b_ref[...] = w2f_ref[...].astype(BF16)


def _inproj_even_body(x_ref, w_ref, ca_ref, sa_ref, cb_ref, sb_ref, w1f_ref, w2f_ref,
                      u_ref, o_ref, w1b_ref, w2b_ref):
    _round_mlp_weight_slices(w1f_ref, w2f_ref, w1b_ref, w2b_ref)
    xb = x_ref[...].astype(BF16)
    tm = xb.shape[0]
    tn = S5_WIDTH
    half = RET_HEAD_DIM // 2
    cos_blk, sin_blk = _rotary_block(ca_ref, sa_ref, cb_ref, sb_ref, tm)
    for seg in range(w_ref.shape[1] // tn):
        acc = jnp.dot(xb, w_ref[:, seg * tn:(seg + 1) * tn], preferred_element_type=F32)
        if seg == SEG_U:
            nl = tm // S5_T
            shift = nl.bit_length() - 1
            perm = _row_permutation(tm, lambda r: (r & (nl - 1)) * S5_T + lax.shift_right_logical(r, shift))
            up = jnp.dot(perm, acc.astype(BF16), preferred_element_type=F32)
            for t in range(S5_T):
                u_ref[t] = up[t * nl:(t + 1) * nl, :]
        elif seg in (SEG_Q, SEG_K):
            scale = RET_HEAD_DIM ** -0.5 if seg == SEG_K else 1.0
            cos = cos_blk * scale
            sin = sin_blk * scale
            for hh in range(RET_HEADS):
                lo = hh * RET_HEAD_DIM
                a = acc[:, lo:lo + half]
                b = acc[:, lo + half:lo + RET_HEAD_DIM]
                o_ref[seg - 1, :, lo:lo + half] = (a * cos - b * sin).astype(BF16)
                o_ref[seg - 1, :, lo + half:lo + RET_HEAD_DIM] = (b * cos + a * sin).astype(BF16)
        elif seg == SEG_GATE:
            o_ref[seg - 1] = (acc * jax.nn.sigmoid(acc)).astype(BF16)
        else:
            o_ref[seg - 1] = acc.astype(BF16)


def _inproj_even(x, w_in_bf, rot, mlp_w1, mlp_w2, layer, tm=256):
    L = x.shape[0]
    tn = S5_WIDTH
    nseg = w_in_bf.shape[1] // tn
    row = lambda n: pl.BlockSpec((tm, n), lambda i: (i, 0))
    cast_in, cast_out, cast_shapes = _mlp_weight_cast_specs(L // tm, mlp_w1, mlp_w2, layer)
    return pl.pallas_call(
        _inproj_even_body,
        grid=(L // tm,),
        in_specs=[
            row(D_MODEL),
            pl.BlockSpec(w_in_bf.shape, lambda i: (0, 0), pipeline_mode=pl.Buffered(1)),
        ] + [pl.BlockSpec(t.shape, lambda i: (0, 0)) for t in rot] + cast_in,
        out_specs=[pl.BlockSpec((S5_T, tm // S5_T, tn), lambda i: (0, i, 0)),
                   pl.BlockSpec((nseg - 1, tm, tn), lambda i: (0, i, 0))] + cast_out,
        out_shape=[jax.ShapeDtypeStruct((S5_T, L // S5_T, tn), F32),
                   jax.ShapeDtypeStruct((nseg - 1, L, tn), BF16)] + cast_shapes,
        compiler_params=_params(("parallel",)),
        name="inproj_even",
    )(x, w_in_bf, *rot, mlp_w1, mlp_w2)


def _cmul(ar, ai, br, bi):
    return ar * br - ai * bi, ar * bi + ai * br


(DISC_A1_RE, DISC_A1_IM, DISC_A2_RE, DISC_A2_IM, DISC_A4_RE, DISC_A4_IM, DISC_A8_RE, DISC_A8_IM,
 DISC_A16_RE, DISC_A16_IM, DISC_AT_RE, DISC_AT_IM, DISC_Z_RE, DISC_Z_IM, DISC_SCAN_A2, DISC_SCAN_A3) = range(16)


def _s5_disc_body(lr_ref, li_ref, ls_ref, sg_ref, o_ref):
    lr = jnp.minimum(lr_ref[...], -1e-4)
    li = li_ref[...]
    step = jnp.exp(ls_ref[...])
    mag = jnp.exp(lr * step)
    ar = mag * jnp.cos(li * step)
    ai = mag * jnp.sin(li * step)
    nr, ni = ar - 1.0, ai
    den = lr * lr + li * li
    o_ref[DISC_Z_RE] = (nr * lr + ni * li) / den
    o_ref[DISC_Z_IM] = (ni * lr - nr * li) / den
    pr, pi = ar, ai
    for k in range(6):
        o_ref[2 * k] = pr
        o_ref[2 * k + 1] = pi
        if k < 5:
            pr, pi = _cmul(pr, pi, pr, pi)
    o_ref[DISC_SCAN_A2] = pi * sg_ref[...]
    o_ref[DISC_SCAN_A3] = -pi * sg_ref[...]


def _s5_disc(lam_re, lam_im, log_step):
    assert S5_T == 32
    rows = 2 * S5_GROUPS
    two = lambda a: jnp.tile(a.astype(F32).reshape(rows, -1), (1, 2))
    lr = two(lam_re)
    li = two(lam_im)
    ls = jnp.broadcast_to(log_step.astype(F32).reshape(rows, 1), (rows, 2 * S5_STATE))
    sg = jnp.broadcast_to(jnp.concatenate([-jnp.ones((S5_STATE,), F32), jnp.ones((S5_STATE,), F32)])[None],
                          (rows, 2 * S5_STATE))
    return pl.pallas_call(
        _s5_disc_body,
        out_shape=jax.ShapeDtypeStruct((16, rows, 2 * S5_STATE), F32),
        name="s5_disc",
    )(lr, li, ls, sg)


def _s5_gen_body(n_round, col_ref, bre_ref, bim_ref, cre_ref, cim_ref, *refs):
    m_ref, ws_ref, wc_ref = refs[n_round:n_round + 3]
    for src, dst in zip(refs[:n_round], refs[n_round + 3:]):
        dst[...] = src[...].astype(BF16)
    P = S5_STATE
    hi = lax.Precision.HIGHEST
    tlo = lax.shift_right_logical(lax.broadcasted_iota(jnp.int32, (P, 128), 1), 4)
    lane = lax.broadcasted_iota(jnp.int32, (S5_GROUP, S5_TL), 1)
    ones = jnp.ones((P, 128), F32)
    zeros = jnp.zeros((P, 128), F32)
    spread = jnp.where((lax.broadcasted_iota(jnp.int32, (S5_GROUP, 128), 1) & (S5_GROUP - 1))
                       == lax.broadcasted_iota(jnp.int32, (S5_GROUP, 128), 0), 1.0, 0.0)
    contract0 = (((0,), (0,)), ((), ()))

    def one_group(gi, carry):
        kt = []
        for d in range(2):
            col = col_ref[gi, d]
            c = lambda k: jnp.broadcast_to(col[:, k:k + 1], (P, 128))
            a1 = (c(DISC_A1_RE), c(DISC_A1_IM))
            a2 = (c(DISC_A2_RE), c(DISC_A2_IM))
            a4 = (c(DISC_A4_RE), c(DISC_A4_IM))
            a8 = (c(DISC_A8_RE), c(DISC_A8_IM))
            a16 = (c(DISC_A16_RE), c(DISC_A16_IM))
            blk = [None, a8, a16, _cmul(*a8, *a16)]

            def low_powers(reverse):
                xr, xi = ones, zeros
                for k, ak in enumerate((a1, a2, a4)):
                    bit = (lax.shift_right_logical(tlo, k) & 1) == (0 if reverse else 1)
                    yr, yi = _cmul(xr, xi, *ak)
                    xr = jnp.where(bit, yr, xr)
                    xi = jnp.where(bit, yi, xi)
                return xr, xi

            def expand(base, reverse):
                out = []
                for j in range(4):
                    f = blk[3 - j] if reverse else blk[j]
                    out.append(base if f is None else _cmul(*base, *f))
                return out

            ct = tuple(lax.dot_general(r[d, gi], spread, contract0, precision=hi, preferred_element_type=F32)
                       for r in (cre_ref, cim_ref))
            bt = tuple(jnp.dot(r[d, gi], spread, precision=hi, preferred_element_type=F32)
                       for r in (bre_ref, bim_ref))
            bbar = _cmul(c(DISC_Z_RE), c(DISC_Z_IM), *bt)
            ca = expand(_cmul(*ct, *low_powers(d == 1)), d == 1)
            ba = expand(_cmul(*bbar, *low_powers(d == 0)), d == 0)
            wcj = [_cmul(*x, *a1) for x in ca]
            cat = lambda parts, k: jnp.concatenate([x[k] for x in parts], axis=1)
            kt.append(lax.dot_general(bbar[0][:, :S5_GROUP], cat(ca, 0), contract0, precision=hi,
                                      preferred_element_type=F32)
                      - lax.dot_general(bbar[1][:, :S5_GROUP], cat(ca, 1), contract0, precision=hi,
                                        preferred_element_type=F32))
            ba_r, ba_i = cat(ba, 0).astype(BF16), cat(ba, 1).astype(BF16)
            for r, part in enumerate((ba_r, ba_i, ba_i, ba_r)):
                ws_ref[gi, (4 * d + r) * P:(4 * d + r + 1) * P, :] = part
            wc_ref[gi, 2 * d * P:(2 * d + 1) * P, :] = cat(wcj, 0).astype(BF16)
            wc_ref[gi, (2 * d + 1) * P:(2 * d + 2) * P, :] = (-cat(wcj, 1)).astype(BF16)
        ktf, ktb = kt
        for s in range(S5_T):
            lo, hi_lane = S5_GROUP * s, S5_GROUP * (s + 1)
            f = ktf if s == 0 else jnp.where(lane >= lo, pltpu.roll(ktf, lo, axis=1), 0.0)
            b = ktb if s == S5_T - 1 else jnp.where(lane < hi_lane, pltpu.roll(ktb, hi_lane, axis=1), 0.0)
            m_ref[gi, lo:hi_lane, :] = (f + b).astype(BF16)
        return carry

    lax.fori_loop(0, col_ref.shape[0], one_group, 0, unroll=4)


def _s5_gen(disc, b_re, b_im, c_re, c_im, to_round):
    G, P, Cg = S5_GROUPS, S5_STATE, S5_GROUP
    gb = S5_GEN_GB
    steps = G // gb
    col = disc[:, :, :P].reshape(16, 2, G, P).transpose(2, 1, 3, 0)
    per_dir = lambda a: pl.BlockSpec((2, gb) + a.shape[2:], lambda i: (0, i, 0, 0))
    out = lambda rows: pl.BlockSpec((gb, rows, S5_TL), lambda i: (i, 0, 0))
    row_slice = lambda w: pl.BlockSpec((w.shape[0] // steps, w.shape[1]), lambda i: (i, 0))
    res = pl.pallas_call(
        functools.partial(_s5_gen_body, len(to_round)),
        grid=(steps,),
        in_specs=[pl.BlockSpec((gb,) + col.shape[1:], lambda i: (i, 0, 0, 0)),
                  per_dir(b_re), per_dir(b_im), per_dir(c_re), per_dir(c_im)]
                 + [row_slice(w) for w in to_round],
        out_specs=[out(S5_TL), out(8 * P), out(4 * P)] + [row_slice(w) for w in to_round],
        out_shape=[jax.ShapeDtypeStruct((G, S5_TL, S5_TL), BF16),
                   jax.ShapeDtypeStruct((G, 8 * P, S5_TL), BF16),
                   jax.ShapeDtypeStruct((G, 4 * P, S5_TL), BF16)]
                  + [jax.ShapeDtypeStruct(w.shape, BF16) for w in to_round],
        compiler_params=_params(("parallel",)),
        name="s5_gen",
    )(col, b_re.astype(F32), b_im.astype(F32), c_re.astype(F32), c_im.astype(F32), *to_round)
    return res[0], res[1], res[2], tuple(res[3:])


def _s5_body(ut_ref, m_ref, ws_ref, wc_ref, dt_ref, a1_ref, a2_ref, a3_ref, y_ref,
             s_ref, h_ref, yg_ref, u_ref):
    nc = u_ref.shape[1]
    gb = u_ref.shape[0]
    w = gb * 128
    for jb in range(S5_T // 8):
        ys = _block_transpose8([ut_ref[8 * jb + t8] for t8 in range(8)])
        for g8 in range(8):
            u_ref[g8, :, jb * 128:(jb + 1) * 128] = ys[g8].astype(BF16)
    for gi in range(gb):
        s = lax.dot_general(u_ref[gi], ws_ref[gi], (((1,), (1,)), ((), ())),
                            preferred_element_type=F32)
        for r in range(4):
            s_ref[:, r * w + gi * 128:r * w + (gi + 1) * 128] = s[:, r * 128:(r + 1) * 128]

    a1f, a2f, a3f = a1_ref[0], a2_ref[0], a3_ref[0]
    a1b, a2b, a3b = a1_ref[1], a2_ref[1], a3_ref[1]

    def step(n, carry):
        hf, gf, hb, gb_ = carry
        m = nc - 1 - n
        h_ref[pl.ds(n, 1), 0:w] = hf
        h_ref[pl.ds(m, 1), w:2 * w] = hb
        sfh = s_ref[pl.ds(n, 1), 0:w]
        sfg = s_ref[pl.ds(n, 1), w:2 * w]
        sbh = s_ref[pl.ds(m, 1), 2 * w:3 * w]
        sbg = s_ref[pl.ds(m, 1), 3 * w:4 * w]
        hf2 = a1f * hf + a2f * gf + sfh
        gf2 = a1f * gf + a3f * hf + sfg
        hb2 = a1b * hb + a2b * gb_ + sbh
        gb2 = a1b * gb_ + a3b * hb + sbg
        return hf2, gf2, hb2, gb2

    z = jnp.zeros((1, w), F32)
    lax.fori_loop(0, nc, step, (z, z, z, z), unroll=4)

    for gi in range(gb):
        u = u_ref[gi]
        hcat = jnp.concatenate([h_ref[:, gi * 128:(gi + 1) * 128],
                                h_ref[:, w + gi * 128:w + (gi + 1) * 128]], axis=1).astype(BF16)
        y = (jnp.dot(u, m_ref[gi], preferred_element_type=F32)
             + jnp.dot(hcat, wc_ref[gi], preferred_element_type=F32)
             + dt_ref[gi] * u.astype(F32))
        yg_ref[gi] = jax.nn.gelu(y)

    for jb in range(S5_T // 8):
        zs = _block_transpose8([yg_ref[g8, :, jb * 128:(jb + 1) * 128] for g8 in range(8)])
        for t8 in range(8):
            y_ref[8 * jb + t8] = zs[t8].astype(BF16)


def _s5_mix(u_steps, m, ws_t, wc, d_tile, a1, a2, a3):
    _, nc, width = u_steps.shape
    G, TL = S5_GROUPS, S5_TL
    gb = S5_GB
    assert gb * S5_GROUP == 128
    w = gb * 128
    steps_spec = pl.BlockSpec((S5_T, nc, 128), lambda i: (0, 0, i))
    a_spec = pl.BlockSpec((2, 1, w), lambda i: (0, 0, i))
    return pl.pallas_call(
        _s5_body,
        grid=(G // gb,),
        in_specs=[
            steps_spec,
            pl.BlockSpec((gb, TL, TL), lambda i: (i, 0, 0)),
            pl.BlockSpec((gb, 512, TL), lambda i: (i, 0, 0)),
            pl.BlockSpec((gb, 256, TL), lambda i: (i, 0, 0)),
            pl.BlockSpec((gb, 1, TL), lambda i: (i, 0, 0)),
            a_spec, a_spec, a_spec,
        ],
        out_specs=steps_spec,
        out_shape=jax.ShapeDtypeStruct((S5_T, nc, width), BF16),
        scratch_shapes=[pltpu.VMEM((nc, 4 * w), F32), pltpu.VMEM((nc, 2 * w), F32),
                        pltpu.VMEM((gb, nc, TL), F32), pltpu.VMEM((gb, nc, TL), BF16)],
        compiler_params=_params(("parallel",)),
        name="s5_mix",
    )(u_steps, m, ws_t, wc, d_tile, a1, a2, a3)


def _row_index(n):
    return lax.broadcasted_iota(jnp.int32, (n, 1), 0).astype(F32)


def _log_decay(lg_ref, d, h):
    return -jnp.abs(jnp.full((1, 1), lg_ref[d, h], F32))


RET_BSTATE_CHUNKS = 4


def _ret_bstate_body(lg_ref, k_ref, v_ref, sb_ref, st_ref):
    C = RET_C

    @pl.when(pl.program_id(0) == 0)
    def _():
        st_ref[...] = jnp.zeros_like(st_ref)

    jj = _row_index(C)
    for sub in reversed(range(k_ref.shape[0] // C)):
        rows = slice(sub * C, (sub + 1) * C)
        for h in range(RET_HEADS):
            lo = h * RET_HEAD_DIM
            lgb = _log_decay(lg_ref, 1, h)
            sb_ref[sub, h] = st_ref[h].astype(BF16)
            kd = (k_ref[rows, lo:lo + RET_HEAD_DIM].astype(F32) * jnp.exp(jj * lgb)).astype(BF16)
            kv = lax.dot_general(kd, v_ref[rows, lo:lo + RET_HEAD_DIM], (((0,), (0,)), ((), ())),
                                 preferred_element_type=F32)
            st_ref[h] = st_ref[h] * jnp.exp(C * lgb) + kv


def _ret_bstate(lg, h5):
    L = h5.shape[1]
    C = RET_C
    nc = L // C
    per = RET_BSTATE_CHUNKS
    steps = nc // per
    return pl.pallas_call(
        _ret_bstate_body,
        grid=(steps,),
        in_specs=[
            pl.BlockSpec(memory_space=pltpu.SMEM),
            pl.BlockSpec((None, per * C, RET_WIDTH), lambda i: (SEG_K - 1, steps - 1 - i, 0)),
            pl.BlockSpec((None, per * C, RET_WIDTH), lambda i: (SEG_V - 1, steps - 1 - i, 0)),
        ],
        out_specs=pl.BlockSpec((per, RET_HEADS, RET_HEAD_DIM, RET_HEAD_DIM),
                               lambda i: (steps - 1 - i, 0, 0, 0)),
        out_shape=jax.ShapeDtypeStruct((nc, RET_HEADS, RET_HEAD_DIM, RET_HEAD_DIM), BF16),
        scratch_shapes=[pltpu.VMEM((RET_HEADS, RET_HEAD_DIM, RET_HEAD_DIM), F32)],
        compiler_params=_params(("arbitrary",)),
        name="ret_bstate",
    )(lg, h5, h5)


RET_MAIN_CHUNKS = 2


def _ret_main_body(lg_ref, q_ref, k_ref, v_ref, g_ref, sb_ref, o_ref, st_ref, dec_ref):
    C = RET_C

    @pl.when(pl.program_id(0) == 0)
    def _():
        st_ref[...] = jnp.zeros_like(st_ref)
        diff = (lax.broadcasted_iota(jnp.int32, (C, C), 0)
                - lax.broadcasted_iota(jnp.int32, (C, C), 1)).astype(F32)
        for h in range(RET_HEADS):
            dec_ref[h] = jnp.where(diff >= 0, jnp.exp(_log_decay(lg_ref, 0, h) * jnp.maximum(diff, 0.0)),
                                   jnp.exp(_log_decay(lg_ref, 1, h) * jnp.maximum(-diff, 0.0)))

    ii = _row_index(C)
    for sub in range(q_ref.shape[0] // C):
        rows = slice(sub * C, (sub + 1) * C)
        for h in range(RET_HEADS):
            lo = h * RET_HEAD_DIM
            cols = slice(lo, lo + RET_HEAD_DIM)
            lgf = _log_decay(lg_ref, 0, h)
            lgb = _log_decay(lg_ref, 1, h)
            q = q_ref[rows, cols]
            k = k_ref[rows, cols]
            v = v_ref[rows, cols]
            qf = q.astype(F32)
            s = lax.dot_general(q, k, (((1,), (1,)), ((), ())), preferred_element_type=F32)
            o = jnp.dot((s * dec_ref[h]).astype(BF16), v, preferred_element_type=F32)
            qdf = (qf * jnp.exp((ii + 1.0) * lgf)).astype(BF16)
            o = o + jnp.dot(qdf, st_ref[h].astype(BF16), preferred_element_type=F32)
            qdb = (qf * jnp.exp((C - ii) * lgb)).astype(BF16)
            o = o + jnp.dot(qdb, sb_ref[sub, h], preferred_element_type=F32)
            mu = jnp.mean(o, axis=-1, keepdims=True)
            oc = o - mu
            var = jnp.mean(oc * oc, axis=-1, keepdims=True)
            on = oc * lax.rsqrt(var + HEAD_NORM_EPS)
            o_ref[rows, cols] = (on * g_ref[rows, cols].astype(F32)).astype(BF16)
            kd = (k.astype(F32) * jnp.exp((C - 1.0 - ii) * lgf)).astype(BF16)
            kv = lax.dot_general(kd, v, (((0,), (0,)), ((), ())), preferred_element_type=F32)
            st_ref[h] = st_ref[h] * jnp.exp(C * lgf) + kv


def _ret_main(lg, h5, sb):
    L = h5.shape[1]
    C = RET_C
    per = RET_MAIN_CHUNKS
    steps = L // (per * C)
    seg = lambda s: pl.BlockSpec((None, per * C, RET_WIDTH), lambda i: (s - 1, i, 0))
    return pl.pallas_call(
        _ret_main_body,
        grid=(steps,),
        in_specs=[
            pl.BlockSpec(memory_space=pltpu.SMEM),
            seg(SEG_Q), seg(SEG_K), seg(SEG_V), seg(SEG_GATE),
            pl.BlockSpec((per, RET_HEADS, RET_HEAD_DIM, RET_HEAD_DIM), lambda i: (i, 0, 0, 0)),
        ],
        out_specs=pl.BlockSpec((per * C, RET_WIDTH), lambda i: (i, 0)),
        out_shape=jax.ShapeDtypeStruct((L, RET_WIDTH), BF16),
        scratch_shapes=[pltpu.VMEM((RET_HEADS, RET_HEAD_DIM, RET_HEAD_DIM), F32),
                        pltpu.VMEM((RET_HEADS, C, C), F32)],
        compiler_params=_params(("arbitrary",)),
        name="ret_main",
    )(lg, h5, h5, h5, h5, sb)


def _outproj_even_body(y_ref, r_ref, x_ref, wg_ref, bg_ref, wo_ref, g_ref, b_ref, o_ref):
    nl = y_ref.shape[1]
    tm = S5_T * nl
    shift = S5_T.bit_length() - 1
    y_steps = y_ref[...].reshape(tm, y_ref.shape[2])
    perm = _row_permutation(tm, lambda r: (r & (S5_T - 1)) * nl + lax.shift_right_logical(r, shift))
    for r in range(0, tm, LN_ROWS):
        rows = slice(r, r + LN_ROWS)
        y = jnp.dot(perm[rows, :], y_steps, preferred_element_type=F32).astype(BF16)
        z = jnp.dot(y, wg_ref[...], preferred_element_type=F32) + bg_ref[...]
        s5 = (y.astype(F32) * jax.nn.sigmoid(z)).astype(BF16)
        mix = (jnp.dot(s5, wo_ref[0:S5_WIDTH, :], preferred_element_type=F32)
               + jnp.dot(r_ref[rows, :], wo_ref[S5_WIDTH:S5_WIDTH + RET_WIDTH, :],
                         preferred_element_type=F32))
        o_ref[rows, :] = _layer_norm_rows(DEEPNORM_ALPHA * x_ref[rows, :] + mix, g_ref[...], b_ref[...])


def _outproj_even(y_steps, ret, x, w_glu_bf, b_glu, w_out_bf, ln_g, ln_b, tm=512):
    L = x.shape[0]
    row = lambda n: pl.BlockSpec((tm, n), lambda i: (i, 0))
    full = lambda a: pl.BlockSpec(a.shape, lambda i: (0,) * a.ndim)
    return pl.pallas_call(
        _outproj_even_body,
        grid=(L // tm,),
        in_specs=[pl.BlockSpec((S5_T, tm // S5_T, S5_WIDTH), lambda i: (0, i, 0)),
                  row(RET_WIDTH), row(D_MODEL), full(w_glu_bf), full(b_glu),
                  full(w_out_bf), full(ln_g), full(ln_b)],
        out_specs=row(D_MODEL),
        out_shape=jax.ShapeDtypeStruct((L, D_MODEL), F32),
        compiler_params=_params(("parallel",)),
        name="outproj_even",
    )(y_steps, ret, x, w_glu_bf, b_glu, w_out_bf, ln_g, ln_b)


def _inproj_odd_body(x_ref, w_ref, ca_ref, sa_ref, cb_ref, sb_ref, w1f_ref, w2f_ref,
                     o_ref, w1b_ref, w2b_ref):
    _round_mlp_weight_slices(w1f_ref, w2f_ref, w1b_ref, w2b_ref)
    xb = x_ref[...].astype(BF16)
    tn = ATT_KV_HEADS * ATT_HEAD_DIM
    nq = (ATT_HEADS * ATT_HEAD_DIM) // tn
    cc0, sin_blk = _rotary_block(ca_ref, sa_ref, cb_ref, sb_ref, xb.shape[0])
    lane = lax.broadcasted_iota(jnp.int32, sin_blk.shape, 1)
    s10 = jnp.where(lane < ROPE_DIM // 2, -sin_blk, 0.0)
    s20 = jnp.where((lane >= ROPE_DIM // 2) & (lane < ROPE_DIM), sin_blk, 0.0)
    for cb in range(w_ref.shape[1] // tn):
        acc = jnp.dot(xb, w_ref[:, cb * tn:(cb + 1) * tn], preferred_element_type=F32)
        if cb > nq:
            o_ref[:, cb * tn:(cb + 1) * tn] = acc.astype(BF16)
            continue
        scale = ATT_HEAD_DIM ** -0.5 * LOG2_E if cb < nq else 1.0
        cc = cc0 * scale
        s1 = s10 * scale
        s2 = s20 * scale
        for hh in range(tn // ATT_HEAD_DIM):
            lo = hh * ATT_HEAD_DIM
            a = acc[:, lo:lo + ATT_HEAD_DIM]
            up = pltpu.roll(a, ATT_HEAD_DIM - ROPE_DIM // 2, axis=1)
            dn = pltpu.roll(a, ROPE_DIM // 2, axis=1)
            o_ref[:, cb * tn + lo:cb * tn + lo + ATT_HEAD_DIM] = (a * cc + up * s1 + dn * s2).astype(BF16)


def _inproj_odd(x, w_in_bf, rot, mlp_w1, mlp_w2, layer, tm=256):
    L = x.shape[0]
    n_out = w_in_bf.shape[1]
    row = lambda n: pl.BlockSpec((tm, n), lambda i: (i, 0))
    cast_in, cast_out, cast_shapes = _mlp_weight_cast_specs(L // tm, mlp_w1, mlp_w2, layer)
    return pl.pallas_call(
        _inproj_odd_body,
        grid=(L // tm,),
        in_specs=[
            row(D_MODEL),
            pl.BlockSpec(w_in_bf.shape, lambda i: (0, 0), pipeline_mode=pl.Buffered(1)),
        ] + [pl.BlockSpec(t.shape, lambda i: (0, 0)) for t in rot] + cast_in,
        out_specs=[row(n_out)] + cast_out,
        out_shape=[jax.ShapeDtypeStruct((L, n_out), BF16)] + cast_shapes,
        compiler_params=_params(("parallel",)),
        name="inproj_odd",
    )(x, w_in_bf, *rot, mlp_w1, mlp_w2)


ATT_QB = 4
ATT_OUT_QB = 2


def _attn_body(sink_ref, q_ref, *refs):
    k_refs = refs[:ATT_QB + 2]
    v_refs = refs[ATT_QB + 2:2 * ATT_QB + 4]
    x_ref, wo_ref, g_ref, b_ref, o_ref, bias_ref, att_ref = refs[2 * ATT_QB + 4:]
    step = pl.program_id(0)
    nb = pl.num_programs(0) * ATT_QB
    B = ATT_BLOCK
    hd = ATT_HEAD_DIM
    rows = ATT_GROUP * B

    @pl.when(step == 0)
    def _():
        r_i = lax.broadcasted_iota(jnp.int32, (rows, 3 * B), 0)
        s_i = lax.broadcasted_iota(jnp.int32, (rows, 3 * B), 1)
        rel = (r_i & (B - 1)) - s_i + B
        in_win = jnp.abs(rel) <= ATT_WINDOW
        bias_ref[0] = jnp.where(in_win & (s_i >= B), 0.0, NEG_INF)
        bias_ref[1] = jnp.where(in_win, 0.0, NEG_INF)
        bias_ref[2] = jnp.where(in_win & (s_i < 2 * B), 0.0, NEG_INF)

    head_of_row = lax.shift_right_logical(lax.broadcasted_iota(jnp.int32, (rows, 1), 0),
                                          int(math.log2(B)))
    for qb in range(ATT_QB):
        c = step * ATT_QB + qb
        bias = bias_ref[jnp.where(c == 0, 0, jnp.where(c == nb - 1, 2, 1))]
        for g in range(ATT_KV_HEADS):
            q = jnp.concatenate([q_ref[qb * B:(qb + 1) * B, (g * ATT_GROUP + hh) * hd:(g * ATT_GROUP + hh + 1) * hd]
                                 for hh in range(ATT_GROUP)], axis=0)
            ksl = slice(g * hd, (g + 1) * hd)
            k = jnp.concatenate([r[:, ksl] for r in k_refs[qb:qb + 3]], axis=0)
            v = jnp.concatenate([r[:, ksl] for r in v_refs[qb:qb + 3]], axis=0)
            s = lax.dot_general(q, k, (((1,), (1,)), ((), ())), preferred_element_type=F32) + bias
            sink = jnp.zeros((rows, 1), F32)
            for hh in range(ATT_GROUP):
                sink = jnp.where(head_of_row == hh, sink_ref[g * ATT_GROUP + hh], sink)
            sink = sink * LOG2_E
            m = jnp.maximum(jnp.max(s, axis=-1, keepdims=True), sink)
            p = jnp.exp2(s - m)
            den = jnp.sum(p, axis=-1, keepdims=True) + jnp.exp2(sink - m)
            o = jnp.dot(p.astype(BF16), v, preferred_element_type=F32) / den
            for hh in range(ATT_GROUP):
                hcol = (g * ATT_GROUP + hh) * hd
                att_ref[qb * B:(qb + 1) * B, hcol:hcol + hd] = o[hh * B:(hh + 1) * B, :].astype(BF16)
        if (qb + 1) % ATT_OUT_QB == 0:
            done = slice((qb + 1 - ATT_OUT_QB) * B, (qb + 1) * B)
            mix = jnp.dot(att_ref[done, :], wo_ref[...], preferred_element_type=F32)
            o_ref[done, :] = _layer_norm_rows(DEEPNORM_ALPHA * x_ref[done, :] + mix, g_ref[...], b_ref[...])


def _attention_outproj(sink, qkv, x, w_out_bf, ln_g, ln_b):
    L = qkv.shape[0]
    B = ATT_BLOCK
    nb = L // B
    assert nb % ATT_QB == 0 and nb >= 2 and ATT_QB % ATT_OUT_QB == 0
    kvw = ATT_KV_HEADS * ATT_HEAD_DIM
    qw = ATT_HEADS * ATT_HEAD_DIM
    kcol = qw // kvw
    vcol = kcol + 1
    tm = ATT_QB * B

    def kv_spec(col, off):
        return pl.BlockSpec((B, kvw), lambda i: (jnp.clip(i * ATT_QB + off, 0, nb - 1), col))

    offs = range(-1, ATT_QB + 1)
    vec = pl.BlockSpec((1, D_MODEL), lambda i: (0, 0))
    return pl.pallas_call(
        _attn_body,
        grid=(nb // ATT_QB,),
        in_specs=[
            pl.BlockSpec(memory_space=pltpu.SMEM),
            pl.BlockSpec((tm, qw), lambda i: (i, 0)),
        ] + [kv_spec(kcol, o) for o in offs] + [kv_spec(vcol, o) for o in offs] + [
            pl.BlockSpec((tm, D_MODEL), lambda i: (i, 0)),
            pl.BlockSpec(w_out_bf.shape, lambda i: (0, 0), pipeline_mode=pl.Buffered(1)),
            vec, vec,
        ],
        out_specs=pl.BlockSpec((tm, D_MODEL), lambda i: (i, 0)),
        out_shape=jax.ShapeDtypeStruct((L, D_MODEL), F32),
        scratch_shapes=[pltpu.VMEM((3, ATT_GROUP * B, 3 * B), F32), pltpu.VMEM((tm, qw), BF16)],
        compiler_params=_params(("arbitrary",)),
        name="attention_outproj",
    )(sink, qkv, *([qkv] * (2 * ATT_QB + 4)), x, w_out_bf, ln_g, ln_b)


def _mlp_body(x_ref, w1_ref, w2_ref, g_ref, b_ref, o_ref, xb_ref, acc_ref):
    f = pl.program_id(1)

    def partial_sum(rows):
        h = jnp.dot(xb_ref[rows, :], w1_ref[...], preferred_element_type=F32)
        h = jnp.square(jnp.maximum(h, 0.0)).astype(BF16)
        return jnp.dot(h, w2_ref[...], preferred_element_type=F32)

    last = pl.num_programs(1) - 1

    @pl.when(f == 0)
    def _():
        xb_ref[...] = x_ref[...].astype(BF16)
        acc_ref[...] = partial_sum(slice(None))

    @pl.when((f > 0) & (f < last))
    def _():
        acc_ref[...] += partial_sum(slice(None))

    @pl.when(f == last)
    def _():
        for r in range(0, x_ref.shape[0], LN_ROWS):
            rows = slice(r, r + LN_ROWS)
            y = DEEPNORM_ALPHA * x_ref[rows, :] + (acc_ref[rows, :] + partial_sum(rows))
            o_ref[rows, :] = _layer_norm_rows(y, g_ref[...], b_ref[...])


def _mlp(x, w1, w2, ln_g, ln_b, tm=512):
    tf = MLP_TF
    L = x.shape[0]
    return pl.pallas_call(
        _mlp_body,
        grid=(L // tm, D_FF // tf),
        in_specs=[
            pl.BlockSpec((tm, D_MODEL), lambda i, f: (i, 0)),
            pl.BlockSpec((None, D_MODEL, tf), lambda i, f: (f, 0, 0)),
            pl.BlockSpec((tf, D_MODEL), lambda i, f: (f, 0)),
            pl.BlockSpec((1, D_MODEL), lambda i, f: (0, 0)),
            pl.BlockSpec((1, D_MODEL), lambda i, f: (0, 0)),
        ],
        out_specs=pl.BlockSpec((tm, D_MODEL), lambda i, f: (i, 0)),
        out_shape=jax.ShapeDtypeStruct((L, D_MODEL), F32),
        scratch_shapes=[pltpu.VMEM((tm, D_MODEL), BF16), pltpu.VMEM((tm, D_MODEL), F32)],
        compiler_params=_params(("parallel", "arbitrary")),
        name="mlp",
    )(x, w1, w2, ln_g, ln_b)


def _rotary_tables(L, rot_dim, theta):
    half = rot_dim // 2
    inv_freq = 1.0 / (theta ** (jnp.arange(half, dtype=F32) / half))
    ang_a = (jnp.arange(L // ROT_LO) * ROT_LO).astype(F32)[:, None] * inv_freq[None, :]
    ang_b = jnp.arange(ROT_LO).astype(F32)[:, None] * inv_freq[None, :]
    return jnp.cos(ang_a), jnp.sin(ang_a), jnp.cos(ang_b), jnp.sin(ang_b)


def _even_layer(x, w_in, w_out, lam_re, lam_im, log_step, b_re, b_im, c_re, c_im,
                d_skip, w_glu, b_glu, ret_log_decay, ln_g, ln_b, mlp_w1, mlp_w2, layer, later_weights):
    L = x.shape[0]
    rot = _rotary_tables(L, RET_HEAD_DIM, RET_ROPE_THETA)
    disc = _s5_disc(lam_re, lam_im, log_step)
    m, ws_t, wc, (w_in_bf, w_glu_bf, w_out_bf, *later_bf) = _s5_gen(
        disc, b_re, b_im, c_re, c_im, (w_in, w_glu, w_out) + tuple(later_weights))
    u_steps, h5, w1_bf, w2_bf = _inproj_even(x, w_in_bf, rot, mlp_w1, mlp_w2, layer)
    d_tile = jnp.tile(d_skip.astype(F32), (1, S5_T)).reshape(S5_GROUPS, 1, S5_TL)
    a1, a2, a3 = (disc[k].reshape(2, 1, S5_GROUPS * 128) for k in (DISC_AT_RE, DISC_SCAN_A2, DISC_SCAN_A3))
    y = _s5_mix(u_steps, m, ws_t, wc, d_tile, a1, a2, a3)
    lg = ret_log_decay.astype(F32)
    sb = _ret_bstate(lg, h5)
    ret = _ret_main(lg, h5, sb)
    x1 = _outproj_even(y, ret, x, w_glu_bf, b_glu.astype(F32).reshape(1, -1),
                       w_out_bf, ln_g.reshape(1, -1), ln_b.reshape(1, -1))
    return x1, w1_bf, w2_bf, later_bf


def _odd_layer(x, w_in_bf, w_out_bf, sink, ln_g, ln_b, mlp_w1, mlp_w2, layer):
    L = x.shape[0]
    pad = ATT_HEAD_DIM - ROPE_DIM
    widen = lambda t, fill: jnp.concatenate([t, t, jnp.full((t.shape[0], pad), fill, F32)], axis=1)
    ca, sa, cb, sb = _rotary_tables(L, ROPE_DIM, ROPE_THETA)
    rot = (widen(ca, 1.0), widen(sa, 0.0), widen(cb, 1.0), widen(sb, 0.0))
    qkv, w1_bf, w2_bf = _inproj_odd(x, w_in_bf, rot, mlp_w1, mlp_w2, layer)
    x1 = _attention_outproj(sink.astype(F32), qkv, x, w_out_bf, ln_g.reshape(1, -1), ln_b.reshape(1, -1))
    return x1, w1_bf, w2_bf


def kernel(x, ln_g, ln_b, mlp_w1, mlp_w2, even_w_in, even_w_out, s5_lambda_re, s5_lambda_im, s5_log_step, s5_b_re, s5_b_im, s5_c_re, s5_c_im, s5_d, s5_w_glu, s5_b_glu, ret_log_decay, odd_w_in, odd_w_out, attn_sink):
    bsz = x.shape[0]
    outs = []
    for b in range(bsz):
        xb = x[b]
        for layer in range(DEPTH):
            if layer % 2 == 0:
                e = layer // 2
                later = (odd_w_in[e], odd_w_out[e]) if layer + 1 < DEPTH else ()
                xb, w1_bf, w2_bf, later_bf = _even_layer(
                    xb, even_w_in[e], even_w_out[e], s5_lambda_re[e], s5_lambda_im[e],
                    s5_log_step[e], s5_b_re[e], s5_b_im[e], s5_c_re[e], s5_c_im[e],
                    s5_d[e], s5_w_glu[e], s5_b_glu[e], ret_log_decay[e],
                    ln_g[layer, 0], ln_b[layer, 0], mlp_w1, mlp_w2, layer, later)
            else:
                o = layer // 2
                xb, w1_bf, w2_bf = _odd_layer(xb, later_bf[0], later_bf[1], attn_sink[o],
                                              ln_g[layer, 0], ln_b[layer, 0], mlp_w1, mlp_w2, layer)
            xb = _mlp(xb, w1_bf, w2_bf,
                      ln_g[layer, 1].reshape(1, -1), ln_b[layer, 1].reshape(1, -1))
        outs.append(xb)
    return jnp.stack(outs, axis=0)
```

```python
import functools
import math

import jax
import jax.numpy as jnp
from jax import lax
from jax.experimental import pallas as pl
from jax.experimental.pallas import tpu as pltpu

F32 = jnp.float32
BF16 = jnp.bfloat16

D_MODEL = 2048
DEPTH = 2
S5_WIDTH = 1024
S5_GROUP = 16
S5_GROUPS = 64
S5_STATE = 64
RET_WIDTH = 1024
RET_HEADS = 4
RET_HEAD_DIM = 256
RET_ROPE_THETA = 10000.0
ATT_HEADS = 16
ATT_KV_HEADS = 4
ATT_HEAD_DIM = 128
ATT_GROUP = 4
ATT_WINDOW = 128
ATT_BLOCK = 128
ROPE_THETA = 500000.0
ROPE_DIM = 32
D_FF = 4 * D_MODEL
DEEPNORM_ALPHA = (2 * DEPTH) ** 0.25
LN_EPS = 1e-5
HEAD_NORM_EPS = 1e-6
NEG_INF = -1e30
LOG2_E = math.log2(math.e)

V7X_VMEM_BYTES = 64 * 1024 * 1024
VMEM_LIMIT = V7X_VMEM_BYTES - 8 * 1024 * 1024

S5_T = 32
S5_TL = S5_T * S5_GROUP
S5_GB = 8
S5_GEN_GB = 4
RET_C = 256
LN_ROWS = 256
MLP_TF = 1024


def _params(sem):
    return pltpu.CompilerParams(dimension_semantics=sem, vmem_limit_bytes=VMEM_LIMIT)


def _layer_norm_rows(y, g, b):
    mu = jnp.mean(y, axis=-1, keepdims=True)
    yc = y - mu
    var = jnp.mean(yc * yc, axis=-1, keepdims=True)
    return yc * lax.rsqrt(var + LN_EPS) * g + b


def _block_transpose8(xs):
    blk = lax.shift_right_logical(lax.broadcasted_iota(jnp.int32, xs[0].shape, 1), 4)
    xs = list(xs)
    for k in range(3):
        d = 1 << k
        upper = (blk & d) != 0
        for i in range(8):
            if i & d:
                continue
            a, b = xs[i], xs[i + d]
            xs[i] = jnp.where(upper, pltpu.roll(b, S5_GROUP * d, axis=1), a)
            xs[i + d] = jnp.where(upper, b, pltpu.roll(a, 128 - S5_GROUP * d, axis=1))
    return xs


ROT_LO = 64


def _rotary_block(ca_ref, sa_ref, cb_ref, sb_ref, tm):
    cb = cb_ref[...]
    sb = sb_ref[...]
    n = tm // ROT_LO
    cos, sin = [], []
    for al in range(n):
        row = pl.ds(pl.program_id(0) * n + al, 1)
        ca = ca_ref[row, :]
        sa = sa_ref[row, :]
        cos.append(ca * cb - sa * sb)
        sin.append(sa * cb + ca * sb)
    return jnp.concatenate(cos, axis=0), jnp.concatenate(sin, axis=0)


SEG_U, SEG_Q, SEG_K, SEG_V, SEG_GATE = range(5)


def _row_permutation(n_rows, src_of_row):
    r = lax.broadcasted_iota(jnp.int32, (n_rows, n_rows), 0)
    c = lax.broadcasted_iota(jnp.int32, (n_rows, n_rows), 1)
    return jnp.where(c == src_of_row(r), 1.0, 0.0).astype(BF16)


def _mlp_weight_cast_specs(n_steps, mlp_w1, mlp_w2, layer):
    ins, outs, shapes = [], [], []
    for w in (mlp_w1, mlp_w2):
        rows, cols = w.shape[1] // n_steps, w.shape[2]
        ins.append(pl.BlockSpec((None, rows, cols), lambda i: (layer, i, 0)))
    d, f = mlp_w1.shape[1:]
    outs.append(pl.BlockSpec((f // MLP_TF, d // n_steps, MLP_TF), lambda i: (0, i, 0)))
    shapes.append(jax.ShapeDtypeStruct((f // MLP_TF, d, MLP_TF), BF16))
    outs.append(pl.BlockSpec((mlp_w2.shape[1] // n_steps, mlp_w2.shape[2]), lambda i: (i, 0)))
    shapes.append(jax.ShapeDtypeStruct(mlp_w2.shape[1:], BF16))
    return ins, outs, shapes


def _round_mlp_weight_slices(w1f_ref, w2f_ref, w1b_ref, w2b_ref):
    for c in range(w1b_ref.shape[0]):
        w1b_ref[c] = w1f_ref[:, c * MLP_TF:(c + 1) * MLP_TF].astype(BF16)
    w2b_ref[...] = w2f_ref[...].astype(BF16)


def _inproj_even_body(x_ref, w_ref, ca_ref, sa_ref, cb_ref, sb_ref, w1f_ref, w2f_ref,
                      u_ref, o_ref, w1b_ref, w2b_ref):
    _round_mlp_weight_slices(w1f_ref, w2f_ref, w1b_ref, w2b_ref)
    xb = x_ref[...].astype(BF16)
    tm = xb.shape[0]
    tn = S5_WIDTH
    half = RET_HEAD_DIM // 2
    cos_blk, sin_blk = _rotary_block(ca_ref, sa_ref, cb_ref, sb_ref, tm)
    for seg in range(w_ref.shape[1] // tn):
        acc = jnp.dot(xb, w_ref[:, seg * tn:(seg + 1) * tn], preferred_element_type=F32)
        if seg == SEG_U:
            nl = tm // S5_T
            shift = nl.bit_length() - 1
            perm = _row_permutation(tm, lambda r: (r & (nl - 1)) * S5_T + lax.shift_right_logical(r, shift))
            up = jnp.dot(perm, acc.astype(BF16), preferred_element_type=F32)
            for t in range(S5_T):
                u_ref[t] = up[t * nl:(t + 1) * nl, :]
        elif seg in (SEG_Q, SEG_K):
            scale = RET_HEAD_DIM ** -0.5 if seg == SEG_K else 1.0
            cos = cos_blk * scale
            sin = sin_blk * scale
            for hh in range(RET_HEADS):
                lo = hh * RET_HEAD_DIM
                a = acc[:, lo:lo + half]
                b = acc[:, lo + half:lo + RET_HEAD_DIM]
                o_ref[seg - 1, :, lo:lo + half] = (a * cos - b * sin).astype(BF16)
                o_ref[seg - 1, :, lo + half:lo + RET_HEAD_DIM] = (b * cos + a * sin).astype(BF16)
        elif seg == SEG_GATE:
            o_ref[seg - 1] = (acc * jax.nn.sigmoid(acc)).astype(BF16)
        else:
            o_ref[seg - 1] = acc.astype(BF16)


def _inproj_even(x, w_in_bf, rot, mlp_w1, mlp_w2, layer, tm=256):
    L = x.shape[0]
    tn = S5_WIDTH
    nseg = w_in_bf.shape[1] // tn
    row = lambda n: pl.BlockSpec((tm, n), lambda i: (i, 0))
    cast_in, cast_out, cast_shapes = _mlp_weight_cast_specs(L // tm, mlp_w1, mlp_w2, layer)
    return pl.pallas_call(
        _inproj_even_body,
        grid=(L // tm,),
        in_specs=[
            row(D_MODEL),
            pl.BlockSpec(w_in_bf.shape, lambda i: (0, 0), pipeline_mode=pl.Buffered(1)),
        ] + [pl.BlockSpec(t.shape, lambda i: (0, 0)) for t in rot] + cast_in,
        out_specs=[pl.BlockSpec((S5_T, tm // S5_T, tn), lambda i: (0, i, 0)),
                   pl.BlockSpec((nseg - 1, tm, tn), lambda i: (0, i, 0))] + cast_out,
        out_shape=[jax.ShapeDtypeStruct((S5_T, L // S5_T, tn), F32),
                   jax.ShapeDtypeStruct((nseg - 1, L, tn), BF16)] + cast_shapes,
        compiler_params=_params(("parallel",)),
        name="inproj_even",
    )(x, w_in_bf, *rot, mlp_w1, mlp_w2)


def _cmul(ar, ai, br, bi):
    return ar * br - ai * bi, ar * bi + ai * br


(DISC_A1_RE, DISC_A1_IM, DISC_A2_RE, DISC_A2_IM, DISC_A4_RE, DISC_A4_IM, DISC_A8_RE, DISC_A8_IM,
 DISC_A16_RE, DISC_A16_IM, DISC_AT_RE, DISC_AT_IM, DISC_Z_RE, DISC_Z_IM, DISC_SCAN_A2, DISC_SCAN_A3) = range(16)


def _s5_disc_body(lr_ref, li_ref, ls_ref, sg_ref, o_ref):
    lr = jnp.minimum(lr_ref[...], -1e-4)
    li = li_ref[...]
    step = jnp.exp(ls_ref[...])
    mag = jnp.exp(lr * step)
    ar = mag * jnp.cos(li * step)
    ai = mag * jnp.sin(li * step)
    nr, ni = ar - 1.0, ai
    den = lr * lr + li * li
    o_ref[DISC_Z_RE] = (nr * lr + ni * li) / den
    o_ref[DISC_Z_IM] = (ni * lr - nr * li) / den
    pr, pi = ar, ai
    for k in range(6):
        o_ref[2 * k] = pr
        o_ref[2 * k + 1] = pi
        if k < 5:
            pr, pi = _cmul(pr, pi, pr, pi)
    o_ref[DISC_SCAN_A2] = pi * sg_ref[...]
    o_ref[DISC_SCAN_A3] = -pi * sg_ref[...]


def _s5_disc(lam_re, lam_im, log_step):
    assert S5_T == 32
    rows = 2 * S5_GROUPS
    two = lambda a: jnp.tile(a.astype(F32).reshape(rows, -1), (1, 2))
    lr = two(lam_re)
    li = two(lam_im)
    ls = jnp.broadcast_to(log_step.astype(F32).reshape(rows, 1), (rows, 2 * S5_STATE))
    sg = jnp.broadcast_to(jnp.concatenate([-jnp.ones((S5_STATE,), F32), jnp.ones((S5_STATE,), F32)])[None],
                          (rows, 2 * S5_STATE))
    return pl.pallas_call(
        _s5_disc_body,
        out_shape=jax.ShapeDtypeStruct((16, rows, 2 * S5_STATE), F32),
        name="s5_disc",
    )(lr, li, ls, sg)


def _s5_gen_body(n_round, col_ref, bre_ref, bim_ref, cre_ref, cim_ref, *refs):
    m_ref, ws_ref, wc_ref = refs[n_round:n_round + 3]
    for src, dst in zip(refs[:n_round], refs[n_round + 3:]):
        dst[...] = src[...].astype(BF16)
    P = S5_STATE
    hi = lax.Precision.HIGHEST
    tlo = lax.shift_right_logical(lax.broadcasted_iota(jnp.int32, (P, 128), 1), 4)
    lane = lax.broadcasted_iota(jnp.int32, (S5_GROUP, S5_TL), 1)
    ones = jnp.ones((P, 128), F32)
    zeros = jnp.zeros((P, 128), F32)
    spread = jnp.where((lax.broadcasted_iota(jnp.int32, (S5_GROUP, 128), 1) & (S5_GROUP - 1))
                       == lax.broadcasted_iota(jnp.int32, (S5_GROUP, 128), 0), 1.0, 0.0)
    contract0 = (((0,), (0,)), ((), ()))

    def one_group(gi, carry):
        kt = []
        for d in range(2):
            col = col_ref[gi, d]
            c = lambda k: jnp.broadcast_to(col[:, k:k + 1], (P, 128))
            a1 = (c(DISC_A1_RE), c(DISC_A1_IM))
            a2 = (c(DISC_A2_RE), c(DISC_A2_IM))
            a4 = (c(DISC_A4_RE), c(DISC_A4_IM))
            a8 = (c(DISC_A8_RE), c(DISC_A8_IM))
            a16 = (c(DISC_A16_RE), c(DISC_A16_IM))
            blk = [None, a8, a16, _cmul(*a8, *a16)]

            def low_powers(reverse):
                xr, xi = ones, zeros
                for k, ak in enumerate((a1, a2, a4)):
                    bit = (lax.shift_right_logical(tlo, k) & 1) == (0 if reverse else 1)
                    yr, yi = _cmul(xr, xi, *ak)
                    xr = jnp.where(bit, yr, xr)
                    xi = jnp.where(bit, yi, xi)
                return xr, xi

            def expand(base, reverse):
                out = []
                for j in range(4):
                    f = blk[3 - j] if reverse else blk[j]
                    out.append(base if f is None else _cmul(*base, *f))
                return out

            ct = tuple(lax.dot_general(r[d, gi], spread, contract0, precision=hi, preferred_element_type=F32)
                       for r in (cre_ref, cim_ref))
            bt = tuple(jnp.dot(r[d, gi], spread, precision=hi, preferred_element_type=F32)
                       for r in (bre_ref, bim_ref))
            bbar = _cmul(c(DISC_Z_RE), c(DISC_Z_IM), *bt)
            ca = expand(_cmul(*ct, *low_powers(d == 1)), d == 1)
            ba = expand(_cmul(*bbar, *low_powers(d == 0)), d == 0)
            wcj = [_cmul(*x, *a1) for x in ca]
            cat = lambda parts, k: jnp.concatenate([x[k] for x in parts], axis=1)
            kt.append(lax.dot_general(bbar[0][:, :S5_GROUP], cat(ca, 0), contract0, precision=hi,
                                      preferred_element_type=F32)
                      - lax.dot_general(bbar[1][:, :S5_GROUP], cat(ca, 1), contract0, precision=hi,
                                        preferred_element_type=F32))
            ba_r, ba_i = cat(ba, 0).astype(BF16), cat(ba, 1).astype(BF16)
            for r, part in enumerate((ba_r, ba_i, ba_i, ba_r)):
                ws_ref[gi, (4 * d + r) * P:(4 * d + r + 1) * P, :] = part
            wc_ref[gi, 2 * d * P:(2 * d + 1) * P, :] = cat(wcj, 0).astype(BF16)
            wc_ref[gi, (2 * d + 1) * P:(2 * d + 2) * P, :] = (-cat(wcj, 1)).astype(BF16)
        ktf, ktb = kt
        for s in range(S5_T):
            lo, hi_lane = S5_GROUP * s, S5_GROUP * (s + 1)
            f = ktf if s == 0 else jnp.where(lane >= lo, pltpu.roll(ktf, lo, axis=1), 0.0)
            b = ktb if s == S5_T - 1 else jnp.where(lane < hi_lane, pltpu.roll(ktb, hi_lane, axis=1), 0.0)
            m_ref[gi, lo:hi_lane, :] = (f + b).astype(BF16)
        return carry

    lax.fori_loop(0, col_ref.shape[0], one_group, 0, unroll=4)


def _s5_gen(disc, b_re, b_im, c_re, c_im, to_round):
    G, P, Cg = S5_GROUPS, S5_STATE, S5_GROUP
    gb = S5_GEN_GB
    steps = G // gb
    col = disc[:, :, :P].reshape(16, 2, G, P).transpose(2, 1, 3, 0)
    per_dir = lambda a: pl.BlockSpec((2, gb) + a.shape[2:], lambda i: (0, i, 0, 0))
    out = lambda rows: pl.BlockSpec((gb, rows, S5_TL), lambda i: (i, 0, 0))
    row_slice = lambda w: pl.BlockSpec((w.shape[0] // steps, w.shape[1]), lambda i: (i, 0))
    res = pl.pallas_call(
        functools.partial(_s5_gen_body, len(to_round)),
        grid=(steps,),
        in_specs=[pl.BlockSpec((gb,) + col.shape[1:], lambda i: (i, 0, 0, 0)),
                  per_dir(b_re), per_dir(b_im), per_dir(c_re), per_dir(c_im)]
                 + [row_slice(w) for w in to_round],
        out_specs=[out(S5_TL), out(8 * P), out(4 * P)] + [row_slice(w) for w in to_round],
        out_shape=[jax.ShapeDtypeStruct((G, S5_TL, S5_TL), BF16),
                   jax.ShapeDtypeStruct((G, 8 * P, S5_TL), BF16),
                   jax.ShapeDtypeStruct((G, 4 * P, S5_TL), BF16)]
                  + [jax.ShapeDtypeStruct(w.shape, BF16) for w in to_round],
        compiler_params=_params(("parallel",)),
        name="s5_gen",
    )(col, b_re.astype(F32), b_im.astype(F32), c_re.astype(F32), c_im.astype(F32), *to_round)
    return res[0], res[1], res[2], tuple(res[3:])


def _s5_body(ut_ref, m_ref, ws_ref, wc_ref, dt_ref, a1_ref, a2_ref, a3_ref, y_ref,
             s_ref, h_ref, yg_ref, u_ref):
    nc = u_ref.shape[1]
    gb = u_ref.shape[0]
    w = gb * 128
    for jb in range(S5_T // 8):
        ys = _block_transpose8([ut_ref[8 * jb + t8] for t8 in range(8)])
        for g8 in range(8):
            u_ref[g8, :, jb * 128:(jb + 1) * 128] = ys[g8].astype(BF16)
    for gi in range(gb):
        s = lax.dot_general(u_ref[gi], ws_ref[gi], (((1,), (1,)), ((), ())),
                            preferred_element_type=F32)
        for r in range(4):
            s_ref[:, r * w + gi * 128:r * w + (gi + 1) * 128] = s[:, r * 128:(r + 1) * 128]

    a1f, a2f, a3f = a1_ref[0], a2_ref[0], a3_ref[0]
    a1b, a2b, a3b = a1_ref[1], a2_ref[1], a3_ref[1]

    def step(n, carry):
        hf, gf, hb, gb_ = carry
        m = nc - 1 - n
        h_ref[pl.ds(n, 1), 0:w] = hf
        h_ref[pl.ds(m, 1), w:2 * w] = hb
        sfh = s_ref[pl.ds(n, 1), 0:w]
        sfg = s_ref[pl.ds(n, 1), w:2 * w]
        sbh = s_ref[pl.ds(m, 1), 2 * w:3 * w]
        sbg = s_ref[pl.ds(m, 1), 3 * w:4 * w]
        hf2 = a1f * hf + a2f * gf + sfh
        gf2 = a1f * gf + a3f * hf + sfg
        hb2 = a1b * hb + a2b * gb_ + sbh
        gb2 = a1b * gb_ + a3b * hb + sbg
        return hf2, gf2, hb2, gb2

    z = jnp.zeros((1, w), F32)
    lax.fori_loop(0, nc, step, (z, z, z, z), unroll=4)

    for gi in range(gb):
        u = u_ref[gi]
        hcat = jnp.concatenate([h_ref[:, gi * 128:(gi + 1) * 128],
                                h_ref[:, w + gi * 128:w + (gi + 1) * 128]], axis=1).astype(BF16)
        y = (jnp.dot(u, m_ref[gi], preferred_element_type=F32)
             + jnp.dot(hcat, wc_ref[gi], preferred_element_type=F32)
             + dt_ref[gi] * u.astype(F32))
        yg_ref[gi] = jax.nn.gelu(y)

    for jb in range(S5_T // 8):
        zs = _block_transpose8([yg_ref[g8, :, jb * 128:(jb + 1) * 128] for g8 in range(8)])
        for t8 in range(8):
            y_ref[8 * jb + t8] = zs[t8].astype(BF16)


def _s5_mix(u_steps, m, ws_t, wc, d_tile, a1, a2, a3):
    _, nc, width = u_steps.shape
    G, TL = S5_GROUPS, S5_TL
    gb = S5_GB
    assert gb * S5_GROUP == 128
    w = gb * 128
    steps_spec = pl.BlockSpec((S5_T, nc, 128), lambda i: (0, 0, i))
    a_spec = pl.BlockSpec((2, 1, w), lambda i: (0, 0, i))
    return pl.pallas_call(
        _s5_body,
        grid=(G // gb,),
        in_specs=[
            steps_spec,
            pl.BlockSpec((gb, TL, TL), lambda i: (i, 0, 0)),
            pl.BlockSpec((gb, 512, TL), lambda i: (i, 0, 0)),
            pl.BlockSpec((gb, 256, TL), lambda i: (i, 0, 0)),
            pl.BlockSpec((gb, 1, TL), lambda i: (i, 0, 0)),
            a_spec, a_spec, a_spec,
        ],
        out_specs=steps_spec,
        out_shape=jax.ShapeDtypeStruct((S5_T, nc, width), BF16),
        scratch_shapes=[pltpu.VMEM((nc, 4 * w), F32), pltpu.VMEM((nc, 2 * w), F32),
                        pltpu.VMEM((gb, nc, TL), F32), pltpu.VMEM((gb, nc, TL), BF16)],
        compiler_params=_params(("parallel",)),
        name="s5_mix",
    )(u_steps, m, ws_t, wc, d_tile, a1, a2, a3)


def _row_index(n):
    return lax.broadcasted_iota(jnp.int32, (n, 1), 0).astype(F32)


def _log_decay(lg_ref, d, h):
    return -jnp.abs(jnp.full((1, 1), lg_ref[d, h], F32))


RET_BSTATE_CHUNKS = 8


def _ret_bstate_body(lg_ref, k_ref, v_ref, sb_ref, st_ref):
    C = RET_C

    @pl.when(pl.program_id(0) == 0)
    def _():
        st_ref[...] = jnp.zeros_like(st_ref)

    jj = _row_index(C)
    for sub in reversed(range(k_ref.shape[0] // C)):
        rows = slice(sub * C, (sub + 1) * C)
        for h in range(RET_HEADS):
            lo = h * RET_HEAD_DIM
            lgb = _log_decay(lg_ref, 1, h)
            sb_ref[sub, h] = st_ref[h].astype(BF16)
            kd = (k_ref[rows, lo:lo + RET_HEAD_DIM].astype(F32) * jnp.exp(jj * lgb)).astype(BF16)
            kv = lax.dot_general(kd, v_ref[rows, lo:lo + RET_HEAD_DIM], (((0,), (0,)), ((), ())),
                                 preferred_element_type=F32)
            st_ref[h] = st_ref[h] * jnp.exp(C * lgb) + kv


def _ret_bstate(lg, h5):
    L = h5.shape[1]
    C = RET_C
    nc = L // C
    per = min(RET_BSTATE_CHUNKS, nc)
    steps = nc // per
    return pl.pallas_call(
        _ret_bstate_body,
        grid=(steps,),
        in_specs=[
            pl.BlockSpec(memory_space=pltpu.SMEM),
            pl.BlockSpec((None, per * C, RET_WIDTH), lambda i: (SEG_K - 1, steps - 1 - i, 0)),
            pl.BlockSpec((None, per * C, RET_WIDTH), lambda i: (SEG_V - 1, steps - 1 - i, 0)),
        ],
        out_specs=pl.BlockSpec((per, RET_HEADS, RET_HEAD_DIM, RET_HEAD_DIM),
                               lambda i: (steps - 1 - i, 0, 0, 0)),
        out_shape=jax.ShapeDtypeStruct((nc, RET_HEADS, RET_HEAD_DIM, RET_HEAD_DIM), BF16),
        scratch_shapes=[pltpu.VMEM((RET_HEADS, RET_HEAD_DIM, RET_HEAD_DIM), F32)],
        compiler_params=_params(("arbitrary",)),
        name="ret_bstate",
    )(lg, h5, h5)


RET_MAIN_CHUNKS = 4


def _ret_main_body(lg_ref, q_ref, k_ref, v_ref, g_ref, sb_ref, o_ref, st_ref, dec_ref):
    C = RET_C

    @pl.when(pl.program_id(0) == 0)
    def _():
        st_ref[...] = jnp.zeros_like(st_ref)
        diff = (lax.broadcasted_iota(jnp.int32, (C, C), 0)
                - lax.broadcasted_iota(jnp.int32, (C, C), 1)).astype(F32)
        for h in range(RET_HEADS):
            dec_ref[h] = jnp.where(diff >= 0, jnp.exp(_log_decay(lg_ref, 0, h) * jnp.maximum(diff, 0.0)),
                                   jnp.exp(_log_decay(lg_ref, 1, h) * jnp.maximum(-diff, 0.0)))

    ii = _row_index(C)
    for sub in range(q_ref.shape[0] // C):
        rows = slice(sub * C, (sub + 1) * C)
        for h in range(RET_HEADS):
            lo = h * RET_HEAD_DIM
            cols = slice(lo, lo + RET_HEAD_DIM)
            lgf = _log_decay(lg_ref, 0, h)
            lgb = _log_decay(lg_ref, 1, h)
            q = q_ref[rows, cols]
            k = k_ref[rows, cols]
            v = v_ref[rows, cols]
            qf = q.astype(F32)
            s = lax.dot_general(q, k, (((1,), (1,)), ((), ())), preferred_element_type=F32)
            o = jnp.dot((s * dec_ref[h]).astype(BF16), v, preferred_element_type=F32)
            qdf = (qf * jnp.exp((ii + 1.0) * lgf)).astype(BF16)
            o = o + jnp.dot(qdf, st_ref[h].astype(BF16), preferred_element_type=F32)
            qdb = (qf * jnp.exp((C - ii) * lgb)).astype(BF16)
            o = o + jnp.dot(qdb, sb_ref[sub, h], preferred_element_type=F32)
            mu = jnp.mean(o, axis=-1, keepdims=True)
            oc = o - mu
            var = jnp.mean(oc * oc, axis=-1, keepdims=True)
            on = oc * lax.rsqrt(var + HEAD_NORM_EPS)
            o_ref[rows, cols] = (on * g_ref[rows, cols].astype(F32)).astype(BF16)
            kd = (k.astype(F32) * jnp.exp((C - 1.0 - ii) * lgf)).astype(BF16)
            kv = lax.dot_general(kd, v, (((0,), (0,)), ((), ())), preferred_element_type=F32)
            st_ref[h] = st_ref[h] * jnp.exp(C * lgf) + kv


def _ret_main(lg, h5, sb):
    L = h5.shape[1]
    C = RET_C
    per = min(RET_MAIN_CHUNKS, L // C)
    steps = L // (per * C)
    seg = lambda s: pl.BlockSpec((None, per * C, RET_WIDTH), lambda i: (s - 1, i, 0))
    return pl.pallas_call(
        _ret_main_body,
        grid=(steps,),
        in_specs=[
            pl.BlockSpec(memory_space=pltpu.SMEM),
            seg(SEG_Q), seg(SEG_K), seg(SEG_V), seg(SEG_GATE),
            pl.BlockSpec((per, RET_HEADS, RET_HEAD_DIM, RET_HEAD_DIM), lambda i: (i, 0, 0, 0)),
        ],
        out_specs=pl.BlockSpec((per * C, RET_WIDTH), lambda i: (i, 0)),
        out_shape=jax.ShapeDtypeStruct((L, RET_WIDTH), BF16),
        scratch_shapes=[pltpu.VMEM((RET_HEADS, RET_HEAD_DIM, RET_HEAD_DIM), F32),
                        pltpu.VMEM((RET_HEADS, C, C), F32)],
        compiler_params=_params(("arbitrary",)),
        name="ret_main",
    )(lg, h5, h5, h5, h5, sb)


def _outproj_even_body(y_ref, r_ref, x_ref, wg_ref, bg_ref, wo_ref, g_ref, b_ref, o_ref):
    nl = y_ref.shape[1]
    tm = S5_T * nl
    shift = S5_T.bit_length() - 1
    y_steps = y_ref[...].reshape(tm, y_ref.shape[2])
    perm = _row_permutation(tm, lambda r: (r & (S5_T - 1)) * nl + lax.shift_right_logical(r, shift))
    for r in range(0, tm, LN_ROWS):
        rows = slice(r, r + LN_ROWS)
        y = jnp.dot(perm[rows, :], y_steps, preferred_element_type=F32).astype(BF16)
        z = jnp.dot(y, wg_ref[...], preferred_element_type=F32) + bg_ref[...]
        s5 = (y.astype(F32) * jax.nn.sigmoid(z)).astype(BF16)
        mix = (jnp.dot(s5, wo_ref[0:S5_WIDTH, :], preferred_element_type=F32)
               + jnp.dot(r_ref[rows, :], wo_ref[S5_WIDTH:S5_WIDTH + RET_WIDTH, :],
                         preferred_element_type=F32))
        o_ref[rows, :] = _layer_norm_rows(DEEPNORM_ALPHA * x_ref[rows, :] + mix, g_ref[...], b_ref[...])


def _outproj_even(y_steps, ret, x, w_glu_bf, b_glu, w_out_bf, ln_g, ln_b, tm=512):
    L = x.shape[0]
    row = lambda n: pl.BlockSpec((tm, n), lambda i: (i, 0))
    full = lambda a: pl.BlockSpec(a.shape, lambda i: (0,) * a.ndim)
    return pl.pallas_call(
        _outproj_even_body,
        grid=(L // tm,),
        in_specs=[pl.BlockSpec((S5_T, tm // S5_T, S5_WIDTH), lambda i: (0, i, 0)),
                  row(RET_WIDTH), row(D_MODEL), full(w_glu_bf), full(b_glu),
                  full(w_out_bf), full(ln_g), full(ln_b)],
        out_specs=row(D_MODEL),
        out_shape=jax.ShapeDtypeStruct((L, D_MODEL), F32),
        compiler_params=_params(("parallel",)),
        name="outproj_even",
    )(y_steps, ret, x, w_glu_bf, b_glu, w_out_bf, ln_g, ln_b)


def _inproj_odd_body(x_ref, w_ref, ca_ref, sa_ref, cb_ref, sb_ref, w1f_ref, w2f_ref,
                     o_ref, w1b_ref, w2b_ref):
    _round_mlp_weight_slices(w1f_ref, w2f_ref, w1b_ref, w2b_ref)
    xb = x_ref[...].astype(BF16)
    tn = ATT_KV_HEADS * ATT_HEAD_DIM
    nq = (ATT_HEADS * ATT_HEAD_DIM) // tn
    cc0, sin_blk = _rotary_block(ca_ref, sa_ref, cb_ref, sb_ref, xb.shape[0])
    lane = lax.broadcasted_iota(jnp.int32, sin_blk.shape, 1)
    s10 = jnp.where(lane < ROPE_DIM // 2, -sin_blk, 0.0)
    s20 = jnp.where((lane >= ROPE_DIM // 2) & (lane < ROPE_DIM), sin_blk, 0.0)
    for cb in range(w_ref.shape[1] // tn):
        acc = jnp.dot(xb, w_ref[:, cb * tn:(cb + 1) * tn], preferred_element_type=F32)
        if cb > nq:
            o_ref[:, cb * tn:(cb + 1) * tn] = acc.astype(BF16)
            continue
        scale = ATT_HEAD_DIM ** -0.5 * LOG2_E if cb < nq else 1.0
        cc = cc0 * scale
        s1 = s10 * scale
        s2 = s20 * scale
        for hh in range(tn // ATT_HEAD_DIM):
            lo = hh * ATT_HEAD_DIM
            a = acc[:, lo:lo + ATT_HEAD_DIM]
            up = pltpu.roll(a, ATT_HEAD_DIM - ROPE_DIM // 2, axis=1)
            dn = pltpu.roll(a, ROPE_DIM // 2, axis=1)
            o_ref[:, cb * tn + lo:cb * tn + lo + ATT_HEAD_DIM] = (a * cc + up * s1 + dn * s2).astype(BF16)


def _inproj_odd(x, w_in_bf, rot, mlp_w1, mlp_w2, layer, tm=256):
    L = x.shape[0]
    n_out = w_in_bf.shape[1]
    row = lambda n: pl.BlockSpec((tm, n), lambda i: (i, 0))
    cast_in, cast_out, cast_shapes = _mlp_weight_cast_specs(L // tm, mlp_w1, mlp_w2, layer)
    return pl.pallas_call(
        _inproj_odd_body,
        grid=(L // tm,),
        in_specs=[
            row(D_MODEL),
            pl.BlockSpec(w_in_bf.shape, lambda i: (0, 0), pipeline_mode=pl.Buffered(1)),
        ] + [pl.BlockSpec(t.shape, lambda i: (0, 0)) for t in rot] + cast_in,
        out_specs=[row(n_out)] + cast_out,
        out_shape=[jax.ShapeDtypeStruct((L, n_out), BF16)] + cast_shapes,
        compiler_params=_params(("parallel",)),
        name="inproj_odd",
    )(x, w_in_bf, *rot, mlp_w1, mlp_w2)


ATT_QB = 4


def _attn_body(sink_ref, q_ref, *refs):
    k_refs = refs[:ATT_QB + 2]
    v_refs = refs[ATT_QB + 2:2 * ATT_QB + 4]
    o_ref, bias_ref = refs[2 * ATT_QB + 4:]
    step = pl.program_id(0)
    nb = pl.num_programs(0) * ATT_QB
    B = ATT_BLOCK
    hd = ATT_HEAD_DIM
    rows = ATT_GROUP * B

    @pl.when(step == 0)
    def _():
        r_i = lax.broadcasted_iota(jnp.int32, (rows, 3 * B), 0)
        s_i = lax.broadcasted_iota(jnp.int32, (rows, 3 * B), 1)
        rel = (r_i & (B - 1)) - s_i + B
        in_win = jnp.abs(rel) <= ATT_WINDOW
        bias_ref[0] = jnp.where(in_win & (s_i >= B), 0.0, NEG_INF)
        bias_ref[1] = jnp.where(in_win, 0.0, NEG_INF)
        bias_ref[2] = jnp.where(in_win & (s_i < 2 * B), 0.0, NEG_INF)

    head_of_row = lax.shift_right_logical(lax.broadcasted_iota(jnp.int32, (rows, 1), 0),
                                          int(math.log2(B)))
    for qb in range(ATT_QB):
        c = step * ATT_QB + qb
        bias = bias_ref[jnp.where(c == 0, 0, jnp.where(c == nb - 1, 2, 1))]
        for g in range(ATT_KV_HEADS):
            q = jnp.concatenate([q_ref[qb * B:(qb + 1) * B, (g * ATT_GROUP + hh) * hd:(g * ATT_GROUP + hh + 1) * hd]
                                 for hh in range(ATT_GROUP)], axis=0)
            ksl = slice(g * hd, (g + 1) * hd)
            k = jnp.concatenate([r[:, ksl] for r in k_refs[qb:qb + 3]], axis=0)
            v = jnp.concatenate([r[:, ksl] for r in v_refs[qb:qb + 3]], axis=0)
            s = lax.dot_general(q, k, (((1,), (1,)), ((), ())), preferred_element_type=F32) + bias
            sink = jnp.zeros((rows, 1), F32)
            for hh in range(ATT_GROUP):
                sink = jnp.where(head_of_row == hh, sink_ref[g * ATT_GROUP + hh], sink)
            sink = sink * LOG2_E
            m = jnp.maximum(jnp.max(s, axis=-1, keepdims=True), sink)
            p = jnp.exp2(s - m)
            den = jnp.sum(p, axis=-1, keepdims=True) + jnp.exp2(sink - m)
            o = jnp.dot(p.astype(BF16), v, preferred_element_type=F32) / den
            for hh in range(ATT_GROUP):
                hcol = (g * ATT_GROUP + hh) * hd
                o_ref[qb * B:(qb + 1) * B, hcol:hcol + hd] = o[hh * B:(hh + 1) * B, :].astype(BF16)


def _attention(sink, qkv):
    L = qkv.shape[0]
    B = ATT_BLOCK
    nb = L // B
    assert nb % ATT_QB == 0 and nb >= 2
    kvw = ATT_KV_HEADS * ATT_HEAD_DIM
    qw = ATT_HEADS * ATT_HEAD_DIM
    kcol = qw // kvw
    vcol = kcol + 1

    def kv_spec(col, off):
        return pl.BlockSpec((B, kvw), lambda i: (jnp.clip(i * ATT_QB + off, 0, nb - 1), col))

    offs = range(-1, ATT_QB + 1)
    return pl.pallas_call(
        _attn_body,
        grid=(nb // ATT_QB,),
        in_specs=[
            pl.BlockSpec(memory_space=pltpu.SMEM),
            pl.BlockSpec((ATT_QB * B, qw), lambda i: (i, 0)),
        ] + [kv_spec(kcol, o) for o in offs] + [kv_spec(vcol, o) for o in offs],
        out_specs=pl.BlockSpec((ATT_QB * B, qw), lambda i: (i, 0)),
        out_shape=jax.ShapeDtypeStruct((L, qw), BF16),
        scratch_shapes=[pltpu.VMEM((3, ATT_GROUP * B, 3 * B), F32)],
        compiler_params=_params(("arbitrary",)),
        name="attention",
    )(sink, qkv, *([qkv] * (2 * ATT_QB + 4)))


def _tapered_row_blocks(tm):
    starts = list(range(0, tm - LN_ROWS, LN_ROWS)) + [tm - LN_ROWS, tm - LN_ROWS // 2]
    return [slice(a, b) for a, b in zip(starts, starts[1:] + [tm])]


def _outproj_odd_body(a_ref, x_ref, wo_ref, g_ref, b_ref, o_ref):
    for rows in _tapered_row_blocks(a_ref.shape[0]):
        mix = jnp.dot(a_ref[rows, :], wo_ref[...], preferred_element_type=F32)
        o_ref[rows, :] = _layer_norm_rows(DEEPNORM_ALPHA * x_ref[rows, :] + mix, g_ref[...], b_ref[...])


def _outproj_odd(a, x, w_out_bf, ln_g, ln_b, tm=512):
    L = x.shape[0]
    row = lambda n: pl.BlockSpec((tm, n), lambda i: (i, 0))
    full = lambda t: pl.BlockSpec(t.shape, lambda i: (0,) * t.ndim)
    return pl.pallas_call(
        _outproj_odd_body,
        grid=(L // tm,),
        in_specs=[row(a.shape[1]), row(D_MODEL),
                  pl.BlockSpec(w_out_bf.shape, lambda i: (0, 0), pipeline_mode=pl.Buffered(1)),
                  full(ln_g), full(ln_b)],
        out_specs=row(D_MODEL),
        out_shape=jax.ShapeDtypeStruct((L, D_MODEL), F32),
        compiler_params=_params(("parallel",)),
        name="outproj_odd",
    )(a, x, w_out_bf, ln_g, ln_b)


def _mlp_body(x_ref, w1_ref, w2_ref, g_ref, b_ref, o_ref, xb_ref, acc_ref):
    f = pl.program_id(1)

    def partial_sum(rows):
        h = jnp.dot(xb_ref[rows, :], w1_ref[...], preferred_element_type=F32)
        h = jnp.square(jnp.maximum(h, 0.0)).astype(BF16)
        return jnp.dot(h, w2_ref[...], preferred_element_type=F32)

    last = pl.num_programs(1) - 1

    @pl.when(f == 0)
    def _():
        xb_ref[...] = x_ref[...].astype(BF16)
        acc_ref[...] = partial_sum(slice(None))

    @pl.when((f > 0) & (f < last))
    def _():
        acc_ref[...] += partial_sum(slice(None))

    @pl.when(f == last)
    def _():
        for r in range(0, x_ref.shape[0], LN_ROWS):
            rows = slice(r, r + LN_ROWS)
            y = DEEPNORM_ALPHA * x_ref[rows, :] + (acc_ref[rows, :] + partial_sum(rows))
            o_ref[rows, :] = _layer_norm_rows(y, g_ref[...], b_ref[...])


def _mlp(x, w1, w2, ln_g, ln_b, tm=512):
    tf = MLP_TF
    L = x.shape[0]
    return pl.pallas_call(
        _mlp_body,
        grid=(L // tm, D_FF // tf),
        in_specs=[
            pl.BlockSpec((tm, D_MODEL), lambda i, f: (i, 0)),
            pl.BlockSpec((None, D_MODEL, tf), lambda i, f: (f, 0, 0)),
            pl.BlockSpec((tf, D_MODEL), lambda i, f: (f, 0)),
            pl.BlockSpec((1, D_MODEL), lambda i, f: (0, 0)),
            pl.BlockSpec((1, D_MODEL), lambda i, f: (0, 0)),
        ],
        out_specs=pl.BlockSpec((tm, D_MODEL), lambda i, f: (i, 0)),
        out_shape=jax.ShapeDtypeStruct((L, D_MODEL), F32),
        scratch_shapes=[pltpu.VMEM((tm, D_MODEL), BF16), pltpu.VMEM((tm, D_MODEL), F32)],
        compiler_params=_params(("parallel", "arbitrary")),
        name="mlp",
    )(x, w1, w2, ln_g, ln_b)


def _rotary_tables(L, rot_dim, theta):
    half = rot_dim // 2
    inv_freq = 1.0 / (theta ** (jnp.arange(half, dtype=F32) / half))
    ang_a = (jnp.arange(L // ROT_LO) * ROT_LO).astype(F32)[:, None] * inv_freq[None, :]
    ang_b = jnp.arange(ROT_LO).astype(F32)[:, None] * inv_freq[None, :]
    return jnp.cos(ang_a), jnp.sin(ang_a), jnp.cos(ang_b), jnp.sin(ang_b)


def _even_layer(x, w_in, w_out, lam_re, lam_im, log_step, b_re, b_im, c_re, c_im,
                d_skip, w_glu, b_glu, ret_log_decay, ln_g, ln_b, mlp_w1, mlp_w2, layer, later_weights):
    L = x.shape[0]
    rot = _rotary_tables(L, RET_HEAD_DIM, RET_ROPE_THETA)
    disc = _s5_disc(lam_re, lam_im, log_step)
    m, ws_t, wc, (w_in_bf, w_glu_bf, w_out_bf, *later_bf) = _s5_gen(
        disc, b_re, b_im, c_re, c_im, (w_in, w_glu, w_out) + tuple(later_weights))
    u_steps, h5, w1_bf, w2_bf = _inproj_even(x, w_in_bf, rot, mlp_w1, mlp_w2, layer)
    d_tile = jnp.tile(d_skip.astype(F32), (1, S5_T)).reshape(S5_GROUPS, 1, S5_TL)
    a1, a2, a3 = (disc[k].reshape(2, 1, S5_GROUPS * 128) for k in (DISC_AT_RE, DISC_SCAN_A2, DISC_SCAN_A3))
    y = _s5_mix(u_steps, m, ws_t, wc, d_tile, a1, a2, a3)
    lg = ret_log_decay.astype(F32)
    sb = _ret_bstate(lg, h5)
    ret = _ret_main(lg, h5, sb)
    x1 = _outproj_even(y, ret, x, w_glu_bf, b_glu.astype(F32).reshape(1, -1),
                       w_out_bf, ln_g.reshape(1, -1), ln_b.reshape(1, -1))
    return x1, w1_bf, w2_bf, later_bf


def _odd_layer(x, w_in_bf, w_out_bf, sink, ln_g, ln_b, mlp_w1, mlp_w2, layer):
    L = x.shape[0]
    pad = ATT_HEAD_DIM - ROPE_DIM
    widen = lambda t, fill: jnp.concatenate([t, t, jnp.full((t.shape[0], pad), fill, F32)], axis=1)
    ca, sa, cb, sb = _rotary_tables(L, ROPE_DIM, ROPE_THETA)
    rot = (widen(ca, 1.0), widen(sa, 0.0), widen(cb, 1.0), widen(sb, 0.0))
    qkv, w1_bf, w2_bf = _inproj_odd(x, w_in_bf, rot, mlp_w1, mlp_w2, layer)
    att = _attention(sink.astype(F32), qkv)
    x1 = _outproj_odd(att, x, w_out_bf, ln_g.reshape(1, -1), ln_b.reshape(1, -1))
    return x1, w1_bf, w2_bf


def kernel(x, ln_g, ln_b, mlp_w1, mlp_w2, even_w_in, even_w_out, s5_lambda_re, s5_lambda_im, s5_log_step, s5_b_re, s5_b_im, s5_c_re, s5_c_im, s5_d, s5_w_glu, s5_b_glu, ret_log_decay, odd_w_in, odd_w_out, attn_sink):
    bsz = x.shape[0]
    outs = []
    for b in range(bsz):
        xb = x[b]
        for layer in range(DEPTH):
            if layer % 2 == 0:
                e = layer // 2
                later = (odd_w_in[e], odd_w_out[e]) if layer + 1 < DEPTH else ()
                xb, w1_bf, w2_bf, later_bf = _even_layer(
                    xb, even_w_in[e], even_w_out[e], s5_lambda_re[e], s5_lambda_im[e],
                    s5_log_step[e], s5_b_re[e], s5_b_im[e], s5_c_re[e], s5_c_im[e],
                    s5_d[e], s5_w_glu[e], s5_b_glu[e], ret_log_decay[e],
                    ln_g[layer, 0], ln_b[layer, 0], mlp_w1, mlp_w2, layer, later)
            else:
                o = layer // 2
                xb, w1_bf, w2_bf = _odd_layer(xb, later_bf[0], later_bf[1], attn_sink[o],
                                              ln_g[layer, 0], ln_b[layer, 0], mlp_w1, mlp_w2, layer)
            xb = _mlp(xb, w1_bf, w2_bf,
                      ln_g[layer, 1].reshape(1, -1), ln_b[layer, 1].reshape(1, -1))
        outs.append(xb)
    return jnp.stack(outs, axis=0)
```
